```python
import jax, jax.numpy as jnp
from jax import lax
import numpy as np

D_MODEL = 1024
BATCH = 32
SEQ = 2048
DEPTH = 1

N_HEADS = 16
N_KV_GROUPS = 2
HEADS_PER_GROUP = N_HEADS // N_KV_GROUPS
HEAD_DIM = 64
NSA_WIDTH = N_HEADS * HEAD_DIM
KV_WIDTH = N_KV_GROUPS * HEAD_DIM
CMP_BLOCK = 32
CMP_STRIDE = 16
CMP_HIDDEN = 256
SEL_BLOCK = 64
N_SELECT = 8
WINDOW = 512
Q_BLOCK = 128
CONV_WIDTH = D_MODEL
CONV_KERNEL = 31
REL_BUCKETS = 32
REL_MAX_DIST = 128
EPS = 1e-6
MASK_VALUE = -1e30
FORCE_SCORE = 1e6
IN_WIDTH = NSA_WIDTH + 6 * KV_WIDTH + 3 * N_HEADS + NSA_WIDTH + 3 * CONV_WIDTH + 2 * D_MODEL

kernel_name = "hybrid_conformer_conv_nsa_gated_block"


def rmsnorm(x, g):
    xf = x.astype(jnp.float32)
    y = xf * lax.rsqrt(jnp.mean(xf * xf, axis=-1, keepdims=True) + EPS)
    return (y * g.astype(jnp.float32)).astype(x.dtype)


def layernorm(x, g, b):
    xf = x.astype(jnp.float32)
    mu = jnp.mean(xf, axis=-1, keepdims=True)
    var = jnp.mean(jnp.square(xf - mu), axis=-1, keepdims=True)
    y = (xf - mu) * lax.rsqrt(var + EPS)
    return (y * g.astype(jnp.float32) + b.astype(jnp.float32)).astype(x.dtype)


def masked_softmax(logits, mask):
    lf = jnp.where(mask, logits.astype(jnp.float32), MASK_VALUE)
    m = jnp.max(lf, axis=-1, keepdims=True)
    e = jnp.where(mask, jnp.exp(lf - m), 0.0)
    return e / jnp.maximum(jnp.sum(e, axis=-1, keepdims=True), 1e-30)


def t5_bucket(rel):
    rel = jnp.maximum(rel, 0)
    max_exact = REL_BUCKETS // 2
    relf = jnp.maximum(rel, 1).astype(jnp.float32)
    large = max_exact + (jnp.log(relf / max_exact) / np.float32(np.log(REL_MAX_DIST / max_exact))
                         * (REL_BUCKETS - max_exact)).astype(jnp.int32)
    large = jnp.minimum(large, REL_BUCKETS - 1)
    return jnp.where(rel < max_exact, rel, large)


def compress(k, pos, w1, w2):
    B, S, G, dk = k.shape
    n_cmp = (S - CMP_BLOCK) // CMP_STRIDE + 1
    idx = jnp.arange(n_cmp)[:, None] * CMP_STRIDE + jnp.arange(CMP_BLOCK)[None, :]
    blocks = k[:, idx] + pos[None, None, :, None, :]
    blocks = blocks.transpose(0, 1, 3, 2, 4).reshape(B, n_cmp, G, CMP_BLOCK * dk)
    return jax.nn.silu(blocks @ w1) @ w2


def nsa_attention(q, k_cmp, v_cmp, k_slc, v_slc, k_win, v_win, branch_gates, rel_bias):
    B, S = q.shape[:2]
    G, hg, dk = N_KV_GROUPS, HEADS_PER_GROUP, HEAD_DIM
    q = q.reshape(B, S, G, hg, dk) * (dk ** -0.5)
    n_cmp = k_cmp.shape[1]
    n_sel = S // SEL_BLOCK
    n_top = min(N_SELECT, n_sel)
    kw_len = Q_BLOCK + WINDOW

    cmp_start = jnp.arange(n_cmp) * CMP_STRIDE
    cmp_end = cmp_start + CMP_BLOCK - 1
    sel_start = jnp.arange(n_sel) * SEL_BLOCK
    overlap = ((cmp_start[:, None] <= sel_start[None, :] + SEL_BLOCK - 1)
               & (cmp_end[:, None] >= sel_start[None, :])).astype(jnp.float32)

    ks_blocks = k_slc.reshape(B, n_sel, SEL_BLOCK, G, dk).transpose(0, 3, 1, 2, 4)
    vs_blocks = v_slc.reshape(B, n_sel, SEL_BLOCK, G, dk).transpose(0, 3, 1, 2, 4)
    kw_pad = jnp.pad(k_win, ((0, 0), (WINDOW, 0), (0, 0), (0, 0)))
    vw_pad = jnp.pad(v_win, ((0, 0), (WINDOW, 0), (0, 0), (0, 0)))
    table_g = rel_bias.reshape(REL_BUCKETS, G, hg).transpose(1, 0, 2)
    gates = jax.nn.sigmoid(branch_gates.astype(jnp.float32)).reshape(B, S, 3, G, hg)
    b_idx = jnp.arange(B)[:, None, None, None]
    g_idx = jnp.arange(G)[None, None, :, None]
    sel_j = jnp.arange(n_sel)

    def block(qi):
        q0 = qi * Q_BLOCK
        t = q0 + jnp.arange(Q_BLOCK)
        qb = lax.dynamic_slice_in_dim(q, q0, Q_BLOCK, axis=1)
        gb = lax.dynamic_slice_in_dim(gates, q0, Q_BLOCK, axis=1)

        rel_c = t[:, None] - cmp_end[None, :]
        bias_c = rel_bias[t5_bucket(rel_c)].reshape(Q_BLOCK, n_cmp, G, hg).transpose(0, 2, 3, 1)
        logit_c = jnp.einsum('bqghd,bcgd->bqghc', qb, k_cmp) + bias_c
        p_c = masked_softmax(logit_c, (rel_c >= 0)[:, None, None, :])
        o_c = jnp.einsum('bqghc,bcgd->bqghd', p_c, v_cmp)

        imp = jnp.einsum('bqghc,cj->bqgj', p_c, overlap)
        blk_t = t // SEL_BLOCK
        valid = sel_j[None, :] <= blk_t[:, None]
        forced = ((sel_j[None, :] == 0) | (sel_j[None, :] == blk_t[:, None])
                  | (sel_j[None, :] == blk_t[:, None] - 1))
        prio = jnp.where(forced[None, :, None, :], FORCE_SCORE, imp)
        prio = jnp.where(valid[None, :, None, :], prio, -FORCE_SCORE)
        top_val, top_idx = lax.top_k(prio, n_top)
        blk_ok = jnp.repeat(top_val > -0.5 * FORCE_SCORE, SEL_BLOCK, axis=-1)
        k_sel = ks_blocks[b_idx, g_idx, top_idx].reshape(B, Q_BLOCK, G, n_top * SEL_BLOCK, dk)
        v_sel = vs_blocks[b_idx, g_idx, top_idx].reshape(B, Q_BLOCK, G, n_top * SEL_BLOCK, dk)
        key_pos = (top_idx[..., None] * SEL_BLOCK + jnp.arange(SEL_BLOCK)).reshape(
            B, Q_BLOCK, G, n_top * SEL_BLOCK)
        rel_s = t[None, :, None, None] - key_pos
        bias_s = table_g[g_idx, t5_bucket(rel_s)].transpose(0, 1, 2, 4, 3)
        logit_s = jnp.einsum('bqghd,bqgkd->bqghk', qb, k_sel) + bias_s
        mask_s = (blk_ok & (rel_s >= 0))[:, :, :, None, :]
        p_s = masked_softmax(logit_s, mask_s)
        o_s = jnp.einsum('bqghk,bqgkd->bqghd', p_s, v_sel)

        kw = lax.dynamic_slice_in_dim(kw_pad, q0, kw_len, axis=1)
        vw = lax.dynamic_slice_in_dim(vw_pad, q0, kw_len, axis=1)
        key_pos_w = q0 - WINDOW + jnp.arange(kw_len)
        rel_w = t[:, None] - key_pos_w[None, :]
        mask_w = (rel_w >= 0) & (rel_w < WINDOW) & (key_pos_w[None, :] >= 0)
        bias_w = rel_bias[t5_bucket(rel_w)].reshape(Q_BLOCK, kw_len, G, hg).transpose(0, 2, 3, 1)
        logit_w = jnp.einsum('bqghd,bkgd->bqghk', qb, kw) + bias_w
        p_w = masked_softmax(logit_w, mask_w[:, None, None, :])
        o_w = jnp.einsum('bqghk,bkgd->bqghd', p_w, vw)

        return (gb[:, :, 0, :, :, None] * o_c + gb[:, :, 1, :, :, None] * o_s
                + gb[:, :, 2, :, :, None] * o_w)

    out = lax.map(block, jnp.arange(S // Q_BLOCK))
    return out.transpose(1, 0, 2, 3, 4, 5).reshape(B, S, NSA_WIDTH)


def causal_depthwise_conv(u, w, b):
    C = u.shape[-1]
    y = lax.conv_general_dilated(
        u, w[:, None, :].astype(u.dtype), window_strides=(1,), padding=[(CONV_KERNEL - 1, 0)],
        dimension_numbers=('NWC', 'WIO', 'NWC'), feature_group_count=C)
    return y + b


def setup_inputs(seed: int = 0) -> dict:
    key = jax.random.key(seed)
    ks = jax.random.split(key, 20)
    f32 = jnp.float32
    nrm = lambda k, shape, scale: (jax.random.normal(k, shape, f32) * scale).astype(f32)
    L = DEPTH
    return {
        "x": nrm(ks[0], (BATCH, SEQ, D_MODEL), 1.0),
        "norm_in_g": 1.0 + nrm(ks[1], (L, D_MODEL), 0.01),
        "w_in": nrm(ks[2], (L, D_MODEL, IN_WIDTH), D_MODEL ** -0.5),
        "pos_ck": nrm(ks[3], (L, CMP_BLOCK, HEAD_DIM), 0.1),
        "w_ck1": nrm(ks[4], (L, CMP_BLOCK * HEAD_DIM, CMP_HIDDEN), (CMP_BLOCK * HEAD_DIM) ** -0.5),
        "w_ck2": nrm(ks[5], (L, CMP_HIDDEN, HEAD_DIM), CMP_HIDDEN ** -0.5),
        "pos_cv": nrm(ks[6], (L, CMP_BLOCK, HEAD_DIM), 0.1),
        "w_cv1": nrm(ks[7], (L, CMP_BLOCK * HEAD_DIM, CMP_HIDDEN), (CMP_BLOCK * HEAD_DIM) ** -0.5),
        "w_cv2": nrm(ks[8], (L, CMP_HIDDEN, HEAD_DIM), CMP_HIDDEN ** -0.5),
        "rel_bias": nrm(ks[9], (REL_BUCKETS, N_HEADS), 0.5),
        "conv_w": nrm(ks[10], (L, CONV_KERNEL, CONV_WIDTH), CONV_KERNEL ** -0.5),
        "conv_b": nrm(ks[11], (L, CONV_WIDTH), 0.01),
        "conv_ln_g": 1.0 + nrm(ks[12], (L, CONV_WIDTH), 0.01),
        "conv_ln_b": nrm(ks[13], (L, CONV_WIDTH), 0.01),
        "w_conv_proj": nrm(ks[14], (L, CONV_WIDTH, D_MODEL), CONV_WIDTH ** -0.5),
        "w_nsa_proj": nrm(ks[15], (L, NSA_WIDTH, D_MODEL), NSA_WIDTH ** -0.5),
        "w_out": nrm(ks[16], (L, D_MODEL, D_MODEL), D_MODEL ** -0.5),
        "norm_f_g": 1.0 + nrm(ks[17], (D_MODEL,), 0.01),
    }


def reference(x, norm_in_g, w_in, pos_ck, w_ck1, w_ck2, pos_cv, w_cv1, w_cv2, rel_bias,
              conv_w, conv_b, conv_ln_g, conv_ln_b, w_conv_proj, w_nsa_proj, w_out, norm_f_g):
    B, S, _ = x.shape
    sizes = [NSA_WIDTH] + [KV_WIDTH] * 6 + [3 * N_HEADS, NSA_WIDTH, 2 * CONV_WIDTH,
                                            CONV_WIDTH, 2 * D_MODEL]
    offsets = np.cumsum(sizes)[:-1].tolist()
    kv_shape = (B, S, N_KV_GROUPS, HEAD_DIM)
    for l in range(DEPTH):
        h = rmsnorm(x, norm_in_g[l])
        proj = h @ w_in[l]
        (q, kc_raw, vc_raw, k_slc, v_slc, k_win, v_win, nsa_gates, z_nsa,
         glu_in, z_conv, merge_g) = jnp.split(proj, offsets, axis=-1)

        k_cmp = compress(kc_raw.reshape(kv_shape), pos_ck[l], w_ck1[l], w_ck2[l])
        v_cmp = compress(vc_raw.reshape(kv_shape), pos_cv[l], w_cv1[l], w_cv2[l])
        o_nsa = nsa_attention(q, k_cmp, v_cmp, k_slc.reshape(kv_shape), v_slc.reshape(kv_shape),
                              k_win.reshape(kv_shape), v_win.reshape(kv_shape), nsa_gates, rel_bias)
        y_nsa = (o_nsa.astype(x.dtype) * jax.nn.silu(z_nsa)) @ w_nsa_proj[l]

        a, b = jnp.split(glu_in, 2, axis=-1)
        u = a * jax.nn.sigmoid(b)
        c = causal_depthwise_conv(u, conv_w[l], conv_b[l])
        c = jax.nn.silu(layernorm(c, conv_ln_g[l], conv_ln_b[l]))
        y_conv = (c * jax.nn.silu(z_conv)) @ w_conv_proj[l]

        g_conv, g_nsa = jnp.split(jax.nn.sigmoid(merge_g), 2, axis=-1)
        x = x + (g_conv * y_conv + g_nsa * y_nsa) @ w_out[l]
    return rmsnorm(x, norm_f_g)
```

```python
import functools

import numpy as np
import jax
import jax.numpy as jnp
from jax import lax
from jax.experimental import pallas as pl
from jax.experimental.pallas import tpu as pltpu

F32 = jnp.float32
BF16 = jnp.bfloat16

D_MODEL = 1024
N_HEADS = 16
N_GROUPS = 2
HEADS_PER_GROUP = N_HEADS // N_GROUPS
PAIRS_PER_GROUP = HEADS_PER_GROUP // 2
HEAD_DIM = 64
NSA_WIDTH = N_HEADS * HEAD_DIM
KV_WIDTH = N_GROUPS * HEAD_DIM
CMP_BLOCK = 32
CMP_STRIDE = 16
CMP_HIDDEN = 256
SEL_BLOCK = 64
N_SELECT = 8
WINDOW = 512
Q_TILE = 128
CONV_KERNEL = 31
CONV_HALO = 32
REL_BUCKETS = 32
REL_MAX_DIST = 128
EPS = 1e-6
NEG = -1e30
FORCE_SCORE = 1e6
LANES = 128
COL_TILE = 1024
N_COL_TILES = 8
WIN_TILES = WINDOW // Q_TILE + 1
VMEM_LIMIT = 56 * 1024 * 1024


def _dot(a, b):
    return jnp.dot(a, b, preferred_element_type=F32)


def _dot_nt(a, b):
    return lax.dot_general(a, b, (((1,), (1,)), ((), ())), preferred_element_type=F32)


def _sigmoid(x):
    return 1.0 / (1.0 + jnp.exp(-x))


def _silu(x):
    return x * _sigmoid(x)


def _proj_kernel(x_ref, g_ref, w_ref, kv_ref, main_ref, h_ref):
    j = pl.program_id(1)

    @pl.when(j == 0)
    def _():
        x = x_ref[...]
        ms = jnp.mean(x * x, axis=-1, keepdims=True)
        h_ref[...] = ((x * lax.rsqrt(ms + EPS)) * g_ref[...]).astype(BF16)

    acc = _dot(h_ref[...], w_ref[...])

    @pl.when(j == 0)
    def _():
        kv_ref[...] = acc

    @pl.when(j > 0)
    def _():
        main_ref[...] = acc.astype(BF16)


def _input_projection(x2, g, w_perm, tm):
    m = x2.shape[0]
    return pl.pallas_call(
        _proj_kernel,
        grid=(m // tm, N_COL_TILES),
        in_specs=[
            pl.BlockSpec((tm, D_MODEL), lambda i, j: (i, 0)),
            pl.BlockSpec((1, D_MODEL), lambda i, j: (0, 0)),
            pl.BlockSpec((D_MODEL, COL_TILE), lambda i, j: (0, j)),
        ],
        out_specs=[
            pl.BlockSpec((tm, COL_TILE), lambda i, j: (i, 0)),
            pl.BlockSpec((tm, COL_TILE), lambda i, j: (i, jnp.maximum(j - 1, 0))),
        ],
        out_shape=[
            jax.ShapeDtypeStruct((m, COL_TILE), F32),
            jax.ShapeDtypeStruct((m, (N_COL_TILES - 1) * COL_TILE), BF16),
        ],
        scratch_shapes=[pltpu.VMEM((tm, D_MODEL), BF16)],
        compiler_params=pltpu.CompilerParams(
            dimension_semantics=("parallel", "arbitrary"), vmem_limit_bytes=VMEM_LIMIT),
        name="input_projection",
    )(x2, g, w_perm)


def _compress_kernel(kf_ref, vf_ref, pk_ref, pv_ref, w1k_ref, w1v_ref, w2k_ref, w2v_ref,
                     kc_ref, vc_ref):
    def one(f_ref, pos_ref, w1_ref, w2_ref, o_ref):
        f = f_ref[0]
        n = f.shape[0]
        hw = N_GROUPS * CMP_HIDDEN
        first = _dot((f + pos_ref[0:1, :]).astype(BF16), w1_ref[:, 0:hw])
        second = _dot((f + pos_ref[1:2, :]).astype(BF16), w1_ref[:, hw:2 * hw])
        hid = first + pltpu.roll(second, n - 1, axis=0)
        o_ref[0] = _dot(_silu(hid).astype(BF16), w2_ref[...])

    one(kf_ref, pk_ref, w1k_ref, w2k_ref, kc_ref)
    one(vf_ref, pv_ref, w1v_ref, w2v_ref, vc_ref)


def _compress(kf, vf, pk, pv, w1k, w1v, w2k, w2v):
    b, n, width = kf.shape
    const = lambda shape: pl.BlockSpec(shape, lambda i: (0,) * len(shape))
    row = pl.BlockSpec((1, n, width), lambda i: (i, 0, 0))
    out = pl.BlockSpec((1, n, LANES), lambda i: (i, 0, 0))
    return pl.pallas_call(
        _compress_kernel,
        grid=(b,),
        in_specs=[row, row, const(pk.shape), const(pv.shape), const(w1k.shape), const(w1v.shape),
                  const(w2k.shape), const(w2v.shape)],
        out_specs=[out, out],
        out_shape=[jax.ShapeDtypeStruct((b, n, LANES), F32)] * 2,
        compiler_params=pltpu.CompilerParams(
            dimension_semantics=("parallel",), vmem_limit_bytes=VMEM_LIMIT),
        name="nsa_compress",
    )(kf, vf, pk, pv, w1k, w1v, w2k, w2v)


def _attn_kernel(q_ref, ks_ref, vs_ref, kw_ref, vw_ref, gt_ref, kc_ref, vc_ref, bc_ref, tw_ref,
                 ovt_ref, exp_ref, o_ref,
                 ksv, vsv, kwv, vwv, kcv, vcv, s_scr, selm, obuf, *, n_top):
    qi = pl.program_id(1)
    n_kt = ksv.shape[1]
    n_sel = ovt_ref.shape[0]
    rows = PAIRS_PER_GROUP * Q_TILE

    @pl.when(qi == 0)
    def _prepare_kv():
        def variants(k, dst):
            lo = lax.broadcasted_iota(jnp.int32, k.shape, 1) < HEAD_DIM
            kr = pltpu.roll(k, HEAD_DIM, axis=1)
            z = jnp.zeros_like(k)
            tiles = k.shape[0] // Q_TILE
            for idx, a in enumerate((jnp.where(lo, k, z), jnp.where(lo, z, kr),
                                     jnp.where(lo, kr, z), jnp.where(lo, z, k))):
                dst[idx] = a.astype(BF16).reshape(tiles, Q_TILE, LANES)

        variants(ks_ref[...], ksv)
        variants(vs_ref[...], vsv)
        variants(kw_ref[...], kwv)
        variants(vw_ref[...], vwv)
        variants(kc_ref[0], kcv)
        variants(vc_ref[0], vcv)

    lane_lo = lax.broadcasted_iota(jnp.int32, (Q_TILE, LANES), 1) < HEAD_DIM
    qt = q_ref[...]
    gsig = _sigmoid(gt_ref[...])

    for g in range(N_GROUPS):
        q4 = jnp.concatenate(
            [qt[:, (g * PAIRS_PER_GROUP + p) * LANES:(g * PAIRS_PER_GROUP + p + 1) * LANES]
             for p in range(PAIRS_PER_GROUP)], axis=0)
        q4 = (q4.astype(F32) * (HEAD_DIM ** -0.5)).astype(BF16)

        psum = jnp.zeros((Q_TILE, LANES), F32)
        o_c = jnp.zeros((rows, LANES), F32)
        for v in range(2):
            gv = g * 2 + v
            s = _dot_nt(q4, kcv[gv, 0]) + bc_ref[g, v].reshape(rows, LANES)
            valid = s > 0.5 * NEG
            m = jnp.max(s, axis=1, keepdims=True)
            e = jnp.where(valid, jnp.exp(s - m), 0.0)
            p = e / jnp.maximum(jnp.sum(e, axis=1, keepdims=True), 1e-30)
            psum = psum + jnp.sum(p.reshape(PAIRS_PER_GROUP, Q_TILE, LANES), axis=0)
            o_c = o_c + _dot(p.astype(BF16), vcv[gv, 0])
        obuf[0] = o_c

        p_hi = psum.astype(BF16)
        r1 = psum - p_hi.astype(F32)
        p_mid = r1.astype(BF16)
        p_lo = (r1 - p_mid.astype(F32)).astype(BF16)
        ovt = ovt_ref[...]
        imp_t = _dot_nt(ovt, p_hi) + _dot_nt(ovt, p_mid) + _dot_nt(ovt, p_lo)

        j_idx = lax.broadcasted_iota(jnp.int32, (n_sel, Q_TILE), 0)
        r_idx = lax.broadcasted_iota(jnp.int32, (n_sel, Q_TILE), 1)
        blk_t = qi * (Q_TILE // SEL_BLOCK) + r_idx // SEL_BLOCK
        valid_blk = j_idx <= blk_t
        forced = (j_idx == 0) | (j_idx == blk_t) | (j_idx == blk_t - 1)
        prio = jnp.where(forced, FORCE_SCORE, imp_t)
        prio = jnp.where(valid_blk, prio, -FORCE_SCORE)
        rank = jnp.zeros((n_sel, Q_TILE), F32)
        for jj in range(n_sel):
            row = prio[jj:jj + 1, :]
            beats = (row > prio) | ((row == prio) & (j_idx > jj))
            rank = rank + jnp.where(beats, 1.0, 0.0)
        sel_t = jnp.where((rank < n_top) & valid_blk, 1.0, 0.0)
        if n_sel < LANES:
            sel_t = jnp.concatenate([sel_t, jnp.zeros((LANES - n_sel, Q_TILE), F32)], axis=0)
        sel = sel_t.T.astype(BF16)
        sel_full = _dot(sel, exp_ref[...])
        for kt in range(n_kt):
            selm[kt] = sel_full[:, kt * Q_TILE:(kt + 1) * Q_TILE]

        def two_pass(n_tiles, tile_of, kvar, vvar, masked):
            o = jnp.zeros((rows, LANES), F32)
            for v in range(2):
                gv = g * 2 + v

                def pass1(i, mx):
                    kt, dd = tile_of(i)
                    s = _dot_nt(q4, kvar[gv, kt]) + tw_ref[g, v, dd].reshape(rows, LANES)
                    if masked:
                        keep = selm[kt] > 0.5
                        s = jnp.where(keep[None], s.reshape(PAIRS_PER_GROUP, Q_TILE, LANES),
                                      NEG).reshape(rows, LANES)
                    s_scr[i] = s
                    return jnp.maximum(mx, s)

                mx = lax.fori_loop(0, n_tiles, pass1, jnp.full((rows, LANES), NEG, F32))
                m = jnp.max(mx, axis=1, keepdims=True)

                def pass2(i, carry):
                    l, acc = carry
                    kt, _ = tile_of(i)
                    e = jnp.exp(s_scr[i] - m)
                    return l + e, acc + _dot(e.astype(BF16), vvar[gv, kt])

                l, acc = lax.fori_loop(0, n_tiles, pass2,
                                       (jnp.zeros((rows, LANES), F32), jnp.zeros((rows, LANES), F32)))
                o = o + acc / jnp.maximum(jnp.sum(l, axis=1, keepdims=True), 1e-30)
            return o

        obuf[1] = two_pass(qi + 1, lambda i: (i, jnp.minimum(qi - i, 2)), ksv, vsv, True)
        obuf[2] = two_pass(jnp.minimum(qi, WIN_TILES - 1) + 1, lambda i: (qi - i, i), kwv, vwv, False)

        for p in range(PAIRS_PER_GROUP):
            tot = jnp.zeros((Q_TILE, LANES), F32)
            for br in range(3):
                c = br * N_HEADS + g * HEADS_PER_GROUP + 2 * p
                gate = jnp.where(lane_lo, gsig[:, c:c + 1], gsig[:, c + 1:c + 2])
                tot = tot + gate * obuf[br, p * Q_TILE:(p + 1) * Q_TILE, :]
            col = (g * PAIRS_PER_GROUP + p) * LANES
            o_ref[:, col:col + LANES] = tot.astype(BF16)


def _attention(proj_main, proj_kv, kcmp, vcmp, bias_c, bias_w, ovt, expand, b, s, n_top):
    n_q = s // Q_TILE
    n_sel = s // SEL_BLOCK
    rows = PAIRS_PER_GROUP * Q_TILE
    kv_col = lambda c: pl.BlockSpec((s, LANES), lambda bi, qi: (bi, c))
    cmp_spec = pl.BlockSpec((1, LANES, LANES), lambda bi, qi: (bi, 0, 0))
    return pl.pallas_call(
        functools.partial(_attn_kernel, n_top=n_top),
        grid=(b, n_q),
        in_specs=[
            pl.BlockSpec((Q_TILE, NSA_WIDTH), lambda bi, qi: (bi * n_q + qi, 0)),
            kv_col(2), kv_col(3), kv_col(4), kv_col(5),
            pl.BlockSpec((Q_TILE, LANES), lambda bi, qi: (bi * n_q + qi, 6)),
            cmp_spec, cmp_spec,
            pl.BlockSpec((N_GROUPS, 2, PAIRS_PER_GROUP, Q_TILE, LANES),
                         lambda bi, qi: (0, 0, 0, qi, 0)),
            pl.BlockSpec(bias_w.shape, lambda bi, qi: (0,) * 6),
            pl.BlockSpec(ovt.shape, lambda bi, qi: (0, 0)),
            pl.BlockSpec(expand.shape, lambda bi, qi: (0, 0)),
        ],
        out_specs=pl.BlockSpec((Q_TILE, NSA_WIDTH), lambda bi, qi: (bi * n_q + qi, 0)),
        out_shape=jax.ShapeDtypeStruct((b * s, NSA_WIDTH), BF16),
        scratch_shapes=[
            pltpu.VMEM((2 * N_GROUPS, n_q, Q_TILE, LANES), BF16),
            pltpu.VMEM((2 * N_GROUPS, n_q, Q_TILE, LANES), BF16),
            pltpu.VMEM((2 * N_GROUPS, n_q, Q_TILE, LANES), BF16),
            pltpu.VMEM((2 * N_GROUPS, n_q, Q_TILE, LANES), BF16),
            pltpu.VMEM((2 * N_GROUPS, 1, LANES, LANES), BF16),
            pltpu.VMEM((2 * N_GROUPS, 1, LANES, LANES), BF16),
            pltpu.VMEM((n_q, rows, LANES), F32),
            pltpu.VMEM((n_q, Q_TILE, LANES), F32),
            pltpu.VMEM((3, rows, LANES), F32),
        ],
        compiler_params=pltpu.CompilerParams(
            dimension_semantics=("parallel", "arbitrary"), vmem_limit_bytes=VMEM_LIMIT),
        name="nsa_attention",
    )(proj_main, proj_kv, proj_kv, proj_kv, proj_kv, proj_kv, kcmp, vcmp, bias_c, bias_w,
      ovt, expand)


def _merge_kernel(zn_ref, a_ref, b_ref, zc_ref, gc_ref, gn_ref, ah_ref, bh_ref, on_ref, x_ref,
                  cw_ref, cb_ref, lg_ref, lb_ref, wcp_ref, wnp_ref, wo_ref, gf_ref, out_ref, uext):
    i = pl.program_id(1)
    ts = a_ref.shape[0]
    f = lambda r: r[...].astype(F32)

    u_halo = f(ah_ref) * _sigmoid(f(bh_ref))
    uext[0:CONV_HALO, :] = jnp.where(i > 0, u_halo, 0.0)
    uext[CONV_HALO:, :] = f(a_ref) * _sigmoid(f(b_ref))

    c = jnp.broadcast_to(cb_ref[...], (ts, D_MODEL))
    for j in range(CONV_KERNEL):
        off = CONV_HALO - (CONV_KERNEL - 1) + j
        c = c + cw_ref[j:j + 1, :] * uext[off:off + ts, :]

    mu = jnp.mean(c, axis=-1, keepdims=True)
    cc = c - mu
    var = jnp.mean(cc * cc, axis=-1, keepdims=True)
    y = (cc * lax.rsqrt(var + EPS)) * lg_ref[...] + lb_ref[...]
    conv_act = _silu(y) * _silu(f(zc_ref))
    y_conv = _dot(conv_act.astype(BF16), wcp_ref[...])

    nsa_act = f(on_ref) * _silu(f(zn_ref))
    y_nsa = _dot(nsa_act.astype(BF16), wnp_ref[...])

    merged = _sigmoid(f(gc_ref)) * y_conv + _sigmoid(f(gn_ref)) * y_nsa
    xo = x_ref[...] + _dot(merged.astype(BF16), wo_ref[...])
    ms = jnp.mean(xo * xo, axis=-1, keepdims=True)
    out_ref[...] = (xo * lax.rsqrt(ms + EPS)) * gf_ref[...]


def _merge(proj_main, o_nsa, x2, cw, cb, lg, lb, wcp, wnp, wo, gf, b, s, ts):
    n_t = s // ts
    halo_per_tile = ts // CONV_HALO
    col = lambda c: pl.BlockSpec((ts, COL_TILE), lambda bi, ti: (bi * n_t + ti, c))
    halo = lambda c: pl.BlockSpec(
        (CONV_HALO, COL_TILE),
        lambda bi, ti: (jnp.maximum((bi * n_t + ti) * halo_per_tile - 1, 0), c))
    const = lambda a: pl.BlockSpec(a.shape, lambda bi, ti: (0, 0))
    rowblk = pl.BlockSpec((ts, D_MODEL), lambda bi, ti: (bi * n_t + ti, 0))
    return pl.pallas_call(
        _merge_kernel,
        grid=(b, n_t),
        in_specs=[col(1), col(2), col(3), col(4), col(5), col(6), halo(2), halo(3), rowblk, rowblk,
                  const(cw), const(cb), const(lg), const(lb), const(wcp), const(wnp), const(wo),
                  const(gf)],
        out_specs=rowblk,
        out_shape=jax.ShapeDtypeStruct((b * s, D_MODEL), F32),
        scratch_shapes=[pltpu.VMEM((ts + CONV_HALO, D_MODEL), F32)],
        compiler_params=pltpu.CompilerParams(
            dimension_semantics=("parallel", "arbitrary"), vmem_limit_bytes=VMEM_LIMIT),
        name="conv_merge",
    )(proj_main, proj_main, proj_main, proj_main, proj_main, proj_main, proj_main, proj_main,
      o_nsa, x2, cw, cb, lg, lb, wcp, wnp, wo, gf)


def _t5_bucket_np(rel):
    rel = np.maximum(rel, 0)
    max_exact = REL_BUCKETS // 2
    relf = np.maximum(rel, 1).astype(np.float32)
    large = max_exact + (np.log(relf / np.float32(max_exact))
                         / np.float32(np.log(REL_MAX_DIST / max_exact))
                         * np.float32(REL_BUCKETS - max_exact)).astype(np.int32)
    large = np.minimum(large, REL_BUCKETS - 1)
    return np.where(rel < max_exact, rel, large)


def _pair_head_index():
    g = np.arange(N_GROUPS)[:, None, None]
    v = np.arange(2)[None, :, None]
    p = np.arange(PAIRS_PER_GROUP)[None, None, :]
    return g * HEADS_PER_GROUP + 2 * p + v


def _bias_tables(rel_bias, s):
    head = _pair_head_index()
    r = np.arange(Q_TILE)[:, None]
    c = np.arange(LANES)[None, :]
    rel_w = np.stack([dd * Q_TILE + r - c for dd in range(WIN_TILES)])
    ok_w = (rel_w >= 0) & (rel_w < WINDOW)
    tw = rel_bias[_t5_bucket_np(rel_w)[None, None, :, None], head[:, :, None, :, None, None]]
    tw = jnp.where(ok_w[None, None, :, None], tw, NEG)
    t = np.arange(s)[:, None]
    rel_c = t - (c * CMP_STRIDE + CMP_BLOCK - 1)
    ok_c = (rel_c >= 0) & (c < (s - CMP_BLOCK) // CMP_STRIDE + 1)
    bc = rel_bias[_t5_bucket_np(rel_c)[None, None, None], head[:, :, :, None, None]]
    bc = jnp.where(ok_c[None, None, None], bc, NEG)
    return tw.astype(F32), bc.astype(F32)


def _selection_constants(s):
    n_cmp = (s - CMP_BLOCK) // CMP_STRIDE + 1
    n_sel = s // SEL_BLOCK
    cs = np.arange(LANES) * CMP_STRIDE
    ss = np.arange(n_sel) * SEL_BLOCK
    ovt = ((cs[None, :] <= ss[:, None] + SEL_BLOCK - 1) & (cs[None, :] + CMP_BLOCK - 1 >= ss[:, None])
           & (np.arange(LANES)[None, :] < n_cmp))
    expand = (np.arange(s)[None, :] // SEL_BLOCK) == np.arange(LANES)[:, None]
    return jnp.asarray(ovt, BF16), jnp.asarray(expand, BF16)


def _permute_w_in(w):
    sizes = [NSA_WIDTH] + [KV_WIDTH] * 6 + [3 * N_HEADS, NSA_WIDTH, 2 * D_MODEL, D_MODEL, 2 * D_MODEL]
    offs = np.concatenate([[0], np.cumsum(sizes)])
    seg = lambda i: w[:, offs[i]:offs[i + 1]]
    pad = jnp.zeros((D_MODEL, COL_TILE - 6 * KV_WIDTH - 3 * N_HEADS), w.dtype)
    cols = [seg(i) for i in range(1, 8)] + [pad, seg(0), seg(8), seg(9), seg(10), seg(11)]
    return jnp.concatenate(cols, axis=1).astype(BF16)


def _compress_weights(pos, w1, w2):
    half = CMP_BLOCK // 2
    eye = jnp.eye(N_GROUPS, dtype=F32)
    w1r = w1.reshape(CMP_BLOCK, HEAD_DIM, CMP_HIDDEN)
    blk = lambda part: jnp.einsum('idn,gh->igdhn', part, eye).reshape(
        half * KV_WIDTH, N_GROUPS * CMP_HIDDEN)
    w1d = jnp.concatenate([blk(w1r[:half]), blk(w1r[half:])], axis=1).astype(BF16)
    w2d = jnp.einsum('nd,gh->gnhd', w2, eye).reshape(N_GROUPS * CMP_HIDDEN, KV_WIDTH).astype(BF16)
    tilepos = lambda part: jnp.broadcast_to(part[:, None, :], (half, N_GROUPS, HEAD_DIM)).reshape(1, -1)
    posd = jnp.concatenate([tilepos(pos[:half]), tilepos(pos[half:])], axis=0).astype(F32)
    return posd, w1d, w2d


def kernel(x, norm_in_g, w_in, pos_ck, w_ck1, w_ck2, pos_cv, w_cv1, w_cv2, rel_bias, conv_w, conv_b,
           conv_ln_g, conv_ln_b, w_conv_proj, w_nsa_proj, w_out, norm_f_g):
    b, s, d = x.shape
    assert d == D_MODEL and w_in.shape[0] == 1, "single-layer block with D_MODEL=1024"
    assert s % Q_TILE == 0 and s // CMP_STRIDE <= LANES and s >= WINDOW
    m = b * s
    x2 = x.reshape(m, d)
    row = lambda a: a.reshape(1, -1).astype(F32)

    proj_kv, proj_main = _input_projection(x2, row(norm_in_g[0]), _permute_w_in(w_in[0]), min(1024, m))

    chunks = s // CMP_STRIDE
    kf = proj_kv[:, 0:KV_WIDTH].reshape(b, chunks, CMP_STRIDE * KV_WIDTH)
    vf = proj_kv[:, KV_WIDTH:2 * KV_WIDTH].reshape(b, chunks, CMP_STRIDE * KV_WIDTH)
    pk, w1k, w2k = _compress_weights(pos_ck[0], w_ck1[0], w_ck2[0])
    pv, w1v, w2v = _compress_weights(pos_cv[0], w_cv1[0], w_cv2[0])
    kcmp, vcmp = _compress(kf, vf, pk, pv, w1k, w1v, w2k, w2v)
    if chunks < LANES:
        padrows = ((0, 0), (0, LANES - chunks), (0, 0))
        kcmp, vcmp = jnp.pad(kcmp, padrows), jnp.pad(vcmp, padrows)

    bias_w, bias_c = _bias_tables(rel_bias, s)
    ovt, expand = _selection_constants(s)
    o_nsa = _attention(proj_main, proj_kv, kcmp, vcmp, bias_c, bias_w, ovt, expand, b, s,
                       min(N_SELECT, s // SEL_BLOCK))

    cw = jnp.pad(conv_w[0].astype(F32), ((0, CONV_HALO - CONV_KERNEL), (0, 0)))
    out = _merge(proj_main, o_nsa, x2, cw, row(conv_b[0]), row(conv_ln_g[0]), row(conv_ln_b[0]),
                 w_conv_proj[0].astype(BF16), w_nsa_proj[0].astype(BF16), w_out[0].astype(BF16),
                 row(norm_f_g), b, s, 256)
    return out.reshape(b, s, d)
```

```python
import functools

import numpy as np
import jax
import jax.numpy as jnp
from jax import lax
from jax.experimental import pallas as pl
from jax.experimental.pallas import tpu as pltpu

F32 = jnp.float32
BF16 = jnp.bfloat16

D_MODEL = 1024
N_HEADS = 16
N_GROUPS = 2
HEADS_PER_GROUP = N_HEADS // N_GROUPS
PAIRS_PER_GROUP = HEADS_PER_GROUP // 2
HEAD_DIM = 64
NSA_WIDTH = N_HEADS * HEAD_DIM
KV_WIDTH = N_GROUPS * HEAD_DIM
CMP_BLOCK = 32
CMP_STRIDE = 16
CMP_HIDDEN = 256
SEL_BLOCK = 64
N_SELECT = 8
WINDOW = 512
Q_TILE = 128
CONV_KERNEL = 31
CONV_HALO = 32
REL_BUCKETS = 32
REL_MAX_DIST = 128
EPS = 1e-6
NEG = -1e30
FORCE_SCORE = 1e6
LANES = 128
COL_TILE = 1024
N_COL_TILES = 8
WIN_TILES = WINDOW // Q_TILE + 1
VMEM_LIMIT = 56 * 1024 * 1024


def _dot(a, b):
    return jnp.dot(a, b, preferred_element_type=F32)


def _dot_nt(a, b):
    return lax.dot_general(a, b, (((1,), (1,)), ((), ())), preferred_element_type=F32)


def _sigmoid(x):
    return 1.0 / (1.0 + jnp.exp(-x))


def _silu(x):
    return x * _sigmoid(x)


def _proj_kernel(x_ref, g_ref, w_ref, kv_ref, main_ref, h_ref):
    j = pl.program_id(1)

    @pl.when(j == 0)
    def _():
        x = x_ref[...]
        ms = jnp.mean(x * x, axis=-1, keepdims=True)
        h_ref[...] = ((x * lax.rsqrt(ms + EPS)) * g_ref[...]).astype(BF16)

    acc = _dot(h_ref[...], w_ref[...])

    @pl.when(j == 0)
    def _():
        kv_ref[...] = acc

    @pl.when(j > 0)
    def _():
        main_ref[...] = acc.astype(BF16)


def _input_projection(x2, g, w_perm, tm):
    m = x2.shape[0]
    return pl.pallas_call(
        _proj_kernel,
        grid=(m // tm, N_COL_TILES),
        in_specs=[
            pl.BlockSpec((tm, D_MODEL), lambda i, j: (i, 0)),
            pl.BlockSpec((1, D_MODEL), lambda i, j: (0, 0)),
            pl.BlockSpec((D_MODEL, COL_TILE), lambda i, j: (0, j)),
        ],
        out_specs=[
            pl.BlockSpec((tm, COL_TILE), lambda i, j: (i, 0)),
            pl.BlockSpec((tm, COL_TILE), lambda i, j: (i, jnp.maximum(j - 1, 0))),
        ],
        out_shape=[
            jax.ShapeDtypeStruct((m, COL_TILE), F32),
            jax.ShapeDtypeStruct((m, (N_COL_TILES - 1) * COL_TILE), BF16),
        ],
        scratch_shapes=[pltpu.VMEM((tm, D_MODEL), BF16)],
        compiler_params=pltpu.CompilerParams(
            dimension_semantics=("parallel", "arbitrary"), vmem_limit_bytes=VMEM_LIMIT),
        name="input_projection",
    )(x2, g, w_perm)


def _compress_kernel(kf_ref, vf_ref, pk_ref, pv_ref, w1k_ref, w1v_ref, w2k_ref, w2v_ref,
                     kc_ref, vc_ref):
    def one(f_ref, pos_ref, w1_ref, w2_ref, o_ref):
        f = f_ref[0]
        n = f.shape[0]
        hw = N_GROUPS * CMP_HIDDEN
        first = _dot((f + pos_ref[0:1, :]).astype(BF16), w1_ref[:, 0:hw])
        second = _dot((f + pos_ref[1:2, :]).astype(BF16), w1_ref[:, hw:2 * hw])
        hid = first + pltpu.roll(second, n - 1, axis=0)
        o_ref[0] = _dot(_silu(hid).astype(BF16), w2_ref[...])

    one(kf_ref, pk_ref, w1k_ref, w2k_ref, kc_ref)
    one(vf_ref, pv_ref, w1v_ref, w2v_ref, vc_ref)


def _compress(kf, vf, pk, pv, w1k, w1v, w2k, w2v):
    b, n, width = kf.shape
    const = lambda shape: pl.BlockSpec(shape, lambda i: (0,) * len(shape))
    row = pl.BlockSpec((1, n, width), lambda i: (i, 0, 0))
    out = pl.BlockSpec((1, n, LANES), lambda i: (i, 0, 0))
    return pl.pallas_call(
        _compress_kernel,
        grid=(b,),
        in_specs=[row, row, const(pk.shape), const(pv.shape), const(w1k.shape), const(w1v.shape),
                  const(w2k.shape), const(w2v.shape)],
        out_specs=[out, out],
        out_shape=[jax.ShapeDtypeStruct((b, n, LANES), F32)] * 2,
        compiler_params=pltpu.CompilerParams(
            dimension_semantics=("parallel",), vmem_limit_bytes=VMEM_LIMIT),
        name="nsa_compress",
    )(kf, vf, pk, pv, w1k, w1v, w2k, w2v)


def _attn_kernel(q_ref, ks_ref, vs_ref, kw_ref, vw_ref, gt_ref, kc_ref, vc_ref, bc_ref, tw_ref,
                 ovt_ref, exp_ref, o_ref,
                 ksv, vsv, kwv, vwv, kcv, vcv, s_scr, selm, obuf, *, n_top):
    qi = pl.program_id(1)
    n_kt = ksv.shape[1]
    n_sel = ovt_ref.shape[0]
    rows = PAIRS_PER_GROUP * Q_TILE

    @pl.when(qi == 0)
    def _prepare_kv():
        def variants(k, dst):
            lo = lax.broadcasted_iota(jnp.int32, k.shape, 1) < HEAD_DIM
            kr = pltpu.roll(k, HEAD_DIM, axis=1)
            z = jnp.zeros_like(k)
            tiles = k.shape[0] // Q_TILE
            for idx, a in enumerate((jnp.where(lo, k, z), jnp.where(lo, z, kr),
                                     jnp.where(lo, kr, z), jnp.where(lo, z, k))):
                dst[idx] = a.astype(BF16).reshape(tiles, Q_TILE, LANES)

        variants(ks_ref[...], ksv)
        variants(vs_ref[...], vsv)
        variants(kw_ref[...], kwv)
        variants(vw_ref[...], vwv)
        variants(kc_ref[0], kcv)
        variants(vc_ref[0], vcv)

    lane_lo = lax.broadcasted_iota(jnp.int32, (Q_TILE, LANES), 1) < HEAD_DIM
    qt = q_ref[...]
    gsig = _sigmoid(gt_ref[...])

    for g in range(N_GROUPS):
        q4 = jnp.concatenate(
            [qt[:, (g * PAIRS_PER_GROUP + p) * LANES:(g * PAIRS_PER_GROUP + p + 1) * LANES]
             for p in range(PAIRS_PER_GROUP)], axis=0)
        q4 = (q4.astype(F32) * (HEAD_DIM ** -0.5)).astype(BF16)

        psum = jnp.zeros((Q_TILE, LANES), F32)
        o_c = jnp.zeros((rows, LANES), F32)
        for v in range(2):
            gv = g * 2 + v
            s = _dot_nt(q4, kcv[gv, 0]) + bc_ref[g, v].reshape(rows, LANES)
            valid = s > 0.5 * NEG
            m = jnp.max(s, axis=1, keepdims=True)
            e = jnp.where(valid, jnp.exp(s - m), 0.0)
            p = e / jnp.maximum(jnp.sum(e, axis=1, keepdims=True), 1e-30)
            psum = psum + jnp.sum(p.reshape(PAIRS_PER_GROUP, Q_TILE, LANES), axis=0)
            o_c = o_c + _dot(p.astype(BF16), vcv[gv, 0])
        obuf[0] = o_c

        p_hi = psum.astype(BF16)
        r1 = psum - p_hi.astype(F32)
        p_mid = r1.astype(BF16)
        p_lo = (r1 - p_mid.astype(F32)).astype(BF16)
        ovt = ovt_ref[...]
        imp_t = _dot_nt(ovt, p_hi) + _dot_nt(ovt, p_mid) + _dot_nt(ovt, p_lo)

        j_idx = lax.broadcasted_iota(jnp.int32, (n_sel, Q_TILE), 0)
        r_idx = lax.broadcasted_iota(jnp.int32, (n_sel, Q_TILE), 1)
        blk_t = qi * (Q_TILE // SEL_BLOCK) + r_idx // SEL_BLOCK
        valid_blk = j_idx <= blk_t
        forced = (j_idx == 0) | (j_idx == blk_t) | (j_idx == blk_t - 1)
        prio = jnp.where(forced, FORCE_SCORE, imp_t)
        prio = jnp.where(valid_blk, prio, -FORCE_SCORE)
        rank = jnp.zeros((n_sel, Q_TILE), F32)
        for jj in range(n_sel):
            row = prio[jj:jj + 1, :]
            beats = (row > prio) | ((row == prio) & (j_idx > jj))
            rank = rank + jnp.where(beats, 1.0, 0.0)
        sel_t = jnp.where((rank < n_top) & valid_blk, 1.0, 0.0)
        if n_sel < LANES:
            sel_t = jnp.concatenate([sel_t, jnp.zeros((LANES - n_sel, Q_TILE), F32)], axis=0)
        sel = sel_t.T.astype(BF16)
        sel_full = _dot(sel, exp_ref[...])
        for kt in range(n_kt):
            selm[kt] = sel_full[:, kt * Q_TILE:(kt + 1) * Q_TILE]

        def two_pass(n_tiles, tile_of, kvar, vvar, masked):
            o = jnp.zeros((rows, LANES), F32)
            for v in range(2):
                gv = g * 2 + v

                def pass1(i, mx):
                    kt, dd = tile_of(i)
                    s = _dot_nt(q4, kvar[gv, kt]) + tw_ref[g, v, dd].reshape(rows, LANES)
                    if masked:
                        keep = selm[kt] > 0.5
                        s = jnp.where(keep[None], s.reshape(PAIRS_PER_GROUP, Q_TILE, LANES),
                                      NEG).reshape(rows, LANES)
                    s_scr[i] = s
                    return jnp.maximum(mx, s)

                mx = lax.fori_loop(0, n_tiles, pass1, jnp.full((rows, LANES), NEG, F32))
                m = jnp.max(mx, axis=1, keepdims=True)

                def pass2(i, carry):
                    l, acc = carry
                    kt, _ = tile_of(i)
                    e = jnp.exp(s_scr[i] - m)
                    return l + e, acc + _dot(e.astype(BF16), vvar[gv, kt])

                l, acc = lax.fori_loop(0, n_tiles, pass2,
                                       (jnp.zeros((rows, LANES), F32), jnp.zeros((rows, LANES), F32)))
                o = o + acc / jnp.maximum(jnp.sum(l, axis=1, keepdims=True), 1e-30)
            return o

        obuf[1] = two_pass(qi + 1, lambda i: (i, jnp.minimum(qi - i, 2)), ksv, vsv, True)
        obuf[2] = two_pass(jnp.minimum(qi, WIN_TILES - 1) + 1, lambda i: (qi - i, i), kwv, vwv, False)

        for p in range(PAIRS_PER_GROUP):
            tot = jnp.zeros((Q_TILE, LANES), F32)
            for br in range(3):
                c = br * N_HEADS + g * HEADS_PER_GROUP + 2 * p
                gate = jnp.where(lane_lo, gsig[:, c:c + 1], gsig[:, c + 1:c + 2])
                tot = tot + gate * obuf[br, p * Q_TILE:(p + 1) * Q_TILE, :]
            col = (g * PAIRS_PER_GROUP + p) * LANES
            o_ref[:, col:col + LANES] = tot.astype(BF16)


def _attention(proj_main, proj_kv, kcmp, vcmp, bias_c, bias_w, ovt, expand, b, s, n_top):
    n_q = s // Q_TILE
    n_sel = s // SEL_BLOCK
    rows = PAIRS_PER_GROUP * Q_TILE
    kv_col = lambda c: pl.BlockSpec((s, LANES), lambda bi, qi: (bi, c))
    cmp_spec = pl.BlockSpec((1, LANES, LANES), lambda bi, qi: (bi, 0, 0))
    return pl.pallas_call(
        functools.partial(_attn_kernel, n_top=n_top),
        grid=(b, n_q),
        in_specs=[
            pl.BlockSpec((Q_TILE, NSA_WIDTH), lambda bi, qi: (bi * n_q + qi, 0)),
            kv_col(2), kv_col(3), kv_col(4), kv_col(5),
            pl.BlockSpec((Q_TILE, LANES), lambda bi, qi: (bi * n_q + qi, 6)),
            cmp_spec, cmp_spec,
            pl.BlockSpec((N_GROUPS, 2, PAIRS_PER_GROUP, Q_TILE, LANES),
                         lambda bi, qi: (0, 0, 0, qi, 0)),
            pl.BlockSpec(bias_w.shape, lambda bi, qi: (0,) * 6),
            pl.BlockSpec(ovt.shape, lambda bi, qi: (0, 0)),
            pl.BlockSpec(expand.shape, lambda bi, qi: (0, 0)),
        ],
        out_specs=pl.BlockSpec((Q_TILE, NSA_WIDTH), lambda bi, qi: (bi * n_q + qi, 0)),
        out_shape=jax.ShapeDtypeStruct((b * s, NSA_WIDTH), BF16),
        scratch_shapes=[
            pltpu.VMEM((2 * N_GROUPS, n_q, Q_TILE, LANES), BF16),
            pltpu.VMEM((2 * N_GROUPS, n_q, Q_TILE, LANES), BF16),
            pltpu.VMEM((2 * N_GROUPS, n_q, Q_TILE, LANES), BF16),
            pltpu.VMEM((2 * N_GROUPS, n_q, Q_TILE, LANES), BF16),
            pltpu.VMEM((2 * N_GROUPS, 1, LANES, LANES), BF16),
            pltpu.VMEM((2 * N_GROUPS, 1, LANES, LANES), BF16),
            pltpu.VMEM((n_q, rows, LANES), F32),
            pltpu.VMEM((n_q, Q_TILE, LANES), F32),
            pltpu.VMEM((3, rows, LANES), F32),
        ],
        compiler_params=pltpu.CompilerParams(
            dimension_semantics=("parallel", "arbitrary"), vmem_limit_bytes=VMEM_LIMIT),
        name="nsa_attention",
    )(proj_main, proj_kv, proj_kv, proj_kv, proj_kv, proj_kv, kcmp, vcmp, bias_c, bias_w,
      ovt, expand)


def _merge_kernel(zn_ref, a_ref, b_ref, zc_ref, gc_ref, gn_ref, ah_ref, bh_ref, on_ref, x_ref,
                  cw_ref, cb_ref, lg_ref, lb_ref, wcp_ref, wnp_ref, wo_ref, gf_ref, out_ref, uext):
    i = pl.program_id(1)
    ts = a_ref.shape[0]
    f = lambda r: r[...].astype(F32)

    u_halo = f(ah_ref) * _sigmoid(f(bh_ref))
    uext[0:CONV_HALO, :] = jnp.where(i > 0, u_halo, 0.0)
    uext[CONV_HALO:, :] = f(a_ref) * _sigmoid(f(b_ref))

    c = jnp.broadcast_to(cb_ref[...], (ts, D_MODEL))
    for j in range(CONV_KERNEL):
        off = CONV_HALO - (CONV_KERNEL - 1) + j
        c = c + cw_ref[j:j + 1, :] * uext[off:off + ts, :]

    mu = jnp.mean(c, axis=-1, keepdims=True)
    cc = c - mu
    var = jnp.mean(cc * cc, axis=-1, keepdims=True)
    y = (cc * lax.rsqrt(var + EPS)) * lg_ref[...] + lb_ref[...]
    conv_act = _silu(y) * _silu(f(zc_ref))
    y_conv = _dot(conv_act.astype(BF16), wcp_ref[...])

    nsa_act = f(on_ref) * _silu(f(zn_ref))
    y_nsa = _dot(nsa_act.astype(BF16), wnp_ref[...])

    merged = _sigmoid(f(gc_ref)) * y_conv + _sigmoid(f(gn_ref)) * y_nsa
    xo = x_ref[...] + _dot(merged.astype(BF16), wo_ref[...])
    ms = jnp.mean(xo * xo, axis=-1, keepdims=True)
    out_ref[...] = (xo * lax.rsqrt(ms + EPS)) * gf_ref[...]


def _merge(proj_main, o_nsa, x2, cw, cb, lg, lb, wcp, wnp, wo, gf, b, s, ts):
    n_t = s // ts
    halo_per_tile = ts // CONV_HALO
    col = lambda c: pl.BlockSpec((ts, COL_TILE), lambda bi, ti: (bi * n_t + ti, c))
    halo = lambda c: pl.BlockSpec(
        (CONV_HALO, COL_TILE),
        lambda bi, ti: (jnp.maximum((bi * n_t + ti) * halo_per_tile - 1, 0), c))
    const = lambda a: pl.BlockSpec(a.shape, lambda bi, ti: (0, 0))
    rowblk = pl.BlockSpec((ts, D_MODEL), lambda bi, ti: (bi * n_t + ti, 0))
    return pl.pallas_call(
        _merge_kernel,
        grid=(b, n_t),
        in_specs=[col(1), col(2), col(3), col(4), col(5), col(6), halo(2), halo(3), rowblk, rowblk,
                  const(cw), const(cb), const(lg), const(lb), const(wcp), const(wnp), const(wo),
                  const(gf)],
        out_specs=rowblk,
        out_shape=jax.ShapeDtypeStruct((b * s, D_MODEL), F32),
        scratch_shapes=[pltpu.VMEM((ts + CONV_HALO, D_MODEL), F32)],
        compiler_params=pltpu.CompilerParams(
            dimension_semantics=("parallel", "arbitrary"), vmem_limit_bytes=VMEM_LIMIT),
        name="conv_merge",
    )(proj_main, proj_main, proj_main, proj_main, proj_main, proj_main, proj_main, proj_main,
      o_nsa, x2, cw, cb, lg, lb, wcp, wnp, wo, gf)


def _t5_bucket_np(rel):
    rel = np.maximum(rel, 0)
    max_exact = REL_BUCKETS // 2
    relf = np.maximum(rel, 1).astype(np.float32)
    large = max_exact + (np.log(relf / np.float32(max_exact))
                         / np.float32(np.log(REL_MAX_DIST / max_exact))
                         * np.float32(REL_BUCKETS - max_exact)).astype(np.int32)
    large = np.minimum(large, REL_BUCKETS - 1)
    return np.where(rel < max_exact, rel, large)


def _pair_head_index():
    g = np.arange(N_GROUPS)[:, None, None]
    v = np.arange(2)[None, :, None]
    p = np.arange(PAIRS_PER_GROUP)[None, None, :]
    return g * HEADS_PER_GROUP + 2 * p + v


def _bias_lookup(rel_bias, rel):
    bucket = _t5_bucket_np(rel).reshape(-1)
    onehot = (jnp.arange(REL_BUCKETS)[:, None] == jnp.asarray(bucket)[None, :]).astype(F32)
    vals = jnp.dot(rel_bias.astype(F32).T, onehot, precision=lax.Precision.HIGHEST)
    vals = vals.reshape((N_HEADS,) + rel.shape)
    head = _pair_head_index()
    return jnp.stack([vals[h] for h in head.reshape(-1)]).reshape(head.shape + rel.shape)


def _bias_tables(rel_bias, s):
    r = np.arange(Q_TILE)[:, None]
    c = np.arange(LANES)[None, :]
    rel_w = np.stack([dd * Q_TILE + r - c for dd in range(WIN_TILES)])
    ok_w = (rel_w >= 0) & (rel_w < WINDOW)
    tw = jnp.where(ok_w, _bias_lookup(rel_bias, rel_w), NEG)
    tw = jnp.transpose(tw, (0, 1, 3, 2, 4, 5))
    t = np.arange(s)[:, None]
    rel_c = t - (c * CMP_STRIDE + CMP_BLOCK - 1)
    ok_c = (rel_c >= 0) & (c < (s - CMP_BLOCK) // CMP_STRIDE + 1)
    bc = jnp.where(ok_c, _bias_lookup(rel_bias, rel_c), NEG)
    return tw, bc


def _selection_constants(s):
    n_cmp = (s - CMP_BLOCK) // CMP_STRIDE + 1
    n_sel = s // SEL_BLOCK
    cs = np.arange(LANES) * CMP_STRIDE
    ss = np.arange(n_sel) * SEL_BLOCK
    ovt = ((cs[None, :] <= ss[:, None] + SEL_BLOCK - 1) & (cs[None, :] + CMP_BLOCK - 1 >= ss[:, None])
           & (np.arange(LANES)[None, :] < n_cmp))
    expand = (np.arange(s)[None, :] // SEL_BLOCK) == np.arange(LANES)[:, None]
    return jnp.asarray(ovt, BF16), jnp.asarray(expand, BF16)


def _permute_w_in(w):
    sizes = [NSA_WIDTH] + [KV_WIDTH] * 6 + [3 * N_HEADS, NSA_WIDTH, 2 * D_MODEL, D_MODEL, 2 * D_MODEL]
    offs = np.concatenate([[0], np.cumsum(sizes)])
    seg = lambda i: w[:, offs[i]:offs[i + 1]]
    pad = jnp.zeros((D_MODEL, COL_TILE - 6 * KV_WIDTH - 3 * N_HEADS), w.dtype)
    cols = [seg(i) for i in range(1, 8)] + [pad, seg(0), seg(8), seg(9), seg(10), seg(11)]
    return jnp.concatenate(cols, axis=1).astype(BF16)


def _compress_weights(pos, w1, w2):
    half = CMP_BLOCK // 2
    eye = jnp.eye(N_GROUPS, dtype=F32)
    w1r = w1.reshape(CMP_BLOCK, HEAD_DIM, CMP_HIDDEN)
    blk = lambda part: jnp.einsum('idn,gh->igdhn', part, eye).reshape(
        half * KV_WIDTH, N_GROUPS * CMP_HIDDEN)
    w1d = jnp.concatenate([blk(w1r[:half]), blk(w1r[half:])], axis=1).astype(BF16)
    w2d = jnp.einsum('nd,gh->gnhd', w2, eye).reshape(N_GROUPS * CMP_HIDDEN, KV_WIDTH).astype(BF16)
    tilepos = lambda part: jnp.broadcast_to(part[:, None, :], (half, N_GROUPS, HEAD_DIM)).reshape(1, -1)
    posd = jnp.concatenate([tilepos(pos[:half]), tilepos(pos[half:])], axis=0).astype(F32)
    return posd, w1d, w2d


def kernel(x, norm_in_g, w_in, pos_ck, w_ck1, w_ck2, pos_cv, w_cv1, w_cv2, rel_bias, conv_w, conv_b,
           conv_ln_g, conv_ln_b, w_conv_proj, w_nsa_proj, w_out, norm_f_g):
    b, s, d = x.shape
    assert d == D_MODEL and w_in.shape[0] == 1, "single-layer block with D_MODEL=1024"
    assert s % Q_TILE == 0 and s // CMP_STRIDE <= LANES and s >= WINDOW
    m = b * s
    x2 = x.reshape(m, d)
    row = lambda a: a.reshape(1, -1).astype(F32)

    proj_kv, proj_main = _input_projection(x2, row(norm_in_g[0]), _permute_w_in(w_in[0]), min(1024, m))

    chunks = s // CMP_STRIDE
    kf = proj_kv[:, 0:KV_WIDTH].reshape(b, chunks, CMP_STRIDE * KV_WIDTH)
    vf = proj_kv[:, KV_WIDTH:2 * KV_WIDTH].reshape(b, chunks, CMP_STRIDE * KV_WIDTH)
    pk, w1k, w2k = _compress_weights(pos_ck[0], w_ck1[0], w_ck2[0])
    pv, w1v, w2v = _compress_weights(pos_cv[0], w_cv1[0], w_cv2[0])
    kcmp, vcmp = _compress(kf, vf, pk, pv, w1k, w1v, w2k, w2v)
    if chunks < LANES:
        padrows = ((0, 0), (0, LANES - chunks), (0, 0))
        kcmp, vcmp = jnp.pad(kcmp, padrows), jnp.pad(vcmp, padrows)

    bias_w, bias_c = _bias_tables(rel_bias, s)
    ovt, expand = _selection_constants(s)
    o_nsa = _attention(proj_main, proj_kv, kcmp, vcmp, bias_c, bias_w, ovt, expand, b, s,
                       min(N_SELECT, s // SEL_BLOCK))

    cw = jnp.pad(conv_w[0].astype(F32), ((0, CONV_HALO - CONV_KERNEL), (0, 0)))
    out = _merge(proj_main, o_nsa, x2, cw, row(conv_b[0]), row(conv_ln_g[0]), row(conv_ln_b[0]),
                 w_conv_proj[0].astype(BF16), w_nsa_proj[0].astype(BF16), w_out[0].astype(BF16),
                 row(norm_f_g), b, s, 256)
    return out.reshape(b, s, d)
```

```python
import functools

import numpy as np
import jax
import jax.numpy as jnp
from jax import lax
from jax.experimental import pallas as pl
from jax.experimental.pallas import tpu as pltpu

F32 = jnp.float32
BF16 = jnp.bfloat16

D_MODEL = 1024
N_HEADS = 16
N_GROUPS = 2
HEADS_PER_GROUP = N_HEADS // N_GROUPS
PAIRS_PER_GROUP = HEADS_PER_GROUP // 2
HEAD_DIM = 64
NSA_WIDTH = N_HEADS * HEAD_DIM
KV_WIDTH = N_GROUPS * HEAD_DIM
CMP_BLOCK = 32
CMP_STRIDE = 16
CMP_HIDDEN = 256
SEL_BLOCK = 64
N_SELECT = 8
WINDOW = 512
Q_TILE = 128
CONV_KERNEL = 31
CONV_HALO = 32
REL_BUCKETS = 32
REL_MAX_DIST = 128
EPS = 1e-6
NEG = -1e30
FORCE_SCORE = 1e6
LANES = 128
COL_TILE = 1024
N_COL_TILES = 8
WIN_TILES = WINDOW // Q_TILE + 1
VMEM_LIMIT = 56 * 1024 * 1024


def _dot(a, b):
    return jnp.dot(a, b, preferred_element_type=F32)


def _dot_nt(a, b):
    return lax.dot_general(a, b, (((1,), (1,)), ((), ())), preferred_element_type=F32)


def _sigmoid(x):
    return 1.0 / (1.0 + jnp.exp(-x))


def _silu(x):
    return x * _sigmoid(x)


def _proj_kernel(x_ref, g_ref, w_ref, kv_ref, main_ref, h_ref):
    j = pl.program_id(1)

    @pl.when(j == 0)
    def _():
        x = x_ref[...]
        ms = jnp.mean(x * x, axis=-1, keepdims=True)
        h_ref[...] = ((x * lax.rsqrt(ms + EPS)) * g_ref[...]).astype(BF16)

    acc = _dot(h_ref[...], w_ref[...])

    @pl.when(j == 0)
    def _():
        kv_ref[...] = acc

    @pl.when(j > 0)
    def _():
        main_ref[...] = acc.astype(BF16)


def _input_projection(x2, g, w_perm, tm):
    m = x2.shape[0]
    return pl.pallas_call(
        _proj_kernel,
        grid=(m // tm, N_COL_TILES),
        in_specs=[
            pl.BlockSpec((tm, D_MODEL), lambda i, j: (i, 0)),
            pl.BlockSpec((1, D_MODEL), lambda i, j: (0, 0)),
            pl.BlockSpec((D_MODEL, COL_TILE), lambda i, j: (0, j)),
        ],
        out_specs=[
            pl.BlockSpec((tm, COL_TILE), lambda i, j: (i, 0)),
            pl.BlockSpec((tm, COL_TILE), lambda i, j: (i, jnp.maximum(j - 1, 0))),
        ],
        out_shape=[
            jax.ShapeDtypeStruct((m, COL_TILE), F32),
            jax.ShapeDtypeStruct((m, (N_COL_TILES - 1) * COL_TILE), BF16),
        ],
        scratch_shapes=[pltpu.VMEM((tm, D_MODEL), BF16)],
        compiler_params=pltpu.CompilerParams(
            dimension_semantics=("parallel", "arbitrary"), vmem_limit_bytes=VMEM_LIMIT),
        name="input_projection",
    )(x2, g, w_perm)


def _compress_kernel(kf_ref, vf_ref, pk_ref, pv_ref, w1k_ref, w1v_ref, w2k_ref, w2v_ref,
                     kc_ref, vc_ref):
    def one(f_ref, pos_ref, w1_ref, w2_ref, o_ref):
        f = f_ref[0]
        n = f.shape[0]
        hw = N_GROUPS * CMP_HIDDEN
        first = _dot((f + pos_ref[0:1, :]).astype(BF16), w1_ref[:, 0:hw])
        second = _dot((f + pos_ref[1:2, :]).astype(BF16), w1_ref[:, hw:2 * hw])
        hid = first + pltpu.roll(second, n - 1, axis=0)
        o_ref[0] = _dot(_silu(hid).astype(BF16), w2_ref[...])

    one(kf_ref, pk_ref, w1k_ref, w2k_ref, kc_ref)
    one(vf_ref, pv_ref, w1v_ref, w2v_ref, vc_ref)


def _compress(kf, vf, pk, pv, w1k, w1v, w2k, w2v):
    b, n, width = kf.shape
    const = lambda shape: pl.BlockSpec(shape, lambda i: (0,) * len(shape))
    row = pl.BlockSpec((1, n, width), lambda i: (i, 0, 0))
    out = pl.BlockSpec((1, n, LANES), lambda i: (i, 0, 0))
    return pl.pallas_call(
        _compress_kernel,
        grid=(b,),
        in_specs=[row, row, const(pk.shape), const(pv.shape), const(w1k.shape), const(w1v.shape),
                  const(w2k.shape), const(w2v.shape)],
        out_specs=[out, out],
        out_shape=[jax.ShapeDtypeStruct((b, n, LANES), F32)] * 2,
        compiler_params=pltpu.CompilerParams(
            dimension_semantics=("parallel",), vmem_limit_bytes=VMEM_LIMIT),
        name="nsa_compress",
    )(kf, vf, pk, pv, w1k, w1v, w2k, w2v)


def _attn_kernel(q_ref, ks_ref, vs_ref, kw_ref, vw_ref, gt_ref, kc_ref, vc_ref, bc_ref, tw_ref,
                 ovt_ref, o_ref,
                 ksv, kwv, kcv, vst, vwt, vct, rhs, s_sel, s_win, acc, obuf, *, n_top):
    qi = pl.program_id(1)
    n_kt = kwv.shape[1]
    n_sel = ovt_ref.shape[0]
    cols = PAIRS_PER_GROUP * Q_TILE
    slabs = Q_TILE // 8

    @pl.when(qi == 0)
    def _prepare_kv():
        def variants(k):
            lo = lax.broadcasted_iota(jnp.int32, k.shape, 1) < HEAD_DIM
            kr = pltpu.roll(k, HEAD_DIM, axis=1)
            z = jnp.zeros_like(k)
            return [a.astype(BF16) for a in (jnp.where(lo, k, z), jnp.where(lo, z, kr),
                                             jnp.where(lo, kr, z), jnp.where(lo, z, k))]

        shp = (n_kt, Q_TILE, LANES)
        key_blk = (lax.broadcasted_iota(jnp.int32, shp, 0) * (Q_TILE // SEL_BLOCK)
                   + lax.broadcasted_iota(jnp.int32, shp, 1) // SEL_BLOCK)
        blk_onehot = jnp.where(key_blk == lax.broadcasted_iota(jnp.int32, shp, 2), 1.0, 0.0)
        for idx, a in enumerate(variants(ks_ref[...])):
            ksv[idx, :, :, 0:LANES] = a.reshape(shp)
            ksv[idx, :, :, LANES:2 * LANES] = blk_onehot.astype(BF16)
        for idx, a in enumerate(variants(kw_ref[...])):
            kwv[idx] = a.reshape(shp)
        for idx, a in enumerate(variants(kc_ref[0])):
            kcv[idx] = a
        for kt in range(n_kt):
            rows = slice(kt * Q_TILE, (kt + 1) * Q_TILE)
            half = slice((kt % 2) * Q_TILE, (kt % 2 + 1) * Q_TILE)
            vst[kt // 2, :, half] = vs_ref[rows, :].T.astype(BF16)
            vwt[kt] = vw_ref[rows, :].T.astype(BF16)
        vct[...] = vc_ref[0].T.astype(BF16)

    qt = q_ref[...]
    gsig_t = _sigmoid(gt_ref[...]).T

    def col_max(s):
        return jnp.max(s.reshape(s.shape[0] // 8, 8, cols), axis=0)

    def col_sum(s):
        return jnp.sum(s.reshape(s.shape[0] // 8, 8, cols), axis=0)

    def group_rows(g):
        return slice(g * HEAD_DIM, (g + 1) * HEAD_DIM)

    for g in range(N_GROUPS):
        q4 = jnp.concatenate(
            [qt[:, (g * PAIRS_PER_GROUP + p) * LANES:(g * PAIRS_PER_GROUP + p + 1) * LANES]
             for p in range(PAIRS_PER_GROUP)], axis=0)
        q4 = (q4.astype(F32) * (HEAD_DIM ** -0.5)).astype(BF16)

        psum_t = jnp.zeros((LANES, Q_TILE), F32)
        halves = []
        for v in range(2):
            gv = g * 2 + v
            s = _dot_nt(kcv[gv], q4) + bc_ref[g, v, 0]
            valid = s > 0.5 * NEG
            m = jnp.max(s, axis=0, keepdims=True)
            e = jnp.where(valid, jnp.exp(s - m), 0.0)
            p = e * (1.0 / jnp.maximum(jnp.sum(e, axis=0, keepdims=True), 1e-30))
            for pp in range(PAIRS_PER_GROUP):
                psum_t = psum_t + p[:, pp * Q_TILE:(pp + 1) * Q_TILE]
            halves.append(_dot(vct[group_rows(g), :], p.astype(BF16)))
        obuf[0, g] = jnp.concatenate(halves, axis=0)

        p_hi = psum_t.astype(BF16)
        r1 = psum_t - p_hi.astype(F32)
        p_mid = r1.astype(BF16)
        p_lo = (r1 - p_mid.astype(F32)).astype(BF16)
        ovt = ovt_ref[...]
        imp_t = _dot(ovt, p_hi) + _dot(ovt, p_mid) + _dot(ovt, p_lo)

        j_idx = lax.broadcasted_iota(jnp.int32, (n_sel, Q_TILE), 0)
        r_idx = lax.broadcasted_iota(jnp.int32, (n_sel, Q_TILE), 1)
        blk_t = qi * (Q_TILE // SEL_BLOCK) + r_idx // SEL_BLOCK
        valid_blk = j_idx <= blk_t
        forced = (j_idx == 0) | (j_idx == blk_t) | (j_idx == blk_t - 1)
        prio = jnp.where(forced, FORCE_SCORE, imp_t)
        prio = jnp.where(valid_blk, prio, -FORCE_SCORE)
        rank = jnp.zeros((n_sel, Q_TILE), F32)
        for jj in range(n_sel):
            row = prio[jj:jj + 1, :]
            beats = (row > prio) | ((row == prio) & (j_idx > jj))
            rank = rank + jnp.where(beats, 1.0, 0.0)
        drop_t = jnp.where((rank < n_top) & valid_blk, 0.0, NEG)
        if n_sel < LANES:
            drop_t = jnp.concatenate([drop_t, jnp.zeros((LANES - n_sel, Q_TILE), F32)], axis=0)
        drop = drop_t.T.astype(BF16)
        rhs[g, :, 0:LANES] = q4
        rhs[g, :, LANES:2 * LANES] = jnp.concatenate([drop] * PAIRS_PER_GROUP, axis=0)

    n_pairs = (qi + 2) // 2

    def sel_pass1(i, mxs):
        out = []
        for gv in range(2 * N_GROUPS):
            g, v = divmod(gv, 2)
            mx = mxs[gv]
            for h in range(2):
                kt = 2 * i + h
                dd = jnp.clip(qi - kt, 0, 2)
                s = _dot_nt(ksv[gv, kt], rhs[g]) + tw_ref[g, v, dd]
                s_sel[gv, i, h * Q_TILE:(h + 1) * Q_TILE, :] = s
                mx = jnp.maximum(mx, col_max(s))
            out.append(mx)
        return tuple(out)

    mxs = lax.fori_loop(0, n_pairs, sel_pass1,
                        tuple(jnp.full((8, cols), NEG, F32) for _ in range(2 * N_GROUPS)))
    ms = [jnp.max(mx, axis=0, keepdims=True) for mx in mxs]
    acc[...] = jnp.zeros(acc.shape, F32)

    def sel_pass2(i, ls):
        out = []
        for gv in range(2 * N_GROUPS):
            g = gv // 2
            e = jnp.exp(s_sel[gv, i] - ms[gv])
            out.append(ls[gv] + col_sum(e))
            acc[gv] += _dot(vst[i, group_rows(g), :], e.astype(BF16))
        return tuple(out)

    ls = lax.fori_loop(0, n_pairs, sel_pass2,
                       tuple(jnp.zeros((8, cols), F32) for _ in range(2 * N_GROUPS)))
    for g in range(N_GROUPS):
        halves = []
        for v in range(2):
            gv = g * 2 + v
            l = jnp.sum(ls[gv], axis=0, keepdims=True)
            halves.append(acc[gv] * (1.0 / jnp.maximum(l, 1e-30)))
        obuf[1, g] = jnp.concatenate(halves, axis=0)

    for g in range(N_GROUPS):
        q4 = rhs[g, :, 0:LANES]
        halves = []
        for v in range(2):
            gv = g * 2 + v
            mx = jnp.full((8, cols), NEG, F32)
            for dd in range(WIN_TILES):
                kt = jnp.maximum(qi - dd, 0)
                table = jnp.where(qi >= dd, dd, WIN_TILES)
                s = _dot_nt(kwv[gv, kt], q4) + tw_ref[g, v, table]
                s_win[dd] = s
                mx = jnp.maximum(mx, col_max(s))
            m = jnp.max(mx, axis=0, keepdims=True)
            l8 = jnp.zeros((8, cols), F32)
            o = jnp.zeros((HEAD_DIM, cols), F32)
            for dd in range(WIN_TILES):
                kt = jnp.maximum(qi - dd, 0)
                e = jnp.exp(s_win[dd] - m)
                l8 = l8 + col_sum(e)
                o = o + _dot(vwt[kt, group_rows(g), :], e.astype(BF16))
            l = jnp.sum(l8, axis=0, keepdims=True)
            halves.append(o * (1.0 / jnp.maximum(l, 1e-30)))
        obuf[2, g] = jnp.concatenate(halves, axis=0)

    for g in range(N_GROUPS):
        for p in range(PAIRS_PER_GROUP):
            halves = []
            for v in range(2):
                tot = jnp.zeros((HEAD_DIM, Q_TILE), F32)
                for br in range(3):
                    c = br * N_HEADS + g * HEADS_PER_GROUP + 2 * p + v
                    tot = tot + gsig_t[c:c + 1, :] * obuf[br, g, v * HEAD_DIM:(v + 1) * HEAD_DIM,
                                                          p * Q_TILE:(p + 1) * Q_TILE]
                halves.append(tot)
            col = (g * PAIRS_PER_GROUP + p) * LANES
            o_ref[:, col:col + LANES] = jnp.concatenate(halves, axis=0).T.astype(BF16)


def _attention(proj_main, proj_kv, kcmp, vcmp, bias_c, bias_w, ovt, b, s, n_top):
    n_q = s // Q_TILE
    cols = PAIRS_PER_GROUP * Q_TILE
    variants = 2 * N_GROUPS
    once = pl.Buffered(1)
    kv_col = lambda c: pl.BlockSpec((s, LANES), lambda bi, qi: (bi, c))
    cmp_spec = pl.BlockSpec((1, LANES, LANES), lambda bi, qi: (bi, 0, 0))
    return pl.pallas_call(
        functools.partial(_attn_kernel, n_top=n_top),
        grid=(b, n_q),
        in_specs=[
            pl.BlockSpec((Q_TILE, NSA_WIDTH), lambda bi, qi: (bi * n_q + qi, 0)),
            kv_col(2), kv_col(3), kv_col(4), kv_col(5),
            pl.BlockSpec((Q_TILE, LANES), lambda bi, qi: (bi * n_q + qi, 6)),
            cmp_spec, cmp_spec,
            pl.BlockSpec((N_GROUPS, 2, 1, LANES, cols), lambda bi, qi: (0, 0, qi, 0, 0)),
            pl.BlockSpec(bias_w.shape, lambda bi, qi: (0,) * 5, pipeline_mode=once),
            pl.BlockSpec(ovt.shape, lambda bi, qi: (0, 0), pipeline_mode=once),
        ],
        out_specs=pl.BlockSpec((Q_TILE, NSA_WIDTH), lambda bi, qi: (bi * n_q + qi, 0)),
        out_shape=jax.ShapeDtypeStruct((b * s, NSA_WIDTH), BF16),
        scratch_shapes=[
            pltpu.VMEM((variants, n_q, Q_TILE, 2 * LANES), BF16),
            pltpu.VMEM((variants, n_q, Q_TILE, LANES), BF16),
            pltpu.VMEM((variants, LANES, LANES), BF16),
            pltpu.VMEM((n_q // 2, LANES, 2 * Q_TILE), BF16),
            pltpu.VMEM((n_q, LANES, Q_TILE), BF16),
            pltpu.VMEM((LANES, LANES), BF16),
            pltpu.VMEM((N_GROUPS, cols, 2 * LANES), BF16),
            pltpu.VMEM((variants, n_q // 2, 2 * Q_TILE, cols), F32),
            pltpu.VMEM((WIN_TILES, Q_TILE, cols), F32),
            pltpu.VMEM((variants, HEAD_DIM, cols), F32),
            pltpu.VMEM((3, N_GROUPS, LANES, cols), F32),
        ],
        compiler_params=pltpu.CompilerParams(
            dimension_semantics=("parallel", "arbitrary"), vmem_limit_bytes=VMEM_LIMIT),
        name="nsa_attention",
    )(proj_main, proj_kv, proj_kv, proj_kv, proj_kv, proj_kv, kcmp, vcmp, bias_c, bias_w, ovt)


def _merge_kernel(zn_ref, a_ref, b_ref, zc_ref, gc_ref, gn_ref, ah_ref, bh_ref, on_ref, x_ref,
                  cw_ref, cb_ref, lg_ref, lb_ref, wcp_ref, wnp_ref, wo_ref, gf_ref, out_ref, uext):
    i = pl.program_id(1)
    ts = a_ref.shape[0]
    f = lambda r: r[...].astype(F32)

    u_halo = f(ah_ref) * _sigmoid(f(bh_ref))
    uext[0:CONV_HALO, :] = jnp.where(i > 0, u_halo, 0.0)
    uext[CONV_HALO:, :] = f(a_ref) * _sigmoid(f(b_ref))

    c = jnp.broadcast_to(cb_ref[...], (ts, D_MODEL))
    for j in range(CONV_KERNEL):
        off = CONV_HALO - (CONV_KERNEL - 1) + j
        c = c + cw_ref[j:j + 1, :] * uext[off:off + ts, :]

    mu = jnp.mean(c, axis=-1, keepdims=True)
    cc = c - mu
    var = jnp.mean(cc * cc, axis=-1, keepdims=True)
    y = (cc * lax.rsqrt(var + EPS)) * lg_ref[...] + lb_ref[...]
    conv_act = _silu(y) * _silu(f(zc_ref))
    y_conv = _dot(conv_act.astype(BF16), wcp_ref[...])

    nsa_act = f(on_ref) * _silu(f(zn_ref))
    y_nsa = _dot(nsa_act.astype(BF16), wnp_ref[...])

    merged = _sigmoid(f(gc_ref)) * y_conv + _sigmoid(f(gn_ref)) * y_nsa
    xo = x_ref[...] + _dot(merged.astype(BF16), wo_ref[...])
    ms = jnp.mean(xo * xo, axis=-1, keepdims=True)
    out_ref[...] = (xo * lax.rsqrt(ms + EPS)) * gf_ref[...]


def _merge(proj_main, o_nsa, x2, cw, cb, lg, lb, wcp, wnp, wo, gf, b, s, ts):
    n_t = s // ts
    halo_per_tile = ts // CONV_HALO
    col = lambda c: pl.BlockSpec((ts, COL_TILE), lambda bi, ti: (bi * n_t + ti, c))
    halo = lambda c: pl.BlockSpec(
        (CONV_HALO, COL_TILE),
        lambda bi, ti: (jnp.maximum((bi * n_t + ti) * halo_per_tile - 1, 0), c))
    const = lambda a: pl.BlockSpec(a.shape, lambda bi, ti: (0, 0))
    rowblk = pl.BlockSpec((ts, D_MODEL), lambda bi, ti: (bi * n_t + ti, 0))
    return pl.pallas_call(
        _merge_kernel,
        grid=(b, n_t),
        in_specs=[col(1), col(2), col(3), col(4), col(5), col(6), halo(2), halo(3), rowblk, rowblk,
                  const(cw), const(cb), const(lg), const(lb), const(wcp), const(wnp), const(wo),
                  const(gf)],
        out_specs=rowblk,
        out_shape=jax.ShapeDtypeStruct((b * s, D_MODEL), F32),
        scratch_shapes=[pltpu.VMEM((ts + CONV_HALO, D_MODEL), F32)],
        compiler_params=pltpu.CompilerParams(
            dimension_semantics=("parallel", "arbitrary"), vmem_limit_bytes=VMEM_LIMIT),
        name="conv_merge",
    )(proj_main, proj_main, proj_main, proj_main, proj_main, proj_main, proj_main, proj_main,
      o_nsa, x2, cw, cb, lg, lb, wcp, wnp, wo, gf)


def _t5_bucket_np(rel):
    rel = np.maximum(rel, 0)
    max_exact = REL_BUCKETS // 2
    relf = np.maximum(rel, 1).astype(np.float32)
    large = max_exact + (np.log(relf / np.float32(max_exact))
                         / np.float32(np.log(REL_MAX_DIST / max_exact))
                         * np.float32(REL_BUCKETS - max_exact)).astype(np.int32)
    large = np.minimum(large, REL_BUCKETS - 1)
    return np.where(rel < max_exact, rel, large)


def _pair_head_index():
    g = np.arange(N_GROUPS)[:, None, None]
    v = np.arange(2)[None, :, None]
    p = np.arange(PAIRS_PER_GROUP)[None, None, :]
    return g * HEADS_PER_GROUP + 2 * p + v


def _bias_lookup(rel_bias, rel):
    bucket = _t5_bucket_np(rel).reshape(-1)
    onehot = (jnp.arange(REL_BUCKETS)[:, None] == jnp.asarray(bucket)[None, :]).astype(F32)
    vals = jnp.dot(rel_bias.astype(F32).T, onehot, precision=lax.Precision.HIGHEST)
    vals = vals.reshape((N_HEADS,) + rel.shape)
    head = _pair_head_index()
    return jnp.stack([vals[h] for h in head.reshape(-1)]).reshape(head.shape + rel.shape)


def _bias_tables(rel_bias, s):
    r = np.arange(Q_TILE)[:, None]
    c = np.arange(LANES)[None, :]
    rel_w = np.stack([dd * Q_TILE + r - c for dd in range(WIN_TILES)])
    ok_w = (rel_w >= 0) & (rel_w < WINDOW)
    tw = jnp.where(ok_w, _bias_lookup(rel_bias, rel_w), NEG)
    tw = jnp.concatenate([tw, jnp.full_like(tw[:, :, :, :1], NEG)], axis=3)
    tw = jnp.transpose(tw, (0, 1, 3, 5, 2, 4))
    tw = tw.reshape(tw.shape[:4] + (PAIRS_PER_GROUP * Q_TILE,))
    t = np.arange(s)[:, None]
    rel_c = t - (c * CMP_STRIDE + CMP_BLOCK - 1)
    ok_c = (rel_c >= 0) & (c < (s - CMP_BLOCK) // CMP_STRIDE + 1)
    bc = jnp.where(ok_c, _bias_lookup(rel_bias, rel_c), NEG)
    bc = bc.reshape(N_GROUPS, 2, PAIRS_PER_GROUP, s // Q_TILE, Q_TILE, LANES)
    bc = jnp.transpose(bc, (0, 1, 3, 5, 2, 4))
    bc = bc.reshape(bc.shape[:4] + (PAIRS_PER_GROUP * Q_TILE,))
    return tw, bc


def _overlap_t(s):
    n_cmp = (s - CMP_BLOCK) // CMP_STRIDE + 1
    cs = np.arange(LANES) * CMP_STRIDE
    ss = np.arange(s // SEL_BLOCK) * SEL_BLOCK
    ovt = ((cs[None, :] <= ss[:, None] + SEL_BLOCK - 1) & (cs[None, :] + CMP_BLOCK - 1 >= ss[:, None])
           & (np.arange(LANES)[None, :] < n_cmp))
    return jnp.asarray(ovt, BF16)


def _permute_w_in(w):
    sizes = [NSA_WIDTH] + [KV_WIDTH] * 6 + [3 * N_HEADS, NSA_WIDTH, 2 * D_MODEL, D_MODEL, 2 * D_MODEL]
    offs = np.concatenate([[0], np.cumsum(sizes)])
    seg = lambda i: w[:, offs[i]:offs[i + 1]]
    pad = jnp.zeros((D_MODEL, COL_TILE - 6 * KV_WIDTH - 3 * N_HEADS), w.dtype)
    cols = [seg(i) for i in range(1, 8)] + [pad, seg(0), seg(8), seg(9), seg(10), seg(11)]
    return jnp.concatenate(cols, axis=1).astype(BF16)


def _compress_weights(pos, w1, w2):
    half = CMP_BLOCK // 2
    eye = jnp.eye(N_GROUPS, dtype=F32)
    w1r = w1.reshape(CMP_BLOCK, HEAD_DIM, CMP_HIDDEN)
    blk = lambda part: jnp.einsum('idn,gh->igdhn', part, eye).reshape(
        half * KV_WIDTH, N_GROUPS * CMP_HIDDEN)
    w1d = jnp.concatenate([blk(w1r[:half]), blk(w1r[half:])], axis=1).astype(BF16)
    w2d = jnp.einsum('nd,gh->gnhd', w2, eye).reshape(N_GROUPS * CMP_HIDDEN, KV_WIDTH).astype(BF16)
    tilepos = lambda part: jnp.broadcast_to(part[:, None, :], (half, N_GROUPS, HEAD_DIM)).reshape(1, -1)
    posd = jnp.concatenate([tilepos(pos[:half]), tilepos(pos[half:])], axis=0).astype(F32)
    return posd, w1d, w2d


def kernel(x, norm_in_g, w_in, pos_ck, w_ck1, w_ck2, pos_cv, w_cv1, w_cv2, rel_bias, conv_w, conv_b,
           conv_ln_g, conv_ln_b, w_conv_proj, w_nsa_proj, w_out, norm_f_g):
    b, s, d = x.shape
    assert d == D_MODEL and w_in.shape[0] == 1, "single-layer block with D_MODEL=1024"
    assert s % (2 * Q_TILE) == 0 and s // CMP_STRIDE <= LANES and s >= WINDOW
    m = b * s
    x2 = x.reshape(m, d)
    row = lambda a: a.reshape(1, -1).astype(F32)

    proj_kv, proj_main = _input_projection(x2, row(norm_in_g[0]), _permute_w_in(w_in[0]), min(1024, m))

    chunks = s // CMP_STRIDE
    kf = proj_kv[:, 0:KV_WIDTH].reshape(b, chunks, CMP_STRIDE * KV_WIDTH)
    vf = proj_kv[:, KV_WIDTH:2 * KV_WIDTH].reshape(b, chunks, CMP_STRIDE * KV_WIDTH)
    pk, w1k, w2k = _compress_weights(pos_ck[0], w_ck1[0], w_ck2[0])
    pv, w1v, w2v = _compress_weights(pos_cv[0], w_cv1[0], w_cv2[0])
    kcmp, vcmp = _compress(kf, vf, pk, pv, w1k, w1v, w2k, w2v)
    if chunks < LANES:
        padrows = ((0, 0), (0, LANES - chunks), (0, 0))
        kcmp, vcmp = jnp.pad(kcmp, padrows), jnp.pad(vcmp, padrows)

    bias_w, bias_c = _bias_tables(rel_bias, s)
    o_nsa = _attention(proj_main, proj_kv, kcmp, vcmp, bias_c, bias_w, _overlap_t(s), b, s,
                       min(N_SELECT, s // SEL_BLOCK))

    cw = jnp.pad(conv_w[0].astype(F32), ((0, CONV_HALO - CONV_KERNEL), (0, 0)))
    out = _merge(proj_main, o_nsa, x2, cw, row(conv_b[0]), row(conv_ln_g[0]), row(conv_ln_b[0]),
                 w_conv_proj[0].astype(BF16), w_nsa_proj[0].astype(BF16), w_out[0].astype(BF16),
                 row(norm_f_g), b, s, 256)
    return out.reshape(b, s, d)
```

```python
import functools
import math

import numpy as np
import jax
import jax.numpy as jnp
from jax import lax
from jax.experimental import pallas as pl
from jax.experimental.pallas import tpu as pltpu

F32 = jnp.float32
BF16 = jnp.bfloat16

D_MODEL = 1024
N_HEADS = 16
N_GROUPS = 2
HEADS_PER_GROUP = N_HEADS // N_GROUPS
PAIRS_PER_GROUP = HEADS_PER_GROUP // 2
HEAD_DIM = 64
NSA_WIDTH = N_HEADS * HEAD_DIM
KV_WIDTH = N_GROUPS * HEAD_DIM
CMP_BLOCK = 32
CMP_STRIDE = 16
CMP_HIDDEN = 256
SEL_BLOCK = 64
N_SELECT = 8
WINDOW = 512
Q_TILE = 128
CONV_KERNEL = 31
CONV_HALO = 32
REL_BUCKETS = 32
REL_MAX_DIST = 128
EPS = 1e-6
NEG = -1e30
FORCE_SCORE = 1e6
LOG2E = math.log2(math.e)
LANES = 128
SUBLANES = 8
COLS = PAIRS_PER_GROUP * Q_TILE
COL_TILE = 1024
N_COL_TILES = 8
WIN_TILES = WINDOW // Q_TILE + 1
VMEM_LIMIT = 56 * 1024 * 1024


def _dot(a, b):
    return jnp.dot(a, b, preferred_element_type=F32)


def _dot_nt(a, b):
    return lax.dot_general(a, b, (((1,), (1,)), ((), ())), preferred_element_type=F32)


def _sigmoid(x):
    return 1.0 / (1.0 + jnp.exp(-x))


def _silu(x):
    return x * _sigmoid(x)


def _proj_kernel(x_ref, g_ref, w_ref, kv_ref, main_ref, h_ref):
    j = pl.program_id(1)

    @pl.when(j == 0)
    def _():
        x = x_ref[...]
        ms = jnp.mean(x * x, axis=-1, keepdims=True)
        h_ref[...] = ((x * lax.rsqrt(ms + EPS)) * g_ref[...]).astype(BF16)

    acc = _dot(h_ref[...], w_ref[...])

    @pl.when(j == 0)
    def _():
        kv_ref[...] = acc

    @pl.when(j > 0)
    def _():
        main_ref[...] = acc.astype(BF16)


def _input_projection(x2, g, w_perm, tm):
    m = x2.shape[0]
    return pl.pallas_call(
        _proj_kernel,
        grid=(m // tm, N_COL_TILES),
        in_specs=[
            pl.BlockSpec((tm, D_MODEL), lambda i, j: (i, 0)),
            pl.BlockSpec((1, D_MODEL), lambda i, j: (0, 0)),
            pl.BlockSpec((D_MODEL, COL_TILE), lambda i, j: (0, j)),
        ],
        out_specs=[
            pl.BlockSpec((tm, COL_TILE), lambda i, j: (i, 0)),
            pl.BlockSpec((tm, COL_TILE), lambda i, j: (i, jnp.maximum(j - 1, 0))),
        ],
        out_shape=[
            jax.ShapeDtypeStruct((m, COL_TILE), F32),
            jax.ShapeDtypeStruct((m, (N_COL_TILES - 1) * COL_TILE), BF16),
        ],
        scratch_shapes=[pltpu.VMEM((tm, D_MODEL), BF16)],
        compiler_params=pltpu.CompilerParams(
            dimension_semantics=("parallel", "arbitrary"), vmem_limit_bytes=VMEM_LIMIT),
        name="input_projection",
    )(x2, g, w_perm)


def _compress_kernel(kf_ref, vf_ref, pk_ref, pv_ref, w1k_ref, w1v_ref, w2k_ref, w2v_ref,
                     kc_ref, vc_ref):
    def one(f_ref, pos_ref, w1_ref, w2_ref, o_ref):
        f = f_ref[0]
        n = f.shape[0]
        hw = N_GROUPS * CMP_HIDDEN
        first = _dot((f + pos_ref[0:1, :]).astype(BF16), w1_ref[:, 0:hw])
        second = _dot((f + pos_ref[1:2, :]).astype(BF16), w1_ref[:, hw:2 * hw])
        hid = first + pltpu.roll(second, n - 1, axis=0)
        o_ref[0] = _dot(_silu(hid).astype(BF16), w2_ref[...])

    one(kf_ref, pk_ref, w1k_ref, w2k_ref, kc_ref)
    one(vf_ref, pv_ref, w1v_ref, w2v_ref, vc_ref)


def _compress(kf, vf, pk, pv, w1k, w1v, w2k, w2v):
    b, n, width = kf.shape
    const = lambda shape: pl.BlockSpec(shape, lambda i: (0,) * len(shape))
    row = pl.BlockSpec((1, n, width), lambda i: (i, 0, 0))
    out = pl.BlockSpec((1, n, LANES), lambda i: (i, 0, 0))
    return pl.pallas_call(
        _compress_kernel,
        grid=(b,),
        in_specs=[row, row, const(pk.shape), const(pv.shape), const(w1k.shape), const(w1v.shape),
                  const(w2k.shape), const(w2v.shape)],
        out_specs=[out, out],
        out_shape=[jax.ShapeDtypeStruct((b, n, LANES), F32)] * 2,
        compiler_params=pltpu.CompilerParams(
            dimension_semantics=("parallel",), vmem_limit_bytes=VMEM_LIMIT),
        name="nsa_compress",
    )(kf, vf, pk, pv, w1k, w1v, w2k, w2v)


def _attn_kernel(q_ref, ks_ref, vs_ref, kw_ref, vw_ref, gt_ref, kc_ref, vc_ref, bc_ref, tw_ref,
                 ovt_ref, bx_ref, o_ref,
                 ksv, kwv, kcv, vst, vwt, vct, rhs, acc, obuf, *, n_top):
    qi = pl.program_id(1)
    n_kt = kwv.shape[1]
    n_sel = ovt_ref.shape[0]
    variants = 2 * N_GROUPS

    @pl.when(qi == 0)
    def _prepare_kv():
        def halves(k):
            lo = lax.broadcasted_iota(jnp.int32, k.shape, 1) < HEAD_DIM
            kr = pltpu.roll(k, HEAD_DIM, axis=1)
            z = jnp.zeros_like(k)
            c = lambda a: a.astype(BF16)
            return ((c(jnp.where(lo, k, z)), c(jnp.where(lo, z, kr))),
                    (c(jnp.where(lo, kr, z)), c(jnp.where(lo, z, k))))

        def extra_lanes(shape, rows_per_tile, v, with_blocks):
            t = lax.broadcasted_iota(jnp.int32, shape, 0)
            r = lax.broadcasted_iota(jnp.int32, shape, 1)
            lane = lax.broadcasted_iota(jnp.int32, shape, 2)
            far = (lane >= n_sel + 2 * v) & (lane < n_sel + 2 * v + 2)
            if with_blocks:
                far = far | (lane == t * (rows_per_tile // SEL_BLOCK) + r // SEL_BLOCK)
            return jnp.where(far, 1.0, 0.0).astype(BF16)

        pair_shape = (n_kt // 2, 2 * Q_TILE, LANES)
        tile_shape = (n_kt, Q_TILE, LANES)
        k_sel, k_win, k_cmp = halves(ks_ref[...]), halves(kw_ref[...]), halves(kc_ref[0])
        for g in range(N_GROUPS):
            for v in range(2):
                rows = slice(v * 2 * Q_TILE, (v + 1) * 2 * Q_TILE)
                ksv[g, :, rows, 0:LANES] = k_sel[g][v].reshape(pair_shape)
                ksv[g, :, rows, LANES:2 * LANES] = extra_lanes(pair_shape, 2 * Q_TILE, v, True)
                rows = slice(v * Q_TILE, (v + 1) * Q_TILE)
                kwv[g, :, rows, 0:LANES] = k_win[g][v].reshape(tile_shape)
                kwv[g, :, rows, LANES:2 * LANES] = extra_lanes(tile_shape, Q_TILE, v, False)
                kcv[g, v * LANES:(v + 1) * LANES, :] = k_cmp[g][v]
        for kt in range(n_kt):
            rows = slice(kt * Q_TILE, (kt + 1) * Q_TILE)
            half = slice((kt % 2) * Q_TILE, (kt % 2 + 1) * Q_TILE)
            vst[kt // 2, :, half] = vs_ref[rows, :].T.astype(BF16)
            vwt[kt] = vw_ref[rows, :].T.astype(BF16)
        vct[...] = vc_ref[0].T.astype(BF16)

    qt = q_ref[...]
    gsig_t = _sigmoid(gt_ref[...]).T

    def col_max(s):
        return jnp.max(s.reshape(s.shape[0] // 8, 8, COLS), axis=0)

    def col_sum(s):
        return jnp.sum(s.reshape(s.shape[0] // 8, 8, COLS), axis=0)

    def group_rows(g):
        return slice(g * HEAD_DIM, (g + 1) * HEAD_DIM)

    def online(gv, s, vt, m, l8):
        m_new = jnp.maximum(m, jnp.max(col_max(s), axis=0, keepdims=True))
        alpha = jnp.exp2(m - m_new)
        e = jnp.exp2(s - m_new)
        acc[gv] = acc[gv] * alpha + _dot(vt, e.astype(BF16))
        return m_new, l8 * alpha + col_sum(e)

    def fresh_state():
        acc[...] = jnp.zeros(acc.shape, F32)
        return (tuple(jnp.full((1, COLS), NEG, F32) for _ in range(variants))
                + tuple(jnp.zeros((8, COLS), F32) for _ in range(variants)))

    def finish(state, branch):
        for g in range(N_GROUPS):
            halves = []
            for v in range(2):
                gv = g * 2 + v
                l = jnp.sum(state[variants + gv], axis=0, keepdims=True)
                halves.append(acc[gv] * (1.0 / jnp.maximum(l, 1e-30)))
            obuf[branch, g] = jnp.concatenate(halves, axis=0)

    for g in range(N_GROUPS):
        q4 = jnp.concatenate(
            [qt[:, (g * PAIRS_PER_GROUP + p) * LANES:(g * PAIRS_PER_GROUP + p + 1) * LANES]
             for p in range(PAIRS_PER_GROUP)], axis=0)
        q4 = (q4.astype(F32) * (HEAD_DIM ** -0.5 * LOG2E)).astype(BF16)

        s_all = _dot_nt(kcv[g], q4) + bc_ref[g, 0]
        psum_t = jnp.zeros((LANES, Q_TILE), F32)
        halves = []
        for v in range(2):
            s = s_all[v * LANES:(v + 1) * LANES]
            valid = s > 0.5 * NEG
            m = jnp.max(s, axis=0, keepdims=True)
            e = jnp.where(valid, jnp.exp2(s - m), 0.0)
            p = e * (1.0 / jnp.maximum(jnp.sum(e, axis=0, keepdims=True), 1e-30))
            for pp in range(PAIRS_PER_GROUP):
                psum_t = psum_t + p[:, pp * Q_TILE:(pp + 1) * Q_TILE]
            halves.append(_dot(vct[group_rows(g), :], p.astype(BF16)))
        obuf[0, g] = jnp.concatenate(halves, axis=0)

        p_hi = psum_t.astype(BF16)
        r1 = psum_t - p_hi.astype(F32)
        p_mid = r1.astype(BF16)
        p_lo = (r1 - p_mid.astype(F32)).astype(BF16)
        ovt = ovt_ref[...]
        imp_t = _dot(ovt, p_hi) + _dot(ovt, p_mid) + _dot(ovt, p_lo)

        j_idx = lax.broadcasted_iota(jnp.int32, (n_sel, Q_TILE), 0)
        r_idx = lax.broadcasted_iota(jnp.int32, (n_sel, Q_TILE), 1)
        blk_t = qi * (Q_TILE // SEL_BLOCK) + r_idx // SEL_BLOCK
        valid_blk = j_idx <= blk_t
        forced = (j_idx == 0) | (j_idx == blk_t) | (j_idx == blk_t - 1)
        prio = jnp.where(forced, FORCE_SCORE, imp_t)
        prio = jnp.where(valid_blk, prio, -FORCE_SCORE)
        rank = jnp.zeros((n_sel, Q_TILE), F32)
        for jj in range(n_sel):
            row = prio[jj:jj + 1, :]
            beats = (row > prio) | ((row == prio) & (j_idx > jj))
            rank = rank + jnp.where(beats, 1.0, 0.0)
        drop_t = jnp.where((rank < n_top) & valid_blk, 0.0, NEG)
        drop_t = jnp.concatenate([drop_t, jnp.zeros((LANES - n_sel, Q_TILE), F32)], axis=0)
        drop = drop_t.T.astype(BF16)
        rhs[g, :, 0:LANES] = q4
        rhs[g, :, LANES:2 * LANES] = jnp.concatenate([drop] * PAIRS_PER_GROUP, axis=0) + bx_ref[g]

    def sel_body(near):
        def body(i, state):
            ms, ls = list(state[:variants]), list(state[variants:])
            for g in range(N_GROUPS):
                s_all = _dot_nt(ksv[g, i], rhs[g])
                for v in range(2):
                    gv = g * 2 + v
                    s = s_all[v * 2 * Q_TILE:(v + 1) * 2 * Q_TILE]
                    if near:
                        parts = []
                        for h in range(2):
                            dd = jnp.clip(qi - (2 * i + h), 0, 2)
                            parts.append(s[h * Q_TILE:(h + 1) * Q_TILE] + tw_ref[g, v, dd])
                        s = jnp.concatenate(parts, axis=0)
                    ms[gv], ls[gv] = online(gv, s, vst[i, group_rows(g), :], ms[gv], ls[gv])
            return tuple(ms) + tuple(ls)
        return body

    n_far = jnp.maximum(qi - 1, 0) // 2
    state = lax.fori_loop(0, n_far, sel_body(False), fresh_state())
    state = lax.fori_loop(n_far, (qi + 2) // 2, sel_body(True), state)
    finish(state, 1)

    state = fresh_state()
    ms, ls = list(state[:variants]), list(state[variants:])
    first = jnp.maximum(qi - (WIN_TILES - 1), 0)
    for t in range(WIN_TILES):
        dd = qi - (first + t)
        table = jnp.where(dd >= 0, dd, WIN_TILES)
        kt = jnp.minimum(first + t, n_kt - 1)
        for g in range(N_GROUPS):
            s_all = _dot_nt(kwv[g, kt], rhs[g])
            for v in range(2):
                gv = g * 2 + v
                s = s_all[v * Q_TILE:(v + 1) * Q_TILE] + tw_ref[g, v, table]
                ms[gv], ls[gv] = online(gv, s, vwt[kt, group_rows(g), :], ms[gv], ls[gv])
    finish(tuple(ms) + tuple(ls), 2)

    for g in range(N_GROUPS):
        for p in range(PAIRS_PER_GROUP):
            halves = []
            for v in range(2):
                tot = jnp.zeros((HEAD_DIM, Q_TILE), F32)
                for br in range(3):
                    c = br * N_HEADS + g * HEADS_PER_GROUP + 2 * p + v
                    tot = tot + gsig_t[c:c + 1, :] * obuf[br, g, v * HEAD_DIM:(v + 1) * HEAD_DIM,
                                                          p * Q_TILE:(p + 1) * Q_TILE]
                halves.append(tot)
            col = (g * PAIRS_PER_GROUP + p) * LANES
            o_ref[:, col:col + LANES] = jnp.concatenate(halves, axis=0).T.astype(BF16)


def _attention(proj_main, proj_kv, kcmp, vcmp, bias_c, bias_w, ovt, far_lanes, b, s, n_top):
    n_q = s // Q_TILE
    once = pl.Buffered(1)
    kv_col = lambda c: pl.BlockSpec((s, LANES), lambda bi, qi: (bi, c))
    cmp_spec = pl.BlockSpec((1, LANES, LANES), lambda bi, qi: (bi, 0, 0))
    return pl.pallas_call(
        functools.partial(_attn_kernel, n_top=n_top),
        grid=(b, n_q),
        in_specs=[
            pl.BlockSpec((Q_TILE, NSA_WIDTH), lambda bi, qi: (bi * n_q + qi, 0)),
            kv_col(2), kv_col(3), kv_col(4), kv_col(5),
            pl.BlockSpec((Q_TILE, LANES), lambda bi, qi: (bi * n_q + qi, 6)),
            cmp_spec, cmp_spec,
            pl.BlockSpec((N_GROUPS, 1, 2 * LANES, COLS), lambda bi, qi: (0, qi, 0, 0)),
            pl.BlockSpec(bias_w.shape, lambda bi, qi: (0,) * 5, pipeline_mode=once),
            pl.BlockSpec(ovt.shape, lambda bi, qi: (0, 0), pipeline_mode=once),
            pl.BlockSpec(far_lanes.shape, lambda bi, qi: (0, 0, 0), pipeline_mode=once),
        ],
        out_specs=pl.BlockSpec((Q_TILE, NSA_WIDTH), lambda bi, qi: (bi * n_q + qi, 0)),
        out_shape=jax.ShapeDtypeStruct((b * s, NSA_WIDTH), BF16),
        scratch_shapes=[
            pltpu.VMEM((N_GROUPS, n_q // 2, 4 * Q_TILE, 2 * LANES), BF16),
            pltpu.VMEM((N_GROUPS, n_q, 2 * Q_TILE, 2 * LANES), BF16),
            pltpu.VMEM((N_GROUPS, 2 * LANES, LANES), BF16),
            pltpu.VMEM((n_q // 2, LANES, 2 * Q_TILE), BF16),
            pltpu.VMEM((n_q, LANES, Q_TILE), BF16),
            pltpu.VMEM((LANES, LANES), BF16),
            pltpu.VMEM((N_GROUPS, COLS, 2 * LANES), BF16),
            pltpu.VMEM((2 * N_GROUPS, HEAD_DIM, COLS), F32),
            pltpu.VMEM((3, N_GROUPS, LANES, COLS), F32),
        ],
        compiler_params=pltpu.CompilerParams(
            dimension_semantics=("parallel", "arbitrary"), vmem_limit_bytes=VMEM_LIMIT),
        name="nsa_attention",
    )(proj_main, proj_kv, proj_kv, proj_kv, proj_kv, proj_kv, kcmp, vcmp, bias_c, bias_w, ovt,
      far_lanes)


def _merge_kernel(zn_ref, a_ref, b_ref, zc_ref, gc_ref, gn_ref, ah_ref, bh_ref, on_ref, x_ref,
                  cw_ref, cb_ref, lg_ref, lb_ref, wcp_ref, wnp_ref, wo_ref, gf_ref, out_ref, uext,
                  conv, shifted):
    i = pl.program_id(1)
    ts = a_ref.shape[0]
    f = lambda r: r[...].astype(F32)

    n_cblk = D_MODEL // LANES
    u_halo = jnp.where(i > 0, f(ah_ref) * _sigmoid(f(bh_ref)), 0.0)
    u = f(a_ref) * _sigmoid(f(b_ref))
    for cblk in range(n_cblk):
        cols = slice(cblk * LANES, (cblk + 1) * LANES)
        uext[cblk, 0:CONV_HALO, :] = u_halo[:, cols]
        uext[cblk, CONV_HALO:CONV_HALO + ts, :] = u[:, cols]
        uext[cblk, CONV_HALO + ts:, :] = jnp.zeros((SUBLANES, LANES), F32)

    lead = CONV_HALO - (CONV_KERNEL - 1)
    half = ts // 2

    def conv_block(cblk, carry):
        for shift in range(SUBLANES):
            shifted[shift] = uext[cblk, shift:shift + ts + CONV_HALO, :]
        w = cw_ref[cblk]
        for h in range(2):
            c = jnp.broadcast_to(cb_ref[cblk], (half, LANES))
            for shift in range(SUBLANES):
                xs = shifted[shift, h * half:h * half + half + CONV_HALO, :]
                for j in range(CONV_KERNEL):
                    if (lead + j) % SUBLANES == shift:
                        base = lead + j - shift
                        c = c + w[j:j + 1, :] * xs[base:base + half]
            conv[cblk, h * half:(h + 1) * half, :] = c
        return carry

    lax.fori_loop(0, n_cblk, conv_block, 0)
    c = jnp.concatenate([conv[cblk] for cblk in range(n_cblk)], axis=1)

    mu = jnp.mean(c, axis=-1, keepdims=True)
    cc = c - mu
    var = jnp.mean(cc * cc, axis=-1, keepdims=True)
    y = (cc * lax.rsqrt(var + EPS)) * lg_ref[...] + lb_ref[...]
    conv_act = _silu(y) * _silu(f(zc_ref))
    y_conv = _dot(conv_act.astype(BF16), wcp_ref[...])

    nsa_act = f(on_ref) * _silu(f(zn_ref))
    y_nsa = _dot(nsa_act.astype(BF16), wnp_ref[...])

    merged = _sigmoid(f(gc_ref)) * y_conv + _sigmoid(f(gn_ref)) * y_nsa
    xo = x_ref[...] + _dot(merged.astype(BF16), wo_ref[...])
    ms = jnp.mean(xo * xo, axis=-1, keepdims=True)
    out_ref[...] = (xo * lax.rsqrt(ms + EPS)) * gf_ref[...]


def _merge(proj_main, o_nsa, x2, cw, cb, lg, lb, wcp, wnp, wo, gf, b, s, ts):
    n_t = s // ts
    n_cblk = D_MODEL // LANES
    halo_per_tile = ts // CONV_HALO
    cw = jnp.transpose(cw.reshape(cw.shape[0], n_cblk, LANES), (1, 0, 2))
    cb = cb.reshape(n_cblk, 1, LANES)
    col = lambda c: pl.BlockSpec((ts, COL_TILE), lambda bi, ti: (bi * n_t + ti, c))
    halo = lambda c: pl.BlockSpec(
        (CONV_HALO, COL_TILE),
        lambda bi, ti: (jnp.maximum((bi * n_t + ti) * halo_per_tile - 1, 0), c))
    const = lambda a: pl.BlockSpec(a.shape, lambda bi, ti: (0,) * a.ndim)
    rowblk = pl.BlockSpec((ts, D_MODEL), lambda bi, ti: (bi * n_t + ti, 0))
    return pl.pallas_call(
        _merge_kernel,
        grid=(b, n_t),
        in_specs=[col(1), col(2), col(3), col(4), col(5), col(6), halo(2), halo(3), rowblk, rowblk,
                  const(cw), const(cb), const(lg), const(lb), const(wcp), const(wnp), const(wo),
                  const(gf)],
        out_specs=rowblk,
        out_shape=jax.ShapeDtypeStruct((b * s, D_MODEL), F32),
        scratch_shapes=[pltpu.VMEM((n_cblk, ts + CONV_HALO + SUBLANES, LANES), F32),
                        pltpu.VMEM((n_cblk, ts, LANES), F32),
                        pltpu.VMEM((SUBLANES, ts + CONV_HALO, LANES), F32)],
        compiler_params=pltpu.CompilerParams(
            dimension_semantics=("parallel", "arbitrary"), vmem_limit_bytes=VMEM_LIMIT),
        name="conv_merge",
    )(proj_main, proj_main, proj_main, proj_main, proj_main, proj_main, proj_main, proj_main,
      o_nsa, x2, cw, cb, lg, lb, wcp, wnp, wo, gf)


def _t5_bucket_np(rel):
    rel = np.maximum(rel, 0)
    max_exact = REL_BUCKETS // 2
    relf = np.maximum(rel, 1).astype(np.float32)
    large = max_exact + (np.log(relf / np.float32(max_exact))
                         / np.float32(np.log(REL_MAX_DIST / max_exact))
                         * np.float32(REL_BUCKETS - max_exact)).astype(np.int32)
    large = np.minimum(large, REL_BUCKETS - 1)
    return np.where(rel < max_exact, rel, large)


def _pair_head_index():
    g = np.arange(N_GROUPS)[:, None, None]
    v = np.arange(2)[None, :, None]
    p = np.arange(PAIRS_PER_GROUP)[None, None, :]
    return g * HEADS_PER_GROUP + 2 * p + v


def _bias_lookup(rel_bias, rel):
    bucket = _t5_bucket_np(rel).reshape(-1)
    onehot = (jnp.arange(REL_BUCKETS)[:, None] == jnp.asarray(bucket)[None, :]).astype(F32)
    vals = jnp.dot(rel_bias.astype(F32).T, onehot, precision=lax.Precision.HIGHEST)
    vals = vals.reshape((N_HEADS,) + rel.shape)
    head = _pair_head_index()
    return jnp.stack([vals[h] for h in head.reshape(-1)]).reshape(head.shape + rel.shape)


def _bias_tables(rel_bias, s, n_sel):
    r = np.arange(Q_TILE)[:, None]
    c = np.arange(LANES)[None, :]
    far = rel_bias.astype(F32)[REL_BUCKETS - 1][_pair_head_index()]
    rel_w = np.stack([dd * Q_TILE + r - c for dd in range(WIN_TILES)])
    assert (_t5_bucket_np(rel_w[2:]) == REL_BUCKETS - 1).all()
    ok_w = (rel_w >= 0) & (rel_w < WINDOW)
    delta = (_bias_lookup(rel_bias, rel_w) - far[..., None, None, None]) * LOG2E
    tw = jnp.where(ok_w, delta, NEG)
    tw = jnp.concatenate([tw, jnp.full_like(tw[:, :, :, :1], NEG)], axis=3)
    tw = jnp.transpose(tw, (0, 1, 3, 5, 2, 4))
    tw = tw.reshape(tw.shape[:4] + (COLS,))
    t = np.arange(s)[:, None]
    rel_c = t - (c * CMP_STRIDE + CMP_BLOCK - 1)
    ok_c = (rel_c >= 0) & (c < (s - CMP_BLOCK) // CMP_STRIDE + 1)
    bc = jnp.where(ok_c, _bias_lookup(rel_bias, rel_c) * LOG2E, NEG)
    bc = bc.reshape(N_GROUPS, 2, PAIRS_PER_GROUP, s // Q_TILE, Q_TILE, LANES)
    bc = jnp.transpose(bc, (0, 3, 1, 5, 2, 4))
    bc = bc.reshape(N_GROUPS, s // Q_TILE, 2 * LANES, COLS)
    far2 = far * LOG2E
    hi = far2.astype(BF16)
    lo = (far2 - hi.astype(F32)).astype(BF16)
    pieces = jnp.stack([hi[:, 0], lo[:, 0], hi[:, 1], lo[:, 1]], axis=-1)
    pieces = jnp.broadcast_to(pieces[:, :, None, :], (N_GROUPS, PAIRS_PER_GROUP, Q_TILE, 4))
    far_lanes = jnp.pad(pieces.reshape(N_GROUPS, COLS, 4),
                        ((0, 0), (0, 0), (n_sel, LANES - n_sel - 4)))
    return tw, bc, far_lanes


def _overlap_t(s):
    n_cmp = (s - CMP_BLOCK) // CMP_STRIDE + 1
    cs = np.arange(LANES) * CMP_STRIDE
    ss = np.arange(s // SEL_BLOCK) * SEL_BLOCK
    ovt = ((cs[None, :] <= ss[:, None] + SEL_BLOCK - 1) & (cs[None, :] + CMP_BLOCK - 1 >= ss[:, None])
           & (np.arange(LANES)[None, :] < n_cmp))
    return jnp.asarray(ovt, BF16)


def _permute_w_in(w):
    sizes = [NSA_WIDTH] + [KV_WIDTH] * 6 + [3 * N_HEADS, NSA_WIDTH, 2 * D_MODEL, D_MODEL, 2 * D_MODEL]
    offs = np.concatenate([[0], np.cumsum(sizes)])
    seg = lambda i: w[:, offs[i]:offs[i + 1]]
    pad = jnp.zeros((D_MODEL, COL_TILE - 6 * KV_WIDTH - 3 * N_HEADS), w.dtype)
    cols = [seg(i) for i in range(1, 8)] + [pad, seg(0), seg(8), seg(9), seg(10), seg(11)]
    return jnp.concatenate(cols, axis=1).astype(BF16)


def _compress_weights(pos, w1, w2):
    half = CMP_BLOCK // 2
    eye = jnp.eye(N_GROUPS, dtype=F32)
    w1r = w1.reshape(CMP_BLOCK, HEAD_DIM, CMP_HIDDEN)
    blk = lambda part: jnp.einsum('idn,gh->igdhn', part, eye).reshape(
        half * KV_WIDTH, N_GROUPS * CMP_HIDDEN)
    w1d = jnp.concatenate([blk(w1r[:half]), blk(w1r[half:])], axis=1).astype(BF16)
    w2d = jnp.einsum('nd,gh->gnhd', w2, eye).reshape(N_GROUPS * CMP_HIDDEN, KV_WIDTH).astype(BF16)
    tilepos = lambda part: jnp.broadcast_to(part[:, None, :], (half, N_GROUPS, HEAD_DIM)).reshape(1, -1)
    posd = jnp.concatenate([tilepos(pos[:half]), tilepos(pos[half:])], axis=0).astype(F32)
    return posd, w1d, w2d


def kernel(x, norm_in_g, w_in, pos_ck, w_ck1, w_ck2, pos_cv, w_cv1, w_cv2, rel_bias, conv_w, conv_b,
           conv_ln_g, conv_ln_b, w_conv_proj, w_nsa_proj, w_out, norm_f_g):
    b, s, d = x.shape
    n_sel = s // SEL_BLOCK
    assert d == D_MODEL and w_in.shape[0] == 1, "single-layer block with D_MODEL=1024"
    assert s % (2 * Q_TILE) == 0 and s // CMP_STRIDE <= LANES and s >= WINDOW
    assert n_sel + 4 <= LANES
    m = b * s
    x2 = x.reshape(m, d)
    row = lambda a: a.reshape(1, -1).astype(F32)

    proj_kv, proj_main = _input_projection(x2, row(norm_in_g[0]), _permute_w_in(w_in[0]), min(1024, m))

    chunks = s // CMP_STRIDE
    kf = proj_kv[:, 0:KV_WIDTH].reshape(b, chunks, CMP_STRIDE * KV_WIDTH)
    vf = proj_kv[:, KV_WIDTH:2 * KV_WIDTH].reshape(b, chunks, CMP_STRIDE * KV_WIDTH)
    pk, w1k, w2k = _compress_weights(pos_ck[0], w_ck1[0], w_ck2[0])
    pv, w1v, w2v = _compress_weights(pos_cv[0], w_cv1[0], w_cv2[0])
    kcmp, vcmp = _compress(kf, vf, pk, pv, w1k, w1v, w2k, w2v)
    if chunks < LANES:
        padrows = ((0, 0), (0, LANES - chunks), (0, 0))
        kcmp, vcmp = jnp.pad(kcmp, padrows), jnp.pad(vcmp, padrows)

    bias_w, bias_c, far_lanes = _bias_tables(rel_bias, s, n_sel)
    o_nsa = _attention(proj_main, proj_kv, kcmp, vcmp, bias_c, bias_w, _overlap_t(s), far_lanes,
                       b, s, min(N_SELECT, n_sel))

    cw = jnp.pad(conv_w[0].astype(F32), ((0, CONV_HALO - CONV_KERNEL), (0, 0)))
    out = _merge(proj_main, o_nsa, x2, cw, row(conv_b[0]), row(conv_ln_g[0]), row(conv_ln_b[0]),
                 w_conv_proj[0].astype(BF16), w_nsa_proj[0].astype(BF16), w_out[0].astype(BF16),
                 row(norm_f_g), b, s, 256)
    return out.reshape(b, s, d)
```

```python
import functools
import math

import numpy as np
import jax
import jax.numpy as jnp
from jax import lax
from jax.experimental import pallas as pl
from jax.experimental.pallas import tpu as pltpu

F32 = jnp.float32
BF16 = jnp.bfloat16

D_MODEL = 1024
N_HEADS = 16
N_GROUPS = 2
HEADS_PER_GROUP = N_HEADS // N_GROUPS
PAIRS_PER_GROUP = HEADS_PER_GROUP // 2
HEAD_DIM = 64
NSA_WIDTH = N_HEADS * HEAD_DIM
KV_WIDTH = N_GROUPS * HEAD_DIM
CMP_BLOCK = 32
CMP_STRIDE = 16
CMP_HIDDEN = 256
SEL_BLOCK = 64
N_SELECT = 8
WINDOW = 512
Q_TILE = 128
CONV_KERNEL = 31
CONV_HALO = 32
REL_BUCKETS = 32
REL_MAX_DIST = 128
EPS = 1e-6
NEG = -1e30
FORCE_SCORE = 1e6
LOG2E = math.log2(math.e)
LANES = 128
SUBLANES = 8
ONES_ROWS = 16
COLS = PAIRS_PER_GROUP * Q_TILE
COL_TILE = 1024
N_COL_TILES = 8
WIN_TILES = WINDOW // Q_TILE + 1
VMEM_LIMIT = 56 * 1024 * 1024


def _dot(a, b):
    return jnp.dot(a, b, preferred_element_type=F32)


def _dot_nt(a, b):
    return lax.dot_general(a, b, (((1,), (1,)), ((), ())), preferred_element_type=F32)


def _sigmoid(x):
    return 1.0 / (1.0 + jnp.exp(-x))


def _silu(x):
    return x * _sigmoid(x)


def _proj_kernel(x_ref, g_ref, w_ref, kv_ref, main_ref, h_ref):
    j = pl.program_id(1)

    @pl.when(j == 0)
    def _():
        x = x_ref[...]
        ms = jnp.mean(x * x, axis=-1, keepdims=True)
        h_ref[...] = ((x * lax.rsqrt(ms + EPS)) * g_ref[...]).astype(BF16)

    acc = _dot(h_ref[...], w_ref[...])

    @pl.when(j == 0)
    def _():
        kv_ref[...] = acc

    @pl.when(j > 0)
    def _():
        main_ref[...] = acc.astype(BF16)


def _input_projection(x2, g, w_perm, tm):
    m = x2.shape[0]
    return pl.pallas_call(
        _proj_kernel,
        grid=(m // tm, N_COL_TILES),
        in_specs=[
            pl.BlockSpec((tm, D_MODEL), lambda i, j: (i, 0)),
            pl.BlockSpec((1, D_MODEL), lambda i, j: (0, 0)),
            pl.BlockSpec((D_MODEL, COL_TILE), lambda i, j: (0, j)),
        ],
        out_specs=[
            pl.BlockSpec((tm, COL_TILE), lambda i, j: (i, 0)),
            pl.BlockSpec((tm, COL_TILE), lambda i, j: (i, jnp.maximum(j - 1, 0))),
        ],
        out_shape=[
            jax.ShapeDtypeStruct((m, COL_TILE), F32),
            jax.ShapeDtypeStruct((m, (N_COL_TILES - 1) * COL_TILE), BF16),
        ],
        scratch_shapes=[pltpu.VMEM((tm, D_MODEL), BF16)],
        compiler_params=pltpu.CompilerParams(
            dimension_semantics=("parallel", "arbitrary"), vmem_limit_bytes=VMEM_LIMIT),
        name="input_projection",
    )(x2, g, w_perm)


def _compress_kernel(kf_ref, vf_ref, pk_ref, pv_ref, w1k_ref, w1v_ref, w2k_ref, w2v_ref,
                     kc_ref, vc_ref):
    def one(f_ref, pos_ref, w1_ref, w2_ref, o_ref):
        f = f_ref[0]
        n = f.shape[0]
        hw = N_GROUPS * CMP_HIDDEN
        first = _dot((f + pos_ref[0:1, :]).astype(BF16), w1_ref[:, 0:hw])
        second = _dot((f + pos_ref[1:2, :]).astype(BF16), w1_ref[:, hw:2 * hw])
        hid = first + pltpu.roll(second, n - 1, axis=0)
        o_ref[0] = _dot(_silu(hid).astype(BF16), w2_ref[...])

    one(kf_ref, pk_ref, w1k_ref, w2k_ref, kc_ref)
    one(vf_ref, pv_ref, w1v_ref, w2v_ref, vc_ref)


def _compress(kf, vf, pk, pv, w1k, w1v, w2k, w2v):
    b, n, width = kf.shape
    const = lambda shape: pl.BlockSpec(shape, lambda i: (0,) * len(shape))
    row = pl.BlockSpec((1, n, width), lambda i: (i, 0, 0))
    out = pl.BlockSpec((1, n, LANES), lambda i: (i, 0, 0))
    return pl.pallas_call(
        _compress_kernel,
        grid=(b,),
        in_specs=[row, row, const(pk.shape), const(pv.shape), const(w1k.shape), const(w1v.shape),
                  const(w2k.shape), const(w2v.shape)],
        out_specs=[out, out],
        out_shape=[jax.ShapeDtypeStruct((b, n, LANES), F32)] * 2,
        compiler_params=pltpu.CompilerParams(
            dimension_semantics=("parallel",), vmem_limit_bytes=VMEM_LIMIT),
        name="nsa_compress",
    )(kf, vf, pk, pv, w1k, w1v, w2k, w2v)


def _attn_kernel(q_ref, ks_ref, vs_ref, kw_ref, vw_ref, gt_ref, kc_ref, vc_ref, bc_ref, tw_ref,
                 ovt_ref, bx_ref, o_ref,
                 ksv, kwv, kcv, vst, vwt, vct, rhs, acc, obuf, s_a, s_b, s_w, *, n_top):
    qi = pl.program_id(1)
    n_kt = kwv.shape[1]
    n_sel = ovt_ref.shape[0]
    variants = 2 * N_GROUPS

    @pl.when(qi == 0)
    def _prepare_kv():
        def halves(k):
            lo = lax.broadcasted_iota(jnp.int32, k.shape, 1) < HEAD_DIM
            kr = pltpu.roll(k, HEAD_DIM, axis=1)
            z = jnp.zeros_like(k)
            c = lambda a: a.astype(BF16)
            return ((c(jnp.where(lo, k, z)), c(jnp.where(lo, z, kr))),
                    (c(jnp.where(lo, kr, z)), c(jnp.where(lo, z, k))))

        def extra_lanes(shape, rows_per_tile, v, with_blocks):
            t = lax.broadcasted_iota(jnp.int32, shape, 0)
            r = lax.broadcasted_iota(jnp.int32, shape, 1)
            lane = lax.broadcasted_iota(jnp.int32, shape, 2)
            far = (lane >= n_sel + 2 * v) & (lane < n_sel + 2 * v + 2)
            if with_blocks:
                far = far | (lane == t * (rows_per_tile // SEL_BLOCK) + r // SEL_BLOCK)
            return jnp.where(far, 1.0, 0.0).astype(BF16)

        pair_shape = (n_kt // 2, 2 * Q_TILE, LANES)
        tile_shape = (n_kt, Q_TILE, LANES)
        k_sel, k_win, k_cmp = halves(ks_ref[...]), halves(kw_ref[...]), halves(kc_ref[0])
        for g in range(N_GROUPS):
            for v in range(2):
                rows = slice(v * 2 * Q_TILE, (v + 1) * 2 * Q_TILE)
                ksv[g, :, rows, 0:LANES] = k_sel[g][v].reshape(pair_shape)
                ksv[g, :, rows, LANES:2 * LANES] = extra_lanes(pair_shape, 2 * Q_TILE, v, True)
                rows = slice(v * Q_TILE, (v + 1) * Q_TILE)
                kwv[g, :, rows, 0:LANES] = k_win[g][v].reshape(tile_shape)
                kwv[g, :, rows, LANES:2 * LANES] = extra_lanes(tile_shape, Q_TILE, v, False)
                kcv[g, v * LANES:(v + 1) * LANES, :] = k_cmp[g][v]
        vst[:, :, HEAD_DIM:, :] = jnp.ones((N_GROUPS, n_kt // 2, ONES_ROWS, 2 * Q_TILE), BF16)
        vwt[:, :, HEAD_DIM:, :] = jnp.ones((N_GROUPS, n_kt, ONES_ROWS, Q_TILE), BF16)
        for kt in range(n_kt):
            rows = slice(kt * Q_TILE, (kt + 1) * Q_TILE)
            half = slice((kt % 2) * Q_TILE, (kt % 2 + 1) * Q_TILE)
            vs_t = vs_ref[rows, :].T.astype(BF16)
            vw_t = vw_ref[rows, :].T.astype(BF16)
            for g in range(N_GROUPS):
                vst[g, kt // 2, 0:HEAD_DIM, half] = vs_t[g * HEAD_DIM:(g + 1) * HEAD_DIM]
                vwt[g, kt, 0:HEAD_DIM, :] = vw_t[g * HEAD_DIM:(g + 1) * HEAD_DIM]
        vct[...] = vc_ref[0].T.astype(BF16)

    qt = q_ref[...]
    gsig_t = _sigmoid(gt_ref[...]).T

    def col_max(s):
        return jnp.max(s.reshape(s.shape[0] // 8, 8, COLS), axis=0)

    def group_rows(g):
        return slice(g * HEAD_DIM, (g + 1) * HEAD_DIM)

    def online(gv, s, vt, m):
        m_new = jnp.maximum(m, jnp.max(col_max(s), axis=0, keepdims=True))
        e = jnp.exp2(s - m_new)
        acc[gv] = acc[gv] * jnp.exp2(m - m_new) + _dot(vt, e.astype(BF16))
        return m_new

    def fresh_state():
        acc[...] = jnp.zeros(acc.shape, F32)
        return tuple(jnp.full((1, COLS), NEG, F32) for _ in range(variants))

    def finish(branch):
        for g in range(N_GROUPS):
            halves = []
            for v in range(2):
                gv = g * 2 + v
                l = acc[gv, HEAD_DIM:HEAD_DIM + 1, :]
                halves.append(acc[gv, 0:HEAD_DIM, :] * (1.0 / jnp.maximum(l, 1e-30)))
            obuf[branch, g] = jnp.concatenate(halves, axis=0)

    for g in range(N_GROUPS):
        q4 = jnp.concatenate(
            [qt[:, (g * PAIRS_PER_GROUP + p) * LANES:(g * PAIRS_PER_GROUP + p + 1) * LANES]
             for p in range(PAIRS_PER_GROUP)], axis=0)
        q4 = (q4.astype(F32) * (HEAD_DIM ** -0.5 * LOG2E)).astype(BF16)

        s_all = _dot_nt(kcv[g], q4) + bc_ref[g, 0]
        psum_t = jnp.zeros((LANES, Q_TILE), F32)
        halves = []
        for v in range(2):
            s = s_all[v * LANES:(v + 1) * LANES]
            valid = s > 0.5 * NEG
            m = jnp.max(s, axis=0, keepdims=True)
            e = jnp.where(valid, jnp.exp2(s - m), 0.0)
            p = e * (1.0 / jnp.maximum(jnp.sum(e, axis=0, keepdims=True), 1e-30))
            for pp in range(PAIRS_PER_GROUP):
                psum_t = psum_t + p[:, pp * Q_TILE:(pp + 1) * Q_TILE]
            halves.append(_dot(vct[group_rows(g), :], p.astype(BF16)))
        obuf[0, g] = jnp.concatenate(halves, axis=0)

        p_hi = psum_t.astype(BF16)
        r1 = psum_t - p_hi.astype(F32)
        p_mid = r1.astype(BF16)
        p_lo = (r1 - p_mid.astype(F32)).astype(BF16)
        ovt = ovt_ref[...]
        imp_t = _dot(ovt, p_hi) + _dot(ovt, p_mid) + _dot(ovt, p_lo)

        j_idx = lax.broadcasted_iota(jnp.int32, (n_sel, Q_TILE), 0)
        r_idx = lax.broadcasted_iota(jnp.int32, (n_sel, Q_TILE), 1)
        blk_t = qi * (Q_TILE // SEL_BLOCK) + r_idx // SEL_BLOCK
        valid_blk = j_idx <= blk_t
        forced = (j_idx == 0) | (j_idx == blk_t) | (j_idx == blk_t - 1)
        prio = jnp.where(forced, FORCE_SCORE, imp_t)
        prio = jnp.where(valid_blk, prio, -FORCE_SCORE)
        rank = jnp.zeros((n_sel, Q_TILE), F32)
        for jj in range(n_sel):
            row = prio[jj:jj + 1, :]
            beats = (row > prio) | ((row == prio) & (j_idx > jj))
            rank = rank + jnp.where(beats, 1.0, 0.0)
        drop_t = jnp.where((rank < n_top) & valid_blk, 0.0, NEG)
        drop_t = jnp.concatenate([drop_t, jnp.zeros((LANES - n_sel, Q_TILE), F32)], axis=0)
        drop = drop_t.T.astype(BF16)
        rhs[g, :, 0:LANES] = q4
        rhs[g, :, LANES:2 * LANES] = jnp.concatenate([drop] * PAIRS_PER_GROUP, axis=0) + bx_ref[g]

    def logits_into(buf, i):
        for g in range(N_GROUPS):
            buf[g] = _dot_nt(ksv[g, i], rhs[g])

    def consume(buf, i, state, near):
        ms = list(state)
        for g in range(N_GROUPS):
            for v in range(2):
                gv = g * 2 + v
                if near:
                    parts = []
                    for h in range(2):
                        dd = jnp.clip(qi - (2 * i + h), 0, 2)
                        r0 = (2 * v + h) * Q_TILE
                        parts.append(buf[g, r0:r0 + Q_TILE, :] + tw_ref[g, v, dd])
                    s = jnp.concatenate(parts, axis=0)
                else:
                    s = buf[g, v * 2 * Q_TILE:(v + 1) * 2 * Q_TILE, :]
                ms[gv] = online(gv, s, vst[g, i], ms[gv])
        return tuple(ms)

    first = jnp.maximum(qi - (WIN_TILES - 1), 0)
    tiles = []
    for t in range(WIN_TILES):
        dd = qi - (first + t)
        table = jnp.where(dd >= 0, dd, WIN_TILES)
        tiles.append((jnp.minimum(first + t, n_kt - 1), table))
    for g in range(N_GROUPS):
        mx = [jnp.full((8, COLS), NEG, F32) for _ in range(2)]
        for t, (kt, table) in enumerate(tiles):
            s_all = _dot_nt(kwv[g, kt], rhs[g])
            for v in range(2):
                s = s_all[v * Q_TILE:(v + 1) * Q_TILE] + tw_ref[g, v, table]
                s_w[g, t, v * Q_TILE:(v + 1) * Q_TILE, :] = s
                mx[v] = jnp.maximum(mx[v], col_max(s))
        halves = []
        for v in range(2):
            m = jnp.max(mx[v], axis=0, keepdims=True)
            o = jnp.zeros((HEAD_DIM + ONES_ROWS, COLS), F32)
            for t, (kt, _) in enumerate(tiles):
                e = jnp.exp2(s_w[g, t, v * Q_TILE:(v + 1) * Q_TILE, :] - m)
                o = o + _dot(vwt[g, kt], e.astype(BF16))
            l = o[HEAD_DIM:HEAD_DIM + 1]
            halves.append(o[0:HEAD_DIM] * (1.0 / jnp.maximum(l, 1e-30)))
        obuf[2, g] = jnp.concatenate(halves, axis=0)

    logits_into(s_a, 0)

    n_far = jnp.maximum(qi - 1, 0) // 2

    def far_two(j, state):
        logits_into(s_b, 2 * j + 1)
        state = consume(s_a, 2 * j, state, False)
        logits_into(s_a, 2 * j + 2)
        return consume(s_b, 2 * j + 1, state, False)

    state = lax.fori_loop(0, n_far // 2, far_two, fresh_state())
    state = lax.fori_loop(0, n_far % 2, lambda _, st: consume(s_a, n_far - 1, st, False), state)

    def near_pair(i, state):
        logits_into(s_b, i)
        return consume(s_b, i, state, True)

    lax.fori_loop(n_far, (qi + 2) // 2, near_pair, state)
    finish(1)

    for g in range(N_GROUPS):
        for p in range(PAIRS_PER_GROUP):
            halves = []
            for v in range(2):
                tot = jnp.zeros((HEAD_DIM, Q_TILE), F32)
                for br in range(3):
                    c = br * N_HEADS + g * HEADS_PER_GROUP + 2 * p + v
                    tot = tot + gsig_t[c:c + 1, :] * obuf[br, g, v * HEAD_DIM:(v + 1) * HEAD_DIM,
                                                          p * Q_TILE:(p + 1) * Q_TILE]
                halves.append(tot)
            col = (g * PAIRS_PER_GROUP + p) * LANES
            o_ref[:, col:col + LANES] = jnp.concatenate(halves, axis=0).T.astype(BF16)


def _attention(proj_main, proj_kv, kcmp, vcmp, bias_c, bias_w, ovt, far_lanes, b, s, n_top):
    n_q = s // Q_TILE
    once = pl.Buffered(1)
    kv_col = lambda c: pl.BlockSpec((s, LANES), lambda bi, qi: (bi, c))
    cmp_spec = pl.BlockSpec((1, LANES, LANES), lambda bi, qi: (bi, 0, 0))
    return pl.pallas_call(
        functools.partial(_attn_kernel, n_top=n_top),
        grid=(b, n_q),
        in_specs=[
            pl.BlockSpec((Q_TILE, NSA_WIDTH), lambda bi, qi: (bi * n_q + qi, 0)),
            kv_col(2), kv_col(3), kv_col(4), kv_col(5),
            pl.BlockSpec((Q_TILE, LANES), lambda bi, qi: (bi * n_q + qi, 6)),
            cmp_spec, cmp_spec,
            pl.BlockSpec((N_GROUPS, 1, 2 * LANES, COLS), lambda bi, qi: (0, qi, 0, 0)),
            pl.BlockSpec(bias_w.shape, lambda bi, qi: (0,) * 5, pipeline_mode=once),
            pl.BlockSpec(ovt.shape, lambda bi, qi: (0, 0), pipeline_mode=once),
            pl.BlockSpec(far_lanes.shape, lambda bi, qi: (0, 0, 0), pipeline_mode=once),
        ],
        out_specs=pl.BlockSpec((Q_TILE, NSA_WIDTH), lambda bi, qi: (bi * n_q + qi, 0)),
        out_shape=jax.ShapeDtypeStruct((b * s, NSA_WIDTH), BF16),
        scratch_shapes=[
            pltpu.VMEM((N_GROUPS, n_q // 2, 4 * Q_TILE, 2 * LANES), BF16),
            pltpu.VMEM((N_GROUPS, n_q, 2 * Q_TILE, 2 * LANES), BF16),
            pltpu.VMEM((N_GROUPS, 2 * LANES, LANES), BF16),
            pltpu.VMEM((N_GROUPS, n_q // 2, HEAD_DIM + ONES_ROWS, 2 * Q_TILE), BF16),
            pltpu.VMEM((N_GROUPS, n_q, HEAD_DIM + ONES_ROWS, Q_TILE), BF16),
            pltpu.VMEM((LANES, LANES), BF16),
            pltpu.VMEM((N_GROUPS, COLS, 2 * LANES), BF16),
            pltpu.VMEM((2 * N_GROUPS, HEAD_DIM + ONES_ROWS, COLS), F32),
            pltpu.VMEM((3, N_GROUPS, LANES, COLS), F32),
            pltpu.VMEM((N_GROUPS, 4 * Q_TILE, COLS), F32),
            pltpu.VMEM((N_GROUPS, 4 * Q_TILE, COLS), F32),
            pltpu.VMEM((N_GROUPS, WIN_TILES, 2 * Q_TILE, COLS), F32),
        ],
        compiler_params=pltpu.CompilerParams(
            dimension_semantics=("parallel", "arbitrary"), vmem_limit_bytes=VMEM_LIMIT),
        name="nsa_attention",
    )(proj_main, proj_kv, proj_kv, proj_kv, proj_kv, proj_kv, kcmp, vcmp, bias_c, bias_w, ovt,
      far_lanes)


def _merge_kernel(zn_ref, a_ref, b_ref, zc_ref, gc_ref, gn_ref, ah_ref, bh_ref, on_ref, x_ref,
                  cw_ref, cb_ref, lg_ref, lb_ref, wcp_ref, wnp_ref, wo_ref, gf_ref, out_ref, uext,
                  conv, shifted):
    i = pl.program_id(1)
    ts = a_ref.shape[0]
    f = lambda r: r[...].astype(F32)

    n_cblk = D_MODEL // LANES
    u_halo = jnp.where(i > 0, f(ah_ref) * _sigmoid(f(bh_ref)), 0.0)
    u = f(a_ref) * _sigmoid(f(b_ref))
    for cblk in range(n_cblk):
        cols = slice(cblk * LANES, (cblk + 1) * LANES)
        uext[cblk, 0:CONV_HALO, :] = u_halo[:, cols]
        uext[cblk, CONV_HALO:CONV_HALO + ts, :] = u[:, cols]
        uext[cblk, CONV_HALO + ts:, :] = jnp.zeros((SUBLANES, LANES), F32)

    lead = CONV_HALO - (CONV_KERNEL - 1)
    half = ts // 2

    def conv_block(cblk, carry):
        for shift in range(SUBLANES):
            shifted[shift] = uext[cblk, shift:shift + ts + CONV_HALO, :]
        w = cw_ref[cblk]
        for h in range(2):
            c = jnp.broadcast_to(cb_ref[cblk], (half, LANES))
            for shift in range(SUBLANES):
                xs = shifted[shift, h * half:h * half + half + CONV_HALO, :]
                for j in range(CONV_KERNEL):
                    if (lead + j) % SUBLANES == shift:
                        base = lead + j - shift
                        c = c + w[j:j + 1, :] * xs[base:base + half]
            conv[cblk, h * half:(h + 1) * half, :] = c
        return carry

    lax.fori_loop(0, n_cblk, conv_block, 0)
    c = jnp.concatenate([conv[cblk] for cblk in range(n_cblk)], axis=1)

    mu = jnp.mean(c, axis=-1, keepdims=True)
    cc = c - mu
    var = jnp.mean(cc * cc, axis=-1, keepdims=True)
    y = (cc * lax.rsqrt(var + EPS)) * lg_ref[...] + lb_ref[...]
    conv_act = _silu(y) * _silu(f(zc_ref))
    y_conv = _dot(conv_act.astype(BF16), wcp_ref[...])

    nsa_act = f(on_ref) * _silu(f(zn_ref))
    y_nsa = _dot(nsa_act.astype(BF16), wnp_ref[...])

    merged = _sigmoid(f(gc_ref)) * y_conv + _sigmoid(f(gn_ref)) * y_nsa
    xo = x_ref[...] + _dot(merged.astype(BF16), wo_ref[...])
    ms = jnp.mean(xo * xo, axis=-1, keepdims=True)
    out_ref[...] = (xo * lax.rsqrt(ms + EPS)) * gf_ref[...]


def _merge(proj_main, o_nsa, x2, cw, cb, lg, lb, wcp, wnp, wo, gf, b, s, ts):
    n_t = s // ts
    n_cblk = D_MODEL // LANES
    halo_per_tile = ts // CONV_HALO
    cw = jnp.transpose(cw.reshape(cw.shape[0], n_cblk, LANES), (1, 0, 2))
    cb = cb.reshape(n_cblk, 1, LANES)
    col = lambda c: pl.BlockSpec((ts, COL_TILE), lambda bi, ti: (bi * n_t + ti, c))
    halo = lambda c: pl.BlockSpec(
        (CONV_HALO, COL_TILE),
        lambda bi, ti: (jnp.maximum((bi * n_t + ti) * halo_per_tile - 1, 0), c))
    const = lambda a: pl.BlockSpec(a.shape, lambda bi, ti: (0,) * a.ndim)
    rowblk = pl.BlockSpec((ts, D_MODEL), lambda bi, ti: (bi * n_t + ti, 0))
    return pl.pallas_call(
        _merge_kernel,
        grid=(b, n_t),
        in_specs=[col(1), col(2), col(3), col(4), col(5), col(6), halo(2), halo(3), rowblk, rowblk,
                  const(cw), const(cb), const(lg), const(lb), const(wcp), const(wnp), const(wo),
                  const(gf)],
        out_specs=rowblk,
        out_shape=jax.ShapeDtypeStruct((b * s, D_MODEL), F32),
        scratch_shapes=[pltpu.VMEM((n_cblk, ts + CONV_HALO + SUBLANES, LANES), F32),
                        pltpu.VMEM((n_cblk, ts, LANES), F32),
                        pltpu.VMEM((SUBLANES, ts + CONV_HALO, LANES), F32)],
        compiler_params=pltpu.CompilerParams(
            dimension_semantics=("parallel", "arbitrary"), vmem_limit_bytes=VMEM_LIMIT),
        name="conv_merge",
    )(proj_main, proj_main, proj_main, proj_main, proj_main, proj_main, proj_main, proj_main,
      o_nsa, x2, cw, cb, lg, lb, wcp, wnp, wo, gf)


def _t5_bucket_np(rel):
    rel = np.maximum(rel, 0)
    max_exact = REL_BUCKETS // 2
    relf = np.maximum(rel, 1).astype(np.float32)
    large = max_exact + (np.log(relf / np.float32(max_exact))
                         / np.float32(np.log(REL_MAX_DIST / max_exact))
                         * np.float32(REL_BUCKETS - max_exact)).astype(np.int32)
    large = np.minimum(large, REL_BUCKETS - 1)
    return np.where(rel < max_exact, rel, large)


def _pair_head_index():
    g = np.arange(N_GROUPS)[:, None, None]
    v = np.arange(2)[None, :, None]
    p = np.arange(PAIRS_PER_GROUP)[None, None, :]
    return g * HEADS_PER_GROUP + 2 * p + v


def _bias_lookup(rel_bias, rel):
    bucket = _t5_bucket_np(rel).reshape(-1)
    onehot = (jnp.arange(REL_BUCKETS)[:, None] == jnp.asarray(bucket)[None, :]).astype(F32)
    vals = jnp.dot(rel_bias.astype(F32).T, onehot, precision=lax.Precision.HIGHEST)
    vals = vals.reshape((N_HEADS,) + rel.shape)
    head = _pair_head_index()
    return jnp.stack([vals[h] for h in head.reshape(-1)]).reshape(head.shape + rel.shape)


def _bias_tables(rel_bias, s, n_sel):
    r = np.arange(Q_TILE)[:, None]
    c = np.arange(LANES)[None, :]
    far = rel_bias.astype(F32)[REL_BUCKETS - 1][_pair_head_index()]
    rel_w = np.stack([dd * Q_TILE + r - c for dd in range(WIN_TILES)])
    assert (_t5_bucket_np(rel_w[2:]) == REL_BUCKETS - 1).all()
    ok_w = (rel_w >= 0) & (rel_w < WINDOW)
    delta = (_bias_lookup(rel_bias, rel_w) - far[..., None, None, None]) * LOG2E
    tw = jnp.where(ok_w, delta, NEG)
    tw = jnp.concatenate([tw, jnp.full_like(tw[:, :, :, :1], NEG)], axis=3)
    tw = jnp.transpose(tw, (0, 1, 3, 5, 2, 4))
    tw = tw.reshape(tw.shape[:4] + (COLS,))
    t = np.arange(s)[:, None]
    rel_c = t - (c * CMP_STRIDE + CMP_BLOCK - 1)
    ok_c = (rel_c >= 0) & (c < (s - CMP_BLOCK) // CMP_STRIDE + 1)
    bc = jnp.where(ok_c, _bias_lookup(rel_bias, rel_c) * LOG2E, NEG)
    bc = bc.reshape(N_GROUPS, 2, PAIRS_PER_GROUP, s // Q_TILE, Q_TILE, LANES)
    bc = jnp.transpose(bc, (0, 3, 1, 5, 2, 4))
    bc = bc.reshape(N_GROUPS, s // Q_TILE, 2 * LANES, COLS)
    far2 = far * LOG2E
    hi = far2.astype(BF16)
    lo = (far2 - hi.astype(F32)).astype(BF16)
    pieces = jnp.stack([hi[:, 0], lo[:, 0], hi[:, 1], lo[:, 1]], axis=-1)
    pieces = jnp.broadcast_to(pieces[:, :, None, :], (N_GROUPS, PAIRS_PER_GROUP, Q_TILE, 4))
    far_lanes = jnp.pad(pieces.reshape(N_GROUPS, COLS, 4),
                        ((0, 0), (0, 0), (n_sel, LANES - n_sel - 4)))
    return tw, bc, far_lanes


def _overlap_t(s):
    n_cmp = (s - CMP_BLOCK) // CMP_STRIDE + 1
    cs = np.arange(LANES) * CMP_STRIDE
    ss = np.arange(s // SEL_BLOCK) * SEL_BLOCK
    ovt = ((cs[None, :] <= ss[:, None] + SEL_BLOCK - 1) & (cs[None, :] + CMP_BLOCK - 1 >= ss[:, None])
           & (np.arange(LANES)[None, :] < n_cmp))
    return jnp.asarray(ovt, BF16)


def _permute_w_in(w):
    sizes = [NSA_WIDTH] + [KV_WIDTH] * 6 + [3 * N_HEADS, NSA_WIDTH, 2 * D_MODEL, D_MODEL, 2 * D_MODEL]
    offs = np.concatenate([[0], np.cumsum(sizes)])
    seg = lambda i: w[:, offs[i]:offs[i + 1]]
    pad = jnp.zeros((D_MODEL, COL_TILE - 6 * KV_WIDTH - 3 * N_HEADS), w.dtype)
    cols = [seg(i) for i in range(1, 8)] + [pad, seg(0), seg(8), seg(9), seg(10), seg(11)]
    return jnp.concatenate(cols, axis=1).astype(BF16)


def _compress_weights(pos, w1, w2):
    half = CMP_BLOCK // 2
    eye = jnp.eye(N_GROUPS, dtype=F32)
    w1r = w1.reshape(CMP_BLOCK, HEAD_DIM, CMP_HIDDEN)
    blk = lambda part: jnp.einsum('idn,gh->igdhn', part, eye).reshape(
        half * KV_WIDTH, N_GROUPS * CMP_HIDDEN)
    w1d = jnp.concatenate([blk(w1r[:half]), blk(w1r[half:])], axis=1).astype(BF16)
    w2d = jnp.einsum('nd,gh->gnhd', w2, eye).reshape(N_GROUPS * CMP_HIDDEN, KV_WIDTH).astype(BF16)
    tilepos = lambda part: jnp.broadcast_to(part[:, None, :], (half, N_GROUPS, HEAD_DIM)).reshape(1, -1)
    posd = jnp.concatenate([tilepos(pos[:half]), tilepos(pos[half:])], axis=0).astype(F32)
    return posd, w1d, w2d


def kernel(x, norm_in_g, w_in, pos_ck, w_ck1, w_ck2, pos_cv, w_cv1, w_cv2, rel_bias, conv_w, conv_b,
           conv_ln_g, conv_ln_b, w_conv_proj, w_nsa_proj, w_out, norm_f_g):
    b, s, d = x.shape
    n_sel = s // SEL_BLOCK
    assert d == D_MODEL and w_in.shape[0] == 1, "single-layer block with D_MODEL=1024"
    assert s % (2 * Q_TILE) == 0 and s // CMP_STRIDE <= LANES and s >= WINDOW
    assert n_sel + 4 <= LANES
    m = b * s
    x2 = x.reshape(m, d)
    row = lambda a: a.reshape(1, -1).astype(F32)

    proj_kv, proj_main = _input_projection(x2, row(norm_in_g[0]), _permute_w_in(w_in[0]), min(1024, m))

    chunks = s // CMP_STRIDE
    kf = proj_kv[:, 0:KV_WIDTH].reshape(b, chunks, CMP_STRIDE * KV_WIDTH)
    vf = proj_kv[:, KV_WIDTH:2 * KV_WIDTH].reshape(b, chunks, CMP_STRIDE * KV_WIDTH)
    pk, w1k, w2k = _compress_weights(pos_ck[0], w_ck1[0], w_ck2[0])
    pv, w1v, w2v = _compress_weights(pos_cv[0], w_cv1[0], w_cv2[0])
    kcmp, vcmp = _compress(kf, vf, pk, pv, w1k, w1v, w2k, w2v)
    if chunks < LANES:
        padrows = ((0, 0), (0, LANES - chunks), (0, 0))
        kcmp, vcmp = jnp.pad(kcmp, padrows), jnp.pad(vcmp, padrows)

    bias_w, bias_c, far_lanes = _bias_tables(rel_bias, s, n_sel)
    o_nsa = _attention(proj_main, proj_kv, kcmp, vcmp, bias_c, bias_w, _overlap_t(s), far_lanes,
                       b, s, min(N_SELECT, n_sel))

    cw = jnp.pad(conv_w[0].astype(F32), ((0, CONV_HALO - CONV_KERNEL), (0, 0)))
    out = _merge(proj_main, o_nsa, x2, cw, row(conv_b[0]), row(conv_ln_g[0]), row(conv_ln_b[0]),
                 w_conv_proj[0].astype(BF16), w_nsa_proj[0].astype(BF16), w_out[0].astype(BF16),
                 row(norm_f_g), b, s, 256)
    return out.reshape(b, s, d)
```

```python
import functools
import math

import numpy as np
import jax
import jax.numpy as jnp
from jax import lax
from jax.experimental import pallas as pl
from jax.experimental.pallas import tpu as pltpu

F32 = jnp.float32
BF16 = jnp.bfloat16

D_MODEL = 1024
N_HEADS = 16
N_GROUPS = 2
HEADS_PER_GROUP = N_HEADS // N_GROUPS
PAIRS_PER_GROUP = HEADS_PER_GROUP // 2
HEAD_DIM = 64
NSA_WIDTH = N_HEADS * HEAD_DIM
KV_WIDTH = N_GROUPS * HEAD_DIM
CMP_BLOCK = 32
CMP_STRIDE = 16
CMP_HIDDEN = 256
SEL_BLOCK = 64
N_SELECT = 8
WINDOW = 512
Q_TILE = 128
CONV_KERNEL = 31
CONV_HALO = 32
REL_BUCKETS = 32
REL_MAX_DIST = 128
EPS = 1e-6
NEG = -1e30
FORCE_SCORE = 1e6
LOG2E = math.log2(math.e)
LANES = 128
SUBLANES = 8
ONES_ROWS = 16
COLS = PAIRS_PER_GROUP * Q_TILE
COL_TILE = 1024
N_COL_TILES = 8
WIN_TILES = WINDOW // Q_TILE + 1
VMEM_LIMIT = 56 * 1024 * 1024


def _dot(a, b):
    return jnp.dot(a, b, preferred_element_type=F32)


def _dot_nt(a, b):
    return lax.dot_general(a, b, (((1,), (1,)), ((), ())), preferred_element_type=F32)


def _sigmoid(x):
    return 1.0 / (1.0 + jnp.exp(-x))


def _silu(x):
    return x * _sigmoid(x)


def _proj_kernel(x_ref, g_ref, w_ref, kv_ref, main_ref, h_ref):
    j = pl.program_id(1)

    @pl.when(j == 0)
    def _():
        x = x_ref[...]
        ms = jnp.mean(x * x, axis=-1, keepdims=True)
        h_ref[...] = ((x * lax.rsqrt(ms + EPS)) * g_ref[...]).astype(BF16)

    acc = _dot(h_ref[...], w_ref[...])

    @pl.when(j == 0)
    def _():
        kv_ref[...] = acc

    @pl.when(j > 0)
    def _():
        main_ref[...] = acc.astype(BF16)


def _input_projection(x2, g, w_perm, tm):
    m = x2.shape[0]
    return pl.pallas_call(
        _proj_kernel,
        grid=(m // tm, N_COL_TILES),
        in_specs=[
            pl.BlockSpec((tm, D_MODEL), lambda i, j: (i, 0)),
            pl.BlockSpec((1, D_MODEL), lambda i, j: (0, 0)),
            pl.BlockSpec((D_MODEL, COL_TILE), lambda i, j: (0, j)),
        ],
        out_specs=[
            pl.BlockSpec((tm, COL_TILE), lambda i, j: (i, 0)),
            pl.BlockSpec((tm, COL_TILE), lambda i, j: (i, jnp.maximum(j - 1, 0))),
        ],
        out_shape=[
            jax.ShapeDtypeStruct((m, COL_TILE), F32),
            jax.ShapeDtypeStruct((m, (N_COL_TILES - 1) * COL_TILE), BF16),
        ],
        scratch_shapes=[pltpu.VMEM((tm, D_MODEL), BF16)],
        compiler_params=pltpu.CompilerParams(
            dimension_semantics=("parallel", "arbitrary"), vmem_limit_bytes=VMEM_LIMIT),
        name="input_projection",
    )(x2, g, w_perm)


def _compress_kernel(kf_ref, vf_ref, pk_ref, pv_ref, w1k_ref, w1v_ref, w2k_ref, w2v_ref,
                     kc_ref, vc_ref):
    def one(f_ref, pos_ref, w1_ref, w2_ref, o_ref):
        f = f_ref[0]
        n = f.shape[0]
        hw = N_GROUPS * CMP_HIDDEN
        first = _dot((f + pos_ref[0:1, :]).astype(BF16), w1_ref[:, 0:hw])
        second = _dot((f + pos_ref[1:2, :]).astype(BF16), w1_ref[:, hw:2 * hw])
        hid = first + pltpu.roll(second, n - 1, axis=0)
        o_ref[0] = _dot(_silu(hid).astype(BF16), w2_ref[...])

    one(kf_ref, pk_ref, w1k_ref, w2k_ref, kc_ref)
    one(vf_ref, pv_ref, w1v_ref, w2v_ref, vc_ref)


def _compress(kf, vf, pk, pv, w1k, w1v, w2k, w2v):
    b, n, width = kf.shape
    const = lambda shape: pl.BlockSpec(shape, lambda i: (0,) * len(shape))
    row = pl.BlockSpec((1, n, width), lambda i: (i, 0, 0))
    out = pl.BlockSpec((1, n, LANES), lambda i: (i, 0, 0))
    return pl.pallas_call(
        _compress_kernel,
        grid=(b,),
        in_specs=[row, row, const(pk.shape), const(pv.shape), const(w1k.shape), const(w1v.shape),
                  const(w2k.shape), const(w2v.shape)],
        out_specs=[out, out],
        out_shape=[jax.ShapeDtypeStruct((b, n, LANES), F32)] * 2,
        compiler_params=pltpu.CompilerParams(
            dimension_semantics=("parallel",), vmem_limit_bytes=VMEM_LIMIT),
        name="nsa_compress",
    )(kf, vf, pk, pv, w1k, w1v, w2k, w2v)


def _attn_kernel(q_ref, ks_ref, vs_ref, kw_ref, vw_ref, gt_ref, kc_ref, vc_ref, bc_ref, tw_ref,
                 ts_ref, ovt_ref, bx_ref, o_ref,
                 ksv, kwv, kcv, vst, vwt, vct, rhs, acc, obuf, s_a, s_b, s_c, s_w, s_n,
                 *, n_top):
    qi = pl.program_id(1)
    n_kt = kwv.shape[1]
    n_sel = ovt_ref.shape[0]
    variants = 2 * N_GROUPS

    @pl.when(qi == 0)
    def _prepare_kv():
        def halves(k):
            lo = lax.broadcasted_iota(jnp.int32, k.shape, 1) < HEAD_DIM
            kr = pltpu.roll(k, HEAD_DIM, axis=1)
            z = jnp.zeros_like(k)
            c = lambda a: a.astype(BF16)
            return ((c(jnp.where(lo, k, z)), c(jnp.where(lo, z, kr))),
                    (c(jnp.where(lo, kr, z)), c(jnp.where(lo, z, k))))

        def extra_lanes(shape, v):
            pair = lax.broadcasted_iota(jnp.int32, shape, 0)
            r = lax.broadcasted_iota(jnp.int32, shape, 1)
            lane = lax.broadcasted_iota(jnp.int32, shape, 2)
            one = ((lane >= n_sel + 2 * v) & (lane < n_sel + 2 * v + 2)
                   | (lane == pair * (2 * Q_TILE // SEL_BLOCK) + r // SEL_BLOCK))
            return jnp.where(one, 1.0, 0.0).astype(BF16)

        pair_shape = (n_kt // 2, 2 * Q_TILE, LANES)
        tile_shape = (n_kt, Q_TILE, LANES)
        k_sel, k_win, k_cmp = halves(ks_ref[...]), halves(kw_ref[...]), halves(kc_ref[0])
        for g in range(N_GROUPS):
            for v in range(2):
                rows = slice(v * 2 * Q_TILE, (v + 1) * 2 * Q_TILE)
                ksv[g, :, rows, 0:LANES] = k_sel[g][v].reshape(pair_shape)
                ksv[g, :, rows, LANES:2 * LANES] = extra_lanes(pair_shape, v)
                kwv[g, :, v * Q_TILE:(v + 1) * Q_TILE, :] = k_win[g][v].reshape(tile_shape)
                kcv[g, v * LANES:(v + 1) * LANES, :] = k_cmp[g][v]
        vst[:, :, HEAD_DIM:, :] = jnp.ones((N_GROUPS, n_kt // 2, ONES_ROWS, 2 * Q_TILE), BF16)
        vwt[:, :, HEAD_DIM:, :] = jnp.ones((N_GROUPS, n_kt, ONES_ROWS, Q_TILE), BF16)
        for kt in range(n_kt):
            rows = slice(kt * Q_TILE, (kt + 1) * Q_TILE)
            half = slice((kt % 2) * Q_TILE, (kt % 2 + 1) * Q_TILE)
            vs_t = vs_ref[rows, :].T.astype(BF16)
            vw_t = vw_ref[rows, :].T.astype(BF16)
            for g in range(N_GROUPS):
                vst[g, kt // 2, 0:HEAD_DIM, half] = vs_t[g * HEAD_DIM:(g + 1) * HEAD_DIM]
                vwt[g, kt, 0:HEAD_DIM, :] = vw_t[g * HEAD_DIM:(g + 1) * HEAD_DIM]
        vct[...] = vc_ref[0].T.astype(BF16)

    qt = q_ref[...]
    gsig_t = _sigmoid(gt_ref[...]).T

    def col_max(s):
        return jnp.max(s.reshape(s.shape[0] // 8, 8, COLS), axis=0)

    def group_rows(g):
        return slice(g * HEAD_DIM, (g + 1) * HEAD_DIM)

    def online(gv, s, vt, m):
        m_new = jnp.maximum(m, jnp.max(col_max(s), axis=0, keepdims=True))
        e = jnp.exp2(s - m_new)
        acc[gv] = acc[gv] * jnp.exp2(m - m_new) + _dot(vt, e.astype(BF16))
        return m_new

    def fresh_state():
        acc[...] = jnp.zeros(acc.shape, F32)
        return tuple(jnp.full((1, COLS), NEG, F32) for _ in range(variants))

    def finish(branch):
        for g in range(N_GROUPS):
            halves = []
            for v in range(2):
                gv = g * 2 + v
                l = acc[gv, HEAD_DIM:HEAD_DIM + 1, :]
                halves.append(acc[gv, 0:HEAD_DIM, :] * (1.0 / jnp.maximum(l, 1e-30)))
            obuf[branch, g] = jnp.concatenate(halves, axis=0)

    groups = range(N_GROUPS)
    for g in groups:
        q4 = jnp.concatenate(
            [qt[:, (g * PAIRS_PER_GROUP + p) * LANES:(g * PAIRS_PER_GROUP + p + 1) * LANES]
             for p in range(PAIRS_PER_GROUP)], axis=0)
        rhs[g, :, 0:LANES] = (q4.astype(F32) * (HEAD_DIM ** -0.5 * LOG2E)).astype(BF16)

    s_cmp = [_dot_nt(kcv[g], rhs[g, :, 0:LANES]) + bc_ref[g, 0] for g in groups]
    psum_t = []
    for g in groups:
        tot = jnp.zeros((LANES, Q_TILE), F32)
        halves = []
        for v in range(2):
            s = s_cmp[g][v * LANES:(v + 1) * LANES]
            valid = s > 0.5 * NEG
            m = jnp.max(s, axis=0, keepdims=True)
            e = jnp.where(valid, jnp.exp2(s - m), 0.0)
            p = e * (1.0 / jnp.maximum(jnp.sum(e, axis=0, keepdims=True), 1e-30))
            for pp in range(PAIRS_PER_GROUP):
                tot = tot + p[:, pp * Q_TILE:(pp + 1) * Q_TILE]
            halves.append(_dot(vct[group_rows(g), :], p.astype(BF16)))
        obuf[0, g] = jnp.concatenate(halves, axis=0)
        psum_t.append(tot)

    ovt = ovt_ref[...]
    imp_t = []
    for g in groups:
        p_hi = psum_t[g].astype(BF16)
        r1 = psum_t[g] - p_hi.astype(F32)
        p_mid = r1.astype(BF16)
        p_lo = (r1 - p_mid.astype(F32)).astype(BF16)
        imp_t.append(_dot(ovt, p_hi) + _dot(ovt, p_mid) + _dot(ovt, p_lo))

    j_idx = lax.broadcasted_iota(jnp.int32, (n_sel, Q_TILE), 0)
    r_idx = lax.broadcasted_iota(jnp.int32, (n_sel, Q_TILE), 1)
    blk_t = qi * (Q_TILE // SEL_BLOCK) + r_idx // SEL_BLOCK
    valid_blk = j_idx <= blk_t
    forced = (j_idx == 0) | (j_idx == blk_t) | (j_idx == blk_t - 1)
    prio = [jnp.where(valid_blk, jnp.where(forced, FORCE_SCORE, imp_t[g]), -FORCE_SCORE) for g in groups]
    rank = [jnp.zeros((n_sel, Q_TILE), F32) for _ in groups]
    for jj in range(n_sel):
        later = j_idx > jj
        for g in groups:
            row = prio[g][jj:jj + 1, :]
            beats = (row > prio[g]) | ((row == prio[g]) & later)
            rank[g] = rank[g] + jnp.where(beats, 1.0, 0.0)
    for g in groups:
        drop_t = jnp.where((rank[g] < n_top) & valid_blk, 0.0, NEG)
        drop_t = jnp.concatenate([drop_t, jnp.zeros((LANES - n_sel, Q_TILE), F32)], axis=0)
        drop = drop_t.T.astype(BF16)
        rhs[g, :, LANES:2 * LANES] = jnp.concatenate([drop] * PAIRS_PER_GROUP, axis=0) + bx_ref[g]

    def logits_into(buf, i):
        for g in range(N_GROUPS):
            buf[g] = _dot_nt(ksv[g, i], rhs[g])

    def consume(buf, i, state):
        ms = list(state)
        for g in range(N_GROUPS):
            for v in range(2):
                gv = g * 2 + v
                ms[gv] = online(gv, buf[g, v * 2 * Q_TILE:(v + 1) * 2 * Q_TILE, :], vst[g, i], ms[gv])
        return tuple(ms)

    n_all = (qi + 2) // 2
    n_far = jnp.maximum(qi - 1, 0) // 2

    first = jnp.maximum(qi - (WIN_TILES - 1), 0)
    tiles = []
    for t in range(WIN_TILES):
        dd = qi - (first + t)
        table = jnp.where(dd >= 0, dd, WIN_TILES)
        tiles.append((jnp.minimum(first + t, n_kt - 1), table))
    m_win = []
    for g in range(N_GROUPS):
        mx = [jnp.full((8, COLS), NEG, F32) for _ in range(2)]
        for t, (kt, table) in enumerate(tiles):
            s_all = _dot_nt(kwv[g, kt], rhs[g, :, 0:LANES])
            for v in range(2):
                s = s_all[v * Q_TILE:(v + 1) * Q_TILE] + tw_ref[g, v, table]
                s_w[g, t, v * Q_TILE:(v + 1) * Q_TILE, :] = s
                mx[v] = jnp.maximum(mx[v], col_max(s))
        m_win += [jnp.max(mx[v], axis=0, keepdims=True) for v in range(2)]

    def near_logits(slot, g):
        i = jnp.maximum(n_all - 1 - slot, 0)
        s_all = _dot_nt(ksv[g, i], rhs[g])
        for v in range(2):
            for h in range(2):
                dd = jnp.clip(qi - (2 * i + h), 0, 2)
                r0 = (2 * v + h) * Q_TILE
                s_n[slot, g, r0:r0 + Q_TILE, :] = s_all[r0:r0 + Q_TILE] + ts_ref[g, v, dd]

    def first_far_logits(g):
        s_a[g] = _dot_nt(ksv[g, 0], rhs[g])

    matmul_units = ([functools.partial(near_logits, slot, g) for slot in range(2) for g in range(N_GROUPS)]
                    + [functools.partial(first_far_logits, g) for g in range(N_GROUPS)])
    o_win = [jnp.zeros((HEAD_DIM + ONES_ROWS, COLS), F32) for _ in range(variants)]
    exp_units = [(t, gv) for t in range(WIN_TILES) for gv in range(variants)]
    per_unit = -(-len(exp_units) // len(matmul_units))
    for k, unit in enumerate(matmul_units):
        unit()
        for t, gv in exp_units[k * per_unit:(k + 1) * per_unit]:
            g, v = divmod(gv, 2)
            e = jnp.exp2(s_w[g, t, v * Q_TILE:(v + 1) * Q_TILE, :] - m_win[gv])
            o_win[gv] = o_win[gv] + _dot(vwt[g, tiles[t][0]], e.astype(BF16))
    for g in range(N_GROUPS):
        halves = []
        for v in range(2):
            o = o_win[g * 2 + v]
            halves.append(o[0:HEAD_DIM] * (1.0 / jnp.maximum(o[HEAD_DIM:HEAD_DIM + 1], 1e-30)))
        obuf[2, g] = jnp.concatenate(halves, axis=0)

    def far_three(j, state):
        logits_into(s_b, 3 * j + 1)
        state = consume(s_a, 3 * j, state)
        logits_into(s_c, 3 * j + 2)
        state = consume(s_b, 3 * j + 1, state)
        logits_into(s_a, jnp.minimum(3 * j + 3, n_kt // 2 - 1))
        return consume(s_c, 3 * j + 2, state)

    def far_rest(k, state):
        @pl.when(k > 0)
        def _():
            logits_into(s_a, n_far - 1)
        return consume(s_a, n_far - n_far % 3 + k, state)

    state = lax.fori_loop(0, n_far // 3, far_three, fresh_state())
    state = lax.fori_loop(0, n_far % 3, far_rest, state)
    lax.fori_loop(0, n_all - n_far, lambda k, st: consume(s_n.at[k], n_all - 1 - k, st), state)
    finish(1)

    for g in range(N_GROUPS):
        for p in range(PAIRS_PER_GROUP):
            halves = []
            for v in range(2):
                tot = jnp.zeros((HEAD_DIM, Q_TILE), F32)
                for br in range(3):
                    c = br * N_HEADS + g * HEADS_PER_GROUP + 2 * p + v
                    tot = tot + gsig_t[c:c + 1, :] * obuf[br, g, v * HEAD_DIM:(v + 1) * HEAD_DIM,
                                                          p * Q_TILE:(p + 1) * Q_TILE]
                halves.append(tot)
            col = (g * PAIRS_PER_GROUP + p) * LANES
            o_ref[:, col:col + LANES] = jnp.concatenate(halves, axis=0).T.astype(BF16)


def _attention(proj_main, proj_kv, kcmp, vcmp, bias_c, bias_w, bias_s, ovt, far_lanes, b, s, n_top):
    n_q = s // Q_TILE
    once = pl.Buffered(1)
    kv_col = lambda c: pl.BlockSpec((s, LANES), lambda bi, qi: (bi, c))
    cmp_spec = pl.BlockSpec((1, LANES, LANES), lambda bi, qi: (bi, 0, 0))
    return pl.pallas_call(
        functools.partial(_attn_kernel, n_top=n_top),
        grid=(b, n_q),
        in_specs=[
            pl.BlockSpec((Q_TILE, NSA_WIDTH), lambda bi, qi: (bi * n_q + qi, 0)),
            kv_col(2), kv_col(3), kv_col(4), kv_col(5),
            pl.BlockSpec((Q_TILE, LANES), lambda bi, qi: (bi * n_q + qi, 6)),
            cmp_spec, cmp_spec,
            pl.BlockSpec((N_GROUPS, 1, 2 * LANES, COLS), lambda bi, qi: (0, qi, 0, 0)),
            pl.BlockSpec(bias_w.shape, lambda bi, qi: (0,) * 5, pipeline_mode=once),
            pl.BlockSpec(bias_s.shape, lambda bi, qi: (0,) * 5, pipeline_mode=once),
            pl.BlockSpec(ovt.shape, lambda bi, qi: (0, 0), pipeline_mode=once),
            pl.BlockSpec(far_lanes.shape, lambda bi, qi: (0, 0, 0), pipeline_mode=once),
        ],
        out_specs=pl.BlockSpec((Q_TILE, NSA_WIDTH), lambda bi, qi: (bi * n_q + qi, 0)),
        out_shape=jax.ShapeDtypeStruct((b * s, NSA_WIDTH), BF16),
        scratch_shapes=[
            pltpu.VMEM((N_GROUPS, n_q // 2, 4 * Q_TILE, 2 * LANES), BF16),
            pltpu.VMEM((N_GROUPS, n_q, 2 * Q_TILE, LANES), BF16),
            pltpu.VMEM((N_GROUPS, 2 * LANES, LANES), BF16),
            pltpu.VMEM((N_GROUPS, n_q // 2, HEAD_DIM + ONES_ROWS, 2 * Q_TILE), BF16),
            pltpu.VMEM((N_GROUPS, n_q, HEAD_DIM + ONES_ROWS, Q_TILE), BF16),
            pltpu.VMEM((LANES, LANES), BF16),
            pltpu.VMEM((N_GROUPS, COLS, 2 * LANES), BF16),
            pltpu.VMEM((2 * N_GROUPS, HEAD_DIM + ONES_ROWS, COLS), F32),
            pltpu.VMEM((3, N_GROUPS, LANES, COLS), F32),
            pltpu.VMEM((N_GROUPS, 4 * Q_TILE, COLS), F32),
            pltpu.VMEM((N_GROUPS, 4 * Q_TILE, COLS), F32),
            pltpu.VMEM((N_GROUPS, 4 * Q_TILE, COLS), F32),
            pltpu.VMEM((N_GROUPS, WIN_TILES, 2 * Q_TILE, COLS), F32),
            pltpu.VMEM((2, N_GROUPS, 4 * Q_TILE, COLS), F32),
        ],
        compiler_params=pltpu.CompilerParams(
            dimension_semantics=("parallel", "arbitrary"), vmem_limit_bytes=VMEM_LIMIT),
        name="nsa_attention",
    )(proj_main, proj_kv, proj_kv, proj_kv, proj_kv, proj_kv, kcmp, vcmp, bias_c, bias_w, bias_s,
      ovt, far_lanes)


def _merge_kernel(zn_ref, a_ref, b_ref, zc_ref, gc_ref, gn_ref, ah_ref, bh_ref, on_ref, x_ref,
                  cw_ref, cb_ref, lg_ref, lb_ref, wcp_ref, wnp_ref, wo_ref, gf_ref, out_ref, uext,
                  conv, shifted):
    i = pl.program_id(1)
    ts = a_ref.shape[0]
    f = lambda r: r[...].astype(F32)

    n_cblk = D_MODEL // LANES
    u_halo = jnp.where(i > 0, f(ah_ref) * _sigmoid(f(bh_ref)), 0.0)
    u = f(a_ref) * _sigmoid(f(b_ref))
    for cblk in range(n_cblk):
        cols = slice(cblk * LANES, (cblk + 1) * LANES)
        uext[cblk, 0:CONV_HALO, :] = u_halo[:, cols]
        uext[cblk, CONV_HALO:CONV_HALO + ts, :] = u[:, cols]
        uext[cblk, CONV_HALO + ts:, :] = jnp.zeros((SUBLANES, LANES), F32)

    lead = CONV_HALO - (CONV_KERNEL - 1)
    half = ts // 2

    def conv_block(cblk, carry):
        for shift in range(SUBLANES):
            shifted[shift] = uext[cblk, shift:shift + ts + CONV_HALO, :]
        w = cw_ref[cblk]
        for h in range(2):
            c = jnp.broadcast_to(cb_ref[cblk], (half, LANES))
            for shift in range(SUBLANES):
                xs = shifted[shift, h * half:h * half + half + CONV_HALO, :]
                for j in range(CONV_KERNEL):
                    if (lead + j) % SUBLANES == shift:
                        base = lead + j - shift
                        c = c + w[j:j + 1, :] * xs[base:base + half]
            conv[cblk, h * half:(h + 1) * half, :] = c
        return carry

    lax.fori_loop(0, n_cblk, conv_block, 0)
    c = jnp.concatenate([conv[cblk] for cblk in range(n_cblk)], axis=1)

    mu = jnp.mean(c, axis=-1, keepdims=True)
    cc = c - mu
    var = jnp.mean(cc * cc, axis=-1, keepdims=True)
    y = (cc * lax.rsqrt(var + EPS)) * lg_ref[...] + lb_ref[...]
    conv_act = _silu(y) * _silu(f(zc_ref))
    y_conv = _dot(conv_act.astype(BF16), wcp_ref[...])

    nsa_act = f(on_ref) * _silu(f(zn_ref))
    y_nsa = _dot(nsa_act.astype(BF16), wnp_ref[...])

    merged = _sigmoid(f(gc_ref)) * y_conv + _sigmoid(f(gn_ref)) * y_nsa
    xo = x_ref[...] + _dot(merged.astype(BF16), wo_ref[...])
    ms = jnp.mean(xo * xo, axis=-1, keepdims=True)
    out_ref[...] = (xo * lax.rsqrt(ms + EPS)) * gf_ref[...]


def _merge(proj_main, o_nsa, x2, cw, cb, lg, lb, wcp, wnp, wo, gf, b, s, ts):
    n_t = s // ts
    n_cblk = D_MODEL // LANES
    halo_per_tile = ts // CONV_HALO
    cw = jnp.transpose(cw.reshape(cw.shape[0], n_cblk, LANES), (1, 0, 2))
    cb = cb.reshape(n_cblk, 1, LANES)
    col = lambda c: pl.BlockSpec((ts, COL_TILE), lambda bi, ti: (bi * n_t + ti, c))
    halo = lambda c: pl.BlockSpec(
        (CONV_HALO, COL_TILE),
        lambda bi, ti: (jnp.maximum((bi * n_t + ti) * halo_per_tile - 1, 0), c))
    const = lambda a: pl.BlockSpec(a.shape, lambda bi, ti: (0,) * a.ndim)
    rowblk = pl.BlockSpec((ts, D_MODEL), lambda bi, ti: (bi * n_t + ti, 0))
    return pl.pallas_call(
        _merge_kernel,
        grid=(b, n_t),
        in_specs=[col(1), col(2), col(3), col(4), col(5), col(6), halo(2), halo(3), rowblk, rowblk,
                  const(cw), const(cb), const(lg), const(lb), const(wcp), const(wnp), const(wo),
                  const(gf)],
        out_specs=rowblk,
        out_shape=jax.ShapeDtypeStruct((b * s, D_MODEL), F32),
        scratch_shapes=[pltpu.VMEM((n_cblk, ts + CONV_HALO + SUBLANES, LANES), F32),
                        pltpu.VMEM((n_cblk, ts, LANES), F32),
                        pltpu.VMEM((SUBLANES, ts + CONV_HALO, LANES), F32)],
        compiler_params=pltpu.CompilerParams(
            dimension_semantics=("parallel", "arbitrary"), vmem_limit_bytes=VMEM_LIMIT),
        name="conv_merge",
    )(proj_main, proj_main, proj_main, proj_main, proj_main, proj_main, proj_main, proj_main,
      o_nsa, x2, cw, cb, lg, lb, wcp, wnp, wo, gf)


def _t5_bucket_np(rel):
    rel = np.maximum(rel, 0)
    max_exact = REL_BUCKETS // 2
    relf = np.maximum(rel, 1).astype(np.float32)
    large = max_exact + (np.log(relf / np.float32(max_exact))
                         / np.float32(np.log(REL_MAX_DIST / max_exact))
                         * np.float32(REL_BUCKETS - max_exact)).astype(np.int32)
    large = np.minimum(large, REL_BUCKETS - 1)
    return np.where(rel < max_exact, rel, large)


def _pair_head_index():
    g = np.arange(N_GROUPS)[:, None, None]
    v = np.arange(2)[None, :, None]
    p = np.arange(PAIRS_PER_GROUP)[None, None, :]
    return g * HEADS_PER_GROUP + 2 * p + v


def _bias_lookup(rel_bias, rel):
    bucket = _t5_bucket_np(rel).reshape(-1)
    onehot = (jnp.arange(REL_BUCKETS)[:, None] == jnp.asarray(bucket)[None, :]).astype(F32)
    vals = jnp.dot(rel_bias.astype(F32).T, onehot, precision=lax.Precision.HIGHEST)
    vals = vals.reshape((N_HEADS,) + rel.shape)
    head = _pair_head_index()
    return jnp.stack([vals[h] for h in head.reshape(-1)]).reshape(head.shape + rel.shape)


def _bias_tables(rel_bias, s, n_sel):
    r = np.arange(Q_TILE)[:, None]
    c = np.arange(LANES)[None, :]
    far = rel_bias.astype(F32)[REL_BUCKETS - 1][_pair_head_index()]
    rel_w = np.stack([dd * Q_TILE + r - c for dd in range(WIN_TILES)])
    assert (_t5_bucket_np(rel_w[2:]) == REL_BUCKETS - 1).all()
    ok_w = (rel_w >= 0) & (rel_w < WINDOW)
    bias_w = _bias_lookup(rel_bias, rel_w)

    def orient(t):
        t = jnp.transpose(t, (0, 1, 3, 5, 2, 4))
        return t.reshape(t.shape[:4] + (COLS,))

    tw = jnp.where(ok_w, bias_w * LOG2E, NEG)
    tw = orient(jnp.concatenate([tw, jnp.full_like(tw[:, :, :, :1], NEG)], axis=3))
    ts = jnp.where(rel_w[:3] >= 0, (bias_w[:, :, :, :3] - far[..., None, None, None]) * LOG2E, NEG)
    ts = orient(ts)
    t = np.arange(s)[:, None]
    rel_c = t - (c * CMP_STRIDE + CMP_BLOCK - 1)
    ok_c = (rel_c >= 0) & (c < (s - CMP_BLOCK) // CMP_STRIDE + 1)
    bc = jnp.where(ok_c, _bias_lookup(rel_bias, rel_c) * LOG2E, NEG)
    bc = bc.reshape(N_GROUPS, 2, PAIRS_PER_GROUP, s // Q_TILE, Q_TILE, LANES)
    bc = jnp.transpose(bc, (0, 3, 1, 5, 2, 4))
    bc = bc.reshape(N_GROUPS, s // Q_TILE, 2 * LANES, COLS)
    far2 = far * LOG2E
    hi = far2.astype(BF16)
    lo = (far2 - hi.astype(F32)).astype(BF16)
    pieces = jnp.stack([hi[:, 0], lo[:, 0], hi[:, 1], lo[:, 1]], axis=-1)
    pieces = jnp.broadcast_to(pieces[:, :, None, :], (N_GROUPS, PAIRS_PER_GROUP, Q_TILE, 4))
    far_lanes = jnp.pad(pieces.reshape(N_GROUPS, COLS, 4),
                        ((0, 0), (0, 0), (n_sel, LANES - n_sel - 4)))
    return tw, ts, bc, far_lanes


def _overlap_t(s):
    n_cmp = (s - CMP_BLOCK) // CMP_STRIDE + 1
    cs = np.arange(LANES) * CMP_STRIDE
    ss = np.arange(s // SEL_BLOCK) * SEL_BLOCK
    ovt = ((cs[None, :] <= ss[:, None] + SEL_BLOCK - 1) & (cs[None, :] + CMP_BLOCK - 1 >= ss[:, None])
           & (np.arange(LANES)[None, :] < n_cmp))
    return jnp.asarray(ovt, BF16)


def _permute_w_in(w):
    sizes = [NSA_WIDTH] + [KV_WIDTH] * 6 + [3 * N_HEADS, NSA_WIDTH, 2 * D_MODEL, D_MODEL, 2 * D_MODEL]
    offs = np.concatenate([[0], np.cumsum(sizes)])
    seg = lambda i: w[:, offs[i]:offs[i + 1]]
    pad = jnp.zeros((D_MODEL, COL_TILE - 6 * KV_WIDTH - 3 * N_HEADS), w.dtype)
    cols = [seg(i) for i in range(1, 8)] + [pad, seg(0), seg(8), seg(9), seg(10), seg(11)]
    return jnp.concatenate(cols, axis=1).astype(BF16)


def _compress_weights(pos, w1, w2):
    half = CMP_BLOCK // 2
    eye = jnp.eye(N_GROUPS, dtype=F32)
    w1r = w1.reshape(CMP_BLOCK, HEAD_DIM, CMP_HIDDEN)
    blk = lambda part: jnp.einsum('idn,gh->igdhn', part, eye).reshape(
        half * KV_WIDTH, N_GROUPS * CMP_HIDDEN)
    w1d = jnp.concatenate([blk(w1r[:half]), blk(w1r[half:])], axis=1).astype(BF16)
    w2d = jnp.einsum('nd,gh->gnhd', w2, eye).reshape(N_GROUPS * CMP_HIDDEN, KV_WIDTH).astype(BF16)
    tilepos = lambda part: jnp.broadcast_to(part[:, None, :], (half, N_GROUPS, HEAD_DIM)).reshape(1, -1)
    posd = jnp.concatenate([tilepos(pos[:half]), tilepos(pos[half:])], axis=0).astype(F32)
    return posd, w1d, w2d


def kernel(x, norm_in_g, w_in, pos_ck, w_ck1, w_ck2, pos_cv, w_cv1, w_cv2, rel_bias, conv_w, conv_b,
           conv_ln_g, conv_ln_b, w_conv_proj, w_nsa_proj, w_out, norm_f_g):
    b, s, d = x.shape
    n_sel = s // SEL_BLOCK
    assert d == D_MODEL and w_in.shape[0] == 1, "single-layer block with D_MODEL=1024"
    assert s % (2 * Q_TILE) == 0 and s // CMP_STRIDE <= LANES and s >= WINDOW
    assert n_sel + 4 <= LANES
    m = b * s
    x2 = x.reshape(m, d)
    row = lambda a: a.reshape(1, -1).astype(F32)

    proj_kv, proj_main = _input_projection(x2, row(norm_in_g[0]), _permute_w_in(w_in[0]), min(1024, m))

    chunks = s // CMP_STRIDE
    kf = proj_kv[:, 0:KV_WIDTH].reshape(b, chunks, CMP_STRIDE * KV_WIDTH)
    vf = proj_kv[:, KV_WIDTH:2 * KV_WIDTH].reshape(b, chunks, CMP_STRIDE * KV_WIDTH)
    pk, w1k, w2k = _compress_weights(pos_ck[0], w_ck1[0], w_ck2[0])
    pv, w1v, w2v = _compress_weights(pos_cv[0], w_cv1[0], w_cv2[0])
    kcmp, vcmp = _compress(kf, vf, pk, pv, w1k, w1v, w2k, w2v)
    if chunks < LANES:
        padrows = ((0, 0), (0, LANES - chunks), (0, 0))
        kcmp, vcmp = jnp.pad(kcmp, padrows), jnp.pad(vcmp, padrows)

    bias_w, bias_s, bias_c, far_lanes = _bias_tables(rel_bias, s, n_sel)
    o_nsa = _attention(proj_main, proj_kv, kcmp, vcmp, bias_c, bias_w, bias_s, _overlap_t(s),
                       far_lanes, b, s, min(N_SELECT, n_sel))

    cw = jnp.pad(conv_w[0].astype(F32), ((0, CONV_HALO - CONV_KERNEL), (0, 0)))
    out = _merge(proj_main, o_nsa, x2, cw, row(conv_b[0]), row(conv_ln_g[0]), row(conv_ln_b[0]),
                 w_conv_proj[0].astype(BF16), w_nsa_proj[0].astype(BF16), w_out[0].astype(BF16),
                 row(norm_f_g), b, s, 256)
    return out.reshape(b, s, d)
```

```python
import functools
import math

import numpy as np
import jax
import jax.numpy as jnp
from jax import lax
from jax.experimental import pallas as pl
from jax.experimental.pallas import tpu as pltpu

F32 = jnp.float32
BF16 = jnp.bfloat16

D_MODEL = 1024
N_HEADS = 16
N_GROUPS = 2
HEADS_PER_GROUP = N_HEADS // N_GROUPS
PAIRS_PER_GROUP = HEADS_PER_GROUP // 2
HEAD_DIM = 64
NSA_WIDTH = N_HEADS * HEAD_DIM
KV_WIDTH = N_GROUPS * HEAD_DIM
CMP_BLOCK = 32
CMP_STRIDE = 16
CMP_HIDDEN = 256
SEL_BLOCK = 64
N_SELECT = 8
WINDOW = 512
Q_TILE = 128
CONV_KERNEL = 31
CONV_HALO = 32
REL_BUCKETS = 32
REL_MAX_DIST = 128
EPS = 1e-6
NEG = -1e30
FORCE_SCORE = 1e6
LOG2E = math.log2(math.e)
LANES = 128
SUBLANES = 8
ONES_ROWS = 16
COLS = PAIRS_PER_GROUP * Q_TILE
COL_TILE = 1024
N_COL_TILES = 8
KV_COL_TILE = 1
WIN_TILES = WINDOW // Q_TILE + 1
VMEM_LIMIT = 56 * 1024 * 1024


def _dot(a, b):
    return jnp.dot(a, b, preferred_element_type=F32)


def _dot_nt(a, b):
    return lax.dot_general(a, b, (((1,), (1,)), ((), ())), preferred_element_type=F32)


def _sigmoid(x):
    return 1.0 / (1.0 + jnp.exp(-x))


def _silu(x):
    return x * _sigmoid(x)


def _proj_kernel(x_ref, g_ref, w_ref, kv_ref, main_ref, h_ref):
    j = pl.program_id(1)

    @pl.when(j == 0)
    def _():
        x = x_ref[...]
        ms = jnp.mean(x * x, axis=-1, keepdims=True)
        h_ref[...] = ((x * lax.rsqrt(ms + EPS)) * g_ref[...]).astype(BF16)

    acc = _dot(h_ref[...], w_ref[...])

    @pl.when(j == KV_COL_TILE)
    def _():
        kv_ref[...] = acc

    @pl.when(j != KV_COL_TILE)
    def _():
        main_ref[...] = acc.astype(BF16)


def _input_projection(x2, g, w_perm, tm):
    m = x2.shape[0]
    return pl.pallas_call(
        _proj_kernel,
        grid=(m // tm, N_COL_TILES),
        in_specs=[
            pl.BlockSpec((tm, D_MODEL), lambda i, j: (i, 0)),
            pl.BlockSpec((1, D_MODEL), lambda i, j: (0, 0)),
            pl.BlockSpec((D_MODEL, COL_TILE), lambda i, j: (0, j)),
        ],
        out_specs=[
            pl.BlockSpec((tm, COL_TILE), lambda i, j: (i, 0)),
            pl.BlockSpec((tm, COL_TILE), lambda i, j: (i, jnp.maximum(j - 1, 0))),
        ],
        out_shape=[
            jax.ShapeDtypeStruct((m, COL_TILE), F32),
            jax.ShapeDtypeStruct((m, (N_COL_TILES - 1) * COL_TILE), BF16),
        ],
        scratch_shapes=[pltpu.VMEM((tm, D_MODEL), BF16)],
        compiler_params=pltpu.CompilerParams(
            dimension_semantics=("parallel", "arbitrary"), vmem_limit_bytes=VMEM_LIMIT),
        name="input_projection",
    )(x2, g, w_perm)


def _compress_kernel(kf_ref, vf_ref, pk_ref, pv_ref, w1k_ref, w1v_ref, w2k_ref, w2v_ref,
                     kc_ref, vc_ref):
    def one(f_ref, pos_ref, w1_ref, w2_ref, o_ref):
        f = f_ref[0]
        n = f.shape[0]
        hw = N_GROUPS * CMP_HIDDEN
        first = _dot((f + pos_ref[0:1, :]).astype(BF16), w1_ref[:, 0:hw])
        second = _dot((f + pos_ref[1:2, :]).astype(BF16), w1_ref[:, hw:2 * hw])
        hid = first + pltpu.roll(second, n - 1, axis=0)
        o_ref[0] = _dot(_silu(hid).astype(BF16), w2_ref[...])

    one(kf_ref, pk_ref, w1k_ref, w2k_ref, kc_ref)
    one(vf_ref, pv_ref, w1v_ref, w2v_ref, vc_ref)


def _compress(kf, vf, pk, pv, w1k, w1v, w2k, w2v):
    b, n, width = kf.shape
    const = lambda shape: pl.BlockSpec(shape, lambda i: (0,) * len(shape))
    row = pl.BlockSpec((1, n, width), lambda i: (i, 0, 0))
    out = pl.BlockSpec((1, n, LANES), lambda i: (i, 0, 0))
    return pl.pallas_call(
        _compress_kernel,
        grid=(b,),
        in_specs=[row, row, const(pk.shape), const(pv.shape), const(w1k.shape), const(w1v.shape),
                  const(w2k.shape), const(w2v.shape)],
        out_specs=[out, out],
        out_shape=[jax.ShapeDtypeStruct((b, n, LANES), F32)] * 2,
        compiler_params=pltpu.CompilerParams(
            dimension_semantics=("parallel",), vmem_limit_bytes=VMEM_LIMIT),
        name="nsa_compress",
    )(kf, vf, pk, pv, w1k, w1v, w2k, w2v)


def _attn_kernel(q_ref, ks_ref, vs_ref, kw_ref, vw_ref, gt_ref, kc_ref, vc_ref, bc_ref, tw_ref,
                 ts_ref, ovt_ref, bx_ref, o_ref,
                 ksv, kwv, kcv, vst, vwt, vct, rhs, acc, obuf, s_a, s_b, s_w, *, n_top):
    qi = pl.program_id(1)
    n_kt = kwv.shape[1]
    n_sel = ovt_ref.shape[0]
    variants = 2 * N_GROUPS

    @pl.when(qi == 0)
    def _prepare_kv():
        def halves(k):
            lo = lax.broadcasted_iota(jnp.int32, k.shape, 1) < HEAD_DIM
            kr = pltpu.roll(k, HEAD_DIM, axis=1)
            z = jnp.zeros_like(k)
            c = lambda a: a.astype(BF16)
            return ((c(jnp.where(lo, k, z)), c(jnp.where(lo, z, kr))),
                    (c(jnp.where(lo, kr, z)), c(jnp.where(lo, z, k))))

        def extra_lanes(shape, v):
            pair = lax.broadcasted_iota(jnp.int32, shape, 0)
            r = lax.broadcasted_iota(jnp.int32, shape, 1)
            lane = lax.broadcasted_iota(jnp.int32, shape, 2)
            one = ((lane >= n_sel + 2 * v) & (lane < n_sel + 2 * v + 2)
                   | (lane == pair * (2 * Q_TILE // SEL_BLOCK) + r // SEL_BLOCK))
            return jnp.where(one, 1.0, 0.0).astype(BF16)

        pair_shape = (n_kt // 2, 2 * Q_TILE, LANES)
        tile_shape = (n_kt, Q_TILE, LANES)
        k_sel, k_win, k_cmp = halves(ks_ref[...]), halves(kw_ref[...]), halves(kc_ref[0])
        for g in range(N_GROUPS):
            for v in range(2):
                rows = slice(v * 2 * Q_TILE, (v + 1) * 2 * Q_TILE)
                ksv[g, :, rows, 0:LANES] = k_sel[g][v].reshape(pair_shape)
                ksv[g, :, rows, LANES:2 * LANES] = extra_lanes(pair_shape, v)
                kwv[g, :, v * Q_TILE:(v + 1) * Q_TILE, :] = k_win[g][v].reshape(tile_shape)
                kcv[g, v * LANES:(v + 1) * LANES, :] = k_cmp[g][v]
        vst[:, :, HEAD_DIM:, :] = jnp.ones((N_GROUPS, n_kt // 2, ONES_ROWS, 2 * Q_TILE), BF16)
        vwt[:, :, HEAD_DIM:, :] = jnp.ones((N_GROUPS, n_kt, ONES_ROWS, Q_TILE), BF16)
        for kt in range(n_kt):
            rows = slice(kt * Q_TILE, (kt + 1) * Q_TILE)
            half = slice((kt % 2) * Q_TILE, (kt % 2 + 1) * Q_TILE)
            vs_t = vs_ref[rows, :].T.astype(BF16)
            vw_t = vw_ref[rows, :].T.astype(BF16)
            for g in range(N_GROUPS):
                vst[g, kt // 2, 0:HEAD_DIM, half] = vs_t[g * HEAD_DIM:(g + 1) * HEAD_DIM]
                vwt[g, kt, 0:HEAD_DIM, :] = vw_t[g * HEAD_DIM:(g + 1) * HEAD_DIM]
        vct[...] = vc_ref[0].T.astype(BF16)

    qt = q_ref[...]
    gsig_t = _sigmoid(gt_ref[...]).T

    def col_max(s):
        return jnp.max(s.reshape(s.shape[0] // 8, 8, COLS), axis=0)

    def group_rows(g):
        return slice(g * HEAD_DIM, (g + 1) * HEAD_DIM)

    def online(gv, s, vt, m):
        m_new = jnp.maximum(m, jnp.max(col_max(s), axis=0, keepdims=True))
        e = jnp.exp2(s - m_new)
        acc[gv] = acc[gv] * jnp.exp2(m - m_new) + _dot(vt, e.astype(BF16))
        return m_new

    def fresh_state():
        acc[...] = jnp.zeros(acc.shape, F32)
        return tuple(jnp.full((1, COLS), NEG, F32) for _ in range(variants))

    def finish(branch):
        for g in range(N_GROUPS):
            halves = []
            for v in range(2):
                gv = g * 2 + v
                l = acc[gv, HEAD_DIM:HEAD_DIM + 1, :]
                halves.append(acc[gv, 0:HEAD_DIM, :] * (1.0 / jnp.maximum(l, 1e-30)))
            obuf[branch, g] = jnp.concatenate(halves, axis=0)

    groups = range(N_GROUPS)
    for g in groups:
        q4 = jnp.concatenate(
            [qt[:, (g * PAIRS_PER_GROUP + p) * LANES:(g * PAIRS_PER_GROUP + p + 1) * LANES]
             for p in range(PAIRS_PER_GROUP)], axis=0)
        rhs[g, :, 0:LANES] = (q4.astype(F32) * (HEAD_DIM ** -0.5 * LOG2E)).astype(BF16)

    s_cmp = [_dot_nt(kcv[g], rhs[g, :, 0:LANES]) + bc_ref[g, 0] for g in groups]
    psum_t = []
    for g in groups:
        tot = jnp.zeros((LANES, Q_TILE), F32)
        halves = []
        for v in range(2):
            s = s_cmp[g][v * LANES:(v + 1) * LANES]
            valid = s > 0.5 * NEG
            m = jnp.max(s, axis=0, keepdims=True)
            e = jnp.where(valid, jnp.exp2(s - m), 0.0)
            p = e * (1.0 / jnp.maximum(jnp.sum(e, axis=0, keepdims=True), 1e-30))
            for pp in range(PAIRS_PER_GROUP):
                tot = tot + p[:, pp * Q_TILE:(pp + 1) * Q_TILE]
            halves.append(_dot(vct[group_rows(g), :], p.astype(BF16)))
        obuf[0, g] = jnp.concatenate(halves, axis=0)
        psum_t.append(tot)

    ovt = ovt_ref[...]
    imp_t = []
    for g in groups:
        p_hi = psum_t[g].astype(BF16)
        r1 = psum_t[g] - p_hi.astype(F32)
        p_mid = r1.astype(BF16)
        p_lo = (r1 - p_mid.astype(F32)).astype(BF16)
        imp_t.append(_dot(ovt, p_hi) + _dot(ovt, p_mid) + _dot(ovt, p_lo))

    j_idx = lax.broadcasted_iota(jnp.int32, (n_sel, Q_TILE), 0)
    r_idx = lax.broadcasted_iota(jnp.int32, (n_sel, Q_TILE), 1)
    blk_t = qi * (Q_TILE // SEL_BLOCK) + r_idx // SEL_BLOCK
    valid_blk = j_idx <= blk_t
    forced = (j_idx == 0) | (j_idx == blk_t) | (j_idx == blk_t - 1)
    prio = [jnp.where(valid_blk, jnp.where(forced, FORCE_SCORE, imp_t[g]), -FORCE_SCORE) for g in groups]
    rank = [jnp.zeros((n_sel, Q_TILE), F32) for _ in groups]
    for jj in range(n_sel):
        later = j_idx > jj
        for g in groups:
            row = prio[g][jj:jj + 1, :]
            beats = (row > prio[g]) | ((row == prio[g]) & later)
            rank[g] = rank[g] + jnp.where(beats, 1.0, 0.0)
    for g in groups:
        drop_t = jnp.where((rank[g] < n_top) & valid_blk, 0.0, NEG)
        drop_t = jnp.concatenate([drop_t, jnp.zeros((LANES - n_sel, Q_TILE), F32)], axis=0)
        drop = drop_t.T.astype(BF16)
        rhs[g, :, LANES:2 * LANES] = jnp.concatenate([drop] * PAIRS_PER_GROUP, axis=0) + bx_ref[g]

    def logits_into(buf, i):
        for g in range(N_GROUPS):
            buf[g] = _dot_nt(ksv[g, i], rhs[g])

    def consume(buf, i, state):
        ms = list(state)
        for g in range(N_GROUPS):
            for v in range(2):
                gv = g * 2 + v
                ms[gv] = online(gv, buf[g, v * 2 * Q_TILE:(v + 1) * 2 * Q_TILE, :], vst[g, i], ms[gv])
        return tuple(ms)

    n_all = (qi + 2) // 2
    n_far = jnp.maximum(qi - 1, 0) // 2

    first = jnp.maximum(qi - (WIN_TILES - 1), 0)
    tiles = []
    for t in range(WIN_TILES):
        dd = qi - (first + t)
        table = jnp.where(dd >= 0, dd, WIN_TILES)
        tiles.append((jnp.minimum(first + t, n_kt - 1), table))
    m_win = []
    for g in range(N_GROUPS):
        mx = [jnp.full((8, COLS), NEG, F32) for _ in range(2)]
        for t, (kt, table) in enumerate(tiles):
            s_all = _dot_nt(kwv[g, kt], rhs[g, :, 0:LANES])
            for v in range(2):
                s = s_all[v * Q_TILE:(v + 1) * Q_TILE] + tw_ref[g, v, table]
                s_w[g, t, v * Q_TILE:(v + 1) * Q_TILE, :] = s
                mx[v] = jnp.maximum(mx[v], col_max(s))
        m_win += [jnp.max(mx[v], axis=0, keepdims=True) for v in range(2)]

    for g in range(N_GROUPS):
        halves = []
        for v in range(2):
            o = jnp.zeros((HEAD_DIM + ONES_ROWS, COLS), F32)
            for t, (kt, _) in enumerate(tiles):
                e = jnp.exp2(s_w[g, t, v * Q_TILE:(v + 1) * Q_TILE, :] - m_win[g * 2 + v])
                o = o + _dot(vwt[g, kt], e.astype(BF16))
            halves.append(o[0:HEAD_DIM] * (1.0 / jnp.maximum(o[HEAD_DIM:HEAD_DIM + 1], 1e-30)))
        obuf[2, g] = jnp.concatenate(halves, axis=0)

    logits_into(s_a, 0)

    def far_two(j, state):
        logits_into(s_b, 2 * j + 1)
        state = consume(s_a, 2 * j, state)
        logits_into(s_a, 2 * j + 2)
        return consume(s_b, 2 * j + 1, state)

    state = lax.fori_loop(0, n_far // 2, far_two, fresh_state())
    state = lax.fori_loop(0, n_far % 2, lambda _, st: consume(s_a, n_far - 1, st), state)

    def near_pair(i, state):
        for g in range(N_GROUPS):
            s_all = _dot_nt(ksv[g, i], rhs[g])
            for v in range(2):
                for h in range(2):
                    dd = jnp.clip(qi - (2 * i + h), 0, 2)
                    r0 = (2 * v + h) * Q_TILE
                    s_b[g, r0:r0 + Q_TILE, :] = s_all[r0:r0 + Q_TILE] + ts_ref[g, v, dd]
        return consume(s_b, i, state)

    lax.fori_loop(n_far, n_all, near_pair, state)
    finish(1)

    for g in range(N_GROUPS):
        for p in range(PAIRS_PER_GROUP):
            halves = []
            for v in range(2):
                tot = jnp.zeros((HEAD_DIM, Q_TILE), F32)
                for br in range(3):
                    c = br * N_HEADS + g * HEADS_PER_GROUP + 2 * p + v
                    tot = tot + gsig_t[c:c + 1, :] * obuf[br, g, v * HEAD_DIM:(v + 1) * HEAD_DIM,
                                                          p * Q_TILE:(p + 1) * Q_TILE]
                halves.append(tot)
            col = (g * PAIRS_PER_GROUP + p) * LANES
            o_ref[:, col:col + LANES] = jnp.concatenate(halves, axis=0).T.astype(BF16)


def _attention(proj_main, proj_kv, kcmp, vcmp, bias_c, bias_w, bias_s, ovt, far_lanes, b, s, n_top):
    n_q = s // Q_TILE
    once = pl.Buffered(1)
    kv_col = lambda c: pl.BlockSpec((s, LANES), lambda bi, qi: (bi, c))
    cmp_spec = pl.BlockSpec((1, LANES, LANES), lambda bi, qi: (bi, 0, 0))
    return pl.pallas_call(
        functools.partial(_attn_kernel, n_top=n_top),
        grid=(b, n_q),
        in_specs=[
            pl.BlockSpec((Q_TILE, NSA_WIDTH), lambda bi, qi: (bi * n_q + qi, 0)),
            kv_col(2), kv_col(3), kv_col(4), kv_col(5),
            pl.BlockSpec((Q_TILE, LANES), lambda bi, qi: (bi * n_q + qi, 6)),
            cmp_spec, cmp_spec,
            pl.BlockSpec((N_GROUPS, 1, 2 * LANES, COLS), lambda bi, qi: (0, qi, 0, 0)),
            pl.BlockSpec(bias_w.shape, lambda bi, qi: (0,) * 5, pipeline_mode=once),
            pl.BlockSpec(bias_s.shape, lambda bi, qi: (0,) * 5, pipeline_mode=once),
            pl.BlockSpec(ovt.shape, lambda bi, qi: (0, 0), pipeline_mode=once),
            pl.BlockSpec(far_lanes.shape, lambda bi, qi: (0, 0, 0), pipeline_mode=once),
        ],
        out_specs=pl.BlockSpec((Q_TILE, NSA_WIDTH), lambda bi, qi: (bi * n_q + qi, 0)),
        out_shape=jax.ShapeDtypeStruct((b * s, NSA_WIDTH), BF16),
        scratch_shapes=[
            pltpu.VMEM((N_GROUPS, n_q // 2, 4 * Q_TILE, 2 * LANES), BF16),
            pltpu.VMEM((N_GROUPS, n_q, 2 * Q_TILE, LANES), BF16),
            pltpu.VMEM((N_GROUPS, 2 * LANES, LANES), BF16),
            pltpu.VMEM((N_GROUPS, n_q // 2, HEAD_DIM + ONES_ROWS, 2 * Q_TILE), BF16),
            pltpu.VMEM((N_GROUPS, n_q, HEAD_DIM + ONES_ROWS, Q_TILE), BF16),
            pltpu.VMEM((LANES, LANES), BF16),
            pltpu.VMEM((N_GROUPS, COLS, 2 * LANES), BF16),
            pltpu.VMEM((2 * N_GROUPS, HEAD_DIM + ONES_ROWS, COLS), F32),
            pltpu.VMEM((3, N_GROUPS, LANES, COLS), F32),
            pltpu.VMEM((N_GROUPS, 4 * Q_TILE, COLS), F32),
            pltpu.VMEM((N_GROUPS, 4 * Q_TILE, COLS), F32),
            pltpu.VMEM((N_GROUPS, WIN_TILES, 2 * Q_TILE, COLS), F32),
        ],
        compiler_params=pltpu.CompilerParams(
            dimension_semantics=("parallel", "arbitrary"), vmem_limit_bytes=VMEM_LIMIT),
        name="nsa_attention",
    )(proj_main, proj_kv, proj_kv, proj_kv, proj_kv, proj_kv, kcmp, vcmp, bias_c, bias_w, bias_s,
      ovt, far_lanes)


def _merge_kernel(zn_ref, a_ref, b_ref, zc_ref, gc_ref, gn_ref, ah_ref, bh_ref, on_ref, x_ref,
                  cw_ref, cb_ref, lg_ref, lb_ref, wcp_ref, wnp_ref, wo_ref, gf_ref, out_ref, uext,
                  conv, shifted):
    i = pl.program_id(1)
    ts = a_ref.shape[0]
    f = lambda r: r[...].astype(F32)

    n_cblk = D_MODEL // LANES
    u_halo = jnp.where(i > 0, f(ah_ref) * _sigmoid(f(bh_ref)), 0.0)
    u = f(a_ref) * _sigmoid(f(b_ref))
    for cblk in range(n_cblk):
        cols = slice(cblk * LANES, (cblk + 1) * LANES)
        uext[cblk, 0:CONV_HALO, :] = u_halo[:, cols]
        uext[cblk, CONV_HALO:CONV_HALO + ts, :] = u[:, cols]
        uext[cblk, CONV_HALO + ts:, :] = jnp.zeros((SUBLANES, LANES), F32)

    lead = CONV_HALO - (CONV_KERNEL - 1)
    half = ts // 2

    def conv_block(cblk, carry):
        for shift in range(SUBLANES):
            shifted[shift] = uext[cblk, shift:shift + ts + CONV_HALO, :]
        w = cw_ref[cblk]
        for h in range(2):
            c = jnp.broadcast_to(cb_ref[cblk], (half, LANES))
            for shift in range(SUBLANES):
                xs = shifted[shift, h * half:h * half + half + CONV_HALO, :]
                for j in range(CONV_KERNEL):
                    if (lead + j) % SUBLANES == shift:
                        base = lead + j - shift
                        c = c + w[j:j + 1, :] * xs[base:base + half]
            conv[cblk, h * half:(h + 1) * half, :] = c
        return carry

    lax.fori_loop(0, n_cblk, conv_block, 0)
    c = jnp.concatenate([conv[cblk] for cblk in range(n_cblk)], axis=1)

    mu = jnp.mean(c, axis=-1, keepdims=True)
    cc = c - mu
    var = jnp.mean(cc * cc, axis=-1, keepdims=True)
    y = (cc * lax.rsqrt(var + EPS)) * lg_ref[...] + lb_ref[...]
    conv_act = _silu(y) * _silu(f(zc_ref))
    y_conv = _dot(conv_act.astype(BF16), wcp_ref[...])

    nsa_act = f(on_ref) * _silu(f(zn_ref))
    y_nsa = _dot(nsa_act.astype(BF16), wnp_ref[...])

    merged = _sigmoid(f(gc_ref)) * y_conv + _sigmoid(f(gn_ref)) * y_nsa
    xo = x_ref[...] + _dot(merged.astype(BF16), wo_ref[...])
    ms = jnp.mean(xo * xo, axis=-1, keepdims=True)
    out_ref[...] = (xo * lax.rsqrt(ms + EPS)) * gf_ref[...]


def _merge(proj_main, o_nsa, x2, cw, cb, lg, lb, wcp, wnp, wo, gf, b, s, ts):
    n_t = s // ts
    n_cblk = D_MODEL // LANES
    halo_per_tile = ts // CONV_HALO
    cw = jnp.transpose(cw.reshape(cw.shape[0], n_cblk, LANES), (1, 0, 2))
    cb = cb.reshape(n_cblk, 1, LANES)
    col = lambda c: pl.BlockSpec((ts, COL_TILE), lambda bi, ti: (bi * n_t + ti, c))
    halo = lambda c: pl.BlockSpec(
        (CONV_HALO, COL_TILE),
        lambda bi, ti: (jnp.maximum((bi * n_t + ti) * halo_per_tile - 1, 0), c))
    const = lambda a: pl.BlockSpec(a.shape, lambda bi, ti: (0,) * a.ndim)
    rowblk = pl.BlockSpec((ts, D_MODEL), lambda bi, ti: (bi * n_t + ti, 0))
    return pl.pallas_call(
        _merge_kernel,
        grid=(b, n_t),
        in_specs=[col(1), col(2), col(3), col(4), col(5), col(6), halo(2), halo(3), rowblk, rowblk,
                  const(cw), const(cb), const(lg), const(lb), const(wcp), const(wnp), const(wo),
                  const(gf)],
        out_specs=rowblk,
        out_shape=jax.ShapeDtypeStruct((b * s, D_MODEL), F32),
        scratch_shapes=[pltpu.VMEM((n_cblk, ts + CONV_HALO + SUBLANES, LANES), F32),
                        pltpu.VMEM((n_cblk, ts, LANES), F32),
                        pltpu.VMEM((SUBLANES, ts + CONV_HALO, LANES), F32)],
        compiler_params=pltpu.CompilerParams(
            dimension_semantics=("parallel", "arbitrary"), vmem_limit_bytes=VMEM_LIMIT),
        name="conv_merge",
    )(proj_main, proj_main, proj_main, proj_main, proj_main, proj_main, proj_main, proj_main,
      o_nsa, x2, cw, cb, lg, lb, wcp, wnp, wo, gf)


def _t5_bucket_np(rel):
    rel = np.maximum(rel, 0)
    max_exact = REL_BUCKETS // 2
    relf = np.maximum(rel, 1).astype(np.float32)
    large = max_exact + (np.log(relf / np.float32(max_exact))
                         / np.float32(np.log(REL_MAX_DIST / max_exact))
                         * np.float32(REL_BUCKETS - max_exact)).astype(np.int32)
    large = np.minimum(large, REL_BUCKETS - 1)
    return np.where(rel < max_exact, rel, large)


def _pair_head_index():
    g = np.arange(N_GROUPS)[:, None, None]
    v = np.arange(2)[None, :, None]
    p = np.arange(PAIRS_PER_GROUP)[None, None, :]
    return g * HEADS_PER_GROUP + 2 * p + v


def _bias_lookup(rel_bias, rel):
    bucket = _t5_bucket_np(rel).reshape(-1)
    onehot = (jnp.arange(REL_BUCKETS)[:, None] == jnp.asarray(bucket)[None, :]).astype(F32)
    vals = jnp.dot(rel_bias.astype(F32).T, onehot, precision=lax.Precision.HIGHEST)
    vals = vals.reshape((N_HEADS,) + rel.shape)
    head = _pair_head_index()
    return jnp.stack([vals[h] for h in head.reshape(-1)]).reshape(head.shape + rel.shape)


def _bias_tables(rel_bias, s, n_sel):
    r = np.arange(Q_TILE)[:, None]
    c = np.arange(LANES)[None, :]
    far = rel_bias.astype(F32)[REL_BUCKETS - 1][_pair_head_index()]
    rel_w = np.stack([dd * Q_TILE + r - c for dd in range(WIN_TILES)])
    assert (_t5_bucket_np(rel_w[2:]) == REL_BUCKETS - 1).all()
    ok_w = (rel_w >= 0) & (rel_w < WINDOW)
    bias_w = _bias_lookup(rel_bias, rel_w)

    def orient(t):
        t = jnp.transpose(t, (0, 1, 3, 5, 2, 4))
        return t.reshape(t.shape[:4] + (COLS,))

    tw = jnp.where(ok_w, bias_w * LOG2E, NEG)
    tw = orient(jnp.concatenate([tw, jnp.full_like(tw[:, :, :, :1], NEG)], axis=3))
    ts = jnp.where(rel_w[:3] >= 0, (bias_w[:, :, :, :3] - far[..., None, None, None]) * LOG2E, NEG)
    ts = orient(ts)
    t = np.arange(s)[:, None]
    rel_c = t - (c * CMP_STRIDE + CMP_BLOCK - 1)
    ok_c = (rel_c >= 0) & (c < (s - CMP_BLOCK) // CMP_STRIDE + 1)
    bc = jnp.where(ok_c, _bias_lookup(rel_bias, rel_c) * LOG2E, NEG)
    bc = bc.reshape(N_GROUPS, 2, PAIRS_PER_GROUP, s // Q_TILE, Q_TILE, LANES)
    bc = jnp.transpose(bc, (0, 3, 1, 5, 2, 4))
    bc = bc.reshape(N_GROUPS, s // Q_TILE, 2 * LANES, COLS)
    far2 = far * LOG2E
    hi = far2.astype(BF16)
    lo = (far2 - hi.astype(F32)).astype(BF16)
    pieces = jnp.stack([hi[:, 0], lo[:, 0], hi[:, 1], lo[:, 1]], axis=-1)
    pieces = jnp.broadcast_to(pieces[:, :, None, :], (N_GROUPS, PAIRS_PER_GROUP, Q_TILE, 4))
    far_lanes = jnp.pad(pieces.reshape(N_GROUPS, COLS, 4),
                        ((0, 0), (0, 0), (n_sel, LANES - n_sel - 4)))
    return tw, ts, bc, far_lanes


def _overlap_t(s):
    n_cmp = (s - CMP_BLOCK) // CMP_STRIDE + 1
    cs = np.arange(LANES) * CMP_STRIDE
    ss = np.arange(s // SEL_BLOCK) * SEL_BLOCK
    ovt = ((cs[None, :] <= ss[:, None] + SEL_BLOCK - 1) & (cs[None, :] + CMP_BLOCK - 1 >= ss[:, None])
           & (np.arange(LANES)[None, :] < n_cmp))
    return jnp.asarray(ovt, BF16)


def _pad_w_in(w):
    used = NSA_WIDTH + 6 * KV_WIDTH + 3 * N_HEADS
    pad = jnp.zeros((D_MODEL, (KV_COL_TILE + 1) * COL_TILE - used), w.dtype)
    return jnp.concatenate([w[:, :used], pad, w[:, used:]], axis=1).astype(BF16)


def _compress_weights(pos, w1, w2):
    half = CMP_BLOCK // 2
    eye = jnp.eye(N_GROUPS, dtype=F32)
    w1r = w1.reshape(CMP_BLOCK, HEAD_DIM, CMP_HIDDEN)
    blk = lambda part: jnp.einsum('idn,gh->igdhn', part, eye).reshape(
        half * KV_WIDTH, N_GROUPS * CMP_HIDDEN)
    w1d = jnp.concatenate([blk(w1r[:half]), blk(w1r[half:])], axis=1).astype(BF16)
    w2d = jnp.einsum('nd,gh->gnhd', w2, eye).reshape(N_GROUPS * CMP_HIDDEN, KV_WIDTH).astype(BF16)
    tilepos = lambda part: jnp.broadcast_to(part[:, None, :], (half, N_GROUPS, HEAD_DIM)).reshape(1, -1)
    posd = jnp.concatenate([tilepos(pos[:half]), tilepos(pos[half:])], axis=0).astype(F32)
    return posd, w1d, w2d


def kernel(x, norm_in_g, w_in, pos_ck, w_ck1, w_ck2, pos_cv, w_cv1, w_cv2, rel_bias, conv_w, conv_b,
           conv_ln_g, conv_ln_b, w_conv_proj, w_nsa_proj, w_out, norm_f_g):
    b, s, d = x.shape
    n_sel = s // SEL_BLOCK
    assert d == D_MODEL and w_in.shape[0] == 1, "single-layer block with D_MODEL=1024"
    assert s % (2 * Q_TILE) == 0 and s // CMP_STRIDE <= LANES and s >= WINDOW
    assert n_sel + 4 <= LANES
    m = b * s
    x2 = x.reshape(m, d)
    row = lambda a: a.reshape(1, -1).astype(F32)

    proj_kv, proj_main = _input_projection(x2, row(norm_in_g[0]), _pad_w_in(w_in[0]), min(1024, m))

    chunks = s // CMP_STRIDE
    kf = proj_kv[:, 0:KV_WIDTH].reshape(b, chunks, CMP_STRIDE * KV_WIDTH)
    vf = proj_kv[:, KV_WIDTH:2 * KV_WIDTH].reshape(b, chunks, CMP_STRIDE * KV_WIDTH)
    pk, w1k, w2k = _compress_weights(pos_ck[0], w_ck1[0], w_ck2[0])
    pv, w1v, w2v = _compress_weights(pos_cv[0], w_cv1[0], w_cv2[0])
    kcmp, vcmp = _compress(kf, vf, pk, pv, w1k, w1v, w2k, w2v)
    if chunks < LANES:
        padrows = ((0, 0), (0, LANES - chunks), (0, 0))
        kcmp, vcmp = jnp.pad(kcmp, padrows), jnp.pad(vcmp, padrows)

    bias_w, bias_s, bias_c, far_lanes = _bias_tables(rel_bias, s, n_sel)
    o_nsa = _attention(proj_main, proj_kv, kcmp, vcmp, bias_c, bias_w, bias_s, _overlap_t(s),
                       far_lanes, b, s, min(N_SELECT, n_sel))

    cw = jnp.pad(conv_w[0].astype(F32), ((0, CONV_HALO - CONV_KERNEL), (0, 0)))
    out = _merge(proj_main, o_nsa, x2, cw, row(conv_b[0]), row(conv_ln_g[0]), row(conv_ln_b[0]),
                 w_conv_proj[0].astype(BF16), w_nsa_proj[0].astype(BF16), w_out[0].astype(BF16),
                 row(norm_f_g), b, s, 256)
    return out.reshape(b, s, d)
```

```python
import functools
import math

import numpy as np
import jax
import jax.numpy as jnp
from jax import lax
from jax.experimental import pallas as pl
from jax.experimental.pallas import tpu as pltpu

F32 = jnp.float32
BF16 = jnp.bfloat16

D_MODEL = 1024
N_HEADS = 16
N_GROUPS = 2
HEADS_PER_GROUP = N_HEADS // N_GROUPS
PAIRS_PER_GROUP = HEADS_PER_GROUP // 2
HEAD_DIM = 64
NSA_WIDTH = N_HEADS * HEAD_DIM
KV_WIDTH = N_GROUPS * HEAD_DIM
CMP_BLOCK = 32
CMP_STRIDE = 16
CMP_HIDDEN = 256
SEL_BLOCK = 64
N_SELECT = 8
WINDOW = 512
Q_TILE = 128
CONV_KERNEL = 31
CONV_HALO = 32
REL_BUCKETS = 32
REL_MAX_DIST = 128
EPS = 1e-6
NEG = -1e30
FORCE_SCORE = 1e6
LOG2E = math.log2(math.e)
LANES = 128
SUBLANES = 8
ONES_ROWS = 16
COLS = PAIRS_PER_GROUP * Q_TILE
COL_TILE = 1024
N_COL_TILES = 8
KV_COL_TILE = 1
WIN_TILES = WINDOW // Q_TILE + 1
VMEM_LIMIT = 56 * 1024 * 1024


def _dot(a, b):
    return jnp.dot(a, b, preferred_element_type=F32)


def _dot_nt(a, b):
    return lax.dot_general(a, b, (((1,), (1,)), ((), ())), preferred_element_type=F32)


def _sigmoid(x):
    return 1.0 / (1.0 + jnp.exp(-x))


def _silu(x):
    return x * _sigmoid(x)


def _proj_kernel(x_ref, g_ref, w_ref, kv_ref, main_ref, h_ref):
    j = pl.program_id(1)

    @pl.when(j == 0)
    def _():
        x = x_ref[...]
        ms = jnp.mean(x * x, axis=-1, keepdims=True)
        h_ref[...] = ((x * lax.rsqrt(ms + EPS)) * g_ref[...]).astype(BF16)

    acc = _dot(h_ref[...], w_ref[...])

    @pl.when(j == KV_COL_TILE)
    def _():
        kv_ref[...] = acc

    @pl.when(j != KV_COL_TILE)
    def _():
        main_ref[...] = acc.astype(BF16)


def _input_projection(x2, g, w_perm, tm):
    m = x2.shape[0]
    return pl.pallas_call(
        _proj_kernel,
        grid=(m // tm, N_COL_TILES),
        in_specs=[
            pl.BlockSpec((tm, D_MODEL), lambda i, j: (i, 0)),
            pl.BlockSpec((1, D_MODEL), lambda i, j: (0, 0)),
            pl.BlockSpec((D_MODEL, COL_TILE), lambda i, j: (0, j)),
        ],
        out_specs=[
            pl.BlockSpec((tm, COL_TILE), lambda i, j: (i, 0)),
            pl.BlockSpec((tm, COL_TILE), lambda i, j: (i, jnp.maximum(j - 1, 0))),
        ],
        out_shape=[
            jax.ShapeDtypeStruct((m, COL_TILE), F32),
            jax.ShapeDtypeStruct((m, (N_COL_TILES - 1) * COL_TILE), BF16),
        ],
        scratch_shapes=[pltpu.VMEM((tm, D_MODEL), BF16)],
        compiler_params=pltpu.CompilerParams(
            dimension_semantics=("parallel", "arbitrary"), vmem_limit_bytes=VMEM_LIMIT),
        name="input_projection",
    )(x2, g, w_perm)


def _compress_kernel(kf_ref, vf_ref, pk_ref, pv_ref, w1k_ref, w1v_ref, w2k_ref, w2v_ref,
                     kc_ref, vc_ref):
    def one(f_ref, pos_ref, w1_ref, w2_ref, o_ref):
        n = f_ref.shape[0] // CMP_STRIDE
        hw = N_GROUPS * CMP_HIDDEN
        first = jnp.zeros((n, hw), F32)
        second = jnp.zeros((n, hw), F32)
        for i in range(CMP_STRIDE):
            tok = f_ref[pl.ds(i, n, stride=CMP_STRIDE), :]
            lanes = slice(i * KV_WIDTH, (i + 1) * KV_WIDTH)
            first = first + _dot((tok + pos_ref[0:1, lanes]).astype(BF16), w1_ref[lanes, 0:hw])
            second = second + _dot((tok + pos_ref[1:2, lanes]).astype(BF16), w1_ref[lanes, hw:2 * hw])
        hid = first + pltpu.roll(second, n - 1, axis=0)
        o_ref[0] = _dot(_silu(hid).astype(BF16), w2_ref[...])

    one(kf_ref, pk_ref, w1k_ref, w2k_ref, kc_ref)
    one(vf_ref, pv_ref, w1v_ref, w2v_ref, vc_ref)


def _compress(proj_kv, pk, pv, w1k, w1v, w2k, w2v, b, s):
    n = s // CMP_STRIDE
    const = lambda shape: pl.BlockSpec(shape, lambda i: (0,) * len(shape))
    kv_col = lambda c: pl.BlockSpec((s, KV_WIDTH), lambda i: (i, c))
    out = pl.BlockSpec((1, n, LANES), lambda i: (i, 0, 0))
    return pl.pallas_call(
        _compress_kernel,
        grid=(b,),
        in_specs=[kv_col(0), kv_col(1), const(pk.shape), const(pv.shape), const(w1k.shape),
                  const(w1v.shape), const(w2k.shape), const(w2v.shape)],
        out_specs=[out, out],
        out_shape=[jax.ShapeDtypeStruct((b, n, LANES), F32)] * 2,
        compiler_params=pltpu.CompilerParams(
            dimension_semantics=("parallel",), vmem_limit_bytes=VMEM_LIMIT),
        name="nsa_compress",
    )(proj_kv, proj_kv, pk, pv, w1k, w1v, w2k, w2v)


def _attn_kernel(q_ref, ks_ref, vs_ref, kw_ref, vw_ref, gt_ref, kc_ref, vc_ref, bc_ref, tw_ref,
                 ts_ref, ovt_ref, bx_ref, o_ref,
                 ksv, kwv, kcv, vst, vwt, vct, rhs, acc, obuf, s_a, s_b, s_w, *, n_top):
    qi = pl.program_id(1)
    n_kt = kwv.shape[1]
    n_sel = ovt_ref.shape[0]
    variants = 2 * N_GROUPS

    @pl.when(qi == 0)
    def _prepare_kv():
        def halves(k):
            lo = lax.broadcasted_iota(jnp.int32, k.shape, 1) < HEAD_DIM
            kr = pltpu.roll(k, HEAD_DIM, axis=1)
            z = jnp.zeros_like(k)
            c = lambda a: a.astype(BF16)
            return ((c(jnp.where(lo, k, z)), c(jnp.where(lo, z, kr))),
                    (c(jnp.where(lo, kr, z)), c(jnp.where(lo, z, k))))

        def extra_lanes(shape, v):
            pair = lax.broadcasted_iota(jnp.int32, shape, 0)
            r = lax.broadcasted_iota(jnp.int32, shape, 1)
            lane = lax.broadcasted_iota(jnp.int32, shape, 2)
            one = ((lane >= n_sel + 2 * v) & (lane < n_sel + 2 * v + 2)
                   | (lane == pair * (2 * Q_TILE // SEL_BLOCK) + r // SEL_BLOCK))
            return jnp.where(one, 1.0, 0.0).astype(BF16)

        pair_shape = (n_kt // 2, 2 * Q_TILE, LANES)
        tile_shape = (n_kt, Q_TILE, LANES)
        k_sel, k_win, k_cmp = halves(ks_ref[...]), halves(kw_ref[...]), halves(kc_ref[0])
        for g in range(N_GROUPS):
            for v in range(2):
                rows = slice(v * 2 * Q_TILE, (v + 1) * 2 * Q_TILE)
                ksv[g, :, rows, 0:LANES] = k_sel[g][v].reshape(pair_shape)
                ksv[g, :, rows, LANES:2 * LANES] = extra_lanes(pair_shape, v)
                kwv[g, :, v * Q_TILE:(v + 1) * Q_TILE, :] = k_win[g][v].reshape(tile_shape)
                kcv[g, v * LANES:(v + 1) * LANES, :] = k_cmp[g][v]
        vst[:, :, HEAD_DIM:, :] = jnp.ones((N_GROUPS, n_kt // 2, ONES_ROWS, 2 * Q_TILE), BF16)
        vwt[:, :, HEAD_DIM:, :] = jnp.ones((N_GROUPS, n_kt, ONES_ROWS, Q_TILE), BF16)
        for kt in range(n_kt):
            rows = slice(kt * Q_TILE, (kt + 1) * Q_TILE)
            half = slice((kt % 2) * Q_TILE, (kt % 2 + 1) * Q_TILE)
            vs_t = vs_ref[rows, :].T.astype(BF16)
            vw_t = vw_ref[rows, :].T.astype(BF16)
            for g in range(N_GROUPS):
                vst[g, kt // 2, 0:HEAD_DIM, half] = vs_t[g * HEAD_DIM:(g + 1) * HEAD_DIM]
                vwt[g, kt, 0:HEAD_DIM, :] = vw_t[g * HEAD_DIM:(g + 1) * HEAD_DIM]
        vct[...] = vc_ref[0].T.astype(BF16)

    qt = q_ref[...]
    gsig_t = _sigmoid(gt_ref[...]).T

    def col_max(s):
        return jnp.max(s.reshape(s.shape[0] // 8, 8, COLS), axis=0)

    def group_rows(g):
        return slice(g * HEAD_DIM, (g + 1) * HEAD_DIM)

    def online(gv, s, vt, m):
        m_new = jnp.maximum(m, jnp.max(col_max(s), axis=0, keepdims=True))
        e = jnp.exp2(s - m_new)
        acc[gv] = acc[gv] * jnp.exp2(m - m_new) + _dot(vt, e.astype(BF16))
        return m_new

    def fresh_state():
        acc[...] = jnp.zeros(acc.shape, F32)
        return tuple(jnp.full((1, COLS), NEG, F32) for _ in range(variants))

    def finish(branch):
        for g in range(N_GROUPS):
            halves = []
            for v in range(2):
                gv = g * 2 + v
                l = acc[gv, HEAD_DIM:HEAD_DIM + 1, :]
                halves.append(acc[gv, 0:HEAD_DIM, :] * (1.0 / jnp.maximum(l, 1e-30)))
            obuf[branch, g] = jnp.concatenate(halves, axis=0)

    groups = range(N_GROUPS)
    for g in groups:
        q4 = jnp.concatenate(
            [qt[:, (g * PAIRS_PER_GROUP + p) * LANES:(g * PAIRS_PER_GROUP + p + 1) * LANES]
             for p in range(PAIRS_PER_GROUP)], axis=0)
        rhs[g, :, 0:LANES] = (q4.astype(F32) * (HEAD_DIM ** -0.5 * LOG2E)).astype(BF16)

    bc_rows = pl.ds(pl.multiple_of(LANES - qi * (Q_TILE // CMP_STRIDE), SUBLANES), LANES)
    s_cmp = [_dot_nt(kcv[g], rhs[g, :, 0:LANES])
             + jnp.concatenate([bc_ref[g, 0, bc_rows, :], bc_ref[g, 1, bc_rows, :]], axis=0)
             for g in groups]
    psum_t = []
    for g in groups:
        tot = jnp.zeros((LANES, Q_TILE), F32)
        halves = []
        for v in range(2):
            s = s_cmp[g][v * LANES:(v + 1) * LANES]
            valid = s > 0.5 * NEG
            m = jnp.max(s, axis=0, keepdims=True)
            e = jnp.where(valid, jnp.exp2(s - m), 0.0)
            p = e * (1.0 / jnp.maximum(jnp.sum(e, axis=0, keepdims=True), 1e-30))
            for pp in range(PAIRS_PER_GROUP):
                tot = tot + p[:, pp * Q_TILE:(pp + 1) * Q_TILE]
            halves.append(_dot(vct[group_rows(g), :], p.astype(BF16)))
        obuf[0, g] = jnp.concatenate(halves, axis=0)
        psum_t.append(tot)

    ovt = ovt_ref[...]
    imp_t = []
    for g in groups:
        p_hi = psum_t[g].astype(BF16)
        r1 = psum_t[g] - p_hi.astype(F32)
        p_mid = r1.astype(BF16)
        p_lo = (r1 - p_mid.astype(F32)).astype(BF16)
        imp_t.append(_dot(ovt, p_hi) + _dot(ovt, p_mid) + _dot(ovt, p_lo))

    j_idx = lax.broadcasted_iota(jnp.int32, (n_sel, Q_TILE), 0)
    r_idx = lax.broadcasted_iota(jnp.int32, (n_sel, Q_TILE), 1)
    blk_t = qi * (Q_TILE // SEL_BLOCK) + r_idx // SEL_BLOCK
    valid_blk = j_idx <= blk_t
    forced = (j_idx == 0) | (j_idx == blk_t) | (j_idx == blk_t - 1)
    prio = [jnp.where(valid_blk, jnp.where(forced, FORCE_SCORE, imp_t[g]), -FORCE_SCORE) for g in groups]
    rank = [jnp.zeros((n_sel, Q_TILE), F32) for _ in groups]
    for jj in range(n_sel):
        later = j_idx > jj
        for g in groups:
            row = prio[g][jj:jj + 1, :]
            beats = (row > prio[g]) | ((row == prio[g]) & later)
            rank[g] = rank[g] + jnp.where(beats, 1.0, 0.0)
    for g in groups:
        drop_t = jnp.where((rank[g] < n_top) & valid_blk, 0.0, NEG)
        drop_t = jnp.concatenate([drop_t, jnp.zeros((LANES - n_sel, Q_TILE), F32)], axis=0)
        drop = drop_t.T.astype(BF16)
        rhs[g, :, LANES:2 * LANES] = jnp.concatenate([drop] * PAIRS_PER_GROUP, axis=0) + bx_ref[g]

    def logits_into(buf, i):
        for g in range(N_GROUPS):
            buf[g] = _dot_nt(ksv[g, i], rhs[g])

    def consume(buf, i, state):
        ms = list(state)
        for g in range(N_GROUPS):
            for v in range(2):
                gv = g * 2 + v
                ms[gv] = online(gv, buf[g, v * 2 * Q_TILE:(v + 1) * 2 * Q_TILE, :], vst[g, i], ms[gv])
        return tuple(ms)

    n_all = (qi + 2) // 2
    n_far = jnp.maximum(qi - 1, 0) // 2

    first = jnp.maximum(qi - (WIN_TILES - 1), 0)
    tiles = []
    for t in range(WIN_TILES):
        dd = qi - (first + t)
        table = jnp.where(dd >= 0, dd, WIN_TILES)
        tiles.append((jnp.minimum(first + t, n_kt - 1), table))
    m_win = []
    for g in range(N_GROUPS):
        mx = [jnp.full((8, COLS), NEG, F32) for _ in range(2)]
        for t, (kt, table) in enumerate(tiles):
            s_all = _dot_nt(kwv[g, kt], rhs[g, :, 0:LANES])
            for v in range(2):
                s = s_all[v * Q_TILE:(v + 1) * Q_TILE] + tw_ref[g, v, table]
                s_w[g, t, v * Q_TILE:(v + 1) * Q_TILE, :] = s
                mx[v] = jnp.maximum(mx[v], col_max(s))
        m_win += [jnp.max(mx[v], axis=0, keepdims=True) for v in range(2)]

    for g in range(N_GROUPS):
        halves = []
        for v in range(2):
            o = jnp.zeros((HEAD_DIM + ONES_ROWS, COLS), F32)
            for t, (kt, _) in enumerate(tiles):
                e = jnp.exp2(s_w[g, t, v * Q_TILE:(v + 1) * Q_TILE, :] - m_win[g * 2 + v])
                o = o + _dot(vwt[g, kt], e.astype(BF16))
            halves.append(o[0:HEAD_DIM] * (1.0 / jnp.maximum(o[HEAD_DIM:HEAD_DIM + 1], 1e-30)))
        obuf[2, g] = jnp.concatenate(halves, axis=0)

    logits_into(s_a, 0)

    def far_two(j, state):
        logits_into(s_b, 2 * j + 1)
        state = consume(s_a, 2 * j, state)
        logits_into(s_a, 2 * j + 2)
        return consume(s_b, 2 * j + 1, state)

    state = lax.fori_loop(0, n_far // 2, far_two, fresh_state())
    state = lax.fori_loop(0, n_far % 2, lambda _, st: consume(s_a, n_far - 1, st), state)

    def near_pair(i, state):
        for g in range(N_GROUPS):
            s_all = _dot_nt(ksv[g, i], rhs[g])
            for v in range(2):
                for h in range(2):
                    dd = jnp.clip(qi - (2 * i + h), 0, 2)
                    r0 = (2 * v + h) * Q_TILE
                    s_b[g, r0:r0 + Q_TILE, :] = s_all[r0:r0 + Q_TILE] + ts_ref[g, v, dd]
        return consume(s_b, i, state)

    lax.fori_loop(n_far, n_all, near_pair, state)
    finish(1)

    for g in range(N_GROUPS):
        for p in range(PAIRS_PER_GROUP):
            halves = []
            for v in range(2):
                tot = jnp.zeros((HEAD_DIM, Q_TILE), F32)
                for br in range(3):
                    c = br * N_HEADS + g * HEADS_PER_GROUP + 2 * p + v
                    tot = tot + gsig_t[c:c + 1, :] * obuf[br, g, v * HEAD_DIM:(v + 1) * HEAD_DIM,
                                                          p * Q_TILE:(p + 1) * Q_TILE]
                halves.append(tot)
            col = (g * PAIRS_PER_GROUP + p) * LANES
            o_ref[:, col:col + LANES] = jnp.concatenate(halves, axis=0).T.astype(BF16)


def _attention(proj_main, proj_kv, kcmp, vcmp, bias_c, bias_w, bias_s, ovt, far_lanes, b, s, n_top):
    n_q = s // Q_TILE
    once = pl.Buffered(1)
    kv_col = lambda c: pl.BlockSpec((s, LANES), lambda bi, qi: (bi, c))
    cmp_spec = pl.BlockSpec((1, LANES, LANES), lambda bi, qi: (bi, 0, 0))
    return pl.pallas_call(
        functools.partial(_attn_kernel, n_top=n_top),
        grid=(b, n_q),
        in_specs=[
            pl.BlockSpec((Q_TILE, NSA_WIDTH), lambda bi, qi: (bi * n_q + qi, 0)),
            kv_col(2), kv_col(3), kv_col(4), kv_col(5),
            pl.BlockSpec((Q_TILE, LANES), lambda bi, qi: (bi * n_q + qi, 6)),
            cmp_spec, cmp_spec,
            pl.BlockSpec(bias_c.shape, lambda bi, qi: (0,) * 4, pipeline_mode=once),
            pl.BlockSpec(bias_w.shape, lambda bi, qi: (0,) * 5, pipeline_mode=once),
            pl.BlockSpec(bias_s.shape, lambda bi, qi: (0,) * 5, pipeline_mode=once),
            pl.BlockSpec(ovt.shape, lambda bi, qi: (0, 0), pipeline_mode=once),
            pl.BlockSpec(far_lanes.shape, lambda bi, qi: (0, 0, 0), pipeline_mode=once),
        ],
        out_specs=pl.BlockSpec((Q_TILE, NSA_WIDTH), lambda bi, qi: (bi * n_q + qi, 0)),
        out_shape=jax.ShapeDtypeStruct((b * s, NSA_WIDTH), BF16),
        scratch_shapes=[
            pltpu.VMEM((N_GROUPS, n_q // 2, 4 * Q_TILE, 2 * LANES), BF16),
            pltpu.VMEM((N_GROUPS, n_q, 2 * Q_TILE, LANES), BF16),
            pltpu.VMEM((N_GROUPS, 2 * LANES, LANES), BF16),
            pltpu.VMEM((N_GROUPS, n_q // 2, HEAD_DIM + ONES_ROWS, 2 * Q_TILE), BF16),
            pltpu.VMEM((N_GROUPS, n_q, HEAD_DIM + ONES_ROWS, Q_TILE), BF16),
            pltpu.VMEM((LANES, LANES), BF16),
            pltpu.VMEM((N_GROUPS, COLS, 2 * LANES), BF16),
            pltpu.VMEM((2 * N_GROUPS, HEAD_DIM + ONES_ROWS, COLS), F32),
            pltpu.VMEM((3, N_GROUPS, LANES, COLS), F32),
            pltpu.VMEM((N_GROUPS, 4 * Q_TILE, COLS), F32),
            pltpu.VMEM((N_GROUPS, 4 * Q_TILE, COLS), F32),
            pltpu.VMEM((N_GROUPS, WIN_TILES, 2 * Q_TILE, COLS), F32),
        ],
        compiler_params=pltpu.CompilerParams(
            dimension_semantics=("parallel", "arbitrary"), vmem_limit_bytes=VMEM_LIMIT),
        name="nsa_attention",
    )(proj_main, proj_kv, proj_kv, proj_kv, proj_kv, proj_kv, kcmp, vcmp, bias_c, bias_w, bias_s,
      ovt, far_lanes)


def _merge_kernel(zn_ref, a_ref, b_ref, zc_ref, gc_ref, gn_ref, ah_ref, bh_ref, on_ref, x_ref,
                  cw_ref, cb_ref, lg_ref, lb_ref, wcp_ref, wnp_ref, wo_ref, gf_ref, out_ref, uext,
                  conv, shifted):
    i = pl.program_id(1)
    ts = a_ref.shape[0]
    f = lambda r: r[...].astype(F32)

    n_cblk = D_MODEL // LANES
    u_halo = jnp.where(i > 0, f(ah_ref) * _sigmoid(f(bh_ref)), 0.0)
    u = f(a_ref) * _sigmoid(f(b_ref))
    for cblk in range(n_cblk):
        cols = slice(cblk * LANES, (cblk + 1) * LANES)
        uext[cblk, 0:CONV_HALO, :] = u_halo[:, cols]
        uext[cblk, CONV_HALO:CONV_HALO + ts, :] = u[:, cols]
        uext[cblk, CONV_HALO + ts:, :] = jnp.zeros((SUBLANES, LANES), F32)

    lead = CONV_HALO - (CONV_KERNEL - 1)
    half = ts // 2

    def conv_block(cblk, carry):
        for shift in range(SUBLANES):
            shifted[shift] = uext[cblk, shift:shift + ts + CONV_HALO, :]
        w = cw_ref[cblk]
        for h in range(2):
            c = jnp.broadcast_to(cb_ref[cblk], (half, LANES))
            for shift in range(SUBLANES):
                xs = shifted[shift, h * half:h * half + half + CONV_HALO, :]
                for j in range(CONV_KERNEL):
                    if (lead + j) % SUBLANES == shift:
                        base = lead + j - shift
                        c = c + w[j:j + 1, :] * xs[base:base + half]
            conv[cblk, h * half:(h + 1) * half, :] = c
        return carry

    lax.fori_loop(0, n_cblk, conv_block, 0)
    c = jnp.concatenate([conv[cblk] for cblk in range(n_cblk)], axis=1)

    mu = jnp.mean(c, axis=-1, keepdims=True)
    cc = c - mu
    var = jnp.mean(cc * cc, axis=-1, keepdims=True)
    y = (cc * lax.rsqrt(var + EPS)) * lg_ref[...] + lb_ref[...]
    conv_act = _silu(y) * _silu(f(zc_ref))
    y_conv = _dot(conv_act.astype(BF16), wcp_ref[...])

    nsa_act = f(on_ref) * _silu(f(zn_ref))
    y_nsa = _dot(nsa_act.astype(BF16), wnp_ref[...])

    merged = _sigmoid(f(gc_ref)) * y_conv + _sigmoid(f(gn_ref)) * y_nsa
    xo = x_ref[...] + _dot(merged.astype(BF16), wo_ref[...])
    ms = jnp.mean(xo * xo, axis=-1, keepdims=True)
    out_ref[...] = (xo * lax.rsqrt(ms + EPS)) * gf_ref[...]


def _merge(proj_main, o_nsa, x2, cw, cb, lg, lb, wcp, wnp, wo, gf, b, s, ts):
    n_t = s // ts
    n_cblk = D_MODEL // LANES
    halo_per_tile = ts // CONV_HALO
    cw = jnp.transpose(cw.reshape(cw.shape[0], n_cblk, LANES), (1, 0, 2))
    cb = cb.reshape(n_cblk, 1, LANES)
    col = lambda c: pl.BlockSpec((ts, COL_TILE), lambda bi, ti: (bi * n_t + ti, c))
    halo = lambda c: pl.BlockSpec(
        (CONV_HALO, COL_TILE),
        lambda bi, ti: (jnp.maximum((bi * n_t + ti) * halo_per_tile - 1, 0), c))
    const = lambda a: pl.BlockSpec(a.shape, lambda bi, ti: (0,) * a.ndim)
    rowblk = pl.BlockSpec((ts, D_MODEL), lambda bi, ti: (bi * n_t + ti, 0))
    return pl.pallas_call(
        _merge_kernel,
        grid=(b, n_t),
        in_specs=[col(1), col(2), col(3), col(4), col(5), col(6), halo(2), halo(3), rowblk, rowblk,
                  const(cw), const(cb), const(lg), const(lb), const(wcp), const(wnp), const(wo),
                  const(gf)],
        out_specs=rowblk,
        out_shape=jax.ShapeDtypeStruct((b * s, D_MODEL), F32),
        scratch_shapes=[pltpu.VMEM((n_cblk, ts + CONV_HALO + SUBLANES, LANES), F32),
                        pltpu.VMEM((n_cblk, ts, LANES), F32),
                        pltpu.VMEM((SUBLANES, ts + CONV_HALO, LANES), F32)],
        compiler_params=pltpu.CompilerParams(
            dimension_semantics=("parallel", "arbitrary"), vmem_limit_bytes=VMEM_LIMIT),
        name="conv_merge",
    )(proj_main, proj_main, proj_main, proj_main, proj_main, proj_main, proj_main, proj_main,
      o_nsa, x2, cw, cb, lg, lb, wcp, wnp, wo, gf)


def _t5_bucket_np(rel):
    rel = np.maximum(rel, 0)
    max_exact = REL_BUCKETS // 2
    relf = np.maximum(rel, 1).astype(np.float32)
    large = max_exact + (np.log(relf / np.float32(max_exact))
                         / np.float32(np.log(REL_MAX_DIST / max_exact))
                         * np.float32(REL_BUCKETS - max_exact)).astype(np.int32)
    large = np.minimum(large, REL_BUCKETS - 1)
    return np.where(rel < max_exact, rel, large)


def _pair_head_index():
    g = np.arange(N_GROUPS)[:, None, None]
    v = np.arange(2)[None, :, None]
    p = np.arange(PAIRS_PER_GROUP)[None, None, :]
    return g * HEADS_PER_GROUP + 2 * p + v


def _bias_lookup(rel_bias, rel):
    bucket = _t5_bucket_np(rel).reshape(-1)
    onehot = (jnp.arange(REL_BUCKETS)[:, None] == jnp.asarray(bucket)[None, :]).astype(F32)
    vals = jnp.dot(rel_bias.astype(F32).T, onehot, precision=lax.Precision.HIGHEST)
    vals = vals.reshape((N_HEADS,) + rel.shape)
    head = _pair_head_index()
    return jnp.stack([vals[h] for h in head.reshape(-1)]).reshape(head.shape + rel.shape)


def _bias_tables(rel_bias, s, n_sel):
    r = np.arange(Q_TILE)[:, None]
    c = np.arange(LANES)[None, :]
    far = rel_bias.astype(F32)[REL_BUCKETS - 1][_pair_head_index()]
    rel_w = np.stack([dd * Q_TILE + r - c for dd in range(WIN_TILES)])
    assert (_t5_bucket_np(rel_w[2:]) == REL_BUCKETS - 1).all()
    ok_w = (rel_w >= 0) & (rel_w < WINDOW)
    bias_w = _bias_lookup(rel_bias, rel_w)

    def orient(t):
        t = jnp.transpose(t, (0, 1, 3, 5, 2, 4))
        return t.reshape(t.shape[:4] + (COLS,))

    tw = jnp.where(ok_w, bias_w * LOG2E, NEG)
    tw = orient(jnp.concatenate([tw, jnp.full_like(tw[:, :, :, :1], NEG)], axis=3))
    ts = jnp.where(rel_w[:3] >= 0, (bias_w[:, :, :, :3] - far[..., None, None, None]) * LOG2E, NEG)
    ts = orient(ts)
    assert s // CMP_STRIDE <= LANES + Q_TILE // CMP_STRIDE
    rho = np.arange(2 * LANES)[None, :]
    rel_c = r - ((rho - LANES) * CMP_STRIDE + CMP_BLOCK - 1)
    bc = jnp.where(rel_c >= 0, _bias_lookup(rel_bias, rel_c) * LOG2E, NEG)
    bc = jnp.transpose(bc, (0, 1, 4, 2, 3)).reshape(N_GROUPS, 2, 2 * LANES, COLS)
    far2 = far * LOG2E
    hi = far2.astype(BF16)
    lo = (far2 - hi.astype(F32)).astype(BF16)
    pieces = jnp.stack([hi[:, 0], lo[:, 0], hi[:, 1], lo[:, 1]], axis=-1)
    pieces = jnp.broadcast_to(pieces[:, :, None, :], (N_GROUPS, PAIRS_PER_GROUP, Q_TILE, 4))
    far_lanes = jnp.pad(pieces.reshape(N_GROUPS, COLS, 4),
                        ((0, 0), (0, 0), (n_sel, LANES - n_sel - 4)))
    return tw, ts, bc, far_lanes


def _overlap_t(s):
    n_cmp = (s - CMP_BLOCK) // CMP_STRIDE + 1
    cs = np.arange(LANES) * CMP_STRIDE
    ss = np.arange(s // SEL_BLOCK) * SEL_BLOCK
    ovt = ((cs[None, :] <= ss[:, None] + SEL_BLOCK - 1) & (cs[None, :] + CMP_BLOCK - 1 >= ss[:, None])
           & (np.arange(LANES)[None, :] < n_cmp))
    return jnp.asarray(ovt, BF16)


def _pad_w_in(w):
    used = NSA_WIDTH + 6 * KV_WIDTH + 3 * N_HEADS
    pad = jnp.zeros((D_MODEL, (KV_COL_TILE + 1) * COL_TILE - used), w.dtype)
    return jnp.concatenate([w[:, :used], pad, w[:, used:]], axis=1).astype(BF16)


def _compress_weights(pos, w1, w2):
    half = CMP_BLOCK // 2
    eye = jnp.eye(N_GROUPS, dtype=F32)
    w1r = w1.reshape(CMP_BLOCK, HEAD_DIM, CMP_HIDDEN)
    blk = lambda part: jnp.einsum('idn,gh->igdhn', part, eye).reshape(
        half * KV_WIDTH, N_GROUPS * CMP_HIDDEN)
    w1d = jnp.concatenate([blk(w1r[:half]), blk(w1r[half:])], axis=1).astype(BF16)
    w2d = jnp.einsum('nd,gh->gnhd', w2, eye).reshape(N_GROUPS * CMP_HIDDEN, KV_WIDTH).astype(BF16)
    tilepos = lambda part: jnp.broadcast_to(part[:, None, :], (half, N_GROUPS, HEAD_DIM)).reshape(1, -1)
    posd = jnp.concatenate([tilepos(pos[:half]), tilepos(pos[half:])], axis=0).astype(F32)
    return posd, w1d, w2d


def kernel(x, norm_in_g, w_in, pos_ck, w_ck1, w_ck2, pos_cv, w_cv1, w_cv2, rel_bias, conv_w, conv_b,
           conv_ln_g, conv_ln_b, w_conv_proj, w_nsa_proj, w_out, norm_f_g):
    b, s, d = x.shape
    n_sel = s // SEL_BLOCK
    assert d == D_MODEL and w_in.shape[0] == 1, "single-layer block with D_MODEL=1024"
    assert s % (2 * Q_TILE) == 0 and s // CMP_STRIDE <= LANES and s >= WINDOW
    assert n_sel + 4 <= LANES
    m = b * s
    x2 = x.reshape(m, d)
    row = lambda a: a.reshape(1, -1).astype(F32)

    proj_kv, proj_main = _input_projection(x2, row(norm_in_g[0]), _pad_w_in(w_in[0]), min(1024, m))

    chunks = s // CMP_STRIDE
    pk, w1k, w2k = _compress_weights(pos_ck[0], w_ck1[0], w_ck2[0])
    pv, w1v, w2v = _compress_weights(pos_cv[0], w_cv1[0], w_cv2[0])
    kcmp, vcmp = _compress(proj_kv, pk, pv, w1k, w1v, w2k, w2v, b, s)
    if chunks < LANES:
        padrows = ((0, 0), (0, LANES - chunks), (0, 0))
        kcmp, vcmp = jnp.pad(kcmp, padrows), jnp.pad(vcmp, padrows)

    bias_w, bias_s, bias_c, far_lanes = _bias_tables(rel_bias, s, n_sel)
    o_nsa = _attention(proj_main, proj_kv, kcmp, vcmp, bias_c, bias_w, bias_s, _overlap_t(s),
                       far_lanes, b, s, min(N_SELECT, n_sel))

    cw = jnp.pad(conv_w[0].astype(F32), ((0, CONV_HALO - CONV_KERNEL), (0, 0)))
    out = _merge(proj_main, o_nsa, x2, cw, row(conv_b[0]), row(conv_ln_g[0]), row(conv_ln_b[0]),
                 w_conv_proj[0].astype(BF16), w_nsa_proj[0].astype(BF16), w_out[0].astype(BF16),
                 row(norm_f_g), b, s, 256)
    return out.reshape(b, s, d)
```

```python
import functools
import math

import numpy as np
import jax
import jax.numpy as jnp
from jax import lax
from jax.experimental import pallas as pl
from jax.experimental.pallas import tpu as pltpu

F32 = jnp.float32
BF16 = jnp.bfloat16

D_MODEL = 1024
N_HEADS = 16
N_GROUPS = 2
HEADS_PER_GROUP = N_HEADS // N_GROUPS
PAIRS_PER_GROUP = HEADS_PER_GROUP // 2
HEAD_DIM = 64
NSA_WIDTH = N_HEADS * HEAD_DIM
KV_WIDTH = N_GROUPS * HEAD_DIM
CMP_BLOCK = 32
CMP_STRIDE = 16
CMP_HIDDEN = 256
SEL_BLOCK = 64
N_SELECT = 8
WINDOW = 512
Q_TILE = 128
CONV_KERNEL = 31
CONV_HALO = 32
REL_BUCKETS = 32
REL_MAX_DIST = 128
EPS = 1e-6
NEG = -1e30
FORCE_SCORE = 1e6
LOG2E = math.log2(math.e)
LANES = 128
SUBLANES = 8
ONES_ROWS = 16
COLS = PAIRS_PER_GROUP * Q_TILE
COL_TILE = 1024
N_COL_TILES = 8
KV_COL_TILE = 1
WIN_TILES = WINDOW // Q_TILE + 1
VMEM_LIMIT = 56 * 1024 * 1024


def _dot(a, b):
    return jnp.dot(a, b, preferred_element_type=F32)


def _dot_nt(a, b):
    return lax.dot_general(a, b, (((1,), (1,)), ((), ())), preferred_element_type=F32)


def _sigmoid(x):
    return 1.0 / (1.0 + jnp.exp2(x * (-LOG2E)))


def _silu(x):
    return x * _sigmoid(x)


def _proj_kernel(x_ref, g_ref, w_ref, kv_ref, main_ref, h_ref):
    j = pl.program_id(1)

    @pl.when(j == 0)
    def _():
        x = x_ref[...]
        ms = jnp.mean(x * x, axis=-1, keepdims=True)
        h_ref[...] = ((x * lax.rsqrt(ms + EPS)) * g_ref[...]).astype(BF16)

    acc = _dot(h_ref[...], w_ref[...])

    @pl.when(j == KV_COL_TILE)
    def _():
        kv_ref[...] = acc

    @pl.when(j != KV_COL_TILE)
    def _():
        main_ref[...] = acc.astype(BF16)


def _input_projection(x2, g, w_perm, tm):
    m = x2.shape[0]
    return pl.pallas_call(
        _proj_kernel,
        grid=(m // tm, N_COL_TILES),
        in_specs=[
            pl.BlockSpec((tm, D_MODEL), lambda i, j: (i, 0)),
            pl.BlockSpec((1, D_MODEL), lambda i, j: (0, 0)),
            pl.BlockSpec((D_MODEL, COL_TILE), lambda i, j: (0, j)),
        ],
        out_specs=[
            pl.BlockSpec((tm, COL_TILE), lambda i, j: (i, 0)),
            pl.BlockSpec((tm, COL_TILE), lambda i, j: (i, jnp.maximum(j - 1, 0))),
        ],
        out_shape=[
            jax.ShapeDtypeStruct((m, COL_TILE), F32),
            jax.ShapeDtypeStruct((m, (N_COL_TILES - 1) * COL_TILE), BF16),
        ],
        scratch_shapes=[pltpu.VMEM((tm, D_MODEL), BF16)],
        compiler_params=pltpu.CompilerParams(
            dimension_semantics=("parallel", "arbitrary"), vmem_limit_bytes=VMEM_LIMIT),
        name="input_projection",
    )(x2, g, w_perm)


def _compress_kernel(kf_ref, vf_ref, pk_ref, pv_ref, w1k_ref, w1v_ref, w2k_ref, w2v_ref,
                     kc_ref, vc_ref):
    def one(f_ref, pos_ref, w1_ref, w2_ref, o_ref):
        n = f_ref.shape[0] // CMP_STRIDE
        hw = N_GROUPS * CMP_HIDDEN
        first = jnp.zeros((n, hw), F32)
        second = jnp.zeros((n, hw), F32)
        for i in range(CMP_STRIDE):
            tok = f_ref[pl.ds(i, n, stride=CMP_STRIDE), :]
            lanes = slice(i * KV_WIDTH, (i + 1) * KV_WIDTH)
            first = first + _dot((tok + pos_ref[0:1, lanes]).astype(BF16), w1_ref[lanes, 0:hw])
            second = second + _dot((tok + pos_ref[1:2, lanes]).astype(BF16), w1_ref[lanes, hw:2 * hw])
        hid = first + pltpu.roll(second, n - 1, axis=0)
        o_ref[0] = _dot(_silu(hid).astype(BF16), w2_ref[...])

    one(kf_ref, pk_ref, w1k_ref, w2k_ref, kc_ref)
    one(vf_ref, pv_ref, w1v_ref, w2v_ref, vc_ref)


def _compress(proj_kv, pk, pv, w1k, w1v, w2k, w2v, b, s):
    n = s // CMP_STRIDE
    const = lambda shape: pl.BlockSpec(shape, lambda i: (0,) * len(shape))
    kv_col = lambda c: pl.BlockSpec((s, KV_WIDTH), lambda i: (i, c))
    out = pl.BlockSpec((1, n, LANES), lambda i: (i, 0, 0))
    return pl.pallas_call(
        _compress_kernel,
        grid=(b,),
        in_specs=[kv_col(0), kv_col(1), const(pk.shape), const(pv.shape), const(w1k.shape),
                  const(w1v.shape), const(w2k.shape), const(w2v.shape)],
        out_specs=[out, out],
        out_shape=[jax.ShapeDtypeStruct((b, n, LANES), F32)] * 2,
        compiler_params=pltpu.CompilerParams(
            dimension_semantics=("parallel",), vmem_limit_bytes=VMEM_LIMIT),
        name="nsa_compress",
    )(proj_kv, proj_kv, pk, pv, w1k, w1v, w2k, w2v)


def _attn_kernel(q_ref, ks_ref, vs_ref, kw_ref, vw_ref, gt_ref, kc_ref, vc_ref, bc_ref, tw_ref,
                 ts_ref, ovt_ref, bx_ref, o_ref,
                 ksv, kwv, kcv, vst, vwt, vct, rhs, acc, obuf, s_a, s_b, s_w, *, n_top):
    qi = pl.program_id(1)
    n_kt = kwv.shape[1]
    n_sel = ovt_ref.shape[0]
    variants = 2 * N_GROUPS

    @pl.when(qi == 0)
    def _prepare_kv():
        def halves(k):
            lo = lax.broadcasted_iota(jnp.int32, k.shape, 1) < HEAD_DIM
            kr = pltpu.roll(k, HEAD_DIM, axis=1)
            z = jnp.zeros_like(k)
            c = lambda a: a.astype(BF16)
            return ((c(jnp.where(lo, k, z)), c(jnp.where(lo, z, kr))),
                    (c(jnp.where(lo, kr, z)), c(jnp.where(lo, z, k))))

        def extra_lanes(shape, v):
            pair = lax.broadcasted_iota(jnp.int32, shape, 0)
            r = lax.broadcasted_iota(jnp.int32, shape, 1)
            lane = lax.broadcasted_iota(jnp.int32, shape, 2)
            one = ((lane >= n_sel + 2 * v) & (lane < n_sel + 2 * v + 2)
                   | (lane == pair * (2 * Q_TILE // SEL_BLOCK) + r // SEL_BLOCK))
            return jnp.where(one, 1.0, 0.0).astype(BF16)

        pair_shape = (n_kt // 2, 2 * Q_TILE, LANES)
        tile_shape = (n_kt, Q_TILE, LANES)
        k_sel, k_win, k_cmp = halves(ks_ref[...]), halves(kw_ref[...]), halves(kc_ref[0])
        for g in range(N_GROUPS):
            for v in range(2):
                rows = slice(v * 2 * Q_TILE, (v + 1) * 2 * Q_TILE)
                ksv[g, :, rows, 0:LANES] = k_sel[g][v].reshape(pair_shape)
                ksv[g, :, rows, LANES:2 * LANES] = extra_lanes(pair_shape, v)
                kwv[g, :, v * Q_TILE:(v + 1) * Q_TILE, :] = k_win[g][v].reshape(tile_shape)
                kcv[g, v * LANES:(v + 1) * LANES, :] = k_cmp[g][v]
        vst[:, :, HEAD_DIM:, :] = jnp.ones((N_GROUPS, n_kt // 2, ONES_ROWS, 2 * Q_TILE), BF16)
        vwt[:, :, HEAD_DIM:, :] = jnp.ones((N_GROUPS, n_kt, ONES_ROWS, Q_TILE), BF16)
        for kt in range(n_kt):
            rows = slice(kt * Q_TILE, (kt + 1) * Q_TILE)
            half = slice((kt % 2) * Q_TILE, (kt % 2 + 1) * Q_TILE)
            vs_t = vs_ref[rows, :].T.astype(BF16)
            vw_t = vw_ref[rows, :].T.astype(BF16)
            for g in range(N_GROUPS):
                vst[g, kt // 2, 0:HEAD_DIM, half] = vs_t[g * HEAD_DIM:(g + 1) * HEAD_DIM]
                vwt[g, kt, 0:HEAD_DIM, :] = vw_t[g * HEAD_DIM:(g + 1) * HEAD_DIM]
        vct[...] = vc_ref[0].T.astype(BF16)

    qt = q_ref[...]
    gsig_t = _sigmoid(gt_ref[...]).T

    def col_max(s):
        return jnp.max(s.reshape(s.shape[0] // 8, 8, COLS), axis=0)

    def group_rows(g):
        return slice(g * HEAD_DIM, (g + 1) * HEAD_DIM)

    def online(gv, s, s_max, vt, m):
        m_new = jnp.maximum(m, jnp.max(s_max, axis=0, keepdims=True))
        e = jnp.exp2(s - m_new)
        acc[gv] = acc[gv] * jnp.exp2(m - m_new) + _dot(vt, e.astype(BF16))
        return m_new

    def fresh_state():
        acc[...] = jnp.zeros(acc.shape, F32)
        return tuple(jnp.full((1, COLS), NEG, F32) for _ in range(variants))

    def finish(branch):
        for g in range(N_GROUPS):
            halves = []
            for v in range(2):
                gv = g * 2 + v
                l = acc[gv, HEAD_DIM:HEAD_DIM + 1, :]
                halves.append(acc[gv, 0:HEAD_DIM, :] * (1.0 / jnp.maximum(l, 1e-30)))
            obuf[branch, g] = jnp.concatenate(halves, axis=0)

    groups = range(N_GROUPS)
    for g in groups:
        q4 = jnp.concatenate(
            [qt[:, (g * PAIRS_PER_GROUP + p) * LANES:(g * PAIRS_PER_GROUP + p + 1) * LANES]
             for p in range(PAIRS_PER_GROUP)], axis=0)
        rhs[g, :, 0:LANES] = (q4.astype(F32) * (HEAD_DIM ** -0.5 * LOG2E)).astype(BF16)

    bc_rows = pl.ds(pl.multiple_of(LANES - qi * (Q_TILE // CMP_STRIDE), SUBLANES), LANES)
    s_cmp = [_dot_nt(kcv[g], rhs[g, :, 0:LANES])
             + jnp.concatenate([bc_ref[g, 0, bc_rows, :], bc_ref[g, 1, bc_rows, :]], axis=0)
             for g in groups]
    psum_t = []
    for g in groups:
        tot = jnp.zeros((LANES, Q_TILE), F32)
        halves = []
        for v in range(2):
            s = s_cmp[g][v * LANES:(v + 1) * LANES]
            valid = s > 0.5 * NEG
            m = jnp.max(s, axis=0, keepdims=True)
            e = jnp.where(valid, jnp.exp2(s - m), 0.0)
            p = e * (1.0 / jnp.maximum(jnp.sum(e, axis=0, keepdims=True), 1e-30))
            for pp in range(PAIRS_PER_GROUP):
                tot = tot + p[:, pp * Q_TILE:(pp + 1) * Q_TILE]
            halves.append(_dot(vct[group_rows(g), :], p.astype(BF16)))
        obuf[0, g] = jnp.concatenate(halves, axis=0)
        psum_t.append(tot)

    ovt = ovt_ref[...]
    imp_t = []
    for g in groups:
        p_hi = psum_t[g].astype(BF16)
        r1 = psum_t[g] - p_hi.astype(F32)
        p_mid = r1.astype(BF16)
        p_lo = (r1 - p_mid.astype(F32)).astype(BF16)
        imp_t.append(_dot(ovt, p_hi) + _dot(ovt, p_mid) + _dot(ovt, p_lo))

    j_idx = lax.broadcasted_iota(jnp.int32, (n_sel, Q_TILE), 0)
    r_idx = lax.broadcasted_iota(jnp.int32, (n_sel, Q_TILE), 1)
    blk_t = qi * (Q_TILE // SEL_BLOCK) + r_idx // SEL_BLOCK
    valid_blk = j_idx <= blk_t
    forced = (j_idx == 0) | (j_idx == blk_t) | (j_idx == blk_t - 1)
    prio = [jnp.where(valid_blk, jnp.where(forced, FORCE_SCORE, imp_t[g]), -FORCE_SCORE) for g in groups]
    rank = [jnp.zeros((n_sel, Q_TILE), F32) for _ in groups]
    for jj in range(n_sel):
        later = j_idx > jj
        for g in groups:
            row = prio[g][jj:jj + 1, :]
            beats = (row > prio[g]) | ((row == prio[g]) & later)
            rank[g] = rank[g] + jnp.where(beats, 1.0, 0.0)
    for g in groups:
        drop_t = jnp.where((rank[g] < n_top) & valid_blk, 0.0, NEG)
        drop_t = jnp.concatenate([drop_t, jnp.zeros((LANES - n_sel, Q_TILE), F32)], axis=0)
        drop = drop_t.T.astype(BF16)
        rhs[g, :, LANES:2 * LANES] = jnp.concatenate([drop] * PAIRS_PER_GROUP, axis=0) + bx_ref[g]

    pair_rows = 4 * Q_TILE

    def half_rows(v):
        return slice(v * 2 * Q_TILE, (v + 1) * 2 * Q_TILE)

    def max_rows(v):
        return slice(pair_rows + v * SUBLANES, pair_rows + (v + 1) * SUBLANES)

    def logits_group(buf, g, i, tables=None):
        s_all = _dot_nt(ksv[g, i], rhs[g])
        for v in range(2):
            s = s_all[half_rows(v)]
            if tables is not None:
                s = jnp.concatenate([s[h * Q_TILE:(h + 1) * Q_TILE] + ts_ref[g, v, tables[h]]
                                     for h in range(2)], axis=0)
            buf[g, half_rows(v), :] = s
            buf[g, max_rows(v), :] = col_max(s)

    def logits_into(buf, i):
        for g in range(N_GROUPS):
            logits_group(buf, g, i)

    def consume(buf, i, state):
        ms = list(state)
        for g in range(N_GROUPS):
            for v in range(2):
                gv = g * 2 + v
                ms[gv] = online(gv, buf[g, half_rows(v), :], buf[g, max_rows(v), :], vst[g, i], ms[gv])
        return tuple(ms)

    n_all = (qi + 2) // 2
    n_far = jnp.maximum(qi - 1, 0) // 2

    first = jnp.maximum(qi - (WIN_TILES - 1), 0)
    tiles = []
    for t in range(WIN_TILES):
        dd = qi - (first + t)
        table = jnp.where(dd >= 0, dd, WIN_TILES)
        tiles.append((jnp.minimum(first + t, n_kt - 1), table))
    m_win = []
    for g in range(N_GROUPS):
        mx = [jnp.full((8, COLS), NEG, F32) for _ in range(2)]
        for t, (kt, table) in enumerate(tiles):
            s_all = _dot_nt(kwv[g, kt], rhs[g, :, 0:LANES])
            for v in range(2):
                s = s_all[v * Q_TILE:(v + 1) * Q_TILE] + tw_ref[g, v, table]
                s_w[g, t, v * Q_TILE:(v + 1) * Q_TILE, :] = s
                mx[v] = jnp.maximum(mx[v], col_max(s))
        m_win += [jnp.max(mx[v], axis=0, keepdims=True) for v in range(2)]

    for g in range(N_GROUPS):
        logits_group(s_a, g, 0)
        halves = []
        for v in range(2):
            o = jnp.zeros((HEAD_DIM + ONES_ROWS, COLS), F32)
            for t, (kt, _) in enumerate(tiles):
                e = jnp.exp2(s_w[g, t, v * Q_TILE:(v + 1) * Q_TILE, :] - m_win[g * 2 + v])
                o = o + _dot(vwt[g, kt], e.astype(BF16))
            halves.append(o[0:HEAD_DIM] * (1.0 / jnp.maximum(o[HEAD_DIM:HEAD_DIM + 1], 1e-30)))
        obuf[2, g] = jnp.concatenate(halves, axis=0)

    def far_two(j, state):
        logits_into(s_b, 2 * j + 1)
        state = consume(s_a, 2 * j, state)
        logits_into(s_a, 2 * j + 2)
        return consume(s_b, 2 * j + 1, state)

    state = lax.fori_loop(0, n_far // 2, far_two, fresh_state())
    state = lax.fori_loop(0, n_far % 2, lambda _, st: consume(s_a, n_far - 1, st), state)

    def near_pair(i, state):
        tables = [jnp.clip(qi - (2 * i + h), 0, 2) for h in range(2)]
        for g in range(N_GROUPS):
            logits_group(s_b, g, i, tables)
        return consume(s_b, i, state)

    lax.fori_loop(n_far, n_all, near_pair, state)
    finish(1)

    for g in range(N_GROUPS):
        for p in range(PAIRS_PER_GROUP):
            halves = []
            for v in range(2):
                tot = jnp.zeros((HEAD_DIM, Q_TILE), F32)
                for br in range(3):
                    c = br * N_HEADS + g * HEADS_PER_GROUP + 2 * p + v
                    tot = tot + gsig_t[c:c + 1, :] * obuf[br, g, v * HEAD_DIM:(v + 1) * HEAD_DIM,
                                                          p * Q_TILE:(p + 1) * Q_TILE]
                halves.append(tot)
            col = (g * PAIRS_PER_GROUP + p) * LANES
            o_ref[:, col:col + LANES] = jnp.concatenate(halves, axis=0).T.astype(BF16)


def _attention(proj_main, proj_kv, kcmp, vcmp, bias_c, bias_w, bias_s, ovt, far_lanes, b, s, n_top):
    n_q = s // Q_TILE
    once = pl.Buffered(1)
    kv_col = lambda c: pl.BlockSpec((s, LANES), lambda bi, qi: (bi, c))
    cmp_spec = pl.BlockSpec((1, LANES, LANES), lambda bi, qi: (bi, 0, 0))
    return pl.pallas_call(
        functools.partial(_attn_kernel, n_top=n_top),
        grid=(b, n_q),
        in_specs=[
            pl.BlockSpec((Q_TILE, NSA_WIDTH), lambda bi, qi: (bi * n_q + qi, 0)),
            kv_col(2), kv_col(3), kv_col(4), kv_col(5),
            pl.BlockSpec((Q_TILE, LANES), lambda bi, qi: (bi * n_q + qi, 6)),
            cmp_spec, cmp_spec,
            pl.BlockSpec(bias_c.shape, lambda bi, qi: (0,) * 4, pipeline_mode=once),
            pl.BlockSpec(bias_w.shape, lambda bi, qi: (0,) * 5, pipeline_mode=once),
            pl.BlockSpec(bias_s.shape, lambda bi, qi: (0,) * 5, pipeline_mode=once),
            pl.BlockSpec(ovt.shape, lambda bi, qi: (0, 0), pipeline_mode=once),
            pl.BlockSpec(far_lanes.shape, lambda bi, qi: (0, 0, 0), pipeline_mode=once),
        ],
        out_specs=pl.BlockSpec((Q_TILE, NSA_WIDTH), lambda bi, qi: (bi * n_q + qi, 0)),
        out_shape=jax.ShapeDtypeStruct((b * s, NSA_WIDTH), BF16),
        scratch_shapes=[
            pltpu.VMEM((N_GROUPS, n_q // 2, 4 * Q_TILE, 2 * LANES), BF16),
            pltpu.VMEM((N_GROUPS, n_q, 2 * Q_TILE, LANES), BF16),
            pltpu.VMEM((N_GROUPS, 2 * LANES, LANES), BF16),
            pltpu.VMEM((N_GROUPS, n_q // 2, HEAD_DIM + ONES_ROWS, 2 * Q_TILE), BF16),
            pltpu.VMEM((N_GROUPS, n_q, HEAD_DIM + ONES_ROWS, Q_TILE), BF16),
            pltpu.VMEM((LANES, LANES), BF16),
            pltpu.VMEM((N_GROUPS, COLS, 2 * LANES), BF16),
            pltpu.VMEM((2 * N_GROUPS, HEAD_DIM + ONES_ROWS, COLS), F32),
            pltpu.VMEM((3, N_GROUPS, LANES, COLS), F32),
            pltpu.VMEM((N_GROUPS, 4 * Q_TILE + 2 * SUBLANES, COLS), F32),
            pltpu.VMEM((N_GROUPS, 4 * Q_TILE + 2 * SUBLANES, COLS), F32),
            pltpu.VMEM((N_GROUPS, WIN_TILES, 2 * Q_TILE, COLS), F32),
        ],
        compiler_params=pltpu.CompilerParams(
            dimension_semantics=("parallel", "arbitrary"), vmem_limit_bytes=VMEM_LIMIT),
        name="nsa_attention",
    )(proj_main, proj_kv, proj_kv, proj_kv, proj_kv, proj_kv, kcmp, vcmp, bias_c, bias_w, bias_s,
      ovt, far_lanes)


def _merge_kernel(zn_ref, a_ref, b_ref, zc_ref, gc_ref, gn_ref, ah_ref, bh_ref, on_ref, x_ref,
                  cw_ref, cb_ref, lg_ref, lb_ref, wcp_ref, wnp_ref, wo_ref, gf_ref, out_ref, uext,
                  conv, shifted):
    i = pl.program_id(1)
    ts = a_ref.shape[0]
    f = lambda r: r[...].astype(F32)

    n_cblk = D_MODEL // LANES
    u_halo = jnp.where(i > 0, f(ah_ref) * _sigmoid(f(bh_ref)), 0.0)
    u = f(a_ref) * _sigmoid(f(b_ref))
    for cblk in range(n_cblk):
        cols = slice(cblk * LANES, (cblk + 1) * LANES)
        uext[cblk, 0:CONV_HALO, :] = u_halo[:, cols]
        uext[cblk, CONV_HALO:CONV_HALO + ts, :] = u[:, cols]
        uext[cblk, CONV_HALO + ts:, :] = jnp.zeros((SUBLANES, LANES), F32)

    lead = CONV_HALO - (CONV_KERNEL - 1)
    half = ts // 2

    def conv_block(cblk, carry):
        for shift in range(SUBLANES):
            shifted[shift] = uext[cblk, shift:shift + ts + CONV_HALO, :]
        w = cw_ref[cblk]
        for h in range(2):
            c = jnp.broadcast_to(cb_ref[cblk], (half, LANES))
            for shift in range(SUBLANES):
                xs = shifted[shift, h * half:h * half + half + CONV_HALO, :]
                for j in range(CONV_KERNEL):
                    if (lead + j) % SUBLANES == shift:
                        base = lead + j - shift
                        c = c + w[j:j + 1, :] * xs[base:base + half]
            conv[cblk, h * half:(h + 1) * half, :] = c
        return carry

    lax.fori_loop(0, n_cblk, conv_block, 0)
    c = jnp.concatenate([conv[cblk] for cblk in range(n_cblk)], axis=1)

    mu = jnp.mean(c, axis=-1, keepdims=True)
    cc = c - mu
    var = jnp.mean(cc * cc, axis=-1, keepdims=True)
    y = (cc * lax.rsqrt(var + EPS)) * lg_ref[...] + lb_ref[...]
    conv_act = _silu(y) * _silu(f(zc_ref))
    y_conv = _dot(conv_act.astype(BF16), wcp_ref[...])

    nsa_act = f(on_ref) * _silu(f(zn_ref))
    y_nsa = _dot(nsa_act.astype(BF16), wnp_ref[...])

    merged = _sigmoid(f(gc_ref)) * y_conv + _sigmoid(f(gn_ref)) * y_nsa
    xo = x_ref[...] + _dot(merged.astype(BF16), wo_ref[...])
    ms = jnp.mean(xo * xo, axis=-1, keepdims=True)
    out_ref[...] = (xo * lax.rsqrt(ms + EPS)) * gf_ref[...]


def _merge(proj_main, o_nsa, x2, cw, cb, lg, lb, wcp, wnp, wo, gf, b, s, ts):
    n_t = s // ts
    n_cblk = D_MODEL // LANES
    halo_per_tile = ts // CONV_HALO
    cw = jnp.transpose(cw.reshape(cw.shape[0], n_cblk, LANES), (1, 0, 2))
    cb = cb.reshape(n_cblk, 1, LANES)
    col = lambda c: pl.BlockSpec((ts, COL_TILE), lambda bi, ti: (bi * n_t + ti, c))
    halo = lambda c: pl.BlockSpec(
        (CONV_HALO, COL_TILE),
        lambda bi, ti: (jnp.maximum((bi * n_t + ti) * halo_per_tile - 1, 0), c))
    const = lambda a: pl.BlockSpec(a.shape, lambda bi, ti: (0,) * a.ndim)
    rowblk = pl.BlockSpec((ts, D_MODEL), lambda bi, ti: (bi * n_t + ti, 0))
    return pl.pallas_call(
        _merge_kernel,
        grid=(b, n_t),
        in_specs=[col(1), col(2), col(3), col(4), col(5), col(6), halo(2), halo(3), rowblk, rowblk,
                  const(cw), const(cb), const(lg), const(lb), const(wcp), const(wnp), const(wo),
                  const(gf)],
        out_specs=rowblk,
        out_shape=jax.ShapeDtypeStruct((b * s, D_MODEL), F32),
        scratch_shapes=[pltpu.VMEM((n_cblk, ts + CONV_HALO + SUBLANES, LANES), F32),
                        pltpu.VMEM((n_cblk, ts, LANES), F32),
                        pltpu.VMEM((SUBLANES, ts + CONV_HALO, LANES), F32)],
        compiler_params=pltpu.CompilerParams(
            dimension_semantics=("parallel", "arbitrary"), vmem_limit_bytes=VMEM_LIMIT),
        name="conv_merge",
    )(proj_main, proj_main, proj_main, proj_main, proj_main, proj_main, proj_main, proj_main,
      o_nsa, x2, cw, cb, lg, lb, wcp, wnp, wo, gf)


def _t5_bucket_np(rel):
    rel = np.maximum(rel, 0)
    max_exact = REL_BUCKETS // 2
    relf = np.maximum(rel, 1).astype(np.float32)
    large = max_exact + (np.log(relf / np.float32(max_exact))
                         / np.float32(np.log(REL_MAX_DIST / max_exact))
                         * np.float32(REL_BUCKETS - max_exact)).astype(np.int32)
    large = np.minimum(large, REL_BUCKETS - 1)
    return np.where(rel < max_exact, rel, large)


def _pair_head_index():
    g = np.arange(N_GROUPS)[:, None, None]
    v = np.arange(2)[None, :, None]
    p = np.arange(PAIRS_PER_GROUP)[None, None, :]
    return g * HEADS_PER_GROUP + 2 * p + v


def _bias_lookup(rel_bias, rel):
    bucket = _t5_bucket_np(rel).reshape(-1)
    onehot = (jnp.arange(REL_BUCKETS)[:, None] == jnp.asarray(bucket)[None, :]).astype(F32)
    vals = jnp.dot(rel_bias.astype(F32).T, onehot, precision=lax.Precision.HIGHEST)
    vals = vals.reshape((N_HEADS,) + rel.shape)
    head = _pair_head_index()
    return jnp.stack([vals[h] for h in head.reshape(-1)]).reshape(head.shape + rel.shape)


def _bias_tables(rel_bias, s, n_sel):
    r = np.arange(Q_TILE)[:, None]
    c = np.arange(LANES)[None, :]
    far = rel_bias.astype(F32)[REL_BUCKETS - 1][_pair_head_index()]
    rel_w = np.stack([dd * Q_TILE + r - c for dd in range(WIN_TILES)])
    assert (_t5_bucket_np(rel_w[2:]) == REL_BUCKETS - 1).all()
    ok_w = (rel_w >= 0) & (rel_w < WINDOW)
    bias_w = _bias_lookup(rel_bias, rel_w)

    def orient(t):
        t = jnp.transpose(t, (0, 1, 3, 5, 2, 4))
        return t.reshape(t.shape[:4] + (COLS,))

    tw = jnp.where(ok_w, bias_w * LOG2E, NEG)
    tw = orient(jnp.concatenate([tw, jnp.full_like(tw[:, :, :, :1], NEG)], axis=3))
    ts = jnp.where(rel_w[:3] >= 0, (bias_w[:, :, :, :3] - far[..., None, None, None]) * LOG2E, NEG)
    ts = orient(ts)
    assert s // CMP_STRIDE <= LANES + Q_TILE // CMP_STRIDE
    rho = np.arange(2 * LANES)[None, :]
    rel_c = r - ((rho - LANES) * CMP_STRIDE + CMP_BLOCK - 1)
    bc = jnp.where(rel_c >= 0, _bias_lookup(rel_bias, rel_c) * LOG2E, NEG)
    bc = jnp.transpose(bc, (0, 1, 4, 2, 3)).reshape(N_GROUPS, 2, 2 * LANES, COLS)
    far2 = far * LOG2E
    hi = far2.astype(BF16)
    lo = (far2 - hi.astype(F32)).astype(BF16)
    pieces = jnp.stack([hi[:, 0], lo[:, 0], hi[:, 1], lo[:, 1]], axis=-1)
    pieces = jnp.broadcast_to(pieces[:, :, None, :], (N_GROUPS, PAIRS_PER_GROUP, Q_TILE, 4))
    far_lanes = jnp.pad(pieces.reshape(N_GROUPS, COLS, 4),
                        ((0, 0), (0, 0), (n_sel, LANES - n_sel - 4)))
    return tw, ts, bc, far_lanes


def _overlap_t(s):
    n_cmp = (s - CMP_BLOCK) // CMP_STRIDE + 1
    cs = np.arange(LANES) * CMP_STRIDE
    ss = np.arange(s // SEL_BLOCK) * SEL_BLOCK
    ovt = ((cs[None, :] <= ss[:, None] + SEL_BLOCK - 1) & (cs[None, :] + CMP_BLOCK - 1 >= ss[:, None])
           & (np.arange(LANES)[None, :] < n_cmp))
    return jnp.asarray(ovt, BF16)


def _pad_w_in(w):
    used = NSA_WIDTH + 6 * KV_WIDTH + 3 * N_HEADS
    w = w.astype(BF16)
    pad = jnp.zeros((D_MODEL, (KV_COL_TILE + 1) * COL_TILE - used), BF16)
    return jnp.concatenate([w[:, :used], pad, w[:, used:]], axis=1)


def _compress_weights(pos, w1, w2):
    half = CMP_BLOCK // 2
    eye = jnp.eye(N_GROUPS, dtype=F32)
    w1r = w1.reshape(CMP_BLOCK, HEAD_DIM, CMP_HIDDEN)
    blk = lambda part: jnp.einsum('idn,gh->igdhn', part, eye).reshape(
        half * KV_WIDTH, N_GROUPS * CMP_HIDDEN)
    w1d = jnp.concatenate([blk(w1r[:half]), blk(w1r[half:])], axis=1).astype(BF16)
    w2d = jnp.einsum('nd,gh->gnhd', w2, eye).reshape(N_GROUPS * CMP_HIDDEN, KV_WIDTH).astype(BF16)
    tilepos = lambda part: jnp.broadcast_to(part[:, None, :], (half, N_GROUPS, HEAD_DIM)).reshape(1, -1)
    posd = jnp.concatenate([tilepos(pos[:half]), tilepos(pos[half:])], axis=0).astype(F32)
    return posd, w1d, w2d


def kernel(x, norm_in_g, w_in, pos_ck, w_ck1, w_ck2, pos_cv, w_cv1, w_cv2, rel_bias, conv_w, conv_b,
           conv_ln_g, conv_ln_b, w_conv_proj, w_nsa_proj, w_out, norm_f_g):
    b, s, d = x.shape
    n_sel = s // SEL_BLOCK
    assert d == D_MODEL and w_in.shape[0] == 1, "single-layer block with D_MODEL=1024"
    assert s % (2 * Q_TILE) == 0 and s // CMP_STRIDE <= LANES and s >= WINDOW
    assert n_sel + 4 <= LANES
    m = b * s
    x2 = x.reshape(m, d)
    row = lambda a: a.reshape(1, -1).astype(F32)

    proj_kv, proj_main = _input_projection(x2, row(norm_in_g[0]), _pad_w_in(w_in[0]), min(1024, m))

    chunks = s // CMP_STRIDE
    pk, w1k, w2k = _compress_weights(pos_ck[0], w_ck1[0], w_ck2[0])
    pv, w1v, w2v = _compress_weights(pos_cv[0], w_cv1[0], w_cv2[0])
    kcmp, vcmp = _compress(proj_kv, pk, pv, w1k, w1v, w2k, w2v, b, s)
    if chunks < LANES:
        padrows = ((0, 0), (0, LANES - chunks), (0, 0))
        kcmp, vcmp = jnp.pad(kcmp, padrows), jnp.pad(vcmp, padrows)

    bias_w, bias_s, bias_c, far_lanes = _bias_tables(rel_bias, s, n_sel)
    o_nsa = _attention(proj_main, proj_kv, kcmp, vcmp, bias_c, bias_w, bias_s, _overlap_t(s),
                       far_lanes, b, s, min(N_SELECT, n_sel))

    cw = jnp.pad(conv_w[0].astype(F32), ((0, CONV_HALO - CONV_KERNEL), (0, 0)))
    out = _merge(proj_main, o_nsa, x2, cw, row(conv_b[0]), row(conv_ln_g[0]), row(conv_ln_b[0]),
                 w_conv_proj[0].astype(BF16), w_nsa_proj[0].astype(BF16), w_out[0].astype(BF16),
                 row(norm_f_g), b, s, 256)
    return out.reshape(b, s, d)
```

```python
import functools
import math

import numpy as np
import jax
import jax.numpy as jnp
from jax import lax
from jax.experimental import pallas as pl
from jax.experimental.pallas import tpu as pltpu

F32 = jnp.float32
BF16 = jnp.bfloat16

D_MODEL = 1024
N_HEADS = 16
N_GROUPS = 2
HEADS_PER_GROUP = N_HEADS // N_GROUPS
PAIRS_PER_GROUP = HEADS_PER_GROUP // 2
HEAD_DIM = 64
NSA_WIDTH = N_HEADS * HEAD_DIM
KV_WIDTH = N_GROUPS * HEAD_DIM
CMP_BLOCK = 32
CMP_STRIDE = 16
CMP_HIDDEN = 256
SEL_BLOCK = 64
N_SELECT = 8
WINDOW = 512
Q_TILE = 128
CONV_KERNEL = 31
CONV_HALO = 32
REL_BUCKETS = 32
REL_MAX_DIST = 128
EPS = 1e-6
NEG = -1e30
FORCE_SCORE = 1e6
LOG2E = math.log2(math.e)
LANES = 128
SUBLANES = 8
ONES_ROWS = 16
COLS = PAIRS_PER_GROUP * Q_TILE
COL_TILE = 1024
N_COL_TILES = 8
KV_COL_TILE = 1
WIN_TILES = WINDOW // Q_TILE + 1
VMEM_LIMIT = 56 * 1024 * 1024


def _dot(a, b):
    return jnp.dot(a, b, preferred_element_type=F32)


def _dot_nt(a, b):
    return lax.dot_general(a, b, (((1,), (1,)), ((), ())), preferred_element_type=F32)


def _sigmoid(x):
    return 1.0 / (1.0 + jnp.exp2(x * (-LOG2E)))


def _silu(x):
    return x * _sigmoid(x)


def _proj_kernel(x_ref, g_ref, w_ref, kv_ref, main_ref, h_ref):
    j = pl.program_id(1)

    @pl.when(j == 0)
    def _():
        x = x_ref[...]
        ms = jnp.mean(x * x, axis=-1, keepdims=True)
        h_ref[...] = ((x * lax.rsqrt(ms + EPS)) * g_ref[...]).astype(BF16)

    acc = _dot(h_ref[...], w_ref[...])

    @pl.when(j == KV_COL_TILE)
    def _():
        kv_ref[...] = acc

    @pl.when(j != KV_COL_TILE)
    def _():
        main_ref[...] = acc.astype(BF16)


def _input_projection(x2, g, w_perm, tm):
    m = x2.shape[0]
    return pl.pallas_call(
        _proj_kernel,
        grid=(m // tm, N_COL_TILES),
        in_specs=[
            pl.BlockSpec((tm, D_MODEL), lambda i, j: (i, 0)),
            pl.BlockSpec((1, D_MODEL), lambda i, j: (0, 0)),
            pl.BlockSpec((D_MODEL, COL_TILE), lambda i, j: (0, j)),
        ],
        out_specs=[
            pl.BlockSpec((tm, COL_TILE), lambda i, j: (i, 0)),
            pl.BlockSpec((tm, COL_TILE), lambda i, j: (i, jnp.maximum(j - 1, 0))),
        ],
        out_shape=[
            jax.ShapeDtypeStruct((m, COL_TILE), F32),
            jax.ShapeDtypeStruct((m, (N_COL_TILES - 1) * COL_TILE), BF16),
        ],
        scratch_shapes=[pltpu.VMEM((tm, D_MODEL), BF16)],
        compiler_params=pltpu.CompilerParams(
            dimension_semantics=("parallel", "arbitrary"), vmem_limit_bytes=VMEM_LIMIT),
        name="input_projection",
    )(x2, g, w_perm)


def _compress_kernel(kf_ref, vf_ref, pk_ref, pv_ref, w1k_ref, w1v_ref, w2k_ref, w2v_ref,
                     kc_ref, vc_ref):
    def one(f_ref, pos_ref, w1_ref, w2_ref, o_ref):
        n = f_ref.shape[0] // CMP_STRIDE
        hw = N_GROUPS * CMP_HIDDEN
        first = jnp.zeros((n, hw), F32)
        second = jnp.zeros((n, hw), F32)
        for i in range(CMP_STRIDE):
            tok = f_ref[pl.ds(i, n, stride=CMP_STRIDE), :]
            lanes = slice(i * KV_WIDTH, (i + 1) * KV_WIDTH)
            first = first + _dot((tok + pos_ref[0:1, lanes]).astype(BF16), w1_ref[lanes, 0:hw])
            second = second + _dot((tok + pos_ref[1:2, lanes]).astype(BF16), w1_ref[lanes, hw:2 * hw])
        hid = first + pltpu.roll(second, n - 1, axis=0)
        o_ref[0] = _dot(_silu(hid).astype(BF16), w2_ref[...])

    one(kf_ref, pk_ref, w1k_ref, w2k_ref, kc_ref)
    one(vf_ref, pv_ref, w1v_ref, w2v_ref, vc_ref)


def _compress(proj_kv, pk, pv, w1k, w1v, w2k, w2v, b, s):
    n = s // CMP_STRIDE
    const = lambda shape: pl.BlockSpec(shape, lambda i: (0,) * len(shape))
    kv_col = lambda c: pl.BlockSpec((s, KV_WIDTH), lambda i: (i, c))
    out = pl.BlockSpec((1, n, LANES), lambda i: (i, 0, 0))
    return pl.pallas_call(
        _compress_kernel,
        grid=(b,),
        in_specs=[kv_col(0), kv_col(1), const(pk.shape), const(pv.shape), const(w1k.shape),
                  const(w1v.shape), const(w2k.shape), const(w2v.shape)],
        out_specs=[out, out],
        out_shape=[jax.ShapeDtypeStruct((b, n, LANES), F32)] * 2,
        compiler_params=pltpu.CompilerParams(
            dimension_semantics=("parallel",), vmem_limit_bytes=VMEM_LIMIT),
        name="nsa_compress",
    )(proj_kv, proj_kv, pk, pv, w1k, w1v, w2k, w2v)


def _attn_kernel(q_ref, ks_ref, vs_ref, kw_ref, vw_ref, gt_ref, kc_ref, vc_ref, bc_ref, tw_ref,
                 ts_ref, ovt_ref, bx_ref, o_ref,
                 ksv, kwv, kcv, vst, vwt, vct, rhs, acc, obuf, s_a, s_b, s_w, *, n_top):
    qi = pl.program_id(1)
    n_kt = kwv.shape[1]
    n_sel = ovt_ref.shape[0]
    variants = 2 * N_GROUPS

    @pl.when(qi == 0)
    def _prepare_kv():
        def halves(k):
            lo = lax.broadcasted_iota(jnp.int32, k.shape, 1) < HEAD_DIM
            kr = pltpu.roll(k, HEAD_DIM, axis=1)
            z = jnp.zeros_like(k)
            c = lambda a: a.astype(BF16)
            return ((c(jnp.where(lo, k, z)), c(jnp.where(lo, z, kr))),
                    (c(jnp.where(lo, kr, z)), c(jnp.where(lo, z, k))))

        def extra_lanes(shape, v):
            pair = lax.broadcasted_iota(jnp.int32, shape, 0)
            r = lax.broadcasted_iota(jnp.int32, shape, 1)
            lane = lax.broadcasted_iota(jnp.int32, shape, 2)
            one = ((lane >= n_sel + 2 * v) & (lane < n_sel + 2 * v + 2)
                   | (lane == pair * (2 * Q_TILE // SEL_BLOCK) + r // SEL_BLOCK))
            return jnp.where(one, 1.0, 0.0).astype(BF16)

        pair_shape = (n_kt // 2, 2 * Q_TILE, LANES)
        tile_shape = (n_kt, Q_TILE, LANES)
        k_sel, k_win, k_cmp = halves(ks_ref[...]), halves(kw_ref[...]), halves(kc_ref[0])
        for g in range(N_GROUPS):
            for v in range(2):
                rows = slice(v * 2 * Q_TILE, (v + 1) * 2 * Q_TILE)
                ksv[g, :, rows, 0:LANES] = k_sel[g][v].reshape(pair_shape)
                ksv[g, :, rows, LANES:2 * LANES] = extra_lanes(pair_shape, v)
                kwv[g, :, v * Q_TILE:(v + 1) * Q_TILE, :] = k_win[g][v].reshape(tile_shape)
                kcv[g, v * LANES:(v + 1) * LANES, :] = k_cmp[g][v]
        vst[:, :, HEAD_DIM:, :] = jnp.ones((N_GROUPS, n_kt // 2, ONES_ROWS, 2 * Q_TILE), BF16)
        vwt[:, :, HEAD_DIM:, :] = jnp.ones((N_GROUPS, n_kt, ONES_ROWS, Q_TILE), BF16)
        for kt in range(n_kt):
            rows = slice(kt * Q_TILE, (kt + 1) * Q_TILE)
            half = slice((kt % 2) * Q_TILE, (kt % 2 + 1) * Q_TILE)
            vs_t = vs_ref[rows, :].T.astype(BF16)
            vw_t = vw_ref[rows, :].T.astype(BF16)
            for g in range(N_GROUPS):
                vst[g, kt // 2, 0:HEAD_DIM, half] = vs_t[g * HEAD_DIM:(g + 1) * HEAD_DIM]
                vwt[g, kt, 0:HEAD_DIM, :] = vw_t[g * HEAD_DIM:(g + 1) * HEAD_DIM]
        vct[...] = vc_ref[0].T.astype(BF16)

    qt = q_ref[...]
    gsig_t = _sigmoid(gt_ref[...]).T

    def col_max(s):
        return jnp.max(s.reshape(s.shape[0] // 8, 8, COLS), axis=0)

    def group_rows(g):
        return slice(g * HEAD_DIM, (g + 1) * HEAD_DIM)

    def online(gv, s, s_max, vt, m):
        m_new = jnp.maximum(m, jnp.max(s_max, axis=0, keepdims=True))
        e = jnp.exp2(s - m_new)
        acc[gv] = acc[gv] * jnp.exp2(m - m_new) + _dot(vt, e.astype(BF16))
        return m_new

    def fresh_state():
        acc[...] = jnp.zeros(acc.shape, F32)
        return tuple(jnp.full((1, COLS), NEG, F32) for _ in range(variants))

    def finish(branch):
        for g in range(N_GROUPS):
            halves = []
            for v in range(2):
                gv = g * 2 + v
                l = acc[gv, HEAD_DIM:HEAD_DIM + 1, :]
                halves.append(acc[gv, 0:HEAD_DIM, :] * (1.0 / jnp.maximum(l, 1e-30)))
            obuf[branch, g] = jnp.concatenate(halves, axis=0)

    groups = range(N_GROUPS)
    for g in groups:
        q4 = jnp.concatenate(
            [qt[:, (g * PAIRS_PER_GROUP + p) * LANES:(g * PAIRS_PER_GROUP + p + 1) * LANES]
             for p in range(PAIRS_PER_GROUP)], axis=0)
        rhs[g, :, 0:LANES] = (q4.astype(F32) * (HEAD_DIM ** -0.5 * LOG2E)).astype(BF16)

    bc_rows = pl.ds(pl.multiple_of(LANES - qi * (Q_TILE // CMP_STRIDE), SUBLANES), LANES)
    s_cmp = [_dot_nt(kcv[g], rhs[g, :, 0:LANES])
             + jnp.concatenate([bc_ref[g, 0, bc_rows, :], bc_ref[g, 1, bc_rows, :]], axis=0)
             for g in groups]
    psum_t = []
    for g in groups:
        tot = jnp.zeros((LANES, Q_TILE), F32)
        halves = []
        for v in range(2):
            s = s_cmp[g][v * LANES:(v + 1) * LANES]
            valid = s > 0.5 * NEG
            m = jnp.max(s, axis=0, keepdims=True)
            e = jnp.where(valid, jnp.exp2(s - m), 0.0)
            p = e * (1.0 / jnp.maximum(jnp.sum(e, axis=0, keepdims=True), 1e-30))
            for pp in range(PAIRS_PER_GROUP):
                tot = tot + p[:, pp * Q_TILE:(pp + 1) * Q_TILE]
            halves.append(_dot(vct[group_rows(g), :], p.astype(BF16)))
        obuf[0, g] = jnp.concatenate(halves, axis=0)
        psum_t.append(tot)

    ovt = ovt_ref[...]
    imp_t = []
    for g in groups:
        p_hi = psum_t[g].astype(BF16)
        r1 = psum_t[g] - p_hi.astype(F32)
        p_mid = r1.astype(BF16)
        p_lo = (r1 - p_mid.astype(F32)).astype(BF16)
        imp_t.append(_dot(ovt, p_hi) + _dot(ovt, p_mid) + _dot(ovt, p_lo))

    j_idx = lax.broadcasted_iota(jnp.int32, (n_sel, Q_TILE), 0)
    r_idx = lax.broadcasted_iota(jnp.int32, (n_sel, Q_TILE), 1)
    blk_t = qi * (Q_TILE // SEL_BLOCK) + r_idx // SEL_BLOCK
    valid_blk = j_idx <= blk_t
    forced = (j_idx == 0) | (j_idx == blk_t) | (j_idx == blk_t - 1)
    prio = [jnp.where(valid_blk, jnp.where(forced, FORCE_SCORE, imp_t[g]), -FORCE_SCORE) for g in groups]
    rank = [jnp.zeros((n_sel, Q_TILE), F32) for _ in groups]
    for jj in range(n_sel):
        later = j_idx > jj
        for g in groups:
            row = prio[g][jj:jj + 1, :]
            beats = (row > prio[g]) | ((row == prio[g]) & later)
            rank[g] = rank[g] + jnp.where(beats, 1.0, 0.0)
    for g in groups:
        drop_t = jnp.where((rank[g] < n_top) & valid_blk, 0.0, NEG)
        drop_t = jnp.concatenate([drop_t, jnp.zeros((LANES - n_sel, Q_TILE), F32)], axis=0)
        drop = drop_t.T.astype(BF16)
        rhs[g, :, LANES:2 * LANES] = jnp.concatenate([drop] * PAIRS_PER_GROUP, axis=0) + bx_ref[g]

    pair_rows = 4 * Q_TILE

    def half_rows(v):
        return slice(v * 2 * Q_TILE, (v + 1) * 2 * Q_TILE)

    def max_rows(v):
        return slice(pair_rows + v * SUBLANES, pair_rows + (v + 1) * SUBLANES)

    def logits_group(buf, g, i, tables=None):
        s_all = _dot_nt(ksv[g, i], rhs[g])
        for v in range(2):
            s = s_all[half_rows(v)]
            if tables is not None:
                s = jnp.concatenate([s[h * Q_TILE:(h + 1) * Q_TILE] + ts_ref[g, v, tables[h]]
                                     for h in range(2)], axis=0)
            buf[g, half_rows(v), :] = s
            buf[g, max_rows(v), :] = col_max(s)

    def logits_into(buf, i):
        for g in range(N_GROUPS):
            logits_group(buf, g, i)

    def consume(buf, i, state):
        ms = list(state)
        for g in range(N_GROUPS):
            for v in range(2):
                gv = g * 2 + v
                ms[gv] = online(gv, buf[g, half_rows(v), :], buf[g, max_rows(v), :], vst[g, i], ms[gv])
        return tuple(ms)

    n_all = (qi + 2) // 2
    n_far = jnp.maximum(qi - 1, 0) // 2

    first = jnp.maximum(qi - (WIN_TILES - 1), 0)
    tiles = []
    for t in range(WIN_TILES):
        dd = qi - (first + t)
        table = jnp.where(dd >= 0, dd, WIN_TILES)
        tiles.append((jnp.minimum(first + t, n_kt - 1), table))
    m_win = []
    for g in range(N_GROUPS):
        mx = [jnp.full((8, COLS), NEG, F32) for _ in range(2)]
        for t, (kt, table) in enumerate(tiles):
            s_all = _dot_nt(kwv[g, kt], rhs[g, :, 0:LANES])
            for v in range(2):
                s = s_all[v * Q_TILE:(v + 1) * Q_TILE] + tw_ref[g, v, table]
                s_w[g, t, v * Q_TILE:(v + 1) * Q_TILE, :] = s
                mx[v] = jnp.maximum(mx[v], col_max(s))
        m_win += [jnp.max(mx[v], axis=0, keepdims=True) for v in range(2)]

    for g in range(N_GROUPS):
        logits_group(s_a, g, 0)
        halves = []
        for v in range(2):
            o = jnp.zeros((HEAD_DIM + ONES_ROWS, COLS), F32)
            for t, (kt, _) in enumerate(tiles):
                e = jnp.exp2(s_w[g, t, v * Q_TILE:(v + 1) * Q_TILE, :] - m_win[g * 2 + v])
                o = o + _dot(vwt[g, kt], e.astype(BF16))
            halves.append(o[0:HEAD_DIM] * (1.0 / jnp.maximum(o[HEAD_DIM:HEAD_DIM + 1], 1e-30)))
        obuf[2, g] = jnp.concatenate(halves, axis=0)

    def far_two(j, state):
        logits_into(s_b, 2 * j + 1)
        state = consume(s_a, 2 * j, state)
        logits_into(s_a, 2 * j + 2)
        return consume(s_b, 2 * j + 1, state)

    state = lax.fori_loop(0, n_far // 2, far_two, fresh_state())
    state = lax.fori_loop(0, n_far % 2, lambda _, st: consume(s_a, n_far - 1, st), state)

    def near_pair(i, state):
        tables = [jnp.clip(qi - (2 * i + h), 0, 2) for h in range(2)]
        for g in range(N_GROUPS):
            logits_group(s_b, g, i, tables)
        return consume(s_b, i, state)

    lax.fori_loop(n_far, n_all, near_pair, state)
    finish(1)

    for g in range(N_GROUPS):
        for p in range(PAIRS_PER_GROUP):
            halves = []
            for v in range(2):
                tot = jnp.zeros((HEAD_DIM, Q_TILE), F32)
                for br in range(3):
                    c = br * N_HEADS + g * HEADS_PER_GROUP + 2 * p + v
                    tot = tot + gsig_t[c:c + 1, :] * obuf[br, g, v * HEAD_DIM:(v + 1) * HEAD_DIM,
                                                          p * Q_TILE:(p + 1) * Q_TILE]
                halves.append(tot)
            col = (g * PAIRS_PER_GROUP + p) * LANES
            o_ref[:, col:col + LANES] = jnp.concatenate(halves, axis=0).T.astype(BF16)


def _attention(proj_main, proj_kv, kcmp, vcmp, bias_c, bias_w, bias_s, ovt, far_lanes, b, s, n_top):
    n_q = s // Q_TILE
    once = pl.Buffered(1)
    kv_col = lambda c: pl.BlockSpec((s, LANES), lambda bi, qi: (bi, c))
    cmp_spec = pl.BlockSpec((1, LANES, LANES), lambda bi, qi: (bi, 0, 0))
    return pl.pallas_call(
        functools.partial(_attn_kernel, n_top=n_top),
        grid=(b, n_q),
        in_specs=[
            pl.BlockSpec((Q_TILE, NSA_WIDTH), lambda bi, qi: (bi * n_q + qi, 0)),
            kv_col(2), kv_col(3), kv_col(4), kv_col(5),
            pl.BlockSpec((Q_TILE, LANES), lambda bi, qi: (bi * n_q + qi, 6)),
            cmp_spec, cmp_spec,
            pl.BlockSpec(bias_c.shape, lambda bi, qi: (0,) * 4, pipeline_mode=once),
            pl.BlockSpec(bias_w.shape, lambda bi, qi: (0,) * 5, pipeline_mode=once),
            pl.BlockSpec(bias_s.shape, lambda bi, qi: (0,) * 5, pipeline_mode=once),
            pl.BlockSpec(ovt.shape, lambda bi, qi: (0, 0), pipeline_mode=once),
            pl.BlockSpec(far_lanes.shape, lambda bi, qi: (0, 0, 0), pipeline_mode=once),
        ],
        out_specs=pl.BlockSpec((Q_TILE, NSA_WIDTH), lambda bi, qi: (bi * n_q + qi, 0)),
        out_shape=jax.ShapeDtypeStruct((b * s, NSA_WIDTH), BF16),
        scratch_shapes=[
            pltpu.VMEM((N_GROUPS, n_q // 2, 4 * Q_TILE, 2 * LANES), BF16),
            pltpu.VMEM((N_GROUPS, n_q, 2 * Q_TILE, LANES), BF16),
            pltpu.VMEM((N_GROUPS, 2 * LANES, LANES), BF16),
            pltpu.VMEM((N_GROUPS, n_q // 2, HEAD_DIM + ONES_ROWS, 2 * Q_TILE), BF16),
            pltpu.VMEM((N_GROUPS, n_q, HEAD_DIM + ONES_ROWS, Q_TILE), BF16),
            pltpu.VMEM((LANES, LANES), BF16),
            pltpu.VMEM((N_GROUPS, COLS, 2 * LANES), BF16),
            pltpu.VMEM((2 * N_GROUPS, HEAD_DIM + ONES_ROWS, COLS), F32),
            pltpu.VMEM((3, N_GROUPS, LANES, COLS), F32),
            pltpu.VMEM((N_GROUPS, 4 * Q_TILE + 2 * SUBLANES, COLS), F32),
            pltpu.VMEM((N_GROUPS, 4 * Q_TILE + 2 * SUBLANES, COLS), F32),
            pltpu.VMEM((N_GROUPS, WIN_TILES, 2 * Q_TILE, COLS), F32),
        ],
        compiler_params=pltpu.CompilerParams(
            dimension_semantics=("parallel", "arbitrary"), vmem_limit_bytes=VMEM_LIMIT),
        name="nsa_attention",
    )(proj_main, proj_kv, proj_kv, proj_kv, proj_kv, proj_kv, kcmp, vcmp, bias_c, bias_w, bias_s,
      ovt, far_lanes)


def _merge_kernel(zn_ref, a_ref, b_ref, zc_ref, gc_ref, gn_ref, ah_ref, bh_ref, on_ref, x_ref,
                  cw_ref, cb_ref, lg_ref, lb_ref, wcp_ref, wnp_ref, wo_ref, gf_ref, out_ref, uext,
                  conv, shifted):
    i = pl.program_id(1)
    ts = a_ref.shape[0]
    f = lambda r: r[...].astype(F32)

    n_cblk = D_MODEL // LANES
    u_halo = jnp.where(i > 0, f(ah_ref) * _sigmoid(f(bh_ref)), 0.0)
    u = f(a_ref) * _sigmoid(f(b_ref))
    for cblk in range(n_cblk):
        cols = slice(cblk * LANES, (cblk + 1) * LANES)
        uext[cblk, 0:CONV_HALO, :] = u_halo[:, cols]
        uext[cblk, CONV_HALO:CONV_HALO + ts, :] = u[:, cols]
        uext[cblk, CONV_HALO + ts:, :] = jnp.zeros((SUBLANES, LANES), F32)

    lead = CONV_HALO - (CONV_KERNEL - 1)
    half = ts // 2

    def conv_block(cblk, carry):
        for shift in range(SUBLANES):
            shifted[shift] = uext[cblk, shift:shift + ts + CONV_HALO, :]
        w = cw_ref[cblk]
        for h in range(2):
            c = jnp.broadcast_to(cb_ref[cblk], (half, LANES))
            for shift in range(SUBLANES):
                xs = shifted[shift, h * half:h * half + half + CONV_HALO, :]
                for j in range(CONV_KERNEL):
                    if (lead + j) % SUBLANES == shift:
                        base = lead + j - shift
                        c = c + w[j:j + 1, :] * xs[base:base + half]
            conv[cblk, h * half:(h + 1) * half, :] = c
        return carry

    lax.fori_loop(0, n_cblk, conv_block, 0)
    c = jnp.concatenate([conv[cblk] for cblk in range(n_cblk)], axis=1)

    mu = jnp.mean(c, axis=-1, keepdims=True)
    cc = c - mu
    var = jnp.mean(cc * cc, axis=-1, keepdims=True)
    y = (cc * lax.rsqrt(var + EPS)) * lg_ref[...] + lb_ref[...]
    conv_act = _silu(y) * _silu(f(zc_ref))
    y_conv = _dot(conv_act.astype(BF16), wcp_ref[...])

    nsa_act = f(on_ref) * _silu(f(zn_ref))
    y_nsa = _dot(nsa_act.astype(BF16), wnp_ref[...])

    merged = _sigmoid(f(gc_ref)) * y_conv + _sigmoid(f(gn_ref)) * y_nsa
    xo = x_ref[...] + _dot(merged.astype(BF16), wo_ref[...])
    ms = jnp.mean(xo * xo, axis=-1, keepdims=True)
    out_ref[...] = (xo * lax.rsqrt(ms + EPS)) * gf_ref[...]


def _merge(proj_main, o_nsa, x2, cw, cb, lg, lb, wcp, wnp, wo, gf, b, s, ts):
    n_t = s // ts
    n_cblk = D_MODEL // LANES
    halo_per_tile = ts // CONV_HALO
    cw = jnp.transpose(cw.reshape(cw.shape[0], n_cblk, LANES), (1, 0, 2))
    cb = cb.reshape(n_cblk, 1, LANES)
    col = lambda c: pl.BlockSpec((ts, COL_TILE), lambda bi, ti: (bi * n_t + ti, c))
    halo = lambda c: pl.BlockSpec(
        (CONV_HALO, COL_TILE),
        lambda bi, ti: (jnp.maximum((bi * n_t + ti) * halo_per_tile - 1, 0), c))
    const = lambda a: pl.BlockSpec(a.shape, lambda bi, ti: (0,) * a.ndim)
    rowblk = pl.BlockSpec((ts, D_MODEL), lambda bi, ti: (bi * n_t + ti, 0))
    return pl.pallas_call(
        _merge_kernel,
        grid=(b, n_t),
        in_specs=[col(1), col(2), col(3), col(4), col(5), col(6), halo(2), halo(3), rowblk, rowblk,
                  const(cw), const(cb), const(lg), const(lb), const(wcp), const(wnp), const(wo),
                  const(gf)],
        out_specs=rowblk,
        out_shape=jax.ShapeDtypeStruct((b * s, D_MODEL), F32),
        scratch_shapes=[pltpu.VMEM((n_cblk, ts + CONV_HALO + SUBLANES, LANES), F32),
                        pltpu.VMEM((n_cblk, ts, LANES), F32),
                        pltpu.VMEM((SUBLANES, ts + CONV_HALO, LANES), F32)],
        compiler_params=pltpu.CompilerParams(
            dimension_semantics=("parallel", "arbitrary"), vmem_limit_bytes=VMEM_LIMIT),
        name="conv_merge",
    )(proj_main, proj_main, proj_main, proj_main, proj_main, proj_main, proj_main, proj_main,
      o_nsa, x2, cw, cb, lg, lb, wcp, wnp, wo, gf)


def _t5_bucket_np(rel):
    rel = np.maximum(rel, 0)
    max_exact = REL_BUCKETS // 2
    relf = np.maximum(rel, 1).astype(np.float32)
    large = max_exact + (np.log(relf / np.float32(max_exact))
                         / np.float32(np.log(REL_MAX_DIST / max_exact))
                         * np.float32(REL_BUCKETS - max_exact)).astype(np.int32)
    large = np.minimum(large, REL_BUCKETS - 1)
    return np.where(rel < max_exact, rel, large)


def _pair_head_index():
    g = np.arange(N_GROUPS)[:, None, None]
    v = np.arange(2)[None, :, None]
    p = np.arange(PAIRS_PER_GROUP)[None, None, :]
    return g * HEADS_PER_GROUP + 2 * p + v


def _bias_lookup(rel_bias, rel):
    bucket = _t5_bucket_np(rel).reshape(-1)
    onehot = (jnp.arange(REL_BUCKETS)[:, None] == jnp.asarray(bucket)[None, :]).astype(F32)
    vals = jnp.dot(rel_bias.astype(F32).T * LOG2E, onehot, precision=lax.Precision.HIGHEST)
    vals = vals.reshape((N_HEADS,) + rel.shape)
    head = _pair_head_index()
    return jnp.stack([jnp.stack([jnp.concatenate([vals[h] for h in head[g, v]], axis=-1)
                                 for v in range(2)]) for g in range(N_GROUPS)])


def _bias_tables(rel_bias, s, n_sel):
    c = np.arange(LANES)[:, None]
    r = np.arange(Q_TILE)[None, :]
    far = rel_bias.astype(F32)[REL_BUCKETS - 1][_pair_head_index()] * LOG2E
    assert s // CMP_STRIDE <= LANES + Q_TILE // CMP_STRIDE
    rel_w = np.stack([dd * Q_TILE + r - c for dd in range(WIN_TILES)])
    assert (_t5_bucket_np(rel_w[2:]) == REL_BUCKETS - 1).all()
    rho = np.arange(2 * LANES)[:, None]
    rel_c = r - ((rho - LANES) * CMP_STRIDE + CMP_BLOCK - 1)
    vals = _bias_lookup(rel_bias, np.concatenate([rel_w, rel_c.reshape(2, LANES, Q_TILE)]))
    bias_w, bias_c = vals[:, :, :WIN_TILES], vals[:, :, WIN_TILES:].reshape(N_GROUPS, 2, 2 * LANES, COLS)
    cols = lambda a: np.tile(a, (1,) * (a.ndim - 1) + (PAIRS_PER_GROUP,))

    tw = jnp.where(cols((rel_w >= 0) & (rel_w < WINDOW)), bias_w, NEG)
    tw = jnp.concatenate([tw, jnp.full_like(tw[:, :, :1], NEG)], axis=2)
    far_cols = jnp.repeat(far, Q_TILE, axis=-1)[:, :, None, None, :]
    ts = jnp.where(cols(rel_w[:3] >= 0), bias_w[:, :, :3] - far_cols, NEG)
    bc = jnp.where(cols(rel_c >= 0), bias_c, NEG)
    hi = far.astype(BF16)
    lo = (far - hi.astype(F32)).astype(BF16)
    pieces = jnp.stack([hi[:, 0], lo[:, 0], hi[:, 1], lo[:, 1]], axis=-1)
    pieces = jnp.broadcast_to(pieces[:, :, None, :], (N_GROUPS, PAIRS_PER_GROUP, Q_TILE, 4))
    far_lanes = jnp.pad(pieces.reshape(N_GROUPS, COLS, 4),
                        ((0, 0), (0, 0), (n_sel, LANES - n_sel - 4)))
    return tw, ts, bc, far_lanes


def _overlap_t(s):
    n_cmp = (s - CMP_BLOCK) // CMP_STRIDE + 1
    cs = np.arange(LANES) * CMP_STRIDE
    ss = np.arange(s // SEL_BLOCK) * SEL_BLOCK
    ovt = ((cs[None, :] <= ss[:, None] + SEL_BLOCK - 1) & (cs[None, :] + CMP_BLOCK - 1 >= ss[:, None])
           & (np.arange(LANES)[None, :] < n_cmp))
    return jnp.asarray(ovt, BF16)


def _pad_w_in(w):
    used = NSA_WIDTH + 6 * KV_WIDTH + 3 * N_HEADS
    w = w.astype(BF16)
    pad = jnp.zeros((D_MODEL, (KV_COL_TILE + 1) * COL_TILE - used), BF16)
    return jnp.concatenate([w[:, :used], pad, w[:, used:]], axis=1)


def _compress_weights(pos, w1, w2):
    half = CMP_BLOCK // 2
    eye = jnp.eye(N_GROUPS, dtype=F32)
    w1r = w1.reshape(CMP_BLOCK, HEAD_DIM, CMP_HIDDEN)
    blk = lambda part: jnp.einsum('idn,gh->igdhn', part, eye).reshape(
        half * KV_WIDTH, N_GROUPS * CMP_HIDDEN)
    w1d = jnp.concatenate([blk(w1r[:half]), blk(w1r[half:])], axis=1).astype(BF16)
    w2d = jnp.einsum('nd,gh->gnhd', w2, eye).reshape(N_GROUPS * CMP_HIDDEN, KV_WIDTH).astype(BF16)
    tilepos = lambda part: jnp.broadcast_to(part[:, None, :], (half, N_GROUPS, HEAD_DIM)).reshape(1, -1)
    posd = jnp.concatenate([tilepos(pos[:half]), tilepos(pos[half:])], axis=0).astype(F32)
    return posd, w1d, w2d


def kernel(x, norm_in_g, w_in, pos_ck, w_ck1, w_ck2, pos_cv, w_cv1, w_cv2, rel_bias, conv_w, conv_b,
           conv_ln_g, conv_ln_b, w_conv_proj, w_nsa_proj, w_out, norm_f_g):
    b, s, d = x.shape
    n_sel = s // SEL_BLOCK
    assert d == D_MODEL and w_in.shape[0] == 1, "single-layer block with D_MODEL=1024"
    assert s % (2 * Q_TILE) == 0 and s // CMP_STRIDE <= LANES and s >= WINDOW
    assert n_sel + 4 <= LANES
    m = b * s
    x2 = x.reshape(m, d)
    row = lambda a: a.reshape(1, -1).astype(F32)

    proj_kv, proj_main = _input_projection(x2, row(norm_in_g[0]), _pad_w_in(w_in[0]), min(1024, m))

    chunks = s // CMP_STRIDE
    pk, w1k, w2k = _compress_weights(pos_ck[0], w_ck1[0], w_ck2[0])
    pv, w1v, w2v = _compress_weights(pos_cv[0], w_cv1[0], w_cv2[0])
    kcmp, vcmp = _compress(proj_kv, pk, pv, w1k, w1v, w2k, w2v, b, s)
    if chunks < LANES:
        padrows = ((0, 0), (0, LANES - chunks), (0, 0))
        kcmp, vcmp = jnp.pad(kcmp, padrows), jnp.pad(vcmp, padrows)

    bias_w, bias_s, bias_c, far_lanes = _bias_tables(rel_bias, s, n_sel)
    o_nsa = _attention(proj_main, proj_kv, kcmp, vcmp, bias_c, bias_w, bias_s, _overlap_t(s),
                       far_lanes, b, s, min(N_SELECT, n_sel))

    cw = jnp.pad(conv_w[0].astype(F32), ((0, CONV_HALO - CONV_KERNEL), (0, 0)))
    out = _merge(proj_main, o_nsa, x2, cw, row(conv_b[0]), row(conv_ln_g[0]), row(conv_ln_b[0]),
                 w_conv_proj[0].astype(BF16), w_nsa_proj[0].astype(BF16), w_out[0].astype(BF16),
                 row(norm_f_g), b, s, 256)
    return out.reshape(b, s, d)
```

```python
import functools
import math

import numpy as np
import jax
import jax.numpy as jnp
from jax import lax
from jax.experimental import pallas as pl
from jax.experimental.pallas import tpu as pltpu

F32 = jnp.float32
BF16 = jnp.bfloat16

D_MODEL = 1024
N_HEADS = 16
N_GROUPS = 2
HEADS_PER_GROUP = N_HEADS // N_GROUPS
PAIRS_PER_GROUP = HEADS_PER_GROUP // 2
HEAD_DIM = 64
NSA_WIDTH = N_HEADS * HEAD_DIM
KV_WIDTH = N_GROUPS * HEAD_DIM
CMP_BLOCK = 32
CMP_STRIDE = 16
CMP_HIDDEN = 256
SEL_BLOCK = 64
N_SELECT = 8
WINDOW = 512
Q_TILE = 128
CONV_KERNEL = 31
CONV_HALO = 32
REL_BUCKETS = 32
REL_MAX_DIST = 128
EPS = 1e-6
NEG = -1e30
FORCE_SCORE = 1e6
LOG2E = math.log2(math.e)
LANES = 128
SUBLANES = 8
ONES_ROWS = 16
COLS = PAIRS_PER_GROUP * Q_TILE
COL_TILE = 1024
N_COL_TILES = 8
KV_COL_TILE = 1
MERGE_ROWS = 512
WIN_TILES = WINDOW // Q_TILE + 1
VMEM_LIMIT = 56 * 1024 * 1024


def _dot(a, b):
    return jnp.dot(a, b, preferred_element_type=F32)


def _dot_nt(a, b):
    return lax.dot_general(a, b, (((1,), (1,)), ((), ())), preferred_element_type=F32)


def _sigmoid(x):
    return 1.0 / (1.0 + jnp.exp2(x * (-LOG2E)))


def _silu(x):
    return x * _sigmoid(x)


def _proj_kernel(x_ref, g_ref, w_ref, kv_ref, main_ref, h_ref):
    j = pl.program_id(1)

    @pl.when(j == 0)
    def _():
        x = x_ref[...]
        ms = jnp.mean(x * x, axis=-1, keepdims=True)
        h_ref[...] = ((x * lax.rsqrt(ms + EPS)) * g_ref[...]).astype(BF16)

    acc = _dot(h_ref[...], w_ref[...])

    @pl.when(j == KV_COL_TILE)
    def _():
        kv_ref[...] = acc

    @pl.when(j != KV_COL_TILE)
    def _():
        main_ref[...] = acc.astype(BF16)


def _input_projection(x2, g, w_perm, tm):
    m = x2.shape[0]
    return pl.pallas_call(
        _proj_kernel,
        grid=(m // tm, N_COL_TILES),
        in_specs=[
            pl.BlockSpec((tm, D_MODEL), lambda i, j: (i, 0)),
            pl.BlockSpec((1, D_MODEL), lambda i, j: (0, 0)),
            pl.BlockSpec((D_MODEL, COL_TILE), lambda i, j: (0, j)),
        ],
        out_specs=[
            pl.BlockSpec((tm, COL_TILE), lambda i, j: (i, 0)),
            pl.BlockSpec((tm, COL_TILE), lambda i, j: (i, jnp.maximum(j - 1, 0))),
        ],
        out_shape=[
            jax.ShapeDtypeStruct((m, COL_TILE), F32),
            jax.ShapeDtypeStruct((m, (N_COL_TILES - 1) * COL_TILE), BF16),
        ],
        scratch_shapes=[pltpu.VMEM((tm, D_MODEL), BF16)],
        compiler_params=pltpu.CompilerParams(
            dimension_semantics=("parallel", "arbitrary"), vmem_limit_bytes=VMEM_LIMIT),
        name="input_projection",
    )(x2, g, w_perm)


def _compress_kernel(kf_ref, vf_ref, pk_ref, pv_ref, w1k_ref, w1v_ref, w2k_ref, w2v_ref,
                     kc_ref, vc_ref):
    def one(f_ref, pos_ref, w1_ref, w2_ref, o_ref):
        n = f_ref.shape[0] // CMP_STRIDE
        hw = N_GROUPS * CMP_HIDDEN
        first = jnp.zeros((n, hw), F32)
        second = jnp.zeros((n, hw), F32)
        for i in range(CMP_STRIDE):
            tok = f_ref[pl.ds(i, n, stride=CMP_STRIDE), :]
            lanes = slice(i * KV_WIDTH, (i + 1) * KV_WIDTH)
            first = first + _dot((tok + pos_ref[0:1, lanes]).astype(BF16), w1_ref[lanes, 0:hw])
            second = second + _dot((tok + pos_ref[1:2, lanes]).astype(BF16), w1_ref[lanes, hw:2 * hw])
        hid = first + pltpu.roll(second, n - 1, axis=0)
        o_ref[0] = _dot(_silu(hid).astype(BF16), w2_ref[...])

    one(kf_ref, pk_ref, w1k_ref, w2k_ref, kc_ref)
    one(vf_ref, pv_ref, w1v_ref, w2v_ref, vc_ref)


def _compress(proj_kv, pk, pv, w1k, w1v, w2k, w2v, b, s):
    n = s // CMP_STRIDE
    const = lambda shape: pl.BlockSpec(shape, lambda i: (0,) * len(shape))
    kv_col = lambda c: pl.BlockSpec((s, KV_WIDTH), lambda i: (i, c))
    out = pl.BlockSpec((1, n, LANES), lambda i: (i, 0, 0))
    return pl.pallas_call(
        _compress_kernel,
        grid=(b,),
        in_specs=[kv_col(0), kv_col(1), const(pk.shape), const(pv.shape), const(w1k.shape),
                  const(w1v.shape), const(w2k.shape), const(w2v.shape)],
        out_specs=[out, out],
        out_shape=[jax.ShapeDtypeStruct((b, n, LANES), F32)] * 2,
        compiler_params=pltpu.CompilerParams(
            dimension_semantics=("parallel",), vmem_limit_bytes=VMEM_LIMIT),
        name="nsa_compress",
    )(proj_kv, proj_kv, pk, pv, w1k, w1v, w2k, w2v)


def _attn_kernel(q_ref, ks_ref, vs_ref, kw_ref, vw_ref, gt_ref, kc_ref, vc_ref, bc_ref, tw_ref,
                 ts_ref, ovt_ref, bx_ref, o_ref,
                 ksv, kwv, kcv, vst, vwt, vct, rhs, acc, obuf, s_a, s_b, s_w, *, n_top):
    qi = pl.program_id(1)
    n_kt = kwv.shape[1]
    n_sel = ovt_ref.shape[0]
    variants = 2 * N_GROUPS

    @pl.when(qi == 0)
    def _prepare_kv():
        def halves(k):
            lo = lax.broadcasted_iota(jnp.int32, k.shape, 1) < HEAD_DIM
            kr = pltpu.roll(k, HEAD_DIM, axis=1)
            z = jnp.zeros_like(k)
            c = lambda a: a.astype(BF16)
            return ((c(jnp.where(lo, k, z)), c(jnp.where(lo, z, kr))),
                    (c(jnp.where(lo, kr, z)), c(jnp.where(lo, z, k))))

        def extra_lanes(shape, v):
            pair = lax.broadcasted_iota(jnp.int32, shape, 0)
            r = lax.broadcasted_iota(jnp.int32, shape, 1)
            lane = lax.broadcasted_iota(jnp.int32, shape, 2)
            one = ((lane >= n_sel + 2 * v) & (lane < n_sel + 2 * v + 2)
                   | (lane == pair * (2 * Q_TILE // SEL_BLOCK) + r // SEL_BLOCK))
            return jnp.where(one, 1.0, 0.0).astype(BF16)

        pair_shape = (n_kt // 2, 2 * Q_TILE, LANES)
        tile_shape = (n_kt, Q_TILE, LANES)
        k_sel, k_win, k_cmp = halves(ks_ref[...]), halves(kw_ref[...]), halves(kc_ref[0])
        for g in range(N_GROUPS):
            for v in range(2):
                rows = slice(v * 2 * Q_TILE, (v + 1) * 2 * Q_TILE)
                ksv[g, :, rows, 0:LANES] = k_sel[g][v].reshape(pair_shape)
                ksv[g, :, rows, LANES:2 * LANES] = extra_lanes(pair_shape, v)
                kwv[g, :, v * Q_TILE:(v + 1) * Q_TILE, :] = k_win[g][v].reshape(tile_shape)
                kcv[g, v * LANES:(v + 1) * LANES, :] = k_cmp[g][v]
        vst[:, :, HEAD_DIM:, :] = jnp.ones((N_GROUPS, n_kt // 2, ONES_ROWS, 2 * Q_TILE), BF16)
        vwt[:, :, HEAD_DIM:, :] = jnp.ones((N_GROUPS, n_kt, ONES_ROWS, Q_TILE), BF16)
        for kt in range(n_kt):
            rows = slice(kt * Q_TILE, (kt + 1) * Q_TILE)
            half = slice((kt % 2) * Q_TILE, (kt % 2 + 1) * Q_TILE)
            vs_t = vs_ref[rows, :].T.astype(BF16)
            vw_t = vw_ref[rows, :].T.astype(BF16)
            for g in range(N_GROUPS):
                vst[g, kt // 2, 0:HEAD_DIM, half] = vs_t[g * HEAD_DIM:(g + 1) * HEAD_DIM]
                vwt[g, kt, 0:HEAD_DIM, :] = vw_t[g * HEAD_DIM:(g + 1) * HEAD_DIM]
        vct[...] = vc_ref[0].T.astype(BF16)

    qt = q_ref[...]
    gsig_t = _sigmoid(gt_ref[...]).T

    def col_max(s):
        return jnp.max(s.reshape(s.shape[0] // 8, 8, COLS), axis=0)

    def group_rows(g):
        return slice(g * HEAD_DIM, (g + 1) * HEAD_DIM)

    def online(gv, s, s_max, vt, m):
        m_new = jnp.maximum(m, jnp.max(s_max, axis=0, keepdims=True))
        e = jnp.exp2(s - m_new)
        acc[gv] = acc[gv] * jnp.exp2(m - m_new) + _dot(vt, e.astype(BF16))
        return m_new

    def fresh_state():
        acc[...] = jnp.zeros(acc.shape, F32)
        return tuple(jnp.full((1, COLS), NEG, F32) for _ in range(variants))

    def finish(branch):
        for g in range(N_GROUPS):
            halves = []
            for v in range(2):
                gv = g * 2 + v
                l = acc[gv, HEAD_DIM:HEAD_DIM + 1, :]
                halves.append(acc[gv, 0:HEAD_DIM, :] * (1.0 / jnp.maximum(l, 1e-30)))
            obuf[branch, g] = jnp.concatenate(halves, axis=0)

    groups = range(N_GROUPS)
    for g in groups:
        q4 = jnp.concatenate(
            [qt[:, (g * PAIRS_PER_GROUP + p) * LANES:(g * PAIRS_PER_GROUP + p + 1) * LANES]
             for p in range(PAIRS_PER_GROUP)], axis=0)
        rhs[g, :, 0:LANES] = (q4.astype(F32) * (HEAD_DIM ** -0.5 * LOG2E)).astype(BF16)

    bc_rows = pl.ds(pl.multiple_of(LANES - qi * (Q_TILE // CMP_STRIDE), SUBLANES), LANES)
    s_cmp = [_dot_nt(kcv[g], rhs[g, :, 0:LANES])
             + jnp.concatenate([bc_ref[g, 0, bc_rows, :], bc_ref[g, 1, bc_rows, :]], axis=0)
             for g in groups]
    psum_t = []
    for g in groups:
        tot = jnp.zeros((LANES, Q_TILE), F32)
        halves = []
        for v in range(2):
            s = s_cmp[g][v * LANES:(v + 1) * LANES]
            valid = s > 0.5 * NEG
            m = jnp.max(s, axis=0, keepdims=True)
            e = jnp.where(valid, jnp.exp2(s - m), 0.0)
            p = e * (1.0 / jnp.maximum(jnp.sum(e, axis=0, keepdims=True), 1e-30))
            for pp in range(PAIRS_PER_GROUP):
                tot = tot + p[:, pp * Q_TILE:(pp + 1) * Q_TILE]
            halves.append(_dot(vct[group_rows(g), :], p.astype(BF16)))
        obuf[0, g] = jnp.concatenate(halves, axis=0)
        psum_t.append(tot)

    ovt = ovt_ref[...]
    imp_t = []
    for g in groups:
        p_hi = psum_t[g].astype(BF16)
        r1 = psum_t[g] - p_hi.astype(F32)
        p_mid = r1.astype(BF16)
        p_lo = (r1 - p_mid.astype(F32)).astype(BF16)
        imp_t.append(_dot(ovt, p_hi) + _dot(ovt, p_mid) + _dot(ovt, p_lo))

    j_idx = lax.broadcasted_iota(jnp.int32, (n_sel, Q_TILE), 0)
    r_idx = lax.broadcasted_iota(jnp.int32, (n_sel, Q_TILE), 1)
    blk_t = qi * (Q_TILE // SEL_BLOCK) + r_idx // SEL_BLOCK
    valid_blk = j_idx <= blk_t
    forced = (j_idx == 0) | (j_idx == blk_t) | (j_idx == blk_t - 1)
    prio = [jnp.where(valid_blk, jnp.where(forced, FORCE_SCORE, imp_t[g]), -FORCE_SCORE) for g in groups]
    rank = [jnp.zeros((n_sel, Q_TILE), F32) for _ in groups]
    for jj in range(n_sel):
        later = j_idx > jj
        for g in groups:
            row = prio[g][jj:jj + 1, :]
            beats = (row > prio[g]) | ((row == prio[g]) & later)
            rank[g] = rank[g] + jnp.where(beats, 1.0, 0.0)
    for g in groups:
        drop_t = jnp.where((rank[g] < n_top) & valid_blk, 0.0, NEG)
        drop_t = jnp.concatenate([drop_t, jnp.zeros((LANES - n_sel, Q_TILE), F32)], axis=0)
        drop = drop_t.T.astype(BF16)
        rhs[g, :, LANES:2 * LANES] = jnp.concatenate([drop] * PAIRS_PER_GROUP, axis=0) + bx_ref[g]

    pair_rows = 4 * Q_TILE

    def half_rows(v):
        return slice(v * 2 * Q_TILE, (v + 1) * 2 * Q_TILE)

    def max_rows(v):
        return slice(pair_rows + v * SUBLANES, pair_rows + (v + 1) * SUBLANES)

    def logits_group(buf, g, i, tables=None):
        s_all = _dot_nt(ksv[g, i], rhs[g])
        for v in range(2):
            s = s_all[half_rows(v)]
            if tables is not None:
                s = jnp.concatenate([s[h * Q_TILE:(h + 1) * Q_TILE] + ts_ref[g, v, tables[h]]
                                     for h in range(2)], axis=0)
            buf[g, half_rows(v), :] = s
            buf[g, max_rows(v), :] = col_max(s)

    def logits_into(buf, i):
        for g in range(N_GROUPS):
            logits_group(buf, g, i)

    def consume(buf, i, state):
        ms = list(state)
        for g in range(N_GROUPS):
            for v in range(2):
                gv = g * 2 + v
                ms[gv] = online(gv, buf[g, half_rows(v), :], buf[g, max_rows(v), :], vst[g, i], ms[gv])
        return tuple(ms)

    n_all = (qi + 2) // 2
    n_far = jnp.maximum(qi - 1, 0) // 2

    first = jnp.maximum(qi - (WIN_TILES - 1), 0)
    tiles = []
    for t in range(WIN_TILES):
        dd = qi - (first + t)
        table = jnp.where(dd >= 0, dd, WIN_TILES)
        tiles.append((jnp.minimum(first + t, n_kt - 1), table))
    m_win = []
    for g in range(N_GROUPS):
        mx = [jnp.full((8, COLS), NEG, F32) for _ in range(2)]
        for t, (kt, table) in enumerate(tiles):
            s_all = _dot_nt(kwv[g, kt], rhs[g, :, 0:LANES])
            for v in range(2):
                s = s_all[v * Q_TILE:(v + 1) * Q_TILE] + tw_ref[g, v, table]
                s_w[g, t, v * Q_TILE:(v + 1) * Q_TILE, :] = s
                mx[v] = jnp.maximum(mx[v], col_max(s))
        m_win += [jnp.max(mx[v], axis=0, keepdims=True) for v in range(2)]

    for g in range(N_GROUPS):
        logits_group(s_a, g, 0)
        halves = []
        for v in range(2):
            o = jnp.zeros((HEAD_DIM + ONES_ROWS, COLS), F32)
            for t, (kt, _) in enumerate(tiles):
                e = jnp.exp2(s_w[g, t, v * Q_TILE:(v + 1) * Q_TILE, :] - m_win[g * 2 + v])
                o = o + _dot(vwt[g, kt], e.astype(BF16))
            halves.append(o[0:HEAD_DIM] * (1.0 / jnp.maximum(o[HEAD_DIM:HEAD_DIM + 1], 1e-30)))
        obuf[2, g] = jnp.concatenate(halves, axis=0)

    def far_two(j, state):
        logits_into(s_b, 2 * j + 1)
        state = consume(s_a, 2 * j, state)
        logits_into(s_a, 2 * j + 2)
        return consume(s_b, 2 * j + 1, state)

    state = lax.fori_loop(0, n_far // 2, far_two, fresh_state())
    state = lax.fori_loop(0, n_far % 2, lambda _, st: consume(s_a, n_far - 1, st), state)

    def near_pair(i, state):
        tables = [jnp.clip(qi - (2 * i + h), 0, 2) for h in range(2)]
        for g in range(N_GROUPS):
            logits_group(s_b, g, i, tables)
        return consume(s_b, i, state)

    lax.fori_loop(n_far, n_all, near_pair, state)
    finish(1)

    for g in range(N_GROUPS):
        for p in range(PAIRS_PER_GROUP):
            halves = []
            for v in range(2):
                tot = jnp.zeros((HEAD_DIM, Q_TILE), F32)
                for br in range(3):
                    c = br * N_HEADS + g * HEADS_PER_GROUP + 2 * p + v
                    tot = tot + gsig_t[c:c + 1, :] * obuf[br, g, v * HEAD_DIM:(v + 1) * HEAD_DIM,
                                                          p * Q_TILE:(p + 1) * Q_TILE]
                halves.append(tot)
            col = (g * PAIRS_PER_GROUP + p) * LANES
            o_ref[:, col:col + LANES] = jnp.concatenate(halves, axis=0).T.astype(BF16)


def _attention(proj_main, proj_kv, kcmp, vcmp, bias_c, bias_w, bias_s, ovt, far_lanes, b, s, n_top):
    n_q = s // Q_TILE
    once = pl.Buffered(1)
    kv_col = lambda c: pl.BlockSpec((s, LANES), lambda bi, qi: (bi, c))
    cmp_spec = pl.BlockSpec((1, LANES, LANES), lambda bi, qi: (bi, 0, 0))
    return pl.pallas_call(
        functools.partial(_attn_kernel, n_top=n_top),
        grid=(b, n_q),
        in_specs=[
            pl.BlockSpec((Q_TILE, NSA_WIDTH), lambda bi, qi: (bi * n_q + qi, 0)),
            kv_col(2), kv_col(3), kv_col(4), kv_col(5),
            pl.BlockSpec((Q_TILE, LANES), lambda bi, qi: (bi * n_q + qi, 6)),
            cmp_spec, cmp_spec,
            pl.BlockSpec(bias_c.shape, lambda bi, qi: (0,) * 4, pipeline_mode=once),
            pl.BlockSpec(bias_w.shape, lambda bi, qi: (0,) * 5, pipeline_mode=once),
            pl.BlockSpec(bias_s.shape, lambda bi, qi: (0,) * 5, pipeline_mode=once),
            pl.BlockSpec(ovt.shape, lambda bi, qi: (0, 0), pipeline_mode=once),
            pl.BlockSpec(far_lanes.shape, lambda bi, qi: (0, 0, 0), pipeline_mode=once),
        ],
        out_specs=pl.BlockSpec((Q_TILE, NSA_WIDTH), lambda bi, qi: (bi * n_q + qi, 0)),
        out_shape=jax.ShapeDtypeStruct((b * s, NSA_WIDTH), BF16),
        scratch_shapes=[
            pltpu.VMEM((N_GROUPS, n_q // 2, 4 * Q_TILE, 2 * LANES), BF16),
            pltpu.VMEM((N_GROUPS, n_q, 2 * Q_TILE, LANES), BF16),
            pltpu.VMEM((N_GROUPS, 2 * LANES, LANES), BF16),
            pltpu.VMEM((N_GROUPS, n_q // 2, HEAD_DIM + ONES_ROWS, 2 * Q_TILE), BF16),
            pltpu.VMEM((N_GROUPS, n_q, HEAD_DIM + ONES_ROWS, Q_TILE), BF16),
            pltpu.VMEM((LANES, LANES), BF16),
            pltpu.VMEM((N_GROUPS, COLS, 2 * LANES), BF16),
            pltpu.VMEM((2 * N_GROUPS, HEAD_DIM + ONES_ROWS, COLS), F32),
            pltpu.VMEM((3, N_GROUPS, LANES, COLS), F32),
            pltpu.VMEM((N_GROUPS, 4 * Q_TILE + 2 * SUBLANES, COLS), F32),
            pltpu.VMEM((N_GROUPS, 4 * Q_TILE + 2 * SUBLANES, COLS), F32),
            pltpu.VMEM((N_GROUPS, WIN_TILES, 2 * Q_TILE, COLS), F32),
        ],
        compiler_params=pltpu.CompilerParams(
            dimension_semantics=("parallel", "arbitrary"), vmem_limit_bytes=VMEM_LIMIT),
        name="nsa_attention",
    )(proj_main, proj_kv, proj_kv, proj_kv, proj_kv, proj_kv, kcmp, vcmp, bias_c, bias_w, bias_s,
      ovt, far_lanes)


def _merge_kernel(zn_ref, a_ref, b_ref, zc_ref, gc_ref, gn_ref, ah_ref, bh_ref, on_ref, x_ref,
                  cw_ref, cb_ref, lg_ref, lb_ref, wcp_ref, wnp_ref, wo_ref, gf_ref, out_ref, uext,
                  conv, shifted):
    i = pl.program_id(1)
    ts = a_ref.shape[0]
    f = lambda r: r[...].astype(F32)

    n_cblk = D_MODEL // LANES
    u_halo = jnp.where(i > 0, f(ah_ref) * _sigmoid(f(bh_ref)), 0.0)
    u = f(a_ref) * _sigmoid(f(b_ref))
    for cblk in range(n_cblk):
        cols = slice(cblk * LANES, (cblk + 1) * LANES)
        uext[cblk, 0:CONV_HALO, :] = u_halo[:, cols]
        uext[cblk, CONV_HALO:CONV_HALO + ts, :] = u[:, cols]
        uext[cblk, CONV_HALO + ts:, :] = jnp.zeros((SUBLANES, LANES), F32)

    lead = CONV_HALO - (CONV_KERNEL - 1)
    chunk = Q_TILE

    def conv_block(cblk, carry):
        for shift in range(SUBLANES):
            shifted[shift] = uext[cblk, shift:shift + ts + CONV_HALO, :]
        w = cw_ref[cblk]
        for h in range(ts // chunk):
            c = jnp.broadcast_to(cb_ref[cblk], (chunk, LANES))
            for shift in range(SUBLANES):
                xs = shifted[shift, h * chunk:h * chunk + chunk + CONV_HALO, :]
                for j in range(CONV_KERNEL):
                    if (lead + j) % SUBLANES == shift:
                        base = lead + j - shift
                        c = c + w[j:j + 1, :] * xs[base:base + chunk]
            conv[cblk, h * chunk:(h + 1) * chunk, :] = c
        return carry

    lax.fori_loop(0, n_cblk, conv_block, 0)
    c = jnp.concatenate([conv[cblk] for cblk in range(n_cblk)], axis=1)

    mu = jnp.mean(c, axis=-1, keepdims=True)
    cc = c - mu
    var = jnp.mean(cc * cc, axis=-1, keepdims=True)
    y = (cc * lax.rsqrt(var + EPS)) * lg_ref[...] + lb_ref[...]
    conv_act = _silu(y) * _silu(f(zc_ref))
    y_conv = _dot(conv_act.astype(BF16), wcp_ref[...])

    nsa_act = f(on_ref) * _silu(f(zn_ref))
    y_nsa = _dot(nsa_act.astype(BF16), wnp_ref[...])

    merged = _sigmoid(f(gc_ref)) * y_conv + _sigmoid(f(gn_ref)) * y_nsa
    xo = x_ref[...] + _dot(merged.astype(BF16), wo_ref[...])
    ms = jnp.mean(xo * xo, axis=-1, keepdims=True)
    out_ref[...] = (xo * lax.rsqrt(ms + EPS)) * gf_ref[...]


def _merge(proj_main, o_nsa, x2, cw, cb, lg, lb, wcp, wnp, wo, gf, b, s, ts):
    n_t = s // ts
    n_cblk = D_MODEL // LANES
    halo_per_tile = ts // CONV_HALO
    cw = jnp.transpose(cw.reshape(cw.shape[0], n_cblk, LANES), (1, 0, 2))
    cb = cb.reshape(n_cblk, 1, LANES)
    col = lambda c: pl.BlockSpec((ts, COL_TILE), lambda bi, ti: (bi * n_t + ti, c))
    halo = lambda c: pl.BlockSpec(
        (CONV_HALO, COL_TILE),
        lambda bi, ti: (jnp.maximum((bi * n_t + ti) * halo_per_tile - 1, 0), c))
    const = lambda a: pl.BlockSpec(a.shape, lambda bi, ti: (0,) * a.ndim, pipeline_mode=pl.Buffered(1))
    rowblk = pl.BlockSpec((ts, D_MODEL), lambda bi, ti: (bi * n_t + ti, 0))
    return pl.pallas_call(
        _merge_kernel,
        grid=(b, n_t),
        in_specs=[col(1), col(2), col(3), col(4), col(5), col(6), halo(2), halo(3), rowblk, rowblk,
                  const(cw), const(cb), const(lg), const(lb), const(wcp), const(wnp), const(wo),
                  const(gf)],
        out_specs=rowblk,
        out_shape=jax.ShapeDtypeStruct((b * s, D_MODEL), F32),
        scratch_shapes=[pltpu.VMEM((n_cblk, ts + CONV_HALO + SUBLANES, LANES), F32),
                        pltpu.VMEM((n_cblk, ts, LANES), F32),
                        pltpu.VMEM((SUBLANES, ts + CONV_HALO, LANES), F32)],
        compiler_params=pltpu.CompilerParams(
            dimension_semantics=("parallel", "arbitrary"), vmem_limit_bytes=VMEM_LIMIT),
        name="conv_merge",
    )(proj_main, proj_main, proj_main, proj_main, proj_main, proj_main, proj_main, proj_main,
      o_nsa, x2, cw, cb, lg, lb, wcp, wnp, wo, gf)


def _t5_bucket_np(rel):
    rel = np.maximum(rel, 0)
    max_exact = REL_BUCKETS // 2
    relf = np.maximum(rel, 1).astype(np.float32)
    large = max_exact + (np.log(relf / np.float32(max_exact))
                         / np.float32(np.log(REL_MAX_DIST / max_exact))
                         * np.float32(REL_BUCKETS - max_exact)).astype(np.int32)
    large = np.minimum(large, REL_BUCKETS - 1)
    return np.where(rel < max_exact, rel, large)


def _pair_head_index():
    g = np.arange(N_GROUPS)[:, None, None]
    v = np.arange(2)[None, :, None]
    p = np.arange(PAIRS_PER_GROUP)[None, None, :]
    return g * HEADS_PER_GROUP + 2 * p + v


def _bias_lookup(rel_bias, rel):
    bucket = _t5_bucket_np(rel).reshape(-1)
    onehot = (jnp.arange(REL_BUCKETS)[:, None] == jnp.asarray(bucket)[None, :]).astype(F32)
    vals = jnp.dot(rel_bias.astype(F32).T * LOG2E, onehot, precision=lax.Precision.HIGHEST)
    vals = vals.reshape((N_HEADS,) + rel.shape)
    head = _pair_head_index()
    return jnp.stack([jnp.stack([jnp.concatenate([vals[h] for h in head[g, v]], axis=-1)
                                 for v in range(2)]) for g in range(N_GROUPS)])


def _bias_tables(rel_bias, s, n_sel):
    c = np.arange(LANES)[:, None]
    r = np.arange(Q_TILE)[None, :]
    far = rel_bias.astype(F32)[REL_BUCKETS - 1][_pair_head_index()] * LOG2E
    assert s // CMP_STRIDE <= LANES + Q_TILE // CMP_STRIDE
    rel_w = np.stack([dd * Q_TILE + r - c for dd in range(WIN_TILES)])
    assert (_t5_bucket_np(rel_w[2:]) == REL_BUCKETS - 1).all()
    rho = np.arange(2 * LANES)[:, None]
    rel_c = r - ((rho - LANES) * CMP_STRIDE + CMP_BLOCK - 1)
    vals = _bias_lookup(rel_bias, np.concatenate([rel_w, rel_c.reshape(2, LANES, Q_TILE)]))
    bias_w, bias_c = vals[:, :, :WIN_TILES], vals[:, :, WIN_TILES:].reshape(N_GROUPS, 2, 2 * LANES, COLS)
    cols = lambda a: np.tile(a, (1,) * (a.ndim - 1) + (PAIRS_PER_GROUP,))

    tw = jnp.where(cols((rel_w >= 0) & (rel_w < WINDOW)), bias_w, NEG)
    tw = jnp.concatenate([tw, jnp.full_like(tw[:, :, :1], NEG)], axis=2)
    far_cols = jnp.repeat(far, Q_TILE, axis=-1)[:, :, None, None, :]
    ts = jnp.where(cols(rel_w[:3] >= 0), bias_w[:, :, :3] - far_cols, NEG)
    bc = jnp.where(cols(rel_c >= 0), bias_c, NEG)
    hi = far.astype(BF16)
    lo = (far - hi.astype(F32)).astype(BF16)
    pieces = jnp.stack([hi[:, 0], lo[:, 0], hi[:, 1], lo[:, 1]], axis=-1)
    pieces = jnp.broadcast_to(pieces[:, :, None, :], (N_GROUPS, PAIRS_PER_GROUP, Q_TILE, 4))
    far_lanes = jnp.pad(pieces.reshape(N_GROUPS, COLS, 4),
                        ((0, 0), (0, 0), (n_sel, LANES - n_sel - 4)))
    return tw, ts, bc, far_lanes


def _overlap_t(s):
    n_cmp = (s - CMP_BLOCK) // CMP_STRIDE + 1
    cs = np.arange(LANES) * CMP_STRIDE
    ss = np.arange(s // SEL_BLOCK) * SEL_BLOCK
    ovt = ((cs[None, :] <= ss[:, None] + SEL_BLOCK - 1) & (cs[None, :] + CMP_BLOCK - 1 >= ss[:, None])
           & (np.arange(LANES)[None, :] < n_cmp))
    return jnp.asarray(ovt, BF16)


def _pad_w_in(w):
    used = NSA_WIDTH + 6 * KV_WIDTH + 3 * N_HEADS
    gap = (KV_COL_TILE + 1) * COL_TILE - used
    left = jnp.pad(w[:, :used], ((0, 0), (0, w.shape[1] - used + gap)))
    right = jnp.pad(w[:, used:], ((0, 0), (used + gap, 0)))
    return (left + right).astype(BF16)


def _compress_weights(pos, w1, w2):
    half = CMP_BLOCK // 2
    eye = jnp.eye(N_GROUPS, dtype=F32)
    w1r = w1.reshape(CMP_BLOCK, HEAD_DIM, CMP_HIDDEN)
    blk = lambda part: jnp.einsum('idn,gh->igdhn', part, eye).reshape(
        half * KV_WIDTH, N_GROUPS * CMP_HIDDEN)
    w1d = jnp.concatenate([blk(w1r[:half]), blk(w1r[half:])], axis=1).astype(BF16)
    w2d = jnp.einsum('nd,gh->gnhd', w2, eye).reshape(N_GROUPS * CMP_HIDDEN, KV_WIDTH).astype(BF16)
    tilepos = lambda part: jnp.broadcast_to(part[:, None, :], (half, N_GROUPS, HEAD_DIM)).reshape(1, -1)
    posd = jnp.concatenate([tilepos(pos[:half]), tilepos(pos[half:])], axis=0).astype(F32)
    return posd, w1d, w2d


def kernel(x, norm_in_g, w_in, pos_ck, w_ck1, w_ck2, pos_cv, w_cv1, w_cv2, rel_bias, conv_w, conv_b,
           conv_ln_g, conv_ln_b, w_conv_proj, w_nsa_proj, w_out, norm_f_g):
    b, s, d = x.shape
    n_sel = s // SEL_BLOCK
    assert d == D_MODEL and w_in.shape[0] == 1, "single-layer block with D_MODEL=1024"
    assert s % (2 * Q_TILE) == 0 and s // CMP_STRIDE <= LANES and s >= WINDOW
    assert n_sel + 4 <= LANES
    m = b * s
    x2 = x.reshape(m, d)
    row = lambda a: a.reshape(1, -1).astype(F32)

    proj_kv, proj_main = _input_projection(x2, row(norm_in_g[0]), _pad_w_in(w_in[0]), min(1024, m))

    chunks = s // CMP_STRIDE
    pk, w1k, w2k = _compress_weights(pos_ck[0], w_ck1[0], w_ck2[0])
    pv, w1v, w2v = _compress_weights(pos_cv[0], w_cv1[0], w_cv2[0])
    kcmp, vcmp = _compress(proj_kv, pk, pv, w1k, w1v, w2k, w2v, b, s)
    if chunks < LANES:
        padrows = ((0, 0), (0, LANES - chunks), (0, 0))
        kcmp, vcmp = jnp.pad(kcmp, padrows), jnp.pad(vcmp, padrows)

    bias_w, bias_s, bias_c, far_lanes = _bias_tables(rel_bias, s, n_sel)
    o_nsa = _attention(proj_main, proj_kv, kcmp, vcmp, bias_c, bias_w, bias_s, _overlap_t(s),
                       far_lanes, b, s, min(N_SELECT, n_sel))

    cw = jnp.pad(conv_w[0].astype(F32), ((0, CONV_HALO - CONV_KERNEL), (0, 0)))
    out = _merge(proj_main, o_nsa, x2, cw, row(conv_b[0]), row(conv_ln_g[0]), row(conv_ln_b[0]),
                 w_conv_proj[0].astype(BF16), w_nsa_proj[0].astype(BF16), w_out[0].astype(BF16),
                 row(norm_f_g), b, s, MERGE_ROWS)
    return out.reshape(b, s, d)
```

```python
import functools
import math

import numpy as np
import jax
import jax.numpy as jnp
from jax import lax
from jax.experimental import pallas as pl
from jax.experimental.pallas import tpu as pltpu

F32 = jnp.float32
BF16 = jnp.bfloat16

D_MODEL = 1024
N_HEADS = 16
N_GROUPS = 2
HEADS_PER_GROUP = N_HEADS // N_GROUPS
PAIRS_PER_GROUP = HEADS_PER_GROUP // 2
HEAD_DIM = 64
NSA_WIDTH = N_HEADS * HEAD_DIM
KV_WIDTH = N_GROUPS * HEAD_DIM
CMP_BLOCK = 32
CMP_STRIDE = 16
CMP_HIDDEN = 256
SEL_BLOCK = 64
N_SELECT = 8
WINDOW = 512
Q_TILE = 128
CONV_KERNEL = 31
CONV_HALO = 32
REL_BUCKETS = 32
REL_MAX_DIST = 128
EPS = 1e-6
NEG = -1e30
FORCE_SCORE = 1e6
LOG2E = math.log2(math.e)
LANES = 128
SUBLANES = 8
ONES_ROWS = 16
COLS = PAIRS_PER_GROUP * Q_TILE
COL_TILE = 1024
N_COL_TILES = 8
KV_COL_TILE = 1
MERGE_ROWS = 512
WIN_TILES = WINDOW // Q_TILE + 1
VMEM_LIMIT = 56 * 1024 * 1024


def _dot(a, b):
    return jnp.dot(a, b, preferred_element_type=F32)


def _dot_nt(a, b):
    return lax.dot_general(a, b, (((1,), (1,)), ((), ())), preferred_element_type=F32)


def _sigmoid(x):
    return 1.0 / (1.0 + jnp.exp2(x * (-LOG2E)))


def _silu(x):
    return x * _sigmoid(x)


def _proj_kernel(x_ref, g_ref, w_ref, kv_ref, main_ref, h_ref):
    j = pl.program_id(1)

    @pl.when(j == 0)
    def _():
        x = x_ref[...]
        ms = jnp.mean(x * x, axis=-1, keepdims=True)
        h_ref[...] = ((x * lax.rsqrt(ms + EPS)) * g_ref[...]).astype(BF16)

    acc = _dot(h_ref[...], w_ref[...])

    @pl.when(j == KV_COL_TILE)
    def _():
        kv_ref[...] = acc

    @pl.when(j != KV_COL_TILE)
    def _():
        main_ref[...] = acc.astype(BF16)


def _input_projection(x2, g, w_perm, tm):
    m = x2.shape[0]
    return pl.pallas_call(
        _proj_kernel,
        grid=(m // tm, N_COL_TILES),
        in_specs=[
            pl.BlockSpec((tm, D_MODEL), lambda i, j: (i, 0)),
            pl.BlockSpec((1, D_MODEL), lambda i, j: (0, 0)),
            pl.BlockSpec((D_MODEL, COL_TILE), lambda i, j: (0, j)),
        ],
        out_specs=[
            pl.BlockSpec((tm, COL_TILE), lambda i, j: (i, 0)),
            pl.BlockSpec((tm, COL_TILE), lambda i, j: (i, jnp.maximum(j - 1, 0))),
        ],
        out_shape=[
            jax.ShapeDtypeStruct((m, COL_TILE), F32),
            jax.ShapeDtypeStruct((m, (N_COL_TILES - 1) * COL_TILE), BF16),
        ],
        scratch_shapes=[pltpu.VMEM((tm, D_MODEL), BF16)],
        compiler_params=pltpu.CompilerParams(
            dimension_semantics=("parallel", "arbitrary"), vmem_limit_bytes=VMEM_LIMIT),
        name="input_projection",
    )(x2, g, w_perm)


def _compress_kernel(kf_ref, vf_ref, pk_ref, pv_ref, w1k_ref, w1v_ref, w2k_ref, w2v_ref,
                     kc_ref, vc_ref):
    def one(f_ref, pos_ref, w1_ref, w2_ref, o_ref):
        n = f_ref.shape[0] // CMP_STRIDE
        hw = N_GROUPS * CMP_HIDDEN
        first = jnp.zeros((n, hw), F32)
        second = jnp.zeros((n, hw), F32)
        for i in range(CMP_STRIDE):
            tok = f_ref[pl.ds(i, n, stride=CMP_STRIDE), :]
            lanes = slice(i * KV_WIDTH, (i + 1) * KV_WIDTH)
            first = first + _dot((tok + pos_ref[0:1, lanes]).astype(BF16), w1_ref[lanes, 0:hw])
            second = second + _dot((tok + pos_ref[1:2, lanes]).astype(BF16), w1_ref[lanes, hw:2 * hw])
        hid = first + pltpu.roll(second, n - 1, axis=0)
        o_ref[0] = _dot(_silu(hid).astype(BF16), w2_ref[...])

    one(kf_ref, pk_ref, w1k_ref, w2k_ref, kc_ref)
    one(vf_ref, pv_ref, w1v_ref, w2v_ref, vc_ref)


def _compress(proj_kv, pk, pv, w1k, w1v, w2k, w2v, b, s):
    n = s // CMP_STRIDE
    const = lambda shape: pl.BlockSpec(shape, lambda i: (0,) * len(shape))
    kv_col = lambda c: pl.BlockSpec((s, KV_WIDTH), lambda i: (i, c))
    out = pl.BlockSpec((1, n, LANES), lambda i: (i, 0, 0))
    return pl.pallas_call(
        _compress_kernel,
        grid=(b,),
        in_specs=[kv_col(0), kv_col(1), const(pk.shape), const(pv.shape), const(w1k.shape),
                  const(w1v.shape), const(w2k.shape), const(w2v.shape)],
        out_specs=[out, out],
        out_shape=[jax.ShapeDtypeStruct((b, n, LANES), F32)] * 2,
        compiler_params=pltpu.CompilerParams(
            dimension_semantics=("parallel",), vmem_limit_bytes=VMEM_LIMIT),
        name="nsa_compress",
    )(proj_kv, proj_kv, pk, pv, w1k, w1v, w2k, w2v)


def _attn_kernel(q_ref, ks_ref, vs_ref, kw_ref, vw_ref, gt_ref, kc_ref, vc_ref, bc_ref, tw_ref,
                 ts_ref, ovt_ref, bx_ref, o_ref,
                 ksv, kwv, kcv, vst, vwt, vct, rhs, acc, obuf, s_a, s_b, s_w, *, n_top):
    qi = pl.program_id(1)
    n_kt = kwv.shape[1]
    n_sel = ovt_ref.shape[0]
    variants = 2 * N_GROUPS

    @pl.when(qi == 0)
    def _prepare_kv():
        def halves(k):
            lo = lax.broadcasted_iota(jnp.int32, k.shape, 1) < HEAD_DIM
            kr = pltpu.roll(k, HEAD_DIM, axis=1)
            z = jnp.zeros_like(k)
            c = lambda a: a.astype(BF16)
            return ((c(jnp.where(lo, k, z)), c(jnp.where(lo, z, kr))),
                    (c(jnp.where(lo, kr, z)), c(jnp.where(lo, z, k))))

        def extra_lanes(shape, v):
            pair = lax.broadcasted_iota(jnp.int32, shape, 0)
            r = lax.broadcasted_iota(jnp.int32, shape, 1)
            lane = lax.broadcasted_iota(jnp.int32, shape, 2)
            one = ((lane >= n_sel + 2 * v) & (lane < n_sel + 2 * v + 2)
                   | (lane == pair * (2 * Q_TILE // SEL_BLOCK) + r // SEL_BLOCK))
            return jnp.where(one, 1.0, 0.0).astype(BF16)

        pair_shape = (n_kt // 2, 2 * Q_TILE, LANES)
        tile_shape = (n_kt, Q_TILE, LANES)
        k_sel, k_win, k_cmp = halves(ks_ref[...]), halves(kw_ref[...]), halves(kc_ref[0])
        for g in range(N_GROUPS):
            for v in range(2):
                rows = slice(v * 2 * Q_TILE, (v + 1) * 2 * Q_TILE)
                ksv[g, :, rows, 0:LANES] = k_sel[g][v].reshape(pair_shape)
                ksv[g, :, rows, LANES:2 * LANES] = extra_lanes(pair_shape, v)
                kwv[g, :, v * Q_TILE:(v + 1) * Q_TILE, :] = k_win[g][v].reshape(tile_shape)
                kcv[g, v * LANES:(v + 1) * LANES, :] = k_cmp[g][v]
        vst[:, :, HEAD_DIM:, :] = jnp.ones((N_GROUPS, n_kt // 2, ONES_ROWS, 2 * Q_TILE), BF16)
        vwt[:, :, HEAD_DIM:, :] = jnp.ones((N_GROUPS, n_kt, ONES_ROWS, Q_TILE), BF16)
        for kt in range(n_kt):
            rows = slice(kt * Q_TILE, (kt + 1) * Q_TILE)
            half = slice((kt % 2) * Q_TILE, (kt % 2 + 1) * Q_TILE)
            vs_t = vs_ref[rows, :].T.astype(BF16)
            vw_t = vw_ref[rows, :].T.astype(BF16)
            for g in range(N_GROUPS):
                vst[g, kt // 2, 0:HEAD_DIM, half] = vs_t[g * HEAD_DIM:(g + 1) * HEAD_DIM]
                vwt[g, kt, 0:HEAD_DIM, :] = vw_t[g * HEAD_DIM:(g + 1) * HEAD_DIM]
        vct[...] = vc_ref[0].T.astype(BF16)

    qt = q_ref[...]
    gsig_t = _sigmoid(gt_ref[...]).T

    def col_max(s):
        return jnp.max(s.reshape(s.shape[0] // 8, 8, COLS), axis=0)

    def group_rows(g):
        return slice(g * HEAD_DIM, (g + 1) * HEAD_DIM)

    def online(gv, s, s_max, vt, m):
        m_new = jnp.maximum(m, jnp.max(s_max, axis=0, keepdims=True))
        e = jnp.exp2(s - m_new)
        acc[gv] = acc[gv] * jnp.exp2(m - m_new) + _dot(vt, e.astype(BF16))
        return m_new

    def fresh_state():
        acc[...] = jnp.zeros(acc.shape, F32)
        return tuple(jnp.full((1, COLS), NEG, F32) for _ in range(variants))

    def finish(branch):
        for g in range(N_GROUPS):
            halves = []
            for v in range(2):
                gv = g * 2 + v
                l = acc[gv, HEAD_DIM:HEAD_DIM + 1, :]
                halves.append(acc[gv, 0:HEAD_DIM, :] * (1.0 / jnp.maximum(l, 1e-30)))
            obuf[branch, g] = jnp.concatenate(halves, axis=0)

    groups = range(N_GROUPS)
    for g in groups:
        q4 = jnp.concatenate(
            [qt[:, (g * PAIRS_PER_GROUP + p) * LANES:(g * PAIRS_PER_GROUP + p + 1) * LANES]
             for p in range(PAIRS_PER_GROUP)], axis=0)
        rhs[g, :, 0:LANES] = (q4.astype(F32) * (HEAD_DIM ** -0.5 * LOG2E)).astype(BF16)

    bc_rows = pl.ds(pl.multiple_of(LANES - qi * (Q_TILE // CMP_STRIDE), SUBLANES), LANES)
    s_cmp = [_dot_nt(kcv[g], rhs[g, :, 0:LANES])
             + jnp.concatenate([bc_ref[g, 0, bc_rows, :], bc_ref[g, 1, bc_rows, :]], axis=0)
             for g in groups]
    psum_t = []
    for g in groups:
        tot = jnp.zeros((LANES, Q_TILE), F32)
        halves = []
        for v in range(2):
            s = s_cmp[g][v * LANES:(v + 1) * LANES]
            valid = s > 0.5 * NEG
            m = jnp.max(s, axis=0, keepdims=True)
            e = jnp.where(valid, jnp.exp2(s - m), 0.0)
            p = e * (1.0 / jnp.maximum(jnp.sum(e, axis=0, keepdims=True), 1e-30))
            for pp in range(PAIRS_PER_GROUP):
                tot = tot + p[:, pp * Q_TILE:(pp + 1) * Q_TILE]
            halves.append(_dot(vct[group_rows(g), :], p.astype(BF16)))
        obuf[0, g] = jnp.concatenate(halves, axis=0)
        psum_t.append(tot)

    ovt = ovt_ref[...]
    imp_t = []
    for g in groups:
        p_hi = psum_t[g].astype(BF16)
        r1 = psum_t[g] - p_hi.astype(F32)
        p_mid = r1.astype(BF16)
        p_lo = (r1 - p_mid.astype(F32)).astype(BF16)
        imp_t.append(_dot(ovt, p_hi) + _dot(ovt, p_mid) + _dot(ovt, p_lo))

    j_idx = lax.broadcasted_iota(jnp.int32, (n_sel, Q_TILE), 0)
    r_idx = lax.broadcasted_iota(jnp.int32, (n_sel, Q_TILE), 1)
    blk_t = qi * (Q_TILE // SEL_BLOCK) + r_idx // SEL_BLOCK
    valid_blk = j_idx <= blk_t
    forced = (j_idx == 0) | (j_idx == blk_t) | (j_idx == blk_t - 1)
    prio = [jnp.where(valid_blk, jnp.where(forced, FORCE_SCORE, imp_t[g]), -FORCE_SCORE) for g in groups]
    rank = [jnp.zeros((n_sel, Q_TILE), F32) for _ in groups]
    for jj in range(n_sel):
        later = j_idx > jj
        for g in groups:
            row = prio[g][jj:jj + 1, :]
            beats = (row > prio[g]) | ((row == prio[g]) & later)
            rank[g] = rank[g] + jnp.where(beats, 1.0, 0.0)
    for g in groups:
        drop_t = jnp.where((rank[g] < n_top) & valid_blk, 0.0, NEG)
        drop_t = jnp.concatenate([drop_t, jnp.zeros((LANES - n_sel, Q_TILE), F32)], axis=0)
        drop = drop_t.T.astype(BF16)
        rhs[g, :, LANES:2 * LANES] = jnp.concatenate([drop] * PAIRS_PER_GROUP, axis=0) + bx_ref[g]

    pair_rows = 4 * Q_TILE

    def half_rows(v):
        return slice(v * 2 * Q_TILE, (v + 1) * 2 * Q_TILE)

    def max_rows(v):
        return slice(pair_rows + v * SUBLANES, pair_rows + (v + 1) * SUBLANES)

    def logits_group(buf, g, i, tables=None):
        s_all = _dot_nt(ksv[g, i], rhs[g])
        for v in range(2):
            s = s_all[half_rows(v)]
            if tables is not None:
                s = jnp.concatenate([s[h * Q_TILE:(h + 1) * Q_TILE] + ts_ref[g, v, tables[h]]
                                     for h in range(2)], axis=0)
            buf[g, half_rows(v), :] = s
            buf[g, max_rows(v), :] = col_max(s)

    def logits_into(buf, i):
        for g in range(N_GROUPS):
            logits_group(buf, g, i)

    def consume(buf, i, state):
        ms = list(state)
        for g in range(N_GROUPS):
            for v in range(2):
                gv = g * 2 + v
                ms[gv] = online(gv, buf[g, half_rows(v), :], buf[g, max_rows(v), :], vst[g, i], ms[gv])
        return tuple(ms)

    n_all = (qi + 2) // 2
    n_far = jnp.maximum(qi - 1, 0) // 2

    first = jnp.maximum(qi - (WIN_TILES - 1), 0)
    tiles = []
    for t in range(WIN_TILES):
        dd = qi - (first + t)
        table = jnp.where(dd >= 0, dd, WIN_TILES)
        tiles.append((jnp.minimum(first + t, n_kt - 1), table))
    m_win = []
    for g in range(N_GROUPS):
        mx = [jnp.full((8, COLS), NEG, F32) for _ in range(2)]
        for t, (kt, table) in enumerate(tiles):
            s_all = _dot_nt(kwv[g, kt], rhs[g, :, 0:LANES])
            for v in range(2):
                s = s_all[v * Q_TILE:(v + 1) * Q_TILE] + tw_ref[g, v, table]
                s_w[g, t, v * Q_TILE:(v + 1) * Q_TILE, :] = s
                mx[v] = jnp.maximum(mx[v], col_max(s))
        m_win += [jnp.max(mx[v], axis=0, keepdims=True) for v in range(2)]

    for g in range(N_GROUPS):
        logits_group(s_a, g, 0)
        halves = []
        for v in range(2):
            o = jnp.zeros((HEAD_DIM + ONES_ROWS, COLS), F32)
            for t, (kt, _) in enumerate(tiles):
                e = jnp.exp2(s_w[g, t, v * Q_TILE:(v + 1) * Q_TILE, :] - m_win[g * 2 + v])
                o = o + _dot(vwt[g, kt], e.astype(BF16))
            halves.append(o[0:HEAD_DIM] * (1.0 / jnp.maximum(o[HEAD_DIM:HEAD_DIM + 1], 1e-30)))
        obuf[2, g] = jnp.concatenate(halves, axis=0)

    def far_two(j, state):
        logits_into(s_b, 2 * j + 1)
        state = consume(s_a, 2 * j, state)
        logits_into(s_a, 2 * j + 2)
        return consume(s_b, 2 * j + 1, state)

    state = lax.fori_loop(0, n_far // 2, far_two, fresh_state())
    state = lax.fori_loop(0, n_far % 2, lambda _, st: consume(s_a, n_far - 1, st), state)

    def near_pair(i, state):
        tables = [jnp.clip(qi - (2 * i + h), 0, 2) for h in range(2)]
        for g in range(N_GROUPS):
            logits_group(s_b, g, i, tables)
        return consume(s_b, i, state)

    diag_alone = 1 - qi % 2
    state = lax.fori_loop(n_far, n_all - diag_alone, near_pair, state)

    def diagonal_tile(_, state):
        ms = list(state)
        i = n_all - 1
        for g in range(N_GROUPS):
            lhs = jnp.concatenate([ksv[g, i, v * 2 * Q_TILE:v * 2 * Q_TILE + Q_TILE, :] for v in range(2)],
                                  axis=0)
            s_all = _dot_nt(lhs, rhs[g])
            for v in range(2):
                s = s_all[v * Q_TILE:(v + 1) * Q_TILE] + ts_ref[g, v, 0]
                s_b[g, v * Q_TILE:(v + 1) * Q_TILE, :] = s
                s_b[g, max_rows(v), :] = col_max(s)
        for g in range(N_GROUPS):
            for v in range(2):
                gv = g * 2 + v
                ms[gv] = online(gv, s_b[g, v * Q_TILE:(v + 1) * Q_TILE, :], s_b[g, max_rows(v), :],
                                vst[g, i, :, 0:Q_TILE], ms[gv])
        return tuple(ms)

    lax.fori_loop(0, diag_alone, diagonal_tile, state)
    finish(1)

    for g in range(N_GROUPS):
        for p in range(PAIRS_PER_GROUP):
            halves = []
            for v in range(2):
                tot = jnp.zeros((HEAD_DIM, Q_TILE), F32)
                for br in range(3):
                    c = br * N_HEADS + g * HEADS_PER_GROUP + 2 * p + v
                    tot = tot + gsig_t[c:c + 1, :] * obuf[br, g, v * HEAD_DIM:(v + 1) * HEAD_DIM,
                                                          p * Q_TILE:(p + 1) * Q_TILE]
                halves.append(tot)
            col = (g * PAIRS_PER_GROUP + p) * LANES
            o_ref[:, col:col + LANES] = jnp.concatenate(halves, axis=0).T.astype(BF16)


def _attention(proj_main, proj_kv, kcmp, vcmp, bias_c, bias_w, bias_s, ovt, far_lanes, b, s, n_top):
    n_q = s // Q_TILE
    once = pl.Buffered(1)
    kv_col = lambda c: pl.BlockSpec((s, LANES), lambda bi, qi: (bi, c))
    cmp_spec = pl.BlockSpec((1, LANES, LANES), lambda bi, qi: (bi, 0, 0))
    return pl.pallas_call(
        functools.partial(_attn_kernel, n_top=n_top),
        grid=(b, n_q),
        in_specs=[
            pl.BlockSpec((Q_TILE, NSA_WIDTH), lambda bi, qi: (bi * n_q + qi, 0)),
            kv_col(2), kv_col(3), kv_col(4), kv_col(5),
            pl.BlockSpec((Q_TILE, LANES), lambda bi, qi: (bi * n_q + qi, 6)),
            cmp_spec, cmp_spec,
            pl.BlockSpec(bias_c.shape, lambda bi, qi: (0,) * 4, pipeline_mode=once),
            pl.BlockSpec(bias_w.shape, lambda bi, qi: (0,) * 5, pipeline_mode=once),
            pl.BlockSpec(bias_s.shape, lambda bi, qi: (0,) * 5, pipeline_mode=once),
            pl.BlockSpec(ovt.shape, lambda bi, qi: (0, 0), pipeline_mode=once),
            pl.BlockSpec(far_lanes.shape, lambda bi, qi: (0, 0, 0), pipeline_mode=once),
        ],
        out_specs=pl.BlockSpec((Q_TILE, NSA_WIDTH), lambda bi, qi: (bi * n_q + qi, 0)),
        out_shape=jax.ShapeDtypeStruct((b * s, NSA_WIDTH), BF16),
        scratch_shapes=[
            pltpu.VMEM((N_GROUPS, n_q // 2, 4 * Q_TILE, 2 * LANES), BF16),
            pltpu.VMEM((N_GROUPS, n_q, 2 * Q_TILE, LANES), BF16),
            pltpu.VMEM((N_GROUPS, 2 * LANES, LANES), BF16),
            pltpu.VMEM((N_GROUPS, n_q // 2, HEAD_DIM + ONES_ROWS, 2 * Q_TILE), BF16),
            pltpu.VMEM((N_GROUPS, n_q, HEAD_DIM + ONES_ROWS, Q_TILE), BF16),
            pltpu.VMEM((LANES, LANES), BF16),
            pltpu.VMEM((N_GROUPS, COLS, 2 * LANES), BF16),
            pltpu.VMEM((2 * N_GROUPS, HEAD_DIM + ONES_ROWS, COLS), F32),
            pltpu.VMEM((3, N_GROUPS, LANES, COLS), F32),
            pltpu.VMEM((N_GROUPS, 4 * Q_TILE + 2 * SUBLANES, COLS), F32),
            pltpu.VMEM((N_GROUPS, 4 * Q_TILE + 2 * SUBLANES, COLS), F32),
            pltpu.VMEM((N_GROUPS, WIN_TILES, 2 * Q_TILE, COLS), F32),
        ],
        compiler_params=pltpu.CompilerParams(
            dimension_semantics=("parallel", "arbitrary"), vmem_limit_bytes=VMEM_LIMIT),
        name="nsa_attention",
    )(proj_main, proj_kv, proj_kv, proj_kv, proj_kv, proj_kv, kcmp, vcmp, bias_c, bias_w, bias_s,
      ovt, far_lanes)


def _merge_kernel(zn_ref, a_ref, b_ref, zc_ref, gc_ref, gn_ref, ah_ref, bh_ref, on_ref, x_ref,
                  cw_ref, cb_ref, lg_ref, lb_ref, wcp_ref, wnp_ref, wo_ref, gf_ref, out_ref, uext,
                  conv, shifted):
    i = pl.program_id(1)
    ts = a_ref.shape[0]
    f = lambda r: r[...].astype(F32)

    n_cblk = D_MODEL // LANES
    u_halo = jnp.where(i > 0, f(ah_ref) * _sigmoid(f(bh_ref)), 0.0)
    u = f(a_ref) * _sigmoid(f(b_ref))
    for cblk in range(n_cblk):
        cols = slice(cblk * LANES, (cblk + 1) * LANES)
        uext[cblk, 0:CONV_HALO, :] = u_halo[:, cols]
        uext[cblk, CONV_HALO:CONV_HALO + ts, :] = u[:, cols]
        uext[cblk, CONV_HALO + ts:, :] = jnp.zeros((SUBLANES, LANES), F32)

    lead = CONV_HALO - (CONV_KERNEL - 1)
    chunk = Q_TILE

    def conv_block(cblk, carry):
        for shift in range(SUBLANES):
            shifted[shift] = uext[cblk, shift:shift + ts + CONV_HALO, :]
        w = cw_ref[cblk]
        for h in range(ts // chunk):
            c = jnp.broadcast_to(cb_ref[cblk], (chunk, LANES))
            for shift in range(SUBLANES):
                xs = shifted[shift, h * chunk:h * chunk + chunk + CONV_HALO, :]
                for j in range(CONV_KERNEL):
                    if (lead + j) % SUBLANES == shift:
                        base = lead + j - shift
                        c = c + w[j:j + 1, :] * xs[base:base + chunk]
            conv[cblk, h * chunk:(h + 1) * chunk, :] = c
        return carry

    lax.fori_loop(0, n_cblk, conv_block, 0)
    c = jnp.concatenate([conv[cblk] for cblk in range(n_cblk)], axis=1)

    mu = jnp.mean(c, axis=-1, keepdims=True)
    cc = c - mu
    var = jnp.mean(cc * cc, axis=-1, keepdims=True)
    y = (cc * lax.rsqrt(var + EPS)) * lg_ref[...] + lb_ref[...]
    conv_act = _silu(y) * _silu(f(zc_ref))
    y_conv = _dot(conv_act.astype(BF16), wcp_ref[...])

    nsa_act = f(on_ref) * _silu(f(zn_ref))
    y_nsa = _dot(nsa_act.astype(BF16), wnp_ref[...])

    merged = _sigmoid(f(gc_ref)) * y_conv + _sigmoid(f(gn_ref)) * y_nsa
    xo = x_ref[...] + _dot(merged.astype(BF16), wo_ref[...])
    ms = jnp.mean(xo * xo, axis=-1, keepdims=True)
    out_ref[...] = (xo * lax.rsqrt(ms + EPS)) * gf_ref[...]


def _merge(proj_main, o_nsa, x2, cw, cb, lg, lb, wcp, wnp, wo, gf, b, s, ts):
    n_t = s // ts
    n_cblk = D_MODEL // LANES
    halo_per_tile = ts // CONV_HALO
    cw = jnp.transpose(cw.reshape(cw.shape[0], n_cblk, LANES), (1, 0, 2))
    cb = cb.reshape(n_cblk, 1, LANES)
    col = lambda c: pl.BlockSpec((ts, COL_TILE), lambda bi, ti: (bi * n_t + ti, c))
    halo = lambda c: pl.BlockSpec(
        (CONV_HALO, COL_TILE),
        lambda bi, ti: (jnp.maximum((bi * n_t + ti) * halo_per_tile - 1, 0), c))
    const = lambda a: pl.BlockSpec(a.shape, lambda bi, ti: (0,) * a.ndim, pipeline_mode=pl.Buffered(1))
    rowblk = pl.BlockSpec((ts, D_MODEL), lambda bi, ti: (bi * n_t + ti, 0))
    return pl.pallas_call(
        _merge_kernel,
        grid=(b, n_t),
        in_specs=[col(1), col(2), col(3), col(4), col(5), col(6), halo(2), halo(3), rowblk, rowblk,
                  const(cw), const(cb), const(lg), const(lb), const(wcp), const(wnp), const(wo),
                  const(gf)],
        out_specs=rowblk,
        out_shape=jax.ShapeDtypeStruct((b * s, D_MODEL), F32),
        scratch_shapes=[pltpu.VMEM((n_cblk, ts + CONV_HALO + SUBLANES, LANES), F32),
                        pltpu.VMEM((n_cblk, ts, LANES), F32),
                        pltpu.VMEM((SUBLANES, ts + CONV_HALO, LANES), F32)],
        compiler_params=pltpu.CompilerParams(
            dimension_semantics=("parallel", "arbitrary"), vmem_limit_bytes=VMEM_LIMIT),
        name="conv_merge",
    )(proj_main, proj_main, proj_main, proj_main, proj_main, proj_main, proj_main, proj_main,
      o_nsa, x2, cw, cb, lg, lb, wcp, wnp, wo, gf)


def _t5_bucket_np(rel):
    rel = np.maximum(rel, 0)
    max_exact = REL_BUCKETS // 2
    relf = np.maximum(rel, 1).astype(np.float32)
    large = max_exact + (np.log(relf / np.float32(max_exact))
                         / np.float32(np.log(REL_MAX_DIST / max_exact))
                         * np.float32(REL_BUCKETS - max_exact)).astype(np.int32)
    large = np.minimum(large, REL_BUCKETS - 1)
    return np.where(rel < max_exact, rel, large)


def _pair_head_index():
    g = np.arange(N_GROUPS)[:, None, None]
    v = np.arange(2)[None, :, None]
    p = np.arange(PAIRS_PER_GROUP)[None, None, :]
    return g * HEADS_PER_GROUP + 2 * p + v


def _bias_lookup(rel_bias, rel):
    bucket = _t5_bucket_np(rel).reshape(-1)
    onehot = (jnp.arange(REL_BUCKETS)[:, None] == jnp.asarray(bucket)[None, :]).astype(F32)
    vals = jnp.dot(rel_bias.astype(F32).T * LOG2E, onehot, precision=lax.Precision.HIGHEST)
    vals = vals.reshape((N_HEADS,) + rel.shape)
    head = _pair_head_index()
    return jnp.stack([jnp.stack([jnp.concatenate([vals[h] for h in head[g, v]], axis=-1)
                                 for v in range(2)]) for g in range(N_GROUPS)])


def _bias_tables(rel_bias, s, n_sel):
    c = np.arange(LANES)[:, None]
    r = np.arange(Q_TILE)[None, :]
    far = rel_bias.astype(F32)[REL_BUCKETS - 1][_pair_head_index()] * LOG2E
    assert s // CMP_STRIDE <= LANES + Q_TILE // CMP_STRIDE
    rel_w = np.stack([dd * Q_TILE + r - c for dd in range(WIN_TILES)])
    assert (_t5_bucket_np(rel_w[2:]) == REL_BUCKETS - 1).all()
    rho = np.arange(2 * LANES)[:, None]
    rel_c = r - ((rho - LANES) * CMP_STRIDE + CMP_BLOCK - 1)
    vals = _bias_lookup(rel_bias, np.concatenate([rel_w, rel_c.reshape(2, LANES, Q_TILE)]))
    bias_w, bias_c = vals[:, :, :WIN_TILES], vals[:, :, WIN_TILES:].reshape(N_GROUPS, 2, 2 * LANES, COLS)
    cols = lambda a: np.tile(a, (1,) * (a.ndim - 1) + (PAIRS_PER_GROUP,))

    tw = jnp.where(cols((rel_w >= 0) & (rel_w < WINDOW)), bias_w, NEG)
    tw = jnp.concatenate([tw, jnp.full_like(tw[:, :, :1], NEG)], axis=2)
    far_cols = jnp.repeat(far, Q_TILE, axis=-1)[:, :, None, None, :]
    ts = jnp.where(cols(rel_w[:3] >= 0), bias_w[:, :, :3] - far_cols, NEG)
    bc = jnp.where(cols(rel_c >= 0), bias_c, NEG)
    hi = far.astype(BF16)
    lo = (far - hi.astype(F32)).astype(BF16)
    pieces = jnp.stack([hi[:, 0], lo[:, 0], hi[:, 1], lo[:, 1]], axis=-1)
    pieces = jnp.broadcast_to(pieces[:, :, None, :], (N_GROUPS, PAIRS_PER_GROUP, Q_TILE, 4))
    far_lanes = jnp.pad(pieces.reshape(N_GROUPS, COLS, 4),
                        ((0, 0), (0, 0), (n_sel, LANES - n_sel - 4)))
    return tw, ts, bc, far_lanes


def _overlap_t(s):
    n_cmp = (s - CMP_BLOCK) // CMP_STRIDE + 1
    cs = np.arange(LANES) * CMP_STRIDE
    ss = np.arange(s // SEL_BLOCK) * SEL_BLOCK
    ovt = ((cs[None, :] <= ss[:, None] + SEL_BLOCK - 1) & (cs[None, :] + CMP_BLOCK - 1 >= ss[:, None])
           & (np.arange(LANES)[None, :] < n_cmp))
    return jnp.asarray(ovt, BF16)


def _pad_w_in(w):
    used = NSA_WIDTH + 6 * KV_WIDTH + 3 * N_HEADS
    gap = (KV_COL_TILE + 1) * COL_TILE - used
    left = jnp.pad(w[:, :used], ((0, 0), (0, w.shape[1] - used + gap)))
    right = jnp.pad(w[:, used:], ((0, 0), (used + gap, 0)))
    return (left + right).astype(BF16)


def _compress_weights(pos, w1, w2):
    half = CMP_BLOCK // 2
    eye = jnp.eye(N_GROUPS, dtype=F32)
    w1r = w1.reshape(CMP_BLOCK, HEAD_DIM, CMP_HIDDEN)
    blk = lambda part: jnp.einsum('idn,gh->igdhn', part, eye).reshape(
        half * KV_WIDTH, N_GROUPS * CMP_HIDDEN)
    w1d = jnp.concatenate([blk(w1r[:half]), blk(w1r[half:])], axis=1).astype(BF16)
    w2d = jnp.einsum('nd,gh->gnhd', w2, eye).reshape(N_GROUPS * CMP_HIDDEN, KV_WIDTH).astype(BF16)
    tilepos = lambda part: jnp.broadcast_to(part[:, None, :], (half, N_GROUPS, HEAD_DIM)).reshape(1, -1)
    posd = jnp.concatenate([tilepos(pos[:half]), tilepos(pos[half:])], axis=0).astype(F32)
    return posd, w1d, w2d


def kernel(x, norm_in_g, w_in, pos_ck, w_ck1, w_ck2, pos_cv, w_cv1, w_cv2, rel_bias, conv_w, conv_b,
           conv_ln_g, conv_ln_b, w_conv_proj, w_nsa_proj, w_out, norm_f_g):
    b, s, d = x.shape
    n_sel = s // SEL_BLOCK
    assert d == D_MODEL and w_in.shape[0] == 1, "single-layer block with D_MODEL=1024"
    assert s % (2 * Q_TILE) == 0 and s // CMP_STRIDE <= LANES and s >= WINDOW
    assert n_sel + 4 <= LANES
    m = b * s
    x2 = x.reshape(m, d)
    row = lambda a: a.reshape(1, -1).astype(F32)

    proj_kv, proj_main = _input_projection(x2, row(norm_in_g[0]), _pad_w_in(w_in[0]), min(1024, m))

    chunks = s // CMP_STRIDE
    pk, w1k, w2k = _compress_weights(pos_ck[0], w_ck1[0], w_ck2[0])
    pv, w1v, w2v = _compress_weights(pos_cv[0], w_cv1[0], w_cv2[0])
    kcmp, vcmp = _compress(proj_kv, pk, pv, w1k, w1v, w2k, w2v, b, s)
    if chunks < LANES:
        padrows = ((0, 0), (0, LANES - chunks), (0, 0))
        kcmp, vcmp = jnp.pad(kcmp, padrows), jnp.pad(vcmp, padrows)

    bias_w, bias_s, bias_c, far_lanes = _bias_tables(rel_bias, s, n_sel)
    o_nsa = _attention(proj_main, proj_kv, kcmp, vcmp, bias_c, bias_w, bias_s, _overlap_t(s),
                       far_lanes, b, s, min(N_SELECT, n_sel))

    cw = jnp.pad(conv_w[0].astype(F32), ((0, CONV_HALO - CONV_KERNEL), (0, 0)))
    out = _merge(proj_main, o_nsa, x2, cw, row(conv_b[0]), row(conv_ln_g[0]), row(conv_ln_b[0]),
                 w_conv_proj[0].astype(BF16), w_nsa_proj[0].astype(BF16), w_out[0].astype(BF16),
                 row(norm_f_g), b, s, MERGE_ROWS)
    return out.reshape(b, s, d)
```

```python
import functools
import math

import numpy as np
import jax
import jax.numpy as jnp
from jax import lax
from jax.experimental import pallas as pl
from jax.experimental.pallas import tpu as pltpu

F32 = jnp.float32
BF16 = jnp.bfloat16

D_MODEL = 1024
N_HEADS = 16
N_GROUPS = 2
HEADS_PER_GROUP = N_HEADS // N_GROUPS
PAIRS_PER_GROUP = HEADS_PER_GROUP // 2
HEAD_DIM = 64
NSA_WIDTH = N_HEADS * HEAD_DIM
KV_WIDTH = N_GROUPS * HEAD_DIM
CMP_BLOCK = 32
CMP_STRIDE = 16
CMP_HIDDEN = 256
SEL_BLOCK = 64
N_SELECT = 8
WINDOW = 512
Q_TILE = 128
CONV_KERNEL = 31
CONV_HALO = 32
REL_BUCKETS = 32
REL_MAX_DIST = 128
EPS = 1e-6
NEG = -1e30
FORCE_SCORE = 1e6
LOG2E = math.log2(math.e)
LANES = 128
SUBLANES = 8
ONES_ROWS = 16
COLS = PAIRS_PER_GROUP * Q_TILE
COL_TILE = 1024
N_COL_TILES = 8
KV_COL_TILE = 1
MERGE_ROWS = 512
WIN_TILES = WINDOW // Q_TILE + 1
VMEM_LIMIT = 56 * 1024 * 1024


def _dot(a, b):
    return jnp.dot(a, b, preferred_element_type=F32)


def _dot_nt(a, b):
    return lax.dot_general(a, b, (((1,), (1,)), ((), ())), preferred_element_type=F32)


def _sigmoid(x):
    return 1.0 / (1.0 + jnp.exp2(x * (-LOG2E)))


def _silu(x):
    return x * _sigmoid(x)


def _proj_kernel(x_ref, g_ref, w_ref, kv_ref, main_ref, h_ref):
    j = pl.program_id(1)

    @pl.when(j == 0)
    def _():
        x = x_ref[...]
        ms = jnp.mean(x * x, axis=-1, keepdims=True)
        h_ref[...] = ((x * lax.rsqrt(ms + EPS)) * g_ref[...]).astype(BF16)

    acc = _dot(h_ref[...], w_ref[...])

    @pl.when(j == KV_COL_TILE)
    def _():
        kv_ref[...] = acc

    @pl.when(j != KV_COL_TILE)
    def _():
        main_ref[...] = acc.astype(BF16)


def _input_projection(x2, g, w_perm, tm):
    m = x2.shape[0]
    return pl.pallas_call(
        _proj_kernel,
        grid=(m // tm, N_COL_TILES),
        in_specs=[
            pl.BlockSpec((tm, D_MODEL), lambda i, j: (i, 0)),
            pl.BlockSpec((1, D_MODEL), lambda i, j: (0, 0)),
            pl.BlockSpec((D_MODEL, COL_TILE), lambda i, j: (0, j)),
        ],
        out_specs=[
            pl.BlockSpec((tm, COL_TILE), lambda i, j: (i, 0)),
            pl.BlockSpec((tm, COL_TILE), lambda i, j: (i, jnp.maximum(j - 1, 0))),
        ],
        out_shape=[
            jax.ShapeDtypeStruct((m, COL_TILE), F32),
            jax.ShapeDtypeStruct((m, (N_COL_TILES - 1) * COL_TILE), BF16),
        ],
        scratch_shapes=[pltpu.VMEM((tm, D_MODEL), BF16)],
        compiler_params=pltpu.CompilerParams(
            dimension_semantics=("parallel", "arbitrary"), vmem_limit_bytes=VMEM_LIMIT),
        name="input_projection",
    )(x2, g, w_perm)


def _compress_kernel(kf_ref, vf_ref, pk_ref, pv_ref, w1k_ref, w1v_ref, w2k_ref, w2v_ref,
                     kc_ref, vc_ref):
    def one(f_ref, pos_ref, w1_ref, w2_ref, o_ref):
        n = f_ref.shape[0] // CMP_STRIDE
        hw = N_GROUPS * CMP_HIDDEN
        first = jnp.zeros((n, hw), F32)
        second = jnp.zeros((n, hw), F32)
        for i in range(CMP_STRIDE):
            tok = f_ref[pl.ds(i, n, stride=CMP_STRIDE), :]
            lanes = slice(i * KV_WIDTH, (i + 1) * KV_WIDTH)
            first = first + _dot((tok + pos_ref[0:1, lanes]).astype(BF16), w1_ref[lanes, 0:hw])
            second = second + _dot((tok + pos_ref[1:2, lanes]).astype(BF16), w1_ref[lanes, hw:2 * hw])
        hid = first + pltpu.roll(second, n - 1, axis=0)
        o_ref[0] = _dot(_silu(hid).astype(BF16), w2_ref[...])

    one(kf_ref, pk_ref, w1k_ref, w2k_ref, kc_ref)
    one(vf_ref, pv_ref, w1v_ref, w2v_ref, vc_ref)


def _compress(proj_kv, pk, pv, w1k, w1v, w2k, w2v, b, s):
    n = s // CMP_STRIDE
    const = lambda shape: pl.BlockSpec(shape, lambda i: (0,) * len(shape))
    kv_col = lambda c: pl.BlockSpec((s, KV_WIDTH), lambda i: (i, c))
    out = pl.BlockSpec((1, n, LANES), lambda i: (i, 0, 0))
    return pl.pallas_call(
        _compress_kernel,
        grid=(b,),
        in_specs=[kv_col(0), kv_col(1), const(pk.shape), const(pv.shape), const(w1k.shape),
                  const(w1v.shape), const(w2k.shape), const(w2v.shape)],
        out_specs=[out, out],
        out_shape=[jax.ShapeDtypeStruct((b, n, LANES), F32)] * 2,
        compiler_params=pltpu.CompilerParams(
            dimension_semantics=("parallel",), vmem_limit_bytes=VMEM_LIMIT),
        name="nsa_compress",
    )(proj_kv, proj_kv, pk, pv, w1k, w1v, w2k, w2v)


def _attn_kernel(q_ref, ks_ref, vs_ref, kw_ref, vw_ref, gt_ref, kc_ref, vc_ref, bc_ref, tw_ref,
                 ts_ref, ovt_ref, bx_ref, o_ref,
                 ksv, kwv, kcv, vst, vwt, vct, rhs, acc, obuf, s_a, s_b, s_w, *, n_top):
    qi = pl.program_id(1)
    n_kt = kwv.shape[1]
    n_sel = ovt_ref.shape[0]
    variants = 2 * N_GROUPS

    @pl.when(qi == 0)
    def _prepare_kv():
        def halves(k):
            lo = lax.broadcasted_iota(jnp.int32, k.shape, 1) < HEAD_DIM
            kr = pltpu.roll(k, HEAD_DIM, axis=1)
            z = jnp.zeros_like(k)
            c = lambda a: a.astype(BF16)
            return ((c(jnp.where(lo, k, z)), c(jnp.where(lo, z, kr))),
                    (c(jnp.where(lo, kr, z)), c(jnp.where(lo, z, k))))

        def extra_lanes(shape, v):
            pair = lax.broadcasted_iota(jnp.int32, shape, 0)
            r = lax.broadcasted_iota(jnp.int32, shape, 1)
            lane = lax.broadcasted_iota(jnp.int32, shape, 2)
            one = ((lane >= n_sel + 2 * v) & (lane < n_sel + 2 * v + 2)
                   | (lane == pair * (2 * Q_TILE // SEL_BLOCK) + r // SEL_BLOCK))
            return jnp.where(one, 1.0, 0.0).astype(BF16)

        pair_shape = (n_kt // 2, 2 * Q_TILE, LANES)
        tile_shape = (n_kt, Q_TILE, LANES)
        k_sel, k_win, k_cmp = halves(ks_ref[...]), halves(kw_ref[...]), halves(kc_ref[0])
        for g in range(N_GROUPS):
            for v in range(2):
                rows = slice(v * 2 * Q_TILE, (v + 1) * 2 * Q_TILE)
                ksv[g, :, rows, 0:LANES] = k_sel[g][v].reshape(pair_shape)
                ksv[g, :, rows, LANES:2 * LANES] = extra_lanes(pair_shape, v)
                kwv[g, :, v * Q_TILE:(v + 1) * Q_TILE, :] = k_win[g][v].reshape(tile_shape)
                kcv[g, v * LANES:(v + 1) * LANES, :] = k_cmp[g][v]
        vst[:, :, HEAD_DIM:, :] = jnp.ones((N_GROUPS, n_kt // 2, ONES_ROWS, 2 * Q_TILE), BF16)
        vwt[:, :, HEAD_DIM:, :] = jnp.ones((N_GROUPS, n_kt, ONES_ROWS, Q_TILE), BF16)
        for kt in range(n_kt):
            rows = slice(kt * Q_TILE, (kt + 1) * Q_TILE)
            half = slice((kt % 2) * Q_TILE, (kt % 2 + 1) * Q_TILE)
            vs_t = vs_ref[rows, :].T.astype(BF16)
            vw_t = vw_ref[rows, :].T.astype(BF16)
            for g in range(N_GROUPS):
                vst[g, kt // 2, 0:HEAD_DIM, half] = vs_t[g * HEAD_DIM:(g + 1) * HEAD_DIM]
                vwt[g, kt, 0:HEAD_DIM, :] = vw_t[g * HEAD_DIM:(g + 1) * HEAD_DIM]
        vct[...] = vc_ref[0].T.astype(BF16)

    qt = q_ref[...]
    gsig_t = _sigmoid(gt_ref[...]).T

    def col_max(s):
        return jnp.max(s.reshape(s.shape[0] // 8, 8, COLS), axis=0)

    def group_rows(g):
        return slice(g * HEAD_DIM, (g + 1) * HEAD_DIM)

    def online(gv, s, s_max, vt, m):
        m_new = jnp.maximum(m, jnp.max(s_max, axis=0, keepdims=True))
        e = jnp.exp2(s - m_new)
        acc[gv] = acc[gv] * jnp.exp2(m - m_new) + _dot(vt, e.astype(BF16))
        return m_new

    def fresh_state():
        acc[...] = jnp.zeros(acc.shape, F32)
        return tuple(jnp.full((1, COLS), NEG, F32) for _ in range(variants))

    def finish(branch):
        for g in range(N_GROUPS):
            halves = []
            for v in range(2):
                gv = g * 2 + v
                l = acc[gv, HEAD_DIM:HEAD_DIM + 1, :]
                halves.append(acc[gv, 0:HEAD_DIM, :] * (1.0 / jnp.maximum(l, 1e-30)))
            obuf[branch, g] = jnp.concatenate(halves, axis=0)

    groups = range(N_GROUPS)
    for g in groups:
        q4 = jnp.concatenate(
            [qt[:, (g * PAIRS_PER_GROUP + p) * LANES:(g * PAIRS_PER_GROUP + p + 1) * LANES]
             for p in range(PAIRS_PER_GROUP)], axis=0)
        rhs[g, :, 0:LANES] = q4

    bc_rows = pl.ds(pl.multiple_of(LANES - qi * (Q_TILE // CMP_STRIDE), SUBLANES), LANES)
    s_cmp = [_dot_nt(kcv[g], rhs[g, :, 0:LANES])
             + jnp.concatenate([bc_ref[g, 0, bc_rows, :], bc_ref[g, 1, bc_rows, :]], axis=0)
             for g in groups]
    psum_t = []
    for g in groups:
        tot = jnp.zeros((LANES, Q_TILE), F32)
        halves = []
        for v in range(2):
            s = s_cmp[g][v * LANES:(v + 1) * LANES]
            m = jnp.max(s, axis=0, keepdims=True)
            e = jnp.exp2(s - jnp.where(m > 0.5 * NEG, m, 0.0))
            p = e * (1.0 / jnp.maximum(jnp.sum(e, axis=0, keepdims=True), 1e-30))
            for pp in range(PAIRS_PER_GROUP):
                tot = tot + p[:, pp * Q_TILE:(pp + 1) * Q_TILE]
            halves.append(_dot(vct[group_rows(g), :], p.astype(BF16)))
        obuf[0, g] = jnp.concatenate(halves, axis=0)
        psum_t.append(tot)

    ovt = ovt_ref[...]
    imp_t = []
    for g in groups:
        p_hi = psum_t[g].astype(BF16)
        r1 = psum_t[g] - p_hi.astype(F32)
        p_mid = r1.astype(BF16)
        p_lo = (r1 - p_mid.astype(F32)).astype(BF16)
        imp_t.append(_dot(ovt, p_hi) + _dot(ovt, p_mid) + _dot(ovt, p_lo))

    j_idx = lax.broadcasted_iota(jnp.int32, (n_sel, Q_TILE), 0)
    r_idx = lax.broadcasted_iota(jnp.int32, (n_sel, Q_TILE), 1)
    blk_t = qi * (Q_TILE // SEL_BLOCK) + r_idx // SEL_BLOCK
    valid_blk = j_idx <= blk_t
    forced = (j_idx == 0) | (j_idx == blk_t) | (j_idx == blk_t - 1)
    prio = [jnp.where(valid_blk, jnp.where(forced, FORCE_SCORE, imp_t[g]), -FORCE_SCORE) for g in groups]
    rank = [jnp.zeros((n_sel, Q_TILE), F32) for _ in groups]
    for jj in range(n_sel):
        later = j_idx > jj
        for g in groups:
            row = prio[g][jj:jj + 1, :]
            beats = (row > prio[g]) | ((row == prio[g]) & later)
            rank[g] = rank[g] + jnp.where(beats, 1.0, 0.0)
    for g in groups:
        drop_t = jnp.where((rank[g] < n_top) & valid_blk, 0.0, NEG)
        drop_t = jnp.concatenate([drop_t, jnp.zeros((LANES - n_sel, Q_TILE), F32)], axis=0)
        drop = drop_t.T.astype(BF16)
        rhs[g, :, LANES:2 * LANES] = jnp.concatenate([drop] * PAIRS_PER_GROUP, axis=0) + bx_ref[g]

    pair_rows = 4 * Q_TILE

    def half_rows(v):
        return slice(v * 2 * Q_TILE, (v + 1) * 2 * Q_TILE)

    def max_rows(v):
        return slice(pair_rows + v * SUBLANES, pair_rows + (v + 1) * SUBLANES)

    def logits_group(buf, g, i, tables=None):
        s_all = _dot_nt(ksv[g, i], rhs[g])
        for v in range(2):
            s = s_all[half_rows(v)]
            if tables is not None:
                s = jnp.concatenate([s[h * Q_TILE:(h + 1) * Q_TILE] + ts_ref[g, v, tables[h]]
                                     for h in range(2)], axis=0)
            buf[g, half_rows(v), :] = s
            buf[g, max_rows(v), :] = col_max(s)

    def logits_into(buf, i):
        for g in range(N_GROUPS):
            logits_group(buf, g, i)

    def consume(buf, i, state):
        ms = list(state)
        for g in range(N_GROUPS):
            for v in range(2):
                gv = g * 2 + v
                ms[gv] = online(gv, buf[g, half_rows(v), :], buf[g, max_rows(v), :], vst[g, i], ms[gv])
        return tuple(ms)

    n_all = (qi + 2) // 2
    n_far = jnp.maximum(qi - 1, 0) // 2

    first = jnp.maximum(qi - (WIN_TILES - 1), 0)
    tiles = []
    for t in range(WIN_TILES):
        dd = qi - (first + t)
        table = jnp.where(dd >= 0, dd, WIN_TILES)
        tiles.append((jnp.minimum(first + t, n_kt - 1), table))
    m_win = []
    one_matmul = n_kt >= WIN_TILES
    for g in range(N_GROUPS):
        mx = [jnp.full((8, COLS), NEG, F32) for _ in range(2)]
        if one_matmul:
            k_rows = kwv[g, pl.ds(first, WIN_TILES)].reshape(WIN_TILES * 2 * Q_TILE, LANES)
            s_five = _dot_nt(k_rows, rhs[g, :, 0:LANES])
        for t, (kt, table) in enumerate(tiles):
            if one_matmul:
                s_all = s_five[t * 2 * Q_TILE:(t + 1) * 2 * Q_TILE]
            else:
                s_all = _dot_nt(kwv[g, kt], rhs[g, :, 0:LANES])
            for v in range(2):
                s = s_all[v * Q_TILE:(v + 1) * Q_TILE] + tw_ref[g, v, table]
                s_w[g, t, v * Q_TILE:(v + 1) * Q_TILE, :] = s
                mx[v] = jnp.maximum(mx[v], col_max(s))
        m_win += [jnp.max(mx[v], axis=0, keepdims=True) for v in range(2)]

    for g in range(N_GROUPS):
        logits_group(s_a, g, 0)
        halves = []
        for v in range(2):
            o = jnp.zeros((HEAD_DIM + ONES_ROWS, COLS), F32)
            for t, (kt, _) in enumerate(tiles):
                e = jnp.exp2(s_w[g, t, v * Q_TILE:(v + 1) * Q_TILE, :] - m_win[g * 2 + v])
                o = o + _dot(vwt[g, kt], e.astype(BF16))
            halves.append(o[0:HEAD_DIM] * (1.0 / jnp.maximum(o[HEAD_DIM:HEAD_DIM + 1], 1e-30)))
        obuf[2, g] = jnp.concatenate(halves, axis=0)

    def far_two(j, state):
        logits_into(s_b, 2 * j + 1)
        state = consume(s_a, 2 * j, state)
        logits_into(s_a, 2 * j + 2)
        return consume(s_b, 2 * j + 1, state)

    state = lax.fori_loop(0, n_far // 2, far_two, fresh_state())
    state = lax.fori_loop(0, n_far % 2, lambda _, st: consume(s_a, n_far - 1, st), state)

    def near_pair(i, state):
        tables = [jnp.clip(qi - (2 * i + h), 0, 2) for h in range(2)]
        for g in range(N_GROUPS):
            logits_group(s_b, g, i, tables)
        return consume(s_b, i, state)

    diag_alone = 1 - qi % 2
    state = lax.fori_loop(n_far, n_all - diag_alone, near_pair, state)

    def diagonal_tile(_, state):
        ms = list(state)
        i = n_all - 1
        for g in range(N_GROUPS):
            lhs = jnp.concatenate([ksv[g, i, v * 2 * Q_TILE:v * 2 * Q_TILE + Q_TILE, :] for v in range(2)],
                                  axis=0)
            s_all = _dot_nt(lhs, rhs[g])
            for v in range(2):
                s = s_all[v * Q_TILE:(v + 1) * Q_TILE] + ts_ref[g, v, 0]
                s_b[g, v * Q_TILE:(v + 1) * Q_TILE, :] = s
                s_b[g, max_rows(v), :] = col_max(s)
        for g in range(N_GROUPS):
            for v in range(2):
                gv = g * 2 + v
                ms[gv] = online(gv, s_b[g, v * Q_TILE:(v + 1) * Q_TILE, :], s_b[g, max_rows(v), :],
                                vst[g, i, :, 0:Q_TILE], ms[gv])
        return tuple(ms)

    lax.fori_loop(0, diag_alone, diagonal_tile, state)
    finish(1)

    for g in range(N_GROUPS):
        for p in range(PAIRS_PER_GROUP):
            halves = []
            for v in range(2):
                tot = jnp.zeros((HEAD_DIM, Q_TILE), F32)
                for br in range(3):
                    c = br * N_HEADS + g * HEADS_PER_GROUP + 2 * p + v
                    tot = tot + gsig_t[c:c + 1, :] * obuf[br, g, v * HEAD_DIM:(v + 1) * HEAD_DIM,
                                                          p * Q_TILE:(p + 1) * Q_TILE]
                halves.append(tot)
            col = (g * PAIRS_PER_GROUP + p) * LANES
            o_ref[:, col:col + LANES] = jnp.concatenate(halves, axis=0).T.astype(BF16)


def _attention(proj_main, proj_kv, kcmp, vcmp, bias_c, bias_w, bias_s, ovt, far_lanes, b, s, n_top):
    n_q = s // Q_TILE
    once = pl.Buffered(1)
    kv_col = lambda c: pl.BlockSpec((s, LANES), lambda bi, qi: (bi, c))
    cmp_spec = pl.BlockSpec((1, LANES, LANES), lambda bi, qi: (bi, 0, 0))
    return pl.pallas_call(
        functools.partial(_attn_kernel, n_top=n_top),
        grid=(b, n_q),
        in_specs=[
            pl.BlockSpec((Q_TILE, NSA_WIDTH), lambda bi, qi: (bi * n_q + qi, 0)),
            kv_col(2), kv_col(3), kv_col(4), kv_col(5),
            pl.BlockSpec((Q_TILE, LANES), lambda bi, qi: (bi * n_q + qi, 6)),
            cmp_spec, cmp_spec,
            pl.BlockSpec(bias_c.shape, lambda bi, qi: (0,) * 4, pipeline_mode=once),
            pl.BlockSpec(bias_w.shape, lambda bi, qi: (0,) * 5, pipeline_mode=once),
            pl.BlockSpec(bias_s.shape, lambda bi, qi: (0,) * 5, pipeline_mode=once),
            pl.BlockSpec(ovt.shape, lambda bi, qi: (0, 0), pipeline_mode=once),
            pl.BlockSpec(far_lanes.shape, lambda bi, qi: (0, 0, 0), pipeline_mode=once),
        ],
        out_specs=pl.BlockSpec((Q_TILE, NSA_WIDTH), lambda bi, qi: (bi * n_q + qi, 0)),
        out_shape=jax.ShapeDtypeStruct((b * s, NSA_WIDTH), BF16),
        scratch_shapes=[
            pltpu.VMEM((N_GROUPS, n_q // 2, 4 * Q_TILE, 2 * LANES), BF16),
            pltpu.VMEM((N_GROUPS, n_q, 2 * Q_TILE, LANES), BF16),
            pltpu.VMEM((N_GROUPS, 2 * LANES, LANES), BF16),
            pltpu.VMEM((N_GROUPS, n_q // 2, HEAD_DIM + ONES_ROWS, 2 * Q_TILE), BF16),
            pltpu.VMEM((N_GROUPS, n_q, HEAD_DIM + ONES_ROWS, Q_TILE), BF16),
            pltpu.VMEM((LANES, LANES), BF16),
            pltpu.VMEM((N_GROUPS, COLS, 2 * LANES), BF16),
            pltpu.VMEM((2 * N_GROUPS, HEAD_DIM + ONES_ROWS, COLS), F32),
            pltpu.VMEM((3, N_GROUPS, LANES, COLS), F32),
            pltpu.VMEM((N_GROUPS, 4 * Q_TILE + 2 * SUBLANES, COLS), F32),
            pltpu.VMEM((N_GROUPS, 4 * Q_TILE + 2 * SUBLANES, COLS), F32),
            pltpu.VMEM((N_GROUPS, WIN_TILES, 2 * Q_TILE, COLS), F32),
        ],
        compiler_params=pltpu.CompilerParams(
            dimension_semantics=("parallel", "arbitrary"), vmem_limit_bytes=VMEM_LIMIT),
        name="nsa_attention",
    )(proj_main, proj_kv, proj_kv, proj_kv, proj_kv, proj_kv, kcmp, vcmp, bias_c, bias_w, bias_s,
      ovt, far_lanes)


def _merge_kernel(zn_ref, a_ref, b_ref, zc_ref, gc_ref, gn_ref, ah_ref, bh_ref, on_ref, x_ref,
                  cw_ref, cb_ref, lg_ref, lb_ref, wcp_ref, wnp_ref, wo_ref, gf_ref, out_ref, uext,
                  conv, shifted):
    i = pl.program_id(1)
    ts = a_ref.shape[0]
    f = lambda r: r[...].astype(F32)

    n_cblk = D_MODEL // LANES
    u_halo = jnp.where(i > 0, f(ah_ref) * _sigmoid(f(bh_ref)), 0.0)
    u = f(a_ref) * _sigmoid(f(b_ref))
    for cblk in range(n_cblk):
        cols = slice(cblk * LANES, (cblk + 1) * LANES)
        uext[cblk, 0:CONV_HALO, :] = u_halo[:, cols]
        uext[cblk, CONV_HALO:CONV_HALO + ts, :] = u[:, cols]
        uext[cblk, CONV_HALO + ts:, :] = jnp.zeros((SUBLANES, LANES), F32)

    lead = CONV_HALO - (CONV_KERNEL - 1)
    chunk = Q_TILE

    def conv_block(cblk, carry):
        for shift in range(SUBLANES):
            shifted[shift] = uext[cblk, shift:shift + ts + CONV_HALO, :]
        w = cw_ref[cblk]
        for h in range(ts // chunk):
            c = jnp.broadcast_to(cb_ref[cblk], (chunk, LANES))
            for shift in range(SUBLANES):
                xs = shifted[shift, h * chunk:h * chunk + chunk + CONV_HALO, :]
                for j in range(CONV_KERNEL):
                    if (lead + j) % SUBLANES == shift:
                        base = lead + j - shift
                        c = c + w[j:j + 1, :] * xs[base:base + chunk]
            conv[cblk, h * chunk:(h + 1) * chunk, :] = c
        return carry

    lax.fori_loop(0, n_cblk, conv_block, 0)
    c = jnp.concatenate([conv[cblk] for cblk in range(n_cblk)], axis=1)

    mu = jnp.mean(c, axis=-1, keepdims=True)
    cc = c - mu
    var = jnp.mean(cc * cc, axis=-1, keepdims=True)
    y = (cc * lax.rsqrt(var + EPS)) * lg_ref[...] + lb_ref[...]
    conv_act = _silu(y) * _silu(f(zc_ref))
    y_conv = _dot(conv_act.astype(BF16), wcp_ref[...])

    nsa_act = f(on_ref) * _silu(f(zn_ref))
    y_nsa = _dot(nsa_act.astype(BF16), wnp_ref[...])

    merged = _sigmoid(f(gc_ref)) * y_conv + _sigmoid(f(gn_ref)) * y_nsa
    xo = x_ref[...] + _dot(merged.astype(BF16), wo_ref[...])
    ms = jnp.mean(xo * xo, axis=-1, keepdims=True)
    out_ref[...] = (xo * lax.rsqrt(ms + EPS)) * gf_ref[...]


def _merge(proj_main, o_nsa, x2, cw, cb, lg, lb, wcp, wnp, wo, gf, b, s, ts):
    n_t = s // ts
    n_cblk = D_MODEL // LANES
    halo_per_tile = ts // CONV_HALO
    cw = jnp.transpose(cw.reshape(cw.shape[0], n_cblk, LANES), (1, 0, 2))
    cb = cb.reshape(n_cblk, 1, LANES)
    col = lambda c: pl.BlockSpec((ts, COL_TILE), lambda bi, ti: (bi * n_t + ti, c))
    halo = lambda c: pl.BlockSpec(
        (CONV_HALO, COL_TILE),
        lambda bi, ti: (jnp.maximum((bi * n_t + ti) * halo_per_tile - 1, 0), c))
    const = lambda a: pl.BlockSpec(a.shape, lambda bi, ti: (0,) * a.ndim, pipeline_mode=pl.Buffered(1))
    rowblk = pl.BlockSpec((ts, D_MODEL), lambda bi, ti: (bi * n_t + ti, 0))
    return pl.pallas_call(
        _merge_kernel,
        grid=(b, n_t),
        in_specs=[col(1), col(2), col(3), col(4), col(5), col(6), halo(2), halo(3), rowblk, rowblk,
                  const(cw), const(cb), const(lg), const(lb), const(wcp), const(wnp), const(wo),
                  const(gf)],
        out_specs=rowblk,
        out_shape=jax.ShapeDtypeStruct((b * s, D_MODEL), F32),
        scratch_shapes=[pltpu.VMEM((n_cblk, ts + CONV_HALO + SUBLANES, LANES), F32),
                        pltpu.VMEM((n_cblk, ts, LANES), F32),
                        pltpu.VMEM((SUBLANES, ts + CONV_HALO, LANES), F32)],
        compiler_params=pltpu.CompilerParams(
            dimension_semantics=("parallel", "arbitrary"), vmem_limit_bytes=VMEM_LIMIT),
        name="conv_merge",
    )(proj_main, proj_main, proj_main, proj_main, proj_main, proj_main, proj_main, proj_main,
      o_nsa, x2, cw, cb, lg, lb, wcp, wnp, wo, gf)


def _t5_bucket_np(rel):
    rel = np.maximum(rel, 0)
    max_exact = REL_BUCKETS // 2
    relf = np.maximum(rel, 1).astype(np.float32)
    large = max_exact + (np.log(relf / np.float32(max_exact))
                         / np.float32(np.log(REL_MAX_DIST / max_exact))
                         * np.float32(REL_BUCKETS - max_exact)).astype(np.int32)
    large = np.minimum(large, REL_BUCKETS - 1)
    return np.where(rel < max_exact, rel, large)


def _pair_head_index():
    g = np.arange(N_GROUPS)[:, None, None]
    v = np.arange(2)[None, :, None]
    p = np.arange(PAIRS_PER_GROUP)[None, None, :]
    return g * HEADS_PER_GROUP + 2 * p + v


def _bias_lookup(rel_bias, rel):
    bucket = _t5_bucket_np(rel).reshape(-1)
    onehot = (jnp.arange(REL_BUCKETS)[:, None] == jnp.asarray(bucket)[None, :]).astype(F32)
    vals = jnp.dot(rel_bias.astype(F32).T * LOG2E, onehot, precision=lax.Precision.HIGHEST)
    vals = vals.reshape((N_HEADS,) + rel.shape)
    head = _pair_head_index()
    return jnp.stack([jnp.stack([jnp.concatenate([vals[h] for h in head[g, v]], axis=-1)
                                 for v in range(2)]) for g in range(N_GROUPS)])


def _bias_tables(rel_bias, s, n_sel):
    c = np.arange(LANES)[:, None]
    r = np.arange(Q_TILE)[None, :]
    far = rel_bias.astype(F32)[REL_BUCKETS - 1][_pair_head_index()] * LOG2E
    assert s // CMP_STRIDE <= LANES + Q_TILE // CMP_STRIDE
    rel_w = np.stack([dd * Q_TILE + r - c for dd in range(WIN_TILES)])
    assert (_t5_bucket_np(rel_w[2:]) == REL_BUCKETS - 1).all()
    rho = np.arange(2 * LANES)[:, None]
    rel_c = r - ((rho - LANES) * CMP_STRIDE + CMP_BLOCK - 1)
    vals = _bias_lookup(rel_bias, np.concatenate([rel_w, rel_c.reshape(2, LANES, Q_TILE)]))
    bias_w, bias_c = vals[:, :, :WIN_TILES], vals[:, :, WIN_TILES:].reshape(N_GROUPS, 2, 2 * LANES, COLS)
    cols = lambda a: np.tile(a, (1,) * (a.ndim - 1) + (PAIRS_PER_GROUP,))

    tw = jnp.where(cols((rel_w >= 0) & (rel_w < WINDOW)), bias_w, NEG)
    tw = jnp.concatenate([tw, jnp.full_like(tw[:, :, :1], NEG)], axis=2)
    far_cols = jnp.repeat(far, Q_TILE, axis=-1)[:, :, None, None, :]
    ts = jnp.where(cols(rel_w[:3] >= 0), bias_w[:, :, :3] - far_cols, NEG)
    bc = jnp.where(cols(rel_c >= 0), bias_c, NEG)
    hi = far.astype(BF16)
    lo = (far - hi.astype(F32)).astype(BF16)
    pieces = jnp.stack([hi[:, 0], lo[:, 0], hi[:, 1], lo[:, 1]], axis=-1)
    pieces = jnp.broadcast_to(pieces[:, :, None, :], (N_GROUPS, PAIRS_PER_GROUP, Q_TILE, 4))
    far_lanes = jnp.pad(pieces.reshape(N_GROUPS, COLS, 4),
                        ((0, 0), (0, 0), (n_sel, LANES - n_sel - 4)))
    return tw, ts, bc, far_lanes


def _overlap_t(s):
    n_cmp = (s - CMP_BLOCK) // CMP_STRIDE + 1
    cs = np.arange(LANES) * CMP_STRIDE
    ss = np.arange(s // SEL_BLOCK) * SEL_BLOCK
    ovt = ((cs[None, :] <= ss[:, None] + SEL_BLOCK - 1) & (cs[None, :] + CMP_BLOCK - 1 >= ss[:, None])
           & (np.arange(LANES)[None, :] < n_cmp))
    return jnp.asarray(ovt, BF16)


def _pad_w_in(w):
    used = NSA_WIDTH + 6 * KV_WIDTH + 3 * N_HEADS
    gap = (KV_COL_TILE + 1) * COL_TILE - used
    col = jnp.arange(w.shape[1])[None, :]
    w = jnp.where(col < NSA_WIDTH, w * (HEAD_DIM ** -0.5 * LOG2E), w)
    left = jnp.pad(w[:, :used], ((0, 0), (0, w.shape[1] - used + gap)))
    right = jnp.pad(w[:, used:], ((0, 0), (used + gap, 0)))
    return (left + right).astype(BF16)


def _compress_weights(pos, w1, w2):
    half = CMP_BLOCK // 2
    eye = jnp.eye(N_GROUPS, dtype=F32)
    w1r = w1.reshape(CMP_BLOCK, HEAD_DIM, CMP_HIDDEN)
    blk = lambda part: jnp.einsum('idn,gh->igdhn', part, eye).reshape(
        half * KV_WIDTH, N_GROUPS * CMP_HIDDEN)
    w1d = jnp.concatenate([blk(w1r[:half]), blk(w1r[half:])], axis=1).astype(BF16)
    w2d = jnp.einsum('nd,gh->gnhd', w2, eye).reshape(N_GROUPS * CMP_HIDDEN, KV_WIDTH).astype(BF16)
    tilepos = lambda part: jnp.broadcast_to(part[:, None, :], (half, N_GROUPS, HEAD_DIM)).reshape(1, -1)
    posd = jnp.concatenate([tilepos(pos[:half]), tilepos(pos[half:])], axis=0).astype(F32)
    return posd, w1d, w2d


def kernel(x, norm_in_g, w_in, pos_ck, w_ck1, w_ck2, pos_cv, w_cv1, w_cv2, rel_bias, conv_w, conv_b,
           conv_ln_g, conv_ln_b, w_conv_proj, w_nsa_proj, w_out, norm_f_g):
    b, s, d = x.shape
    n_sel = s // SEL_BLOCK
    assert d == D_MODEL and w_in.shape[0] == 1, "single-layer block with D_MODEL=1024"
    assert s % (2 * Q_TILE) == 0 and s // CMP_STRIDE <= LANES and s >= WINDOW
    assert n_sel + 4 <= LANES
    m = b * s
    x2 = x.reshape(m, d)
    row = lambda a: a.reshape(1, -1).astype(F32)

    proj_kv, proj_main = _input_projection(x2, row(norm_in_g[0]), _pad_w_in(w_in[0]), min(1024, m))

    chunks = s // CMP_STRIDE
    pk, w1k, w2k = _compress_weights(pos_ck[0], w_ck1[0], w_ck2[0])
    pv, w1v, w2v = _compress_weights(pos_cv[0], w_cv1[0], w_cv2[0])
    kcmp, vcmp = _compress(proj_kv, pk, pv, w1k, w1v, w2k, w2v, b, s)
    if chunks < LANES:
        padrows = ((0, 0), (0, LANES - chunks), (0, 0))
        kcmp, vcmp = jnp.pad(kcmp, padrows), jnp.pad(vcmp, padrows)

    bias_w, bias_s, bias_c, far_lanes = _bias_tables(rel_bias, s, n_sel)
    o_nsa = _attention(proj_main, proj_kv, kcmp, vcmp, bias_c, bias_w, bias_s, _overlap_t(s),
                       far_lanes, b, s, min(N_SELECT, n_sel))

    cw = jnp.pad(conv_w[0].astype(F32), ((0, CONV_HALO - CONV_KERNEL), (0, 0)))
    out = _merge(proj_main, o_nsa, x2, cw, row(conv_b[0]), row(conv_ln_g[0]), row(conv_ln_b[0]),
                 w_conv_proj[0].astype(BF16), w_nsa_proj[0].astype(BF16), w_out[0].astype(BF16),
                 row(norm_f_g), b, s, MERGE_ROWS)
    return out.reshape(b, s, d)
```

```python
import functools
import math

import numpy as np
import jax
import jax.numpy as jnp
from jax import lax
from jax.experimental import pallas as pl
from jax.experimental.pallas import tpu as pltpu

F32 = jnp.float32
BF16 = jnp.bfloat16

D_MODEL = 1024
N_HEADS = 16
N_GROUPS = 2
HEADS_PER_GROUP = N_HEADS // N_GROUPS
PAIRS_PER_GROUP = HEADS_PER_GROUP // 2
HEAD_DIM = 64
NSA_WIDTH = N_HEADS * HEAD_DIM
KV_WIDTH = N_GROUPS * HEAD_DIM
CMP_BLOCK = 32
CMP_STRIDE = 16
CMP_HIDDEN = 256
SEL_BLOCK = 64
N_SELECT = 8
WINDOW = 512
Q_TILE = 128
CONV_KERNEL = 31
CONV_HALO = 32
REL_BUCKETS = 32
REL_MAX_DIST = 128
EPS = 1e-6
NEG = -1e30
FORCE_SCORE = 1e6
LOG2E = math.log2(math.e)
LANES = 128
SUBLANES = 8
ONES_ROWS = 16
COLS = PAIRS_PER_GROUP * Q_TILE
COL_TILE = 1024
N_COL_TILES = 8
KV_COL_TILE = 1
MERGE_ROWS = 512
WIN_TILES = WINDOW // Q_TILE + 1
VMEM_LIMIT = 56 * 1024 * 1024


def _dot(a, b):
    return jnp.dot(a, b, preferred_element_type=F32)


def _dot_nt(a, b):
    return lax.dot_general(a, b, (((1,), (1,)), ((), ())), preferred_element_type=F32)


def _sigmoid(x):
    return 1.0 / (1.0 + jnp.exp2(x * (-LOG2E)))


def _silu(x):
    return x * _sigmoid(x)


def _proj_kernel(x_ref, g_ref, w_ref, kv_ref, main_ref, h_ref):
    j = pl.program_id(1)

    @pl.when(j == 0)
    def _():
        x = x_ref[...]
        ms = jnp.mean(x * x, axis=-1, keepdims=True)
        h_ref[...] = ((x * lax.rsqrt(ms + EPS)) * g_ref[...]).astype(BF16)

    acc = _dot(h_ref[...], w_ref[...])
    main_ref[...] = acc.astype(BF16)

    @pl.when(j == KV_COL_TILE // 2)
    def _():
        kv_ref[...] = acc[:, (KV_COL_TILE % 2) * COL_TILE:(KV_COL_TILE % 2 + 1) * COL_TILE]


def _input_projection(x2, g, w_perm, tm):
    m = x2.shape[0]
    return pl.pallas_call(
        _proj_kernel,
        grid=(m // tm, N_COL_TILES // 2),
        in_specs=[
            pl.BlockSpec((tm, D_MODEL), lambda i, j: (i, 0)),
            pl.BlockSpec((1, D_MODEL), lambda i, j: (0, 0)),
            pl.BlockSpec((D_MODEL, 2 * COL_TILE), lambda i, j: (0, j)),
        ],
        out_specs=[
            pl.BlockSpec((tm, COL_TILE), lambda i, j: (i, 0)),
            pl.BlockSpec((tm, 2 * COL_TILE), lambda i, j: (i, j)),
        ],
        out_shape=[
            jax.ShapeDtypeStruct((m, COL_TILE), F32),
            jax.ShapeDtypeStruct((m, N_COL_TILES * COL_TILE), BF16),
        ],
        scratch_shapes=[pltpu.VMEM((tm, D_MODEL), BF16)],
        compiler_params=pltpu.CompilerParams(
            dimension_semantics=("parallel", "arbitrary"), vmem_limit_bytes=VMEM_LIMIT),
        name="input_projection",
    )(x2, g, w_perm)


def _compress_kernel(kf_ref, vf_ref, pk_ref, pv_ref, w1k_ref, w1v_ref, w2k_ref, w2v_ref,
                     kc_ref, vc_ref):
    def one(f_ref, pos_ref, w1_ref, w2_ref, o_ref):
        n = f_ref.shape[0] // CMP_STRIDE
        hw = N_GROUPS * CMP_HIDDEN
        first = jnp.zeros((n, hw), F32)
        second = jnp.zeros((n, hw), F32)
        for i in range(CMP_STRIDE):
            tok = f_ref[pl.ds(i, n, stride=CMP_STRIDE), :]
            lanes = slice(i * KV_WIDTH, (i + 1) * KV_WIDTH)
            first = first + _dot((tok + pos_ref[0:1, lanes]).astype(BF16), w1_ref[lanes, 0:hw])
            second = second + _dot((tok + pos_ref[1:2, lanes]).astype(BF16), w1_ref[lanes, hw:2 * hw])
        hid = first + pltpu.roll(second, n - 1, axis=0)
        o_ref[0] = _dot(_silu(hid).astype(BF16), w2_ref[...])

    one(kf_ref, pk_ref, w1k_ref, w2k_ref, kc_ref)
    one(vf_ref, pv_ref, w1v_ref, w2v_ref, vc_ref)


def _compress(proj_kv, pk, pv, w1k, w1v, w2k, w2v, b, s):
    n = s // CMP_STRIDE
    const = lambda shape: pl.BlockSpec(shape, lambda i: (0,) * len(shape))
    kv_col = lambda c: pl.BlockSpec((s, KV_WIDTH), lambda i: (i, c))
    out = pl.BlockSpec((1, n, LANES), lambda i: (i, 0, 0))
    return pl.pallas_call(
        _compress_kernel,
        grid=(b,),
        in_specs=[kv_col(0), kv_col(1), const(pk.shape), const(pv.shape), const(w1k.shape),
                  const(w1v.shape), const(w2k.shape), const(w2v.shape)],
        out_specs=[out, out],
        out_shape=[jax.ShapeDtypeStruct((b, n, LANES), F32)] * 2,
        compiler_params=pltpu.CompilerParams(
            dimension_semantics=("parallel",), vmem_limit_bytes=VMEM_LIMIT),
        name="nsa_compress",
    )(proj_kv, proj_kv, pk, pv, w1k, w1v, w2k, w2v)


def _attn_kernel(q_ref, ks_ref, vs_ref, kw_ref, vw_ref, gt_ref, kc_ref, vc_ref, bc_ref, tw_ref,
                 ts_ref, ovt_ref, bx_ref, o_ref,
                 ksv, kwv, kcv, vst, vwt, vct, rhs, acc, obuf, s_a, s_b, s_w, *, n_top):
    qi = pl.program_id(1)
    n_kt = kwv.shape[1]
    n_sel = ovt_ref.shape[0]
    variants = 2 * N_GROUPS

    @pl.when(qi == 0)
    def _prepare_kv():
        def halves(k):
            lo = lax.broadcasted_iota(jnp.int32, k.shape, 1) < HEAD_DIM
            kr = pltpu.roll(k, HEAD_DIM, axis=1)
            z = jnp.zeros_like(k)
            c = lambda a: a.astype(BF16)
            return ((c(jnp.where(lo, k, z)), c(jnp.where(lo, z, kr))),
                    (c(jnp.where(lo, kr, z)), c(jnp.where(lo, z, k))))

        def extra_lanes(shape, v):
            pair = lax.broadcasted_iota(jnp.int32, shape, 0)
            r = lax.broadcasted_iota(jnp.int32, shape, 1)
            lane = lax.broadcasted_iota(jnp.int32, shape, 2)
            one = ((lane >= n_sel + 2 * v) & (lane < n_sel + 2 * v + 2)
                   | (lane == pair * (2 * Q_TILE // SEL_BLOCK) + r // SEL_BLOCK))
            return jnp.where(one, 1.0, 0.0).astype(BF16)

        pair_shape = (n_kt // 2, 2 * Q_TILE, LANES)
        tile_shape = (n_kt, Q_TILE, LANES)
        k_sel, k_win, k_cmp = halves(ks_ref[...]), halves(kw_ref[...]), halves(kc_ref[0])
        for g in range(N_GROUPS):
            for v in range(2):
                rows = slice(v * 2 * Q_TILE, (v + 1) * 2 * Q_TILE)
                ksv[g, :, rows, 0:LANES] = k_sel[g][v].reshape(pair_shape)
                ksv[g, :, rows, LANES:2 * LANES] = extra_lanes(pair_shape, v)
                kwv[g, :, v * Q_TILE:(v + 1) * Q_TILE, :] = k_win[g][v].reshape(tile_shape)
                kcv[g, v * LANES:(v + 1) * LANES, :] = k_cmp[g][v]
        vst[:, :, HEAD_DIM:, :] = jnp.ones((N_GROUPS, n_kt // 2, ONES_ROWS, 2 * Q_TILE), BF16)
        vwt[:, :, HEAD_DIM:, :] = jnp.ones((N_GROUPS, n_kt, ONES_ROWS, Q_TILE), BF16)
        for kt in range(n_kt):
            rows = slice(kt * Q_TILE, (kt + 1) * Q_TILE)
            half = slice((kt % 2) * Q_TILE, (kt % 2 + 1) * Q_TILE)
            vs_t = vs_ref[rows, :].T.astype(BF16)
            vw_t = vw_ref[rows, :].T.astype(BF16)
            for g in range(N_GROUPS):
                vst[g, kt // 2, 0:HEAD_DIM, half] = vs_t[g * HEAD_DIM:(g + 1) * HEAD_DIM]
                vwt[g, kt, 0:HEAD_DIM, :] = vw_t[g * HEAD_DIM:(g + 1) * HEAD_DIM]
        vct[...] = vc_ref[0].T.astype(BF16)

    qt = q_ref[...]
    gsig_t = _sigmoid(gt_ref[...]).T

    def col_max(s):
        return jnp.max(s.reshape(s.shape[0] // 8, 8, COLS), axis=0)

    def group_rows(g):
        return slice(g * HEAD_DIM, (g + 1) * HEAD_DIM)

    def online(gv, s, s_max, vt, m):
        m_new = jnp.maximum(m, jnp.max(s_max, axis=0, keepdims=True))
        e = jnp.exp2(s - m_new)
        acc[gv] = acc[gv] * jnp.exp2(m - m_new) + _dot(vt, e.astype(BF16))
        return m_new

    def fresh_state():
        acc[...] = jnp.zeros(acc.shape, F32)
        return tuple(jnp.full((1, COLS), NEG, F32) for _ in range(variants))

    def finish(branch):
        for g in range(N_GROUPS):
            halves = []
            for v in range(2):
                gv = g * 2 + v
                l = acc[gv, HEAD_DIM:HEAD_DIM + 1, :]
                halves.append(acc[gv, 0:HEAD_DIM, :] * (1.0 / jnp.maximum(l, 1e-30)))
            obuf[branch, g] = jnp.concatenate(halves, axis=0)

    groups = range(N_GROUPS)
    for g in groups:
        q4 = jnp.concatenate(
            [qt[:, (g * PAIRS_PER_GROUP + p) * LANES:(g * PAIRS_PER_GROUP + p + 1) * LANES]
             for p in range(PAIRS_PER_GROUP)], axis=0)
        rhs[g, :, 0:LANES] = q4

    bc_rows = pl.ds(pl.multiple_of(LANES - qi * (Q_TILE // CMP_STRIDE), SUBLANES), LANES)
    s_cmp = [_dot_nt(kcv[g], rhs[g, :, 0:LANES])
             + jnp.concatenate([bc_ref[g, 0, bc_rows, :], bc_ref[g, 1, bc_rows, :]], axis=0)
             for g in groups]
    psum_t = []
    for g in groups:
        tot = jnp.zeros((LANES, Q_TILE), F32)
        halves = []
        for v in range(2):
            s = s_cmp[g][v * LANES:(v + 1) * LANES]
            m = jnp.max(s, axis=0, keepdims=True)
            e = jnp.exp2(s - jnp.where(m > 0.5 * NEG, m, 0.0))
            p = e * (1.0 / jnp.maximum(jnp.sum(e, axis=0, keepdims=True), 1e-30))
            for pp in range(PAIRS_PER_GROUP):
                tot = tot + p[:, pp * Q_TILE:(pp + 1) * Q_TILE]
            halves.append(_dot(vct[group_rows(g), :], p.astype(BF16)))
        obuf[0, g] = jnp.concatenate(halves, axis=0)
        psum_t.append(tot)

    ovt = ovt_ref[...]
    imp_t = []
    for g in groups:
        p_hi = psum_t[g].astype(BF16)
        r1 = psum_t[g] - p_hi.astype(F32)
        p_mid = r1.astype(BF16)
        p_lo = (r1 - p_mid.astype(F32)).astype(BF16)
        imp_t.append(_dot(ovt, p_hi) + _dot(ovt, p_mid) + _dot(ovt, p_lo))

    j_idx = lax.broadcasted_iota(jnp.int32, (n_sel, Q_TILE), 0)
    r_idx = lax.broadcasted_iota(jnp.int32, (n_sel, Q_TILE), 1)
    blk_t = qi * (Q_TILE // SEL_BLOCK) + r_idx // SEL_BLOCK
    valid_blk = j_idx <= blk_t
    forced = (j_idx == 0) | (j_idx == blk_t) | (j_idx == blk_t - 1)
    prio = [jnp.where(valid_blk, jnp.where(forced, FORCE_SCORE, imp_t[g]), -FORCE_SCORE) for g in groups]
    rank = [jnp.zeros((n_sel, Q_TILE), F32) for _ in groups]
    for jj in range(n_sel):
        later = j_idx > jj
        for g in groups:
            row = prio[g][jj:jj + 1, :]
            beats = (row > prio[g]) | ((row == prio[g]) & later)
            rank[g] = rank[g] + jnp.where(beats, 1.0, 0.0)
    for g in groups:
        drop_t = jnp.where((rank[g] < n_top) & valid_blk, 0.0, NEG)
        drop_t = jnp.concatenate([drop_t, jnp.zeros((LANES - n_sel, Q_TILE), F32)], axis=0)
        drop = drop_t.T.astype(BF16)
        rhs[g, :, LANES:2 * LANES] = jnp.concatenate([drop] * PAIRS_PER_GROUP, axis=0) + bx_ref[g]

    pair_rows = 4 * Q_TILE

    def half_rows(v):
        return slice(v * 2 * Q_TILE, (v + 1) * 2 * Q_TILE)

    def max_rows(v):
        return slice(pair_rows + v * SUBLANES, pair_rows + (v + 1) * SUBLANES)

    def logits_group(buf, g, i, tables=None):
        s_all = _dot_nt(ksv[g, i], rhs[g])
        for v in range(2):
            s = s_all[half_rows(v)]
            if tables is not None:
                s = jnp.concatenate([s[h * Q_TILE:(h + 1) * Q_TILE] + ts_ref[g, v, tables[h]]
                                     for h in range(2)], axis=0)
            buf[g, half_rows(v), :] = s
            buf[g, max_rows(v), :] = col_max(s)

    def logits_into(buf, i):
        for g in range(N_GROUPS):
            logits_group(buf, g, i)

    def consume(buf, i, state):
        ms = list(state)
        for g in range(N_GROUPS):
            for v in range(2):
                gv = g * 2 + v
                ms[gv] = online(gv, buf[g, half_rows(v), :], buf[g, max_rows(v), :], vst[g, i], ms[gv])
        return tuple(ms)

    n_all = (qi + 2) // 2
    n_far = jnp.maximum(qi - 1, 0) // 2

    first = jnp.maximum(qi - (WIN_TILES - 1), 0)
    tiles = []
    for t in range(WIN_TILES):
        dd = qi - (first + t)
        table = jnp.where(dd >= 0, dd, WIN_TILES)
        tiles.append((jnp.minimum(first + t, n_kt - 1), table))
    m_win = []
    one_matmul = n_kt >= WIN_TILES
    for g in range(N_GROUPS):
        mx = [jnp.full((8, COLS), NEG, F32) for _ in range(2)]
        if one_matmul:
            k_rows = kwv[g, pl.ds(first, WIN_TILES)].reshape(WIN_TILES * 2 * Q_TILE, LANES)
            s_five = _dot_nt(k_rows, rhs[g, :, 0:LANES])
        for t, (kt, table) in enumerate(tiles):
            if one_matmul:
                s_all = s_five[t * 2 * Q_TILE:(t + 1) * 2 * Q_TILE]
            else:
                s_all = _dot_nt(kwv[g, kt], rhs[g, :, 0:LANES])
            for v in range(2):
                s = s_all[v * Q_TILE:(v + 1) * Q_TILE] + tw_ref[g, v, table]
                s_w[g, t, v * Q_TILE:(v + 1) * Q_TILE, :] = s
                mx[v] = jnp.maximum(mx[v], col_max(s))
        m_win += [jnp.max(mx[v], axis=0, keepdims=True) for v in range(2)]

    for g in range(N_GROUPS):
        logits_group(s_a, g, 0)
        halves = []
        for v in range(2):
            o = jnp.zeros((HEAD_DIM + ONES_ROWS, COLS), F32)
            for t, (kt, _) in enumerate(tiles):
                e = jnp.exp2(s_w[g, t, v * Q_TILE:(v + 1) * Q_TILE, :] - m_win[g * 2 + v])
                o = o + _dot(vwt[g, kt], e.astype(BF16))
            halves.append(o[0:HEAD_DIM] * (1.0 / jnp.maximum(o[HEAD_DIM:HEAD_DIM + 1], 1e-30)))
        obuf[2, g] = jnp.concatenate(halves, axis=0)

    def far_two(j, state):
        logits_into(s_b, 2 * j + 1)
        state = consume(s_a, 2 * j, state)
        logits_into(s_a, 2 * j + 2)
        return consume(s_b, 2 * j + 1, state)

    state = lax.fori_loop(0, n_far // 2, far_two, fresh_state())
    state = lax.fori_loop(0, n_far % 2, lambda _, st: consume(s_a, n_far - 1, st), state)

    def near_pair(i, state):
        tables = [jnp.clip(qi - (2 * i + h), 0, 2) for h in range(2)]
        for g in range(N_GROUPS):
            logits_group(s_b, g, i, tables)
        return consume(s_b, i, state)

    diag_alone = 1 - qi % 2
    state = lax.fori_loop(n_far, n_all - diag_alone, near_pair, state)

    def diagonal_tile(_, state):
        ms = list(state)
        i = n_all - 1
        for g in range(N_GROUPS):
            lhs = jnp.concatenate([ksv[g, i, v * 2 * Q_TILE:v * 2 * Q_TILE + Q_TILE, :] for v in range(2)],
                                  axis=0)
            s_all = _dot_nt(lhs, rhs[g])
            for v in range(2):
                s = s_all[v * Q_TILE:(v + 1) * Q_TILE] + ts_ref[g, v, 0]
                s_b[g, v * Q_TILE:(v + 1) * Q_TILE, :] = s
                s_b[g, max_rows(v), :] = col_max(s)
        for g in range(N_GROUPS):
            for v in range(2):
                gv = g * 2 + v
                ms[gv] = online(gv, s_b[g, v * Q_TILE:(v + 1) * Q_TILE, :], s_b[g, max_rows(v), :],
                                vst[g, i, :, 0:Q_TILE], ms[gv])
        return tuple(ms)

    lax.fori_loop(0, diag_alone, diagonal_tile, state)
    finish(1)

    for g in range(N_GROUPS):
        for p in range(PAIRS_PER_GROUP):
            halves = []
            for v in range(2):
                tot = jnp.zeros((HEAD_DIM, Q_TILE), F32)
                for br in range(3):
                    c = br * N_HEADS + g * HEADS_PER_GROUP + 2 * p + v
                    tot = tot + gsig_t[c:c + 1, :] * obuf[br, g, v * HEAD_DIM:(v + 1) * HEAD_DIM,
                                                          p * Q_TILE:(p + 1) * Q_TILE]
                halves.append(tot)
            col = (g * PAIRS_PER_GROUP + p) * LANES
            o_ref[:, col:col + LANES] = jnp.concatenate(halves, axis=0).T.astype(BF16)


def _attention(proj_main, proj_kv, kcmp, vcmp, bias_c, bias_w, bias_s, ovt, far_lanes, b, s, n_top):
    n_q = s // Q_TILE
    once = pl.Buffered(1)
    kv_col = lambda c: pl.BlockSpec((s, LANES), lambda bi, qi: (bi, c))
    cmp_spec = pl.BlockSpec((1, LANES, LANES), lambda bi, qi: (bi, 0, 0))
    return pl.pallas_call(
        functools.partial(_attn_kernel, n_top=n_top),
        grid=(b, n_q),
        in_specs=[
            pl.BlockSpec((Q_TILE, NSA_WIDTH), lambda bi, qi: (bi * n_q + qi, 0)),
            kv_col(2), kv_col(3), kv_col(4), kv_col(5),
            pl.BlockSpec((Q_TILE, LANES), lambda bi, qi: (bi * n_q + qi, 6)),
            cmp_spec, cmp_spec,
            pl.BlockSpec(bias_c.shape, lambda bi, qi: (0,) * 4, pipeline_mode=once),
            pl.BlockSpec(bias_w.shape, lambda bi, qi: (0,) * 5, pipeline_mode=once),
            pl.BlockSpec(bias_s.shape, lambda bi, qi: (0,) * 5, pipeline_mode=once),
            pl.BlockSpec(ovt.shape, lambda bi, qi: (0, 0), pipeline_mode=once),
            pl.BlockSpec(far_lanes.shape, lambda bi, qi: (0, 0, 0), pipeline_mode=once),
        ],
        out_specs=pl.BlockSpec((Q_TILE, NSA_WIDTH), lambda bi, qi: (bi * n_q + qi, 0)),
        out_shape=jax.ShapeDtypeStruct((b * s, NSA_WIDTH), BF16),
        scratch_shapes=[
            pltpu.VMEM((N_GROUPS, n_q // 2, 4 * Q_TILE, 2 * LANES), BF16),
            pltpu.VMEM((N_GROUPS, n_q, 2 * Q_TILE, LANES), BF16),
            pltpu.VMEM((N_GROUPS, 2 * LANES, LANES), BF16),
            pltpu.VMEM((N_GROUPS, n_q // 2, HEAD_DIM + ONES_ROWS, 2 * Q_TILE), BF16),
            pltpu.VMEM((N_GROUPS, n_q, HEAD_DIM + ONES_ROWS, Q_TILE), BF16),
            pltpu.VMEM((LANES, LANES), BF16),
            pltpu.VMEM((N_GROUPS, COLS, 2 * LANES), BF16),
            pltpu.VMEM((2 * N_GROUPS, HEAD_DIM + ONES_ROWS, COLS), F32),
            pltpu.VMEM((3, N_GROUPS, LANES, COLS), F32),
            pltpu.VMEM((N_GROUPS, 4 * Q_TILE + 2 * SUBLANES, COLS), F32),
            pltpu.VMEM((N_GROUPS, 4 * Q_TILE + 2 * SUBLANES, COLS), F32),
            pltpu.VMEM((N_GROUPS, WIN_TILES, 2 * Q_TILE, COLS), F32),
        ],
        compiler_params=pltpu.CompilerParams(
            dimension_semantics=("parallel", "arbitrary"), vmem_limit_bytes=VMEM_LIMIT),
        name="nsa_attention",
    )(proj_main, proj_kv, proj_kv, proj_kv, proj_kv, proj_kv, kcmp, vcmp, bias_c, bias_w, bias_s,
      ovt, far_lanes)


def _merge_kernel(zn_ref, a_ref, b_ref, zc_ref, gc_ref, gn_ref, ah_ref, bh_ref, on_ref, x_ref,
                  cw_ref, cb_ref, lg_ref, lb_ref, wcp_ref, wnp_ref, wo_ref, gf_ref, out_ref, uext,
                  conv, shifted):
    i = pl.program_id(1)
    ts = a_ref.shape[0]
    f = lambda r: r[...].astype(F32)

    n_cblk = D_MODEL // LANES
    u_halo = jnp.where(i > 0, f(ah_ref) * _sigmoid(f(bh_ref)), 0.0)
    u = f(a_ref) * _sigmoid(f(b_ref))
    for cblk in range(n_cblk):
        cols = slice(cblk * LANES, (cblk + 1) * LANES)
        uext[cblk, 0:CONV_HALO, :] = u_halo[:, cols]
        uext[cblk, CONV_HALO:CONV_HALO + ts, :] = u[:, cols]
        uext[cblk, CONV_HALO + ts:, :] = jnp.zeros((SUBLANES, LANES), F32)

    lead = CONV_HALO - (CONV_KERNEL - 1)
    chunk = Q_TILE

    def conv_block(cblk, carry):
        for shift in range(SUBLANES):
            shifted[shift] = uext[cblk, shift:shift + ts + CONV_HALO, :]
        w = cw_ref[cblk]
        for h in range(ts // chunk):
            c = jnp.broadcast_to(cb_ref[cblk], (chunk, LANES))
            for shift in range(SUBLANES):
                xs = shifted[shift, h * chunk:h * chunk + chunk + CONV_HALO, :]
                for j in range(CONV_KERNEL):
                    if (lead + j) % SUBLANES == shift:
                        base = lead + j - shift
                        c = c + w[j:j + 1, :] * xs[base:base + chunk]
            conv[cblk, h * chunk:(h + 1) * chunk, :] = c
        return carry

    lax.fori_loop(0, n_cblk, conv_block, 0)
    c = jnp.concatenate([conv[cblk] for cblk in range(n_cblk)], axis=1)

    mu = jnp.mean(c, axis=-1, keepdims=True)
    cc = c - mu
    var = jnp.mean(cc * cc, axis=-1, keepdims=True)
    y = (cc * lax.rsqrt(var + EPS)) * lg_ref[...] + lb_ref[...]
    conv_act = _silu(y) * _silu(f(zc_ref))
    y_conv = _dot(conv_act.astype(BF16), wcp_ref[...])

    nsa_act = f(on_ref) * _silu(f(zn_ref))
    y_nsa = _dot(nsa_act.astype(BF16), wnp_ref[...])

    merged = _sigmoid(f(gc_ref)) * y_conv + _sigmoid(f(gn_ref)) * y_nsa
    xo = x_ref[...] + _dot(merged.astype(BF16), wo_ref[...])
    ms = jnp.mean(xo * xo, axis=-1, keepdims=True)
    out_ref[...] = (xo * lax.rsqrt(ms + EPS)) * gf_ref[...]


def _merge(proj_main, o_nsa, x2, cw, cb, lg, lb, wcp, wnp, wo, gf, b, s, ts):
    n_t = s // ts
    n_cblk = D_MODEL // LANES
    halo_per_tile = ts // CONV_HALO
    cw = jnp.transpose(cw.reshape(cw.shape[0], n_cblk, LANES), (1, 0, 2))
    cb = cb.reshape(n_cblk, 1, LANES)
    col = lambda c: pl.BlockSpec((ts, COL_TILE), lambda bi, ti: (bi * n_t + ti, c))
    halo = lambda c: pl.BlockSpec(
        (CONV_HALO, COL_TILE),
        lambda bi, ti: (jnp.maximum((bi * n_t + ti) * halo_per_tile - 1, 0), c))
    const = lambda a: pl.BlockSpec(a.shape, lambda bi, ti: (0,) * a.ndim, pipeline_mode=pl.Buffered(1))
    rowblk = pl.BlockSpec((ts, D_MODEL), lambda bi, ti: (bi * n_t + ti, 0))
    return pl.pallas_call(
        _merge_kernel,
        grid=(b, n_t),
        in_specs=[col(2), col(3), col(4), col(5), col(6), col(7), halo(3), halo(4), rowblk, rowblk,
                  const(cw), const(cb), const(lg), const(lb), const(wcp), const(wnp), const(wo),
                  const(gf)],
        out_specs=rowblk,
        out_shape=jax.ShapeDtypeStruct((b * s, D_MODEL), F32),
        scratch_shapes=[pltpu.VMEM((n_cblk, ts + CONV_HALO + SUBLANES, LANES), F32),
                        pltpu.VMEM((n_cblk, ts, LANES), F32),
                        pltpu.VMEM((SUBLANES, ts + CONV_HALO, LANES), F32)],
        compiler_params=pltpu.CompilerParams(
            dimension_semantics=("parallel", "arbitrary"), vmem_limit_bytes=VMEM_LIMIT),
        name="conv_merge",
    )(proj_main, proj_main, proj_main, proj_main, proj_main, proj_main, proj_main, proj_main,
      o_nsa, x2, cw, cb, lg, lb, wcp, wnp, wo, gf)


def _t5_bucket_np(rel):
    rel = np.maximum(rel, 0)
    max_exact = REL_BUCKETS // 2
    relf = np.maximum(rel, 1).astype(np.float32)
    large = max_exact + (np.log(relf / np.float32(max_exact))
                         / np.float32(np.log(REL_MAX_DIST / max_exact))
                         * np.float32(REL_BUCKETS - max_exact)).astype(np.int32)
    large = np.minimum(large, REL_BUCKETS - 1)
    return np.where(rel < max_exact, rel, large)


def _pair_head_index():
    g = np.arange(N_GROUPS)[:, None, None]
    v = np.arange(2)[None, :, None]
    p = np.arange(PAIRS_PER_GROUP)[None, None, :]
    return g * HEADS_PER_GROUP + 2 * p + v


def _bias_lookup(rel_bias, rel):
    bucket = _t5_bucket_np(rel).reshape(-1)
    onehot = (jnp.arange(REL_BUCKETS)[:, None] == jnp.asarray(bucket)[None, :]).astype(F32)
    vals = jnp.dot(rel_bias.astype(F32).T * LOG2E, onehot, precision=lax.Precision.HIGHEST)
    vals = vals.reshape((N_HEADS,) + rel.shape)
    head = _pair_head_index()
    return jnp.stack([jnp.stack([jnp.concatenate([vals[h] for h in head[g, v]], axis=-1)
                                 for v in range(2)]) for g in range(N_GROUPS)])


def _bias_tables(rel_bias, s, n_sel):
    c = np.arange(LANES)[:, None]
    r = np.arange(Q_TILE)[None, :]
    far = rel_bias.astype(F32)[REL_BUCKETS - 1][_pair_head_index()] * LOG2E
    assert s // CMP_STRIDE <= LANES + Q_TILE // CMP_STRIDE
    rel_w = np.stack([dd * Q_TILE + r - c for dd in range(WIN_TILES)])
    assert (_t5_bucket_np(rel_w[2:]) == REL_BUCKETS - 1).all()
    rho = np.arange(2 * LANES)[:, None]
    rel_c = r - ((rho - LANES) * CMP_STRIDE + CMP_BLOCK - 1)
    vals = _bias_lookup(rel_bias, np.concatenate([rel_w, rel_c.reshape(2, LANES, Q_TILE)]))
    bias_w, bias_c = vals[:, :, :WIN_TILES], vals[:, :, WIN_TILES:].reshape(N_GROUPS, 2, 2 * LANES, COLS)
    cols = lambda a: np.tile(a, (1,) * (a.ndim - 1) + (PAIRS_PER_GROUP,))

    tw = jnp.where(cols((rel_w >= 0) & (rel_w < WINDOW)), bias_w, NEG)
    tw = jnp.concatenate([tw, jnp.full_like(tw[:, :, :1], NEG)], axis=2)
    far_cols = jnp.repeat(far, Q_TILE, axis=-1)[:, :, None, None, :]
    ts = jnp.where(cols(rel_w[:3] >= 0), bias_w[:, :, :3] - far_cols, NEG)
    bc = jnp.where(cols(rel_c >= 0), bias_c, NEG)
    hi = far.astype(BF16)
    lo = (far - hi.astype(F32)).astype(BF16)
    pieces = jnp.stack([hi[:, 0], lo[:, 0], hi[:, 1], lo[:, 1]], axis=-1)
    pieces = jnp.broadcast_to(pieces[:, :, None, :], (N_GROUPS, PAIRS_PER_GROUP, Q_TILE, 4))
    far_lanes = jnp.pad(pieces.reshape(N_GROUPS, COLS, 4),
                        ((0, 0), (0, 0), (n_sel, LANES - n_sel - 4)))
    return tw, ts, bc, far_lanes


def _overlap_t(s):
    n_cmp = (s - CMP_BLOCK) // CMP_STRIDE + 1
    cs = np.arange(LANES) * CMP_STRIDE
    ss = np.arange(s // SEL_BLOCK) * SEL_BLOCK
    ovt = ((cs[None, :] <= ss[:, None] + SEL_BLOCK - 1) & (cs[None, :] + CMP_BLOCK - 1 >= ss[:, None])
           & (np.arange(LANES)[None, :] < n_cmp))
    return jnp.asarray(ovt, BF16)


def _pad_w_in(w):
    used = NSA_WIDTH + 6 * KV_WIDTH + 3 * N_HEADS
    gap = (KV_COL_TILE + 1) * COL_TILE - used
    col = jnp.arange(w.shape[1])[None, :]
    w = jnp.where(col < NSA_WIDTH, w * (HEAD_DIM ** -0.5 * LOG2E), w)
    left = jnp.pad(w[:, :used], ((0, 0), (0, w.shape[1] - used + gap)))
    right = jnp.pad(w[:, used:], ((0, 0), (used + gap, 0)))
    return (left + right).astype(BF16)


def _compress_weights(pos, w1, w2):
    half = CMP_BLOCK // 2
    eye = jnp.eye(N_GROUPS, dtype=F32)
    w1r = w1.reshape(CMP_BLOCK, HEAD_DIM, CMP_HIDDEN)
    blk = lambda part: jnp.einsum('idn,gh->igdhn', part, eye).reshape(
        half * KV_WIDTH, N_GROUPS * CMP_HIDDEN)
    w1d = jnp.concatenate([blk(w1r[:half]), blk(w1r[half:])], axis=1).astype(BF16)
    w2d = jnp.einsum('nd,gh->gnhd', w2, eye).reshape(N_GROUPS * CMP_HIDDEN, KV_WIDTH).astype(BF16)
    tilepos = lambda part: jnp.broadcast_to(part[:, None, :], (half, N_GROUPS, HEAD_DIM)).reshape(1, -1)
    posd = jnp.concatenate([tilepos(pos[:half]), tilepos(pos[half:])], axis=0).astype(F32)
    return posd, w1d, w2d


def kernel(x, norm_in_g, w_in, pos_ck, w_ck1, w_ck2, pos_cv, w_cv1, w_cv2, rel_bias, conv_w, conv_b,
           conv_ln_g, conv_ln_b, w_conv_proj, w_nsa_proj, w_out, norm_f_g):
    b, s, d = x.shape
    n_sel = s // SEL_BLOCK
    assert d == D_MODEL and w_in.shape[0] == 1, "single-layer block with D_MODEL=1024"
    assert s % (2 * Q_TILE) == 0 and s // CMP_STRIDE <= LANES and s >= WINDOW
    assert n_sel + 4 <= LANES
    m = b * s
    x2 = x.reshape(m, d)
    row = lambda a: a.reshape(1, -1).astype(F32)

    proj_kv, proj_main = _input_projection(x2, row(norm_in_g[0]), _pad_w_in(w_in[0]), min(1024, m))

    chunks = s // CMP_STRIDE
    pk, w1k, w2k = _compress_weights(pos_ck[0], w_ck1[0], w_ck2[0])
    pv, w1v, w2v = _compress_weights(pos_cv[0], w_cv1[0], w_cv2[0])
    kcmp, vcmp = _compress(proj_kv, pk, pv, w1k, w1v, w2k, w2v, b, s)
    if chunks < LANES:
        padrows = ((0, 0), (0, LANES - chunks), (0, 0))
        kcmp, vcmp = jnp.pad(kcmp, padrows), jnp.pad(vcmp, padrows)

    bias_w, bias_s, bias_c, far_lanes = _bias_tables(rel_bias, s, n_sel)
    o_nsa = _attention(proj_main, proj_kv, kcmp, vcmp, bias_c, bias_w, bias_s, _overlap_t(s),
                       far_lanes, b, s, min(N_SELECT, n_sel))

    cw = jnp.pad(conv_w[0].astype(F32), ((0, CONV_HALO - CONV_KERNEL), (0, 0)))
    out = _merge(proj_main, o_nsa, x2, cw, row(conv_b[0]), row(conv_ln_g[0]), row(conv_ln_b[0]),
                 w_conv_proj[0].astype(BF16), w_nsa_proj[0].astype(BF16), w_out[0].astype(BF16),
                 row(norm_f_g), b, s, MERGE_ROWS)
    return out.reshape(b, s, d)
```

```python
import functools
import math

import numpy as np
import jax
import jax.numpy as jnp
from jax import lax
from jax.experimental import pallas as pl
from jax.experimental.pallas import tpu as pltpu

F32 = jnp.float32
BF16 = jnp.bfloat16

D_MODEL = 1024
N_HEADS = 16
N_GROUPS = 2
HEADS_PER_GROUP = N_HEADS // N_GROUPS
PAIRS_PER_GROUP = HEADS_PER_GROUP // 2
HEAD_DIM = 64
NSA_WIDTH = N_HEADS * HEAD_DIM
KV_WIDTH = N_GROUPS * HEAD_DIM
CMP_BLOCK = 32
CMP_STRIDE = 16
CMP_HIDDEN = 256
SEL_BLOCK = 64
N_SELECT = 8
WINDOW = 512
Q_TILE = 128
CONV_KERNEL = 31
CONV_HALO = 32
REL_BUCKETS = 32
REL_MAX_DIST = 128
EPS = 1e-6
NEG = -1e30
FORCE_SCORE = 1e6
LOG2E = math.log2(math.e)
LANES = 128
SUBLANES = 8
ONES_ROWS = 16
COLS = PAIRS_PER_GROUP * Q_TILE
COL_TILE = 1024
N_COL_TILES = 8
KV_COL_TILE = 1
MERGE_ROWS = 512
WIN_TILES = WINDOW // Q_TILE + 1
VMEM_LIMIT = 56 * 1024 * 1024


def _dot(a, b):
    return jnp.dot(a, b, preferred_element_type=F32)


def _dot_nt(a, b):
    return lax.dot_general(a, b, (((1,), (1,)), ((), ())), preferred_element_type=F32)


def _sigmoid(x):
    return 1.0 / (1.0 + jnp.exp2(x * (-LOG2E)))


def _silu(x):
    return x * _sigmoid(x)


def _proj_kernel(x_ref, g_ref, w_ref, kv_ref, main_ref, h_ref):
    j = pl.program_id(1)

    @pl.when(j == 0)
    def _():
        x = x_ref[...]
        ms = jnp.mean(x * x, axis=-1, keepdims=True)
        h_ref[...] = ((x * lax.rsqrt(ms + EPS)) * g_ref[...]).astype(BF16)

    acc = _dot(h_ref[...], w_ref[...])
    main_ref[...] = acc.astype(BF16)

    @pl.when(j == KV_COL_TILE // 2)
    def _():
        kv_ref[...] = acc[:, (KV_COL_TILE % 2) * COL_TILE:(KV_COL_TILE % 2 + 1) * COL_TILE]


def _input_projection(x2, g, w_perm, tm):
    m = x2.shape[0]
    return pl.pallas_call(
        _proj_kernel,
        grid=(m // tm, N_COL_TILES // 2),
        in_specs=[
            pl.BlockSpec((tm, D_MODEL), lambda i, j: (i, 0)),
            pl.BlockSpec((1, D_MODEL), lambda i, j: (0, 0)),
            pl.BlockSpec((D_MODEL, 2 * COL_TILE), lambda i, j: (0, j)),
        ],
        out_specs=[
            pl.BlockSpec((tm, COL_TILE), lambda i, j: (i, 0)),
            pl.BlockSpec((tm, 2 * COL_TILE), lambda i, j: (i, j)),
        ],
        out_shape=[
            jax.ShapeDtypeStruct((m, COL_TILE), F32),
            jax.ShapeDtypeStruct((m, N_COL_TILES * COL_TILE), BF16),
        ],
        scratch_shapes=[pltpu.VMEM((tm, D_MODEL), BF16)],
        compiler_params=pltpu.CompilerParams(
            dimension_semantics=("parallel", "arbitrary"), vmem_limit_bytes=VMEM_LIMIT),
        name="input_projection",
    )(x2, g, w_perm)


def _compress_kernel(kf_ref, vf_ref, pk_ref, pv_ref, w1k_ref, w1v_ref, w2k_ref, w2v_ref,
                     kc_ref, vc_ref):
    def one(f_ref, pos_ref, w1_ref, w2_ref, o_ref):
        n = f_ref.shape[0] // CMP_STRIDE
        hw = N_GROUPS * CMP_HIDDEN
        first = jnp.zeros((n, hw), F32)
        second = jnp.zeros((n, hw), F32)
        for i in range(0, CMP_STRIDE, 2):
            tok = jnp.concatenate([f_ref[pl.ds(i + k, n, stride=CMP_STRIDE), :] for k in range(2)],
                                  axis=1)
            lanes = slice(i * KV_WIDTH, (i + 2) * KV_WIDTH)
            first = first + _dot((tok + pos_ref[0:1, lanes]).astype(BF16), w1_ref[lanes, 0:hw])
            second = second + _dot((tok + pos_ref[1:2, lanes]).astype(BF16), w1_ref[lanes, hw:2 * hw])
        hid = first + pltpu.roll(second, n - 1, axis=0)
        o_ref[0] = _dot(_silu(hid).astype(BF16), w2_ref[...])

    one(kf_ref, pk_ref, w1k_ref, w2k_ref, kc_ref)
    one(vf_ref, pv_ref, w1v_ref, w2v_ref, vc_ref)


def _compress(proj_kv, pk, pv, w1k, w1v, w2k, w2v, b, s):
    n = s // CMP_STRIDE
    const = lambda shape: pl.BlockSpec(shape, lambda i: (0,) * len(shape))
    kv_col = lambda c: pl.BlockSpec((s, KV_WIDTH), lambda i: (i, c))
    out = pl.BlockSpec((1, n, LANES), lambda i: (i, 0, 0))
    return pl.pallas_call(
        _compress_kernel,
        grid=(b,),
        in_specs=[kv_col(0), kv_col(1), const(pk.shape), const(pv.shape), const(w1k.shape),
                  const(w1v.shape), const(w2k.shape), const(w2v.shape)],
        out_specs=[out, out],
        out_shape=[jax.ShapeDtypeStruct((b, n, LANES), F32)] * 2,
        compiler_params=pltpu.CompilerParams(
            dimension_semantics=("parallel",), vmem_limit_bytes=VMEM_LIMIT),
        name="nsa_compress",
    )(proj_kv, proj_kv, pk, pv, w1k, w1v, w2k, w2v)


def _attn_kernel(q_ref, ks_ref, vs_ref, kw_ref, vw_ref, gt_ref, kc_ref, vc_ref, bc_ref, tw_ref,
                 ts_ref, ovt_ref, bx_ref, o_ref,
                 ksv, kwv, kcv, vst, vwt, vct, rhs, acc, obuf, s_a, s_b, s_w, *, n_top):
    qi = pl.program_id(1)
    n_kt = kwv.shape[1]
    n_sel = ovt_ref.shape[0]
    variants = 2 * N_GROUPS

    @pl.when(qi == 0)
    def _prepare_kv():
        def halves(k):
            lo = lax.broadcasted_iota(jnp.int32, k.shape, 1) < HEAD_DIM
            kr = pltpu.roll(k, HEAD_DIM, axis=1)
            z = jnp.zeros_like(k)
            c = lambda a: a.astype(BF16)
            return ((c(jnp.where(lo, k, z)), c(jnp.where(lo, z, kr))),
                    (c(jnp.where(lo, kr, z)), c(jnp.where(lo, z, k))))

        def extra_lanes(shape, v):
            pair = lax.broadcasted_iota(jnp.int32, shape, 0)
            r = lax.broadcasted_iota(jnp.int32, shape, 1)
            lane = lax.broadcasted_iota(jnp.int32, shape, 2)
            one = ((lane >= n_sel + 2 * v) & (lane < n_sel + 2 * v + 2)
                   | (lane == pair * (2 * Q_TILE // SEL_BLOCK) + r // SEL_BLOCK))
            return jnp.where(one, 1.0, 0.0).astype(BF16)

        pair_shape = (n_kt // 2, 2 * Q_TILE, LANES)
        tile_shape = (n_kt, Q_TILE, LANES)
        k_sel, k_win, k_cmp = halves(ks_ref[...]), halves(kw_ref[...]), halves(kc_ref[0])
        for g in range(N_GROUPS):
            for v in range(2):
                rows = slice(v * 2 * Q_TILE, (v + 1) * 2 * Q_TILE)
                ksv[g, :, rows, 0:LANES] = k_sel[g][v].reshape(pair_shape)
                ksv[g, :, rows, LANES:2 * LANES] = extra_lanes(pair_shape, v)
                kwv[g, :, v * Q_TILE:(v + 1) * Q_TILE, :] = k_win[g][v].reshape(tile_shape)
                kcv[g, v * LANES:(v + 1) * LANES, :] = k_cmp[g][v]
        vst[:, :, HEAD_DIM:, :] = jnp.ones((N_GROUPS, n_kt // 2, ONES_ROWS, 2 * Q_TILE), BF16)
        vwt[:, :, HEAD_DIM:, :] = jnp.ones((N_GROUPS, n_kt, ONES_ROWS, Q_TILE), BF16)
        for kt in range(n_kt):
            rows = slice(kt * Q_TILE, (kt + 1) * Q_TILE)
            half = slice((kt % 2) * Q_TILE, (kt % 2 + 1) * Q_TILE)
            vs_t = vs_ref[rows, :].T.astype(BF16)
            vw_t = vw_ref[rows, :].T.astype(BF16)
            for g in range(N_GROUPS):
                vst[g, kt // 2, 0:HEAD_DIM, half] = vs_t[g * HEAD_DIM:(g + 1) * HEAD_DIM]
                vwt[g, kt, 0:HEAD_DIM, :] = vw_t[g * HEAD_DIM:(g + 1) * HEAD_DIM]
        vct[...] = vc_ref[0].T.astype(BF16)

    qt = q_ref[...]
    gsig_t = _sigmoid(gt_ref[...]).T

    def col_max(s):
        return jnp.max(s.reshape(s.shape[0] // 8, 8, COLS), axis=0)

    def group_rows(g):
        return slice(g * HEAD_DIM, (g + 1) * HEAD_DIM)

    def online(gv, s, s_max, vt, m):
        m_new = jnp.maximum(m, jnp.max(s_max, axis=0, keepdims=True))
        e = jnp.exp2(s - m_new)
        acc[gv] = acc[gv] * jnp.exp2(m - m_new) + _dot(vt, e.astype(BF16))
        return m_new

    def fresh_state():
        acc[...] = jnp.zeros(acc.shape, F32)
        return tuple(jnp.full((1, COLS), NEG, F32) for _ in range(variants))

    def finish(branch):
        for g in range(N_GROUPS):
            halves = []
            for v in range(2):
                gv = g * 2 + v
                l = acc[gv, HEAD_DIM:HEAD_DIM + 1, :]
                halves.append(acc[gv, 0:HEAD_DIM, :] * (1.0 / jnp.maximum(l, 1e-30)))
            obuf[branch, g] = jnp.concatenate(halves, axis=0)

    groups = range(N_GROUPS)
    for g in groups:
        q4 = jnp.concatenate(
            [qt[:, (g * PAIRS_PER_GROUP + p) * LANES:(g * PAIRS_PER_GROUP + p + 1) * LANES]
             for p in range(PAIRS_PER_GROUP)], axis=0)
        rhs[g, :, 0:LANES] = q4

    bc_rows = pl.ds(pl.multiple_of(LANES - qi * (Q_TILE // CMP_STRIDE), SUBLANES), LANES)
    s_cmp = [_dot_nt(kcv[g], rhs[g, :, 0:LANES])
             + jnp.concatenate([bc_ref[g, 0, bc_rows, :], bc_ref[g, 1, bc_rows, :]], axis=0)
             for g in groups]
    psum_t = []
    for g in groups:
        tot = jnp.zeros((LANES, Q_TILE), F32)
        halves = []
        for v in range(2):
            s = s_cmp[g][v * LANES:(v + 1) * LANES]
            m = jnp.max(s, axis=0, keepdims=True)
            e = jnp.exp2(s - jnp.where(m > 0.5 * NEG, m, 0.0))
            p = e * (1.0 / jnp.maximum(jnp.sum(e, axis=0, keepdims=True), 1e-30))
            for pp in range(PAIRS_PER_GROUP):
                tot = tot + p[:, pp * Q_TILE:(pp + 1) * Q_TILE]
            halves.append(_dot(vct[group_rows(g), :], p.astype(BF16)))
        obuf[0, g] = jnp.concatenate(halves, axis=0)
        psum_t.append(tot)

    ovt = ovt_ref[...]
    imp_t = []
    for g in groups:
        p_hi = psum_t[g].astype(BF16)
        r1 = psum_t[g] - p_hi.astype(F32)
        p_mid = r1.astype(BF16)
        p_lo = (r1 - p_mid.astype(F32)).astype(BF16)
        imp_t.append(_dot(ovt, p_hi) + _dot(ovt, p_mid) + _dot(ovt, p_lo))

    j_idx = lax.broadcasted_iota(jnp.int32, (n_sel, Q_TILE), 0)
    r_idx = lax.broadcasted_iota(jnp.int32, (n_sel, Q_TILE), 1)
    blk_t = qi * (Q_TILE // SEL_BLOCK) + r_idx // SEL_BLOCK
    valid_blk = j_idx <= blk_t
    forced = (j_idx == 0) | (j_idx == blk_t) | (j_idx == blk_t - 1)
    prio = [jnp.where(valid_blk, jnp.where(forced, FORCE_SCORE, imp_t[g]), -FORCE_SCORE) for g in groups]
    rank = [jnp.zeros((n_sel, Q_TILE), F32) for _ in groups]
    for jj in range(n_sel):
        later = j_idx > jj
        for g in groups:
            row = prio[g][jj:jj + 1, :]
            beats = (row > prio[g]) | ((row == prio[g]) & later)
            rank[g] = rank[g] + jnp.where(beats, 1.0, 0.0)
    for g in groups:
        drop_t = jnp.where((rank[g] < n_top) & valid_blk, 0.0, NEG)
        drop_t = jnp.concatenate([drop_t, jnp.zeros((LANES - n_sel, Q_TILE), F32)], axis=0)
        drop = drop_t.T.astype(BF16)
        rhs[g, :, LANES:2 * LANES] = jnp.concatenate([drop] * PAIRS_PER_GROUP, axis=0) + bx_ref[g]

    pair_rows = 4 * Q_TILE

    def half_rows(v):
        return slice(v * 2 * Q_TILE, (v + 1) * 2 * Q_TILE)

    def max_rows(v):
        return slice(pair_rows + v * SUBLANES, pair_rows + (v + 1) * SUBLANES)

    def logits_group(buf, g, i, tables=None):
        s_all = _dot_nt(ksv[g, i], rhs[g])
        for v in range(2):
            s = s_all[half_rows(v)]
            if tables is not None:
                s = jnp.concatenate([s[h * Q_TILE:(h + 1) * Q_TILE] + ts_ref[g, v, tables[h]]
                                     for h in range(2)], axis=0)
            buf[g, half_rows(v), :] = s
            buf[g, max_rows(v), :] = col_max(s)

    def logits_into(buf, i):
        for g in range(N_GROUPS):
            logits_group(buf, g, i)

    def consume(buf, i, state):
        ms = list(state)
        for g in range(N_GROUPS):
            for v in range(2):
                gv = g * 2 + v
                ms[gv] = online(gv, buf[g, half_rows(v), :], buf[g, max_rows(v), :], vst[g, i], ms[gv])
        return tuple(ms)

    n_all = (qi + 2) // 2
    n_far = jnp.maximum(qi - 1, 0) // 2

    first = jnp.maximum(qi - (WIN_TILES - 1), 0)
    tiles = []
    for t in range(WIN_TILES):
        dd = qi - (first + t)
        table = jnp.where(dd >= 0, dd, WIN_TILES)
        tiles.append((jnp.minimum(first + t, n_kt - 1), table))
    m_win = []
    one_matmul = n_kt >= WIN_TILES
    for g in range(N_GROUPS):
        mx = [jnp.full((8, COLS), NEG, F32) for _ in range(2)]
        if one_matmul:
            k_rows = kwv[g, pl.ds(first, WIN_TILES)].reshape(WIN_TILES * 2 * Q_TILE, LANES)
            s_five = _dot_nt(k_rows, rhs[g, :, 0:LANES])
        for t, (kt, table) in enumerate(tiles):
            if one_matmul:
                s_all = s_five[t * 2 * Q_TILE:(t + 1) * 2 * Q_TILE]
            else:
                s_all = _dot_nt(kwv[g, kt], rhs[g, :, 0:LANES])
            for v in range(2):
                s = s_all[v * Q_TILE:(v + 1) * Q_TILE] + tw_ref[g, v, table]
                s_w[g, t, v * Q_TILE:(v + 1) * Q_TILE, :] = s
                mx[v] = jnp.maximum(mx[v], col_max(s))
        m_win += [jnp.max(mx[v], axis=0, keepdims=True) for v in range(2)]

    for g in range(N_GROUPS):
        logits_group(s_a, g, 0)
        halves = []
        for v in range(2):
            o = jnp.zeros((HEAD_DIM + ONES_ROWS, COLS), F32)
            for t, (kt, _) in enumerate(tiles):
                e = jnp.exp2(s_w[g, t, v * Q_TILE:(v + 1) * Q_TILE, :] - m_win[g * 2 + v])
                o = o + _dot(vwt[g, kt], e.astype(BF16))
            halves.append(o[0:HEAD_DIM] * (1.0 / jnp.maximum(o[HEAD_DIM:HEAD_DIM + 1], 1e-30)))
        obuf[2, g] = jnp.concatenate(halves, axis=0)

    def far_two(j, state):
        logits_into(s_b, 2 * j + 1)
        state = consume(s_a, 2 * j, state)
        logits_into(s_a, 2 * j + 2)
        return consume(s_b, 2 * j + 1, state)

    state = lax.fori_loop(0, n_far // 2, far_two, fresh_state())
    state = lax.fori_loop(0, n_far % 2, lambda _, st: consume(s_a, n_far - 1, st), state)

    def near_pair(i, state):
        tables = [jnp.clip(qi - (2 * i + h), 0, 2) for h in range(2)]
        for g in range(N_GROUPS):
            logits_group(s_b, g, i, tables)
        return consume(s_b, i, state)

    diag_alone = 1 - qi % 2
    state = lax.fori_loop(n_far, n_all - diag_alone, near_pair, state)

    def diagonal_tile(_, state):
        ms = list(state)
        i = n_all - 1
        for g in range(N_GROUPS):
            lhs = jnp.concatenate([ksv[g, i, v * 2 * Q_TILE:v * 2 * Q_TILE + Q_TILE, :] for v in range(2)],
                                  axis=0)
            s_all = _dot_nt(lhs, rhs[g])
            for v in range(2):
                s = s_all[v * Q_TILE:(v + 1) * Q_TILE] + ts_ref[g, v, 0]
                s_b[g, v * Q_TILE:(v + 1) * Q_TILE, :] = s
                s_b[g, max_rows(v), :] = col_max(s)
        for g in range(N_GROUPS):
            for v in range(2):
                gv = g * 2 + v
                ms[gv] = online(gv, s_b[g, v * Q_TILE:(v + 1) * Q_TILE, :], s_b[g, max_rows(v), :],
                                vst[g, i, :, 0:Q_TILE], ms[gv])
        return tuple(ms)

    lax.fori_loop(0, diag_alone, diagonal_tile, state)
    finish(1)

    for g in range(N_GROUPS):
        for p in range(PAIRS_PER_GROUP):
            halves = []
            for v in range(2):
                tot = jnp.zeros((HEAD_DIM, Q_TILE), F32)
                for br in range(3):
                    c = br * N_HEADS + g * HEADS_PER_GROUP + 2 * p + v
                    tot = tot + gsig_t[c:c + 1, :] * obuf[br, g, v * HEAD_DIM:(v + 1) * HEAD_DIM,
                                                          p * Q_TILE:(p + 1) * Q_TILE]
                halves.append(tot)
            col = (g * PAIRS_PER_GROUP + p) * LANES
            o_ref[:, col:col + LANES] = jnp.concatenate(halves, axis=0).T.astype(BF16)


def _attention(proj_main, proj_kv, kcmp, vcmp, bias_c, bias_w, bias_s, ovt, far_lanes, b, s, n_top):
    n_q = s // Q_TILE
    once = pl.Buffered(1)
    kv_col = lambda c: pl.BlockSpec((s, LANES), lambda bi, qi: (bi, c))
    cmp_spec = pl.BlockSpec((1, LANES, LANES), lambda bi, qi: (bi, 0, 0))
    return pl.pallas_call(
        functools.partial(_attn_kernel, n_top=n_top),
        grid=(b, n_q),
        in_specs=[
            pl.BlockSpec((Q_TILE, NSA_WIDTH), lambda bi, qi: (bi * n_q + qi, 0)),
            kv_col(2), kv_col(3), kv_col(4), kv_col(5),
            pl.BlockSpec((Q_TILE, LANES), lambda bi, qi: (bi * n_q + qi, 6)),
            cmp_spec, cmp_spec,
            pl.BlockSpec(bias_c.shape, lambda bi, qi: (0,) * 4, pipeline_mode=once),
            pl.BlockSpec(bias_w.shape, lambda bi, qi: (0,) * 5, pipeline_mode=once),
            pl.BlockSpec(bias_s.shape, lambda bi, qi: (0,) * 5, pipeline_mode=once),
            pl.BlockSpec(ovt.shape, lambda bi, qi: (0, 0), pipeline_mode=once),
            pl.BlockSpec(far_lanes.shape, lambda bi, qi: (0, 0, 0), pipeline_mode=once),
        ],
        out_specs=pl.BlockSpec((Q_TILE, NSA_WIDTH), lambda bi, qi: (bi * n_q + qi, 0)),
        out_shape=jax.ShapeDtypeStruct((b * s, NSA_WIDTH), BF16),
        scratch_shapes=[
            pltpu.VMEM((N_GROUPS, n_q // 2, 4 * Q_TILE, 2 * LANES), BF16),
            pltpu.VMEM((N_GROUPS, n_q, 2 * Q_TILE, LANES), BF16),
            pltpu.VMEM((N_GROUPS, 2 * LANES, LANES), BF16),
            pltpu.VMEM((N_GROUPS, n_q // 2, HEAD_DIM + ONES_ROWS, 2 * Q_TILE), BF16),
            pltpu.VMEM((N_GROUPS, n_q, HEAD_DIM + ONES_ROWS, Q_TILE), BF16),
            pltpu.VMEM((LANES, LANES), BF16),
            pltpu.VMEM((N_GROUPS, COLS, 2 * LANES), BF16),
            pltpu.VMEM((2 * N_GROUPS, HEAD_DIM + ONES_ROWS, COLS), F32),
            pltpu.VMEM((3, N_GROUPS, LANES, COLS), F32),
            pltpu.VMEM((N_GROUPS, 4 * Q_TILE + 2 * SUBLANES, COLS), F32),
            pltpu.VMEM((N_GROUPS, 4 * Q_TILE + 2 * SUBLANES, COLS), F32),
            pltpu.VMEM((N_GROUPS, WIN_TILES, 2 * Q_TILE, COLS), F32),
        ],
        compiler_params=pltpu.CompilerParams(
            dimension_semantics=("parallel", "arbitrary"), vmem_limit_bytes=VMEM_LIMIT),
        name="nsa_attention",
    )(proj_main, proj_kv, proj_kv, proj_kv, proj_kv, proj_kv, kcmp, vcmp, bias_c, bias_w, bias_s,
      ovt, far_lanes)


def _merge_kernel(zn_ref, a_ref, b_ref, zc_ref, gc_ref, gn_ref, ah_ref, bh_ref, on_ref, x_ref,
                  cw_ref, cb_ref, lg_ref, lb_ref, wcp_ref, wnp_ref, wo_ref, gf_ref, out_ref, uext,
                  conv, shifted):
    i = pl.program_id(1)
    ts = a_ref.shape[0]
    f = lambda r: r[...].astype(F32)

    n_cblk = D_MODEL // LANES
    u_halo = jnp.where(i > 0, f(ah_ref) * _sigmoid(f(bh_ref)), 0.0)
    u = f(a_ref) * _sigmoid(f(b_ref))
    for cblk in range(n_cblk):
        cols = slice(cblk * LANES, (cblk + 1) * LANES)
        uext[cblk, 0:CONV_HALO, :] = u_halo[:, cols]
        uext[cblk, CONV_HALO:CONV_HALO + ts, :] = u[:, cols]
        uext[cblk, CONV_HALO + ts:, :] = jnp.zeros((SUBLANES, LANES), F32)

    lead = CONV_HALO - (CONV_KERNEL - 1)
    chunk = Q_TILE

    def conv_block(cblk, carry):
        for shift in range(SUBLANES):
            shifted[shift] = uext[cblk, shift:shift + ts + CONV_HALO, :]
        w = cw_ref[cblk]
        for h in range(ts // chunk):
            c = jnp.broadcast_to(cb_ref[cblk], (chunk, LANES))
            for shift in range(SUBLANES):
                xs = shifted[shift, h * chunk:h * chunk + chunk + CONV_HALO, :]
                for j in range(CONV_KERNEL):
                    if (lead + j) % SUBLANES == shift:
                        base = lead + j - shift
                        c = c + w[j:j + 1, :] * xs[base:base + chunk]
            conv[cblk, h * chunk:(h + 1) * chunk, :] = c
        return carry

    lax.fori_loop(0, n_cblk, conv_block, 0)
    c = jnp.concatenate([conv[cblk] for cblk in range(n_cblk)], axis=1)

    mu = jnp.mean(c, axis=-1, keepdims=True)
    cc = c - mu
    var = jnp.mean(cc * cc, axis=-1, keepdims=True)
    y = (cc * lax.rsqrt(var + EPS)) * lg_ref[...] + lb_ref[...]
    conv_act = _silu(y) * _silu(f(zc_ref))
    y_conv = _dot(conv_act.astype(BF16), wcp_ref[...])

    nsa_act = f(on_ref) * _silu(f(zn_ref))
    y_nsa = _dot(nsa_act.astype(BF16), wnp_ref[...])

    merged = _sigmoid(f(gc_ref)) * y_conv + _sigmoid(f(gn_ref)) * y_nsa
    xo = x_ref[...] + _dot(merged.astype(BF16), wo_ref[...])
    ms = jnp.mean(xo * xo, axis=-1, keepdims=True)
    out_ref[...] = (xo * lax.rsqrt(ms + EPS)) * gf_ref[...]


def _merge(proj_main, o_nsa, x2, cw, cb, lg, lb, wcp, wnp, wo, gf, b, s, ts):
    n_t = s // ts
    n_cblk = D_MODEL // LANES
    halo_per_tile = ts // CONV_HALO
    cw = jnp.transpose(cw.reshape(cw.shape[0], n_cblk, LANES), (1, 0, 2))
    cb = cb.reshape(n_cblk, 1, LANES)
    col = lambda c: pl.BlockSpec((ts, COL_TILE), lambda bi, ti: (bi * n_t + ti, c))
    halo = lambda c: pl.BlockSpec(
        (CONV_HALO, COL_TILE),
        lambda bi, ti: (jnp.maximum((bi * n_t + ti) * halo_per_tile - 1, 0), c))
    const = lambda a: pl.BlockSpec(a.shape, lambda bi, ti: (0,) * a.ndim, pipeline_mode=pl.Buffered(1))
    rowblk = pl.BlockSpec((ts, D_MODEL), lambda bi, ti: (bi * n_t + ti, 0))
    return pl.pallas_call(
        _merge_kernel,
        grid=(b, n_t),
        in_specs=[col(2), col(3), col(4), col(5), col(6), col(7), halo(3), halo(4), rowblk, rowblk,
                  const(cw), const(cb), const(lg), const(lb), const(wcp), const(wnp), const(wo),
                  const(gf)],
        out_specs=rowblk,
        out_shape=jax.ShapeDtypeStruct((b * s, D_MODEL), F32),
        scratch_shapes=[pltpu.VMEM((n_cblk, ts + CONV_HALO + SUBLANES, LANES), F32),
                        pltpu.VMEM((n_cblk, ts, LANES), F32),
                        pltpu.VMEM((SUBLANES, ts + CONV_HALO, LANES), F32)],
        compiler_params=pltpu.CompilerParams(
            dimension_semantics=("parallel", "arbitrary"), vmem_limit_bytes=VMEM_LIMIT),
        name="conv_merge",
    )(proj_main, proj_main, proj_main, proj_main, proj_main, proj_main, proj_main, proj_main,
      o_nsa, x2, cw, cb, lg, lb, wcp, wnp, wo, gf)


def _t5_bucket_np(rel):
    rel = np.maximum(rel, 0)
    max_exact = REL_BUCKETS // 2
    relf = np.maximum(rel, 1).astype(np.float32)
    large = max_exact + (np.log(relf / np.float32(max_exact))
                         / np.float32(np.log(REL_MAX_DIST / max_exact))
                         * np.float32(REL_BUCKETS - max_exact)).astype(np.int32)
    large = np.minimum(large, REL_BUCKETS - 1)
    return np.where(rel < max_exact, rel, large)


def _pair_head_index():
    g = np.arange(N_GROUPS)[:, None, None]
    v = np.arange(2)[None, :, None]
    p = np.arange(PAIRS_PER_GROUP)[None, None, :]
    return g * HEADS_PER_GROUP + 2 * p + v


def _bias_lookup(rel_bias, rel):
    bucket = _t5_bucket_np(rel).reshape(-1)
    onehot = (jnp.arange(REL_BUCKETS)[:, None] == jnp.asarray(bucket)[None, :]).astype(F32)
    vals = jnp.dot(rel_bias.astype(F32).T * LOG2E, onehot, precision=lax.Precision.HIGHEST)
    vals = vals.reshape((N_HEADS,) + rel.shape)
    head = _pair_head_index()
    return jnp.stack([jnp.stack([jnp.concatenate([vals[h] for h in head[g, v]], axis=-1)
                                 for v in range(2)]) for g in range(N_GROUPS)])


def _bias_tables(rel_bias, s, n_sel):
    c = np.arange(LANES)[:, None]
    r = np.arange(Q_TILE)[None, :]
    far = rel_bias.astype(F32)[REL_BUCKETS - 1][_pair_head_index()] * LOG2E
    assert s // CMP_STRIDE <= LANES + Q_TILE // CMP_STRIDE
    rel_w = np.stack([dd * Q_TILE + r - c for dd in range(WIN_TILES)])
    assert (_t5_bucket_np(rel_w[2:]) == REL_BUCKETS - 1).all()
    rho = np.arange(2 * LANES)[:, None]
    rel_c = r - ((rho - LANES) * CMP_STRIDE + CMP_BLOCK - 1)
    vals = _bias_lookup(rel_bias, np.concatenate([rel_w, rel_c.reshape(2, LANES, Q_TILE)]))
    bias_w, bias_c = vals[:, :, :WIN_TILES], vals[:, :, WIN_TILES:].reshape(N_GROUPS, 2, 2 * LANES, COLS)
    cols = lambda a: np.tile(a, (1,) * (a.ndim - 1) + (PAIRS_PER_GROUP,))

    tw = jnp.where(cols((rel_w >= 0) & (rel_w < WINDOW)), bias_w, NEG)
    tw = jnp.concatenate([tw, jnp.full_like(tw[:, :, :1], NEG)], axis=2)
    far_cols = jnp.repeat(far, Q_TILE, axis=-1)[:, :, None, None, :]
    ts = jnp.where(cols(rel_w[:3] >= 0), bias_w[:, :, :3] - far_cols, NEG)
    bc = jnp.where(cols(rel_c >= 0), bias_c, NEG)
    hi = far.astype(BF16)
    lo = (far - hi.astype(F32)).astype(BF16)
    pieces = jnp.stack([hi[:, 0], lo[:, 0], hi[:, 1], lo[:, 1]], axis=-1)
    pieces = jnp.broadcast_to(pieces[:, :, None, :], (N_GROUPS, PAIRS_PER_GROUP, Q_TILE, 4))
    far_lanes = jnp.pad(pieces.reshape(N_GROUPS, COLS, 4),
                        ((0, 0), (0, 0), (n_sel, LANES - n_sel - 4)))
    return tw, ts, bc, far_lanes


def _overlap_t(s):
    n_cmp = (s - CMP_BLOCK) // CMP_STRIDE + 1
    cs = np.arange(LANES) * CMP_STRIDE
    ss = np.arange(s // SEL_BLOCK) * SEL_BLOCK
    ovt = ((cs[None, :] <= ss[:, None] + SEL_BLOCK - 1) & (cs[None, :] + CMP_BLOCK - 1 >= ss[:, None])
           & (np.arange(LANES)[None, :] < n_cmp))
    return jnp.asarray(ovt, BF16)


def _pad_w_in(w):
    used = NSA_WIDTH + 6 * KV_WIDTH + 3 * N_HEADS
    gap = (KV_COL_TILE + 1) * COL_TILE - used
    col = jnp.arange(w.shape[1])[None, :]
    w = jnp.where(col < NSA_WIDTH, w * (HEAD_DIM ** -0.5 * LOG2E), w)
    left = jnp.pad(w[:, :used], ((0, 0), (0, w.shape[1] - used + gap)))
    right = jnp.pad(w[:, used:], ((0, 0), (used + gap, 0)))
    return (left + right).astype(BF16)


def _compress_weights(pos, w1, w2):
    assert N_GROUPS == 2
    half = CMP_BLOCK // 2

    def per_group(w):
        zeros = [(0, 0)] * (w.ndim - 1)
        top = jnp.pad(w, zeros + [(0, w.shape[-1])])
        bottom = jnp.pad(w, zeros + [(w.shape[-1], 0)])
        return jnp.concatenate([top, bottom], axis=-2)

    w1g = per_group(w1.reshape(CMP_BLOCK, HEAD_DIM, CMP_HIDDEN))
    blk = lambda part: part.reshape(half * KV_WIDTH, N_GROUPS * CMP_HIDDEN)
    w1d = jnp.concatenate([blk(w1g[:half]), blk(w1g[half:])], axis=1).astype(BF16)
    w2d = per_group(w2).astype(BF16)
    tilepos = lambda part: jnp.broadcast_to(part[:, None, :], (half, N_GROUPS, HEAD_DIM)).reshape(1, -1)
    posd = jnp.concatenate([tilepos(pos[:half]), tilepos(pos[half:])], axis=0).astype(F32)
    return posd, w1d, w2d


def kernel(x, norm_in_g, w_in, pos_ck, w_ck1, w_ck2, pos_cv, w_cv1, w_cv2, rel_bias, conv_w, conv_b,
           conv_ln_g, conv_ln_b, w_conv_proj, w_nsa_proj, w_out, norm_f_g):
    b, s, d = x.shape
    n_sel = s // SEL_BLOCK
    assert d == D_MODEL and w_in.shape[0] == 1, "single-layer block with D_MODEL=1024"
    assert s % (2 * Q_TILE) == 0 and s // CMP_STRIDE <= LANES and s >= WINDOW
    assert n_sel + 4 <= LANES
    m = b * s
    x2 = x.reshape(m, d)
    row = lambda a: a.reshape(1, -1).astype(F32)

    proj_kv, proj_main = _input_projection(x2, row(norm_in_g[0]), _pad_w_in(w_in[0]), min(1024, m))

    chunks = s // CMP_STRIDE
    pk, w1k, w2k = _compress_weights(pos_ck[0], w_ck1[0], w_ck2[0])
    pv, w1v, w2v = _compress_weights(pos_cv[0], w_cv1[0], w_cv2[0])
    kcmp, vcmp = _compress(proj_kv, pk, pv, w1k, w1v, w2k, w2v, b, s)
    if chunks < LANES:
        padrows = ((0, 0), (0, LANES - chunks), (0, 0))
        kcmp, vcmp = jnp.pad(kcmp, padrows), jnp.pad(vcmp, padrows)

    bias_w, bias_s, bias_c, far_lanes = _bias_tables(rel_bias, s, n_sel)
    o_nsa = _attention(proj_main, proj_kv, kcmp, vcmp, bias_c, bias_w, bias_s, _overlap_t(s),
                       far_lanes, b, s, min(N_SELECT, n_sel))

    cw = jnp.pad(conv_w[0].astype(F32), ((0, CONV_HALO - CONV_KERNEL), (0, 0)))
    out = _merge(proj_main, o_nsa, x2, cw, row(conv_b[0]), row(conv_ln_g[0]), row(conv_ln_b[0]),
                 w_conv_proj[0].astype(BF16), w_nsa_proj[0].astype(BF16), w_out[0].astype(BF16),
                 row(norm_f_g), b, s, MERGE_ROWS)
    return out.reshape(b, s, d)
```

```python
import functools
import math

import numpy as np
import jax
import jax.numpy as jnp
from jax import lax
from jax.experimental import pallas as pl
from jax.experimental.pallas import tpu as pltpu

F32 = jnp.float32
BF16 = jnp.bfloat16

D_MODEL = 1024
N_HEADS = 16
N_GROUPS = 2
HEADS_PER_GROUP = N_HEADS // N_GROUPS
PAIRS_PER_GROUP = HEADS_PER_GROUP // 2
HEAD_DIM = 64
NSA_WIDTH = N_HEADS * HEAD_DIM
KV_WIDTH = N_GROUPS * HEAD_DIM
CMP_BLOCK = 32
CMP_STRIDE = 16
CMP_HIDDEN = 256
SEL_BLOCK = 64
N_SELECT = 8
WINDOW = 512
Q_TILE = 128
CONV_KERNEL = 31
CONV_HALO = 32
REL_BUCKETS = 32
REL_MAX_DIST = 128
EPS = 1e-6
NEG = -1e30
FORCE_SCORE = 1e6
LOG2E = math.log2(math.e)
LANES = 128
SUBLANES = 8
ONES_ROWS = 16
COLS = PAIRS_PER_GROUP * Q_TILE
COL_TILE = 1024
N_COL_TILES = 8
KV_COL_TILE = 1
MERGE_ROWS = 512
WIN_TILES = WINDOW // Q_TILE + 1
VMEM_LIMIT = 56 * 1024 * 1024


def _dot(a, b):
    return jnp.dot(a, b, preferred_element_type=F32)


def _dot_nt(a, b):
    return lax.dot_general(a, b, (((1,), (1,)), ((), ())), preferred_element_type=F32)


def _sigmoid(x):
    return 1.0 / (1.0 + jnp.exp2(x * (-LOG2E)))


def _silu(x):
    return x * _sigmoid(x)


def _proj_kernel(x_ref, g_ref, w_ref, kv_ref, main_ref, h_ref):
    j = pl.program_id(1)

    @pl.when(j == 0)
    def _():
        x = x_ref[...]
        ms = jnp.mean(x * x, axis=-1, keepdims=True)
        h_ref[...] = ((x * lax.rsqrt(ms + EPS)) * g_ref[...]).astype(BF16)

    acc = _dot(h_ref[...], w_ref[...])
    main_ref[...] = acc.astype(BF16)

    @pl.when(j == KV_COL_TILE // 2)
    def _():
        kv_ref[...] = acc[:, (KV_COL_TILE % 2) * COL_TILE:(KV_COL_TILE % 2 + 1) * COL_TILE]


def _input_projection(x2, g, w_perm, tm):
    m = x2.shape[0]
    return pl.pallas_call(
        _proj_kernel,
        grid=(m // tm, N_COL_TILES // 2),
        in_specs=[
            pl.BlockSpec((tm, D_MODEL), lambda i, j: (i, 0)),
            pl.BlockSpec((1, D_MODEL), lambda i, j: (0, 0)),
            pl.BlockSpec((D_MODEL, 2 * COL_TILE), lambda i, j: (0, j)),
        ],
        out_specs=[
            pl.BlockSpec((tm, COL_TILE), lambda i, j: (i, 0)),
            pl.BlockSpec((tm, 2 * COL_TILE), lambda i, j: (i, j)),
        ],
        out_shape=[
            jax.ShapeDtypeStruct((m, COL_TILE), F32),
            jax.ShapeDtypeStruct((m, N_COL_TILES * COL_TILE), BF16),
        ],
        scratch_shapes=[pltpu.VMEM((tm, D_MODEL), BF16)],
        compiler_params=pltpu.CompilerParams(
            dimension_semantics=("parallel", "arbitrary"), vmem_limit_bytes=VMEM_LIMIT),
        name="input_projection",
    )(x2, g, w_perm)


def _compress_kernel(kf_ref, vf_ref, pk_ref, pv_ref, w1k_ref, w1v_ref, w2k_ref, w2v_ref,
                     kc_ref, vc_ref):
    def one(f_ref, pos_ref, w1_ref, w2_ref, o_ref):
        n = f_ref.shape[0] // CMP_STRIDE
        hw = N_GROUPS * CMP_HIDDEN
        first = jnp.zeros((n, hw), F32)
        second = jnp.zeros((n, hw), F32)
        for i in range(0, CMP_STRIDE, 2):
            tok = jnp.concatenate([f_ref[pl.ds(i + k, n, stride=CMP_STRIDE), :] for k in range(2)],
                                  axis=1)
            lanes = slice(i * KV_WIDTH, (i + 2) * KV_WIDTH)
            first = first + _dot((tok + pos_ref[0:1, lanes]).astype(BF16), w1_ref[lanes, 0:hw])
            second = second + _dot((tok + pos_ref[1:2, lanes]).astype(BF16), w1_ref[lanes, hw:2 * hw])
        hid = first + pltpu.roll(second, n - 1, axis=0)
        o_ref[0] = _dot(_silu(hid).astype(BF16), w2_ref[...])

    one(kf_ref, pk_ref, w1k_ref, w2k_ref, kc_ref)
    one(vf_ref, pv_ref, w1v_ref, w2v_ref, vc_ref)


def _compress(proj_kv, pk, pv, w1k, w1v, w2k, w2v, b, s):
    n = s // CMP_STRIDE
    const = lambda shape: pl.BlockSpec(shape, lambda i: (0,) * len(shape))
    kv_col = lambda c: pl.BlockSpec((s, KV_WIDTH), lambda i: (i, c))
    out = pl.BlockSpec((1, n, LANES), lambda i: (i, 0, 0))
    return pl.pallas_call(
        _compress_kernel,
        grid=(b,),
        in_specs=[kv_col(0), kv_col(1), const(pk.shape), const(pv.shape), const(w1k.shape),
                  const(w1v.shape), const(w2k.shape), const(w2v.shape)],
        out_specs=[out, out],
        out_shape=[jax.ShapeDtypeStruct((b, n, LANES), F32)] * 2,
        compiler_params=pltpu.CompilerParams(
            dimension_semantics=("parallel",), vmem_limit_bytes=VMEM_LIMIT),
        name="nsa_compress",
    )(proj_kv, proj_kv, pk, pv, w1k, w1v, w2k, w2v)


def _attn_kernel(q_ref, ks_ref, vs_ref, kw_ref, vw_ref, gt_ref, kc_ref, vc_ref, bc_ref, tw_ref,
                 ts_ref, ovt_ref, bx_ref, o_ref,
                 ksv, kwv, kcv, vst, vwt, vct, rhs, acc, obuf, s_a, s_b, s_w, *, n_top):
    qi = pl.program_id(1)
    n_kt = kwv.shape[1]
    n_sel = ovt_ref.shape[0]
    variants = 2 * N_GROUPS

    @pl.when(qi == 0)
    def _prepare_kv():
        def halves(k):
            lo = lax.broadcasted_iota(jnp.int32, k.shape, 1) < HEAD_DIM
            kr = pltpu.roll(k, HEAD_DIM, axis=1)
            z = jnp.zeros_like(k)
            c = lambda a: a.astype(BF16)
            return ((c(jnp.where(lo, k, z)), c(jnp.where(lo, z, kr))),
                    (c(jnp.where(lo, kr, z)), c(jnp.where(lo, z, k))))

        def extra_lanes(shape, v):
            pair = lax.broadcasted_iota(jnp.int32, shape, 0)
            r = lax.broadcasted_iota(jnp.int32, shape, 1)
            lane = lax.broadcasted_iota(jnp.int32, shape, 2)
            one = ((lane >= n_sel + 2 * v) & (lane < n_sel + 2 * v + 2)
                   | (lane == pair * (2 * Q_TILE // SEL_BLOCK) + r // SEL_BLOCK))
            return jnp.where(one, 1.0, 0.0).astype(BF16)

        pair_shape = (n_kt // 2, 2 * Q_TILE, LANES)
        tile_shape = (n_kt, Q_TILE, LANES)
        k_sel, k_win, k_cmp = halves(ks_ref[...]), halves(kw_ref[...]), halves(kc_ref[0])
        for g in range(N_GROUPS):
            for v in range(2):
                rows = slice(v * 2 * Q_TILE, (v + 1) * 2 * Q_TILE)
                ksv[g, :, rows, 0:LANES] = k_sel[g][v].reshape(pair_shape)
                ksv[g, :, rows, LANES:2 * LANES] = extra_lanes(pair_shape, v)
                kwv[g, :, v * Q_TILE:(v + 1) * Q_TILE, :] = k_win[g][v].reshape(tile_shape)
                kcv[g, v * LANES:(v + 1) * LANES, :] = k_cmp[g][v]
        vst[:, :, HEAD_DIM:, :] = jnp.ones((N_GROUPS, n_kt // 2, ONES_ROWS, 2 * Q_TILE), BF16)
        vwt[:, :, HEAD_DIM:, :] = jnp.ones((N_GROUPS, n_kt, ONES_ROWS, Q_TILE), BF16)
        for kt in range(n_kt):
            rows = slice(kt * Q_TILE, (kt + 1) * Q_TILE)
            half = slice((kt % 2) * Q_TILE, (kt % 2 + 1) * Q_TILE)
            vs_t = vs_ref[rows, :].T.astype(BF16)
            vw_t = vw_ref[rows, :].T.astype(BF16)
            for g in range(N_GROUPS):
                vst[g, kt // 2, 0:HEAD_DIM, half] = vs_t[g * HEAD_DIM:(g + 1) * HEAD_DIM]
                vwt[g, kt, 0:HEAD_DIM, :] = vw_t[g * HEAD_DIM:(g + 1) * HEAD_DIM]
        vct[...] = vc_ref[0].T.astype(BF16)

    qt = q_ref[...]
    gsig_t = _sigmoid(gt_ref[...]).T

    def col_max(s):
        return jnp.max(s.reshape(s.shape[0] // 8, 8, COLS), axis=0)

    def group_rows(g):
        return slice(g * HEAD_DIM, (g + 1) * HEAD_DIM)

    def online(gv, s, s_max, vt, m):
        m_new = jnp.maximum(m, jnp.max(s_max, axis=0, keepdims=True))
        e = jnp.exp2(s - m_new)
        acc[gv] = acc[gv] * jnp.exp2(m - m_new) + _dot(vt, e.astype(BF16))
        return m_new

    def fresh_state():
        acc[...] = jnp.zeros(acc.shape, F32)
        return tuple(jnp.full((1, COLS), NEG, F32) for _ in range(variants))

    def finish(branch):
        for g in range(N_GROUPS):
            halves = []
            for v in range(2):
                gv = g * 2 + v
                l = acc[gv, HEAD_DIM:HEAD_DIM + 1, :]
                halves.append(acc[gv, 0:HEAD_DIM, :] * (1.0 / jnp.maximum(l, 1e-30)))
            obuf[branch, g] = jnp.concatenate(halves, axis=0)

    groups = range(N_GROUPS)
    for g in groups:
        q4 = jnp.concatenate(
            [qt[:, (g * PAIRS_PER_GROUP + p) * LANES:(g * PAIRS_PER_GROUP + p + 1) * LANES]
             for p in range(PAIRS_PER_GROUP)], axis=0)
        rhs[g, :, 0:LANES] = q4

    bc_rows = pl.ds(pl.multiple_of(LANES - qi * (Q_TILE // CMP_STRIDE), SUBLANES), LANES)
    s_cmp = [_dot_nt(kcv[g], rhs[g, :, 0:LANES])
             + jnp.concatenate([bc_ref[g, 0, bc_rows, :], bc_ref[g, 1, bc_rows, :]], axis=0)
             for g in groups]
    psum_t = []
    for g in groups:
        tot = jnp.zeros((LANES, Q_TILE), F32)
        halves = []
        for v in range(2):
            s = s_cmp[g][v * LANES:(v + 1) * LANES]
            m = jnp.max(s, axis=0, keepdims=True)
            e = jnp.exp2(s - jnp.where(m > 0.5 * NEG, m, 0.0))
            p = e * (1.0 / jnp.maximum(jnp.sum(e, axis=0, keepdims=True), 1e-30))
            for pp in range(PAIRS_PER_GROUP):
                tot = tot + p[:, pp * Q_TILE:(pp + 1) * Q_TILE]
            halves.append(_dot(vct[group_rows(g), :], p.astype(BF16)))
        obuf[0, g] = jnp.concatenate(halves, axis=0)
        psum_t.append(tot)

    ovt = ovt_ref[...]
    imp_t = []
    for g in groups:
        p_hi = psum_t[g].astype(BF16)
        r1 = psum_t[g] - p_hi.astype(F32)
        p_mid = r1.astype(BF16)
        p_lo = (r1 - p_mid.astype(F32)).astype(BF16)
        imp_t.append(_dot(ovt, p_hi) + _dot(ovt, p_mid) + _dot(ovt, p_lo))

    j_idx = lax.broadcasted_iota(jnp.int32, (n_sel, Q_TILE), 0)
    r_idx = lax.broadcasted_iota(jnp.int32, (n_sel, Q_TILE), 1)
    blk_t = qi * (Q_TILE // SEL_BLOCK) + r_idx // SEL_BLOCK
    valid_blk = j_idx <= blk_t
    forced = (j_idx == 0) | (j_idx == blk_t) | (j_idx == blk_t - 1)
    prio = [jnp.where(valid_blk, jnp.where(forced, FORCE_SCORE, imp_t[g]), -FORCE_SCORE) for g in groups]
    rank = [jnp.zeros((n_sel, Q_TILE), F32) for _ in groups]
    for jj in range(n_sel):
        later = j_idx > jj
        for g in groups:
            row = prio[g][jj:jj + 1, :]
            beats = (row > prio[g]) | ((row == prio[g]) & later)
            rank[g] = rank[g] + jnp.where(beats, 1.0, 0.0)
    for g in groups:
        drop_t = jnp.where((rank[g] < n_top) & valid_blk, 0.0, NEG)
        drop_t = jnp.concatenate([drop_t, jnp.zeros((LANES - n_sel, Q_TILE), F32)], axis=0)
        drop = drop_t.T.astype(BF16)
        rhs[g, :, LANES:2 * LANES] = jnp.concatenate([drop] * PAIRS_PER_GROUP, axis=0) + bx_ref[g]

    pair_rows = 4 * Q_TILE

    def half_rows(v):
        return slice(v * 2 * Q_TILE, (v + 1) * 2 * Q_TILE)

    def max_rows(v):
        return slice(pair_rows + v * SUBLANES, pair_rows + (v + 1) * SUBLANES)

    def logits_group(buf, g, i, tables=None):
        s_all = _dot_nt(ksv[g, i], rhs[g])
        for v in range(2):
            s = s_all[half_rows(v)]
            if tables is not None:
                s = jnp.concatenate([s[h * Q_TILE:(h + 1) * Q_TILE] + ts_ref[g, v, tables[h]]
                                     for h in range(2)], axis=0)
            buf[g, half_rows(v), :] = s
            buf[g, max_rows(v), :] = col_max(s)

    def logits_into(buf, i):
        for g in range(N_GROUPS):
            logits_group(buf, g, i)

    def consume(buf, i, state):
        ms = list(state)
        for g in range(N_GROUPS):
            for v in range(2):
                gv = g * 2 + v
                ms[gv] = online(gv, buf[g, half_rows(v), :], buf[g, max_rows(v), :], vst[g, i], ms[gv])
        return tuple(ms)

    n_all = (qi + 2) // 2
    n_far = jnp.maximum(qi - 1, 0) // 2

    first = jnp.maximum(qi - (WIN_TILES - 1), 0)
    tiles = []
    for t in range(WIN_TILES):
        dd = qi - (first + t)
        table = jnp.where(dd >= 0, dd, WIN_TILES)
        tiles.append((jnp.minimum(first + t, n_kt - 1), table))
    m_win = []
    one_matmul = n_kt >= WIN_TILES
    for g in range(N_GROUPS):
        mx = [jnp.full((8, COLS), NEG, F32) for _ in range(2)]
        if one_matmul:
            k_rows = kwv[g, pl.ds(first, WIN_TILES)].reshape(WIN_TILES * 2 * Q_TILE, LANES)
            s_five = _dot_nt(k_rows, rhs[g, :, 0:LANES])
        for t, (kt, table) in enumerate(tiles):
            if one_matmul:
                s_all = s_five[t * 2 * Q_TILE:(t + 1) * 2 * Q_TILE]
            else:
                s_all = _dot_nt(kwv[g, kt], rhs[g, :, 0:LANES])
            for v in range(2):
                s = s_all[v * Q_TILE:(v + 1) * Q_TILE] + tw_ref[g, v, table]
                s_w[g, t, v * Q_TILE:(v + 1) * Q_TILE, :] = s
                mx[v] = jnp.maximum(mx[v], col_max(s))
        m_win += [jnp.max(mx[v], axis=0, keepdims=True) for v in range(2)]

    for g in range(N_GROUPS):
        logits_group(s_a, g, 0)
        halves = []
        for v in range(2):
            o = jnp.zeros((HEAD_DIM + ONES_ROWS, COLS), F32)
            for t, (kt, _) in enumerate(tiles):
                e = jnp.exp2(s_w[g, t, v * Q_TILE:(v + 1) * Q_TILE, :] - m_win[g * 2 + v])
                o = o + _dot(vwt[g, kt], e.astype(BF16))
            halves.append(o[0:HEAD_DIM] * (1.0 / jnp.maximum(o[HEAD_DIM:HEAD_DIM + 1], 1e-30)))
        obuf[2, g] = jnp.concatenate(halves, axis=0)

    def far_two(j, state):
        logits_into(s_b, 2 * j + 1)
        state = consume(s_a, 2 * j, state)
        logits_into(s_a, 2 * j + 2)
        return consume(s_b, 2 * j + 1, state)

    state = lax.fori_loop(0, n_far // 2, far_two, fresh_state())
    state = lax.fori_loop(0, n_far % 2, lambda _, st: consume(s_a, n_far - 1, st), state)

    def near_pair(i, state):
        tables = [jnp.clip(qi - (2 * i + h), 0, 2) for h in range(2)]
        for g in range(N_GROUPS):
            logits_group(s_b, g, i, tables)
        return consume(s_b, i, state)

    diag_alone = 1 - qi % 2
    state = lax.fori_loop(n_far, n_all - diag_alone, near_pair, state)

    def diagonal_tile(_, state):
        ms = list(state)
        i = n_all - 1
        for g in range(N_GROUPS):
            lhs = jnp.concatenate([ksv[g, i, v * 2 * Q_TILE:v * 2 * Q_TILE + Q_TILE, :] for v in range(2)],
                                  axis=0)
            s_all = _dot_nt(lhs, rhs[g])
            for v in range(2):
                s = s_all[v * Q_TILE:(v + 1) * Q_TILE] + ts_ref[g, v, 0]
                s_b[g, v * Q_TILE:(v + 1) * Q_TILE, :] = s
                s_b[g, max_rows(v), :] = col_max(s)
        for g in range(N_GROUPS):
            for v in range(2):
                gv = g * 2 + v
                ms[gv] = online(gv, s_b[g, v * Q_TILE:(v + 1) * Q_TILE, :], s_b[g, max_rows(v), :],
                                vst[g, i, :, 0:Q_TILE], ms[gv])
        return tuple(ms)

    lax.fori_loop(0, diag_alone, diagonal_tile, state)
    finish(1)

    for g in range(N_GROUPS):
        for p in range(PAIRS_PER_GROUP):
            halves = []
            for v in range(2):
                tot = jnp.zeros((HEAD_DIM, Q_TILE), F32)
                for br in range(3):
                    c = br * N_HEADS + g * HEADS_PER_GROUP + 2 * p + v
                    tot = tot + gsig_t[c:c + 1, :] * obuf[br, g, v * HEAD_DIM:(v + 1) * HEAD_DIM,
                                                          p * Q_TILE:(p + 1) * Q_TILE]
                halves.append(tot)
            col = (g * PAIRS_PER_GROUP + p) * LANES
            o_ref[:, col:col + LANES] = jnp.concatenate(halves, axis=0).T.astype(BF16)


def _attention(proj_main, proj_kv, kcmp, vcmp, bias_c, bias_w, bias_s, ovt, far_lanes, b, s, n_top):
    n_q = s // Q_TILE
    once = pl.Buffered(1)
    kv_col = lambda c: pl.BlockSpec((s, LANES), lambda bi, qi: (bi, c))
    cmp_spec = pl.BlockSpec((1, LANES, LANES), lambda bi, qi: (bi, 0, 0))
    return pl.pallas_call(
        functools.partial(_attn_kernel, n_top=n_top),
        grid=(b, n_q),
        in_specs=[
            pl.BlockSpec((Q_TILE, NSA_WIDTH), lambda bi, qi: (bi * n_q + qi, 0)),
            kv_col(2), kv_col(3), kv_col(4), kv_col(5),
            pl.BlockSpec((Q_TILE, LANES), lambda bi, qi: (bi * n_q + qi, 6)),
            cmp_spec, cmp_spec,
            pl.BlockSpec(bias_c.shape, lambda bi, qi: (0,) * 4, pipeline_mode=once),
            pl.BlockSpec(bias_w.shape, lambda bi, qi: (0,) * 5, pipeline_mode=once),
            pl.BlockSpec(bias_s.shape, lambda bi, qi: (0,) * 5, pipeline_mode=once),
            pl.BlockSpec(ovt.shape, lambda bi, qi: (0, 0), pipeline_mode=once),
            pl.BlockSpec(far_lanes.shape, lambda bi, qi: (0, 0, 0), pipeline_mode=once),
        ],
        out_specs=pl.BlockSpec((Q_TILE, NSA_WIDTH), lambda bi, qi: (bi * n_q + qi, 0)),
        out_shape=jax.ShapeDtypeStruct((b * s, NSA_WIDTH), BF16),
        scratch_shapes=[
            pltpu.VMEM((N_GROUPS, n_q // 2, 4 * Q_TILE, 2 * LANES), BF16),
            pltpu.VMEM((N_GROUPS, n_q, 2 * Q_TILE, LANES), BF16),
            pltpu.VMEM((N_GROUPS, 2 * LANES, LANES), BF16),
            pltpu.VMEM((N_GROUPS, n_q // 2, HEAD_DIM + ONES_ROWS, 2 * Q_TILE), BF16),
            pltpu.VMEM((N_GROUPS, n_q, HEAD_DIM + ONES_ROWS, Q_TILE), BF16),
            pltpu.VMEM((LANES, LANES), BF16),
            pltpu.VMEM((N_GROUPS, COLS, 2 * LANES), BF16),
            pltpu.VMEM((2 * N_GROUPS, HEAD_DIM + ONES_ROWS, COLS), F32),
            pltpu.VMEM((3, N_GROUPS, LANES, COLS), F32),
            pltpu.VMEM((N_GROUPS, 4 * Q_TILE + 2 * SUBLANES, COLS), F32),
            pltpu.VMEM((N_GROUPS, 4 * Q_TILE + 2 * SUBLANES, COLS), F32),
            pltpu.VMEM((N_GROUPS, WIN_TILES, 2 * Q_TILE, COLS), F32),
        ],
        compiler_params=pltpu.CompilerParams(
            dimension_semantics=("parallel", "arbitrary"), vmem_limit_bytes=VMEM_LIMIT),
        name="nsa_attention",
    )(proj_main, proj_kv, proj_kv, proj_kv, proj_kv, proj_kv, kcmp, vcmp, bias_c, bias_w, bias_s,
      ovt, far_lanes)


def _merge_kernel(zn_ref, a_ref, b_ref, zc_ref, gc_ref, gn_ref, ah_ref, bh_ref, on_ref, x_ref,
                  cw_ref, cb_ref, lg_ref, lb_ref, wcp_ref, wnp_ref, wo_ref, gf_ref, out_ref, uext,
                  conv, shifted):
    i = pl.program_id(1)
    ts = a_ref.shape[0]
    f = lambda r: r[...].astype(F32)

    n_cblk = D_MODEL // LANES
    u_halo = jnp.where(i > 0, f(ah_ref) * _sigmoid(f(bh_ref)), 0.0)
    u = f(a_ref) * _sigmoid(f(b_ref))
    for cblk in range(n_cblk):
        cols = slice(cblk * LANES, (cblk + 1) * LANES)
        uext[cblk, 0:CONV_HALO, :] = u_halo[:, cols]
        uext[cblk, CONV_HALO:CONV_HALO + ts, :] = u[:, cols]
        uext[cblk, CONV_HALO + ts:, :] = jnp.zeros((SUBLANES, LANES), F32)

    lead = CONV_HALO - (CONV_KERNEL - 1)
    chunk = Q_TILE

    def conv_block(cblk, carry):
        for shift in range(SUBLANES):
            shifted[shift] = uext[cblk, shift:shift + ts + CONV_HALO, :]
        w = cw_ref[cblk]
        for h in range(ts // chunk):
            c = jnp.broadcast_to(cb_ref[cblk], (chunk, LANES))
            for shift in range(SUBLANES):
                xs = shifted[shift, h * chunk:h * chunk + chunk + CONV_HALO, :]
                for j in range(CONV_KERNEL):
                    if (lead + j) % SUBLANES == shift:
                        base = lead + j - shift
                        c = c + w[j:j + 1, :] * xs[base:base + chunk]
            conv[cblk, h * chunk:(h + 1) * chunk, :] = c
        return carry

    lax.fori_loop(0, n_cblk, conv_block, 0)
    c = jnp.concatenate([conv[cblk] for cblk in range(n_cblk)], axis=1)

    mu = jnp.mean(c, axis=-1, keepdims=True)
    cc = c - mu
    var = jnp.mean(cc * cc, axis=-1, keepdims=True)
    y = (cc * lax.rsqrt(var + EPS)) * lg_ref[...] + lb_ref[...]
    conv_act = _silu(y) * _silu(f(zc_ref))
    y_conv = _dot(conv_act.astype(BF16), wcp_ref[...])

    nsa_act = f(on_ref) * _silu(f(zn_ref))
    y_nsa = _dot(nsa_act.astype(BF16), wnp_ref[...])

    merged = _sigmoid(f(gc_ref)) * y_conv + _sigmoid(f(gn_ref)) * y_nsa
    xo = x_ref[...] + _dot(merged.astype(BF16), wo_ref[...])
    ms = jnp.mean(xo * xo, axis=-1, keepdims=True)
    out_ref[...] = (xo * lax.rsqrt(ms + EPS)) * gf_ref[...]


def _merge(proj_main, o_nsa, x2, cw, cb, lg, lb, wcp, wnp, wo, gf, b, s, ts):
    n_t = s // ts
    n_cblk = D_MODEL // LANES
    halo_per_tile = ts // CONV_HALO
    cw = jnp.transpose(cw.reshape(cw.shape[0], n_cblk, LANES), (1, 0, 2))
    cb = cb.reshape(n_cblk, 1, LANES)
    col = lambda c: pl.BlockSpec((ts, COL_TILE), lambda bi, ti: (bi * n_t + ti, c))
    halo = lambda c: pl.BlockSpec(
        (CONV_HALO, COL_TILE),
        lambda bi, ti: (jnp.maximum((bi * n_t + ti) * halo_per_tile - 1, 0), c))
    const = lambda a: pl.BlockSpec(a.shape, lambda bi, ti: (0,) * a.ndim, pipeline_mode=pl.Buffered(1))
    rowblk = pl.BlockSpec((ts, D_MODEL), lambda bi, ti: (bi * n_t + ti, 0))
    return pl.pallas_call(
        _merge_kernel,
        grid=(b, n_t),
        in_specs=[col(2), col(3), col(4), col(5), col(6), col(7), halo(3), halo(4), rowblk, rowblk,
                  const(cw), const(cb), const(lg), const(lb), const(wcp), const(wnp), const(wo),
                  const(gf)],
        out_specs=rowblk,
        out_shape=jax.ShapeDtypeStruct((b * s, D_MODEL), F32),
        scratch_shapes=[pltpu.VMEM((n_cblk, ts + CONV_HALO + SUBLANES, LANES), F32),
                        pltpu.VMEM((n_cblk, ts, LANES), F32),
                        pltpu.VMEM((SUBLANES, ts + CONV_HALO, LANES), F32)],
        compiler_params=pltpu.CompilerParams(
            dimension_semantics=("parallel", "arbitrary"), vmem_limit_bytes=VMEM_LIMIT),
        name="conv_merge",
    )(proj_main, proj_main, proj_main, proj_main, proj_main, proj_main, proj_main, proj_main,
      o_nsa, x2, cw, cb, lg, lb, wcp, wnp, wo, gf)


def _t5_bucket_np(rel):
    rel = np.maximum(rel, 0)
    max_exact = REL_BUCKETS // 2
    relf = np.maximum(rel, 1).astype(np.float32)
    large = max_exact + (np.log(relf / np.float32(max_exact))
                         / np.float32(np.log(REL_MAX_DIST / max_exact))
                         * np.float32(REL_BUCKETS - max_exact)).astype(np.int32)
    large = np.minimum(large, REL_BUCKETS - 1)
    return np.where(rel < max_exact, rel, large)


def _pair_head_index():
    g = np.arange(N_GROUPS)[:, None, None]
    v = np.arange(2)[None, :, None]
    p = np.arange(PAIRS_PER_GROUP)[None, None, :]
    return g * HEADS_PER_GROUP + 2 * p + v


def _bias_lookup(rel_bias, rel):
    bucket = _t5_bucket_np(rel).reshape(-1)
    onehot = (jnp.arange(REL_BUCKETS)[:, None] == jnp.asarray(bucket)[None, :]).astype(F32)
    vals = jnp.dot(rel_bias.astype(F32).T * LOG2E, onehot, precision=lax.Precision.HIGHEST)
    vals = vals.reshape((N_HEADS,) + rel.shape)
    head = _pair_head_index()
    return jnp.stack([jnp.stack([jnp.concatenate([vals[h] for h in head[g, v]], axis=-1)
                                 for v in range(2)]) for g in range(N_GROUPS)])


def _bias_tables(rel_bias, s, n_sel):
    c = np.arange(LANES)[:, None]
    r = np.arange(Q_TILE)[None, :]
    far = rel_bias.astype(F32)[REL_BUCKETS - 1][_pair_head_index()] * LOG2E
    assert s // CMP_STRIDE <= LANES + Q_TILE // CMP_STRIDE
    rel_w = np.stack([dd * Q_TILE + r - c for dd in range(WIN_TILES)])
    assert (_t5_bucket_np(rel_w[2:]) == REL_BUCKETS - 1).all()
    rho = np.arange(2 * LANES)[:, None]
    rel_c = r - ((rho - LANES) * CMP_STRIDE + CMP_BLOCK - 1)
    vals = _bias_lookup(rel_bias, np.concatenate([rel_w, rel_c.reshape(2, LANES, Q_TILE)]))
    bias_w, bias_c = vals[:, :, :WIN_TILES], vals[:, :, WIN_TILES:].reshape(N_GROUPS, 2, 2 * LANES, COLS)
    cols = lambda a: np.tile(a, (1,) * (a.ndim - 1) + (PAIRS_PER_GROUP,))

    tw = jnp.where(cols((rel_w >= 0) & (rel_w < WINDOW)), bias_w, NEG)
    tw = jnp.concatenate([tw, jnp.full_like(tw[:, :, :1], NEG)], axis=2)
    far_cols = jnp.repeat(far, Q_TILE, axis=-1)[:, :, None, None, :]
    ts = jnp.where(cols(rel_w[:3] >= 0), bias_w[:, :, :3] - far_cols, NEG)
    bc = jnp.where(cols(rel_c >= 0), bias_c, NEG)
    hi = far.astype(BF16)
    lo = (far - hi.astype(F32)).astype(BF16)
    pieces = jnp.stack([hi[:, 0], lo[:, 0], hi[:, 1], lo[:, 1]], axis=-1)
    pieces = jnp.broadcast_to(pieces[:, :, None, :], (N_GROUPS, PAIRS_PER_GROUP, Q_TILE, 4))
    far_lanes = jnp.pad(pieces.reshape(N_GROUPS, COLS, 4),
                        ((0, 0), (0, 0), (n_sel, LANES - n_sel - 4)))
    return tw, ts, bc, far_lanes


def _overlap_t(s):
    n_cmp = (s - CMP_BLOCK) // CMP_STRIDE + 1
    cs = np.arange(LANES) * CMP_STRIDE
    ss = np.arange(s // SEL_BLOCK) * SEL_BLOCK
    ovt = ((cs[None, :] <= ss[:, None] + SEL_BLOCK - 1) & (cs[None, :] + CMP_BLOCK - 1 >= ss[:, None])
           & (np.arange(LANES)[None, :] < n_cmp))
    return jnp.asarray(ovt, BF16)


def _pad_w_in(w):
    used = NSA_WIDTH + 6 * KV_WIDTH + 3 * N_HEADS
    gap = (KV_COL_TILE + 1) * COL_TILE - used
    n = w.shape[-1]
    col = jnp.arange(n)[None, None, :]
    w = jnp.where(col < NSA_WIDTH, w * (HEAD_DIM ** -0.5 * LOG2E), w)
    left = jnp.pad(w[..., :used], ((0, 0), (0, 0), (0, n - used + gap)))
    right = jnp.pad(w[..., used:], ((0, 0), (0, 0), (used + gap, 0)))
    return (left + right).astype(BF16).reshape(D_MODEL, n + gap)


def _compress_weights(pos, w1, w2):
    assert N_GROUPS == 2
    half = CMP_BLOCK // 2

    def per_group(w):
        zeros = [(0, 0)] * (w.ndim - 1)
        top = jnp.pad(w, zeros + [(0, w.shape[-1])])
        bottom = jnp.pad(w, zeros + [(w.shape[-1], 0)])
        return jnp.concatenate([top, bottom], axis=-2)

    w1g = per_group(w1.reshape(CMP_BLOCK, HEAD_DIM, CMP_HIDDEN))
    blk = lambda part: part.reshape(half * KV_WIDTH, N_GROUPS * CMP_HIDDEN)
    w1d = jnp.concatenate([blk(w1g[:half]), blk(w1g[half:])], axis=1).astype(BF16)
    w2d = per_group(w2).astype(BF16)
    tilepos = lambda part: jnp.broadcast_to(part[:, None, :], (half, N_GROUPS, HEAD_DIM)).reshape(1, -1)
    posd = jnp.concatenate([tilepos(pos[:half]), tilepos(pos[half:])], axis=0).astype(F32)
    return posd, w1d, w2d


def kernel(x, norm_in_g, w_in, pos_ck, w_ck1, w_ck2, pos_cv, w_cv1, w_cv2, rel_bias, conv_w, conv_b,
           conv_ln_g, conv_ln_b, w_conv_proj, w_nsa_proj, w_out, norm_f_g):
    b, s, d = x.shape
    n_sel = s // SEL_BLOCK
    assert d == D_MODEL and w_in.shape[0] == 1, "single-layer block with D_MODEL=1024"
    assert s % (2 * Q_TILE) == 0 and s // CMP_STRIDE <= LANES and s >= WINDOW
    assert n_sel + 4 <= LANES
    m = b * s
    x2 = x.reshape(m, d)
    row = lambda a: a.reshape(1, -1).astype(F32)

    proj_kv, proj_main = _input_projection(x2, row(norm_in_g[0]), _pad_w_in(w_in), min(1024, m))

    chunks = s // CMP_STRIDE
    pk, w1k, w2k = _compress_weights(pos_ck[0], w_ck1[0], w_ck2[0])
    pv, w1v, w2v = _compress_weights(pos_cv[0], w_cv1[0], w_cv2[0])
    kcmp, vcmp = _compress(proj_kv, pk, pv, w1k, w1v, w2k, w2v, b, s)
    if chunks < LANES:
        padrows = ((0, 0), (0, LANES - chunks), (0, 0))
        kcmp, vcmp = jnp.pad(kcmp, padrows), jnp.pad(vcmp, padrows)

    bias_w, bias_s, bias_c, far_lanes = _bias_tables(rel_bias, s, n_sel)
    o_nsa = _attention(proj_main, proj_kv, kcmp, vcmp, bias_c, bias_w, bias_s, _overlap_t(s),
                       far_lanes, b, s, min(N_SELECT, n_sel))

    cw = jnp.pad(conv_w[0].astype(F32), ((0, CONV_HALO - CONV_KERNEL), (0, 0)))
    out = _merge(proj_main, o_nsa, x2, cw, row(conv_b[0]), row(conv_ln_g[0]), row(conv_ln_b[0]),
                 w_conv_proj[0].astype(BF16), w_nsa_proj[0].astype(BF16), w_out[0].astype(BF16),
                 row(norm_f_g), b, s, MERGE_ROWS)
    return out.reshape(b, s, d)
```

```python
import functools
import math

import numpy as np
import jax
import jax.numpy as jnp
from jax import lax
from jax.experimental import pallas as pl
from jax.experimental.pallas import tpu as pltpu

F32 = jnp.float32
BF16 = jnp.bfloat16

D_MODEL = 1024
N_HEADS = 16
N_GROUPS = 2
HEADS_PER_GROUP = N_HEADS // N_GROUPS
PAIRS_PER_GROUP = HEADS_PER_GROUP // 2
HEAD_DIM = 64
NSA_WIDTH = N_HEADS * HEAD_DIM
KV_WIDTH = N_GROUPS * HEAD_DIM
CMP_BLOCK = 32
CMP_STRIDE = 16
CMP_HIDDEN = 256
SEL_BLOCK = 64
N_SELECT = 8
WINDOW = 512
Q_TILE = 128
CONV_KERNEL = 31
CONV_HALO = 32
REL_BUCKETS = 32
REL_MAX_DIST = 128
EPS = 1e-6
NEG = -1e30
FORCE_SCORE = 1e6
LOG2E = math.log2(math.e)
LANES = 128
SUBLANES = 8
ONES_ROWS = 16
COLS = PAIRS_PER_GROUP * Q_TILE
COL_TILE = 1024
N_COL_TILES = 8
KV_COL_TILE = 1
MERGE_ROWS = 512
WIN_TILES = WINDOW // Q_TILE + 1
VMEM_LIMIT = 56 * 1024 * 1024


def _dot(a, b):
    return jnp.dot(a, b, preferred_element_type=F32)


def _dot_nt(a, b):
    return lax.dot_general(a, b, (((1,), (1,)), ((), ())), preferred_element_type=F32)


def _sigmoid(x):
    return 1.0 / (1.0 + jnp.exp2(x * (-LOG2E)))


def _silu(x):
    return x * _sigmoid(x)


def _proj_kernel(x_ref, g_ref, w_ref, kv_ref, main_ref, h_ref):
    j = pl.program_id(1)

    @pl.when(j == 0)
    def _():
        x = x_ref[...]
        ms = jnp.mean(x * x, axis=-1, keepdims=True)
        h_ref[...] = ((x * lax.rsqrt(ms + EPS)) * g_ref[...]).astype(BF16)

    acc = _dot(h_ref[...], w_ref[...])
    main_ref[...] = acc.astype(BF16)

    @pl.when(j == KV_COL_TILE // 2)
    def _():
        kv_ref[...] = acc[:, (KV_COL_TILE % 2) * COL_TILE:(KV_COL_TILE % 2 + 1) * COL_TILE]


def _input_projection(x2, g, w_perm, tm):
    m = x2.shape[0]
    return pl.pallas_call(
        _proj_kernel,
        grid=(m // tm, N_COL_TILES // 2),
        in_specs=[
            pl.BlockSpec((tm, D_MODEL), lambda i, j: (i, 0)),
            pl.BlockSpec((1, D_MODEL), lambda i, j: (0, 0)),
            pl.BlockSpec((D_MODEL, 2 * COL_TILE), lambda i, j: (0, j)),
        ],
        out_specs=[
            pl.BlockSpec((tm, COL_TILE), lambda i, j: (i, 0)),
            pl.BlockSpec((tm, 2 * COL_TILE), lambda i, j: (i, j)),
        ],
        out_shape=[
            jax.ShapeDtypeStruct((m, COL_TILE), F32),
            jax.ShapeDtypeStruct((m, N_COL_TILES * COL_TILE), BF16),
        ],
        scratch_shapes=[pltpu.VMEM((tm, D_MODEL), BF16)],
        compiler_params=pltpu.CompilerParams(
            dimension_semantics=("parallel", "arbitrary"), vmem_limit_bytes=VMEM_LIMIT),
        name="input_projection",
    )(x2, g, w_perm)


def _compress_kernel(kf_ref, vf_ref, pk_ref, pv_ref, w1k_ref, w1v_ref, w2k_ref, w2v_ref,
                     kc_ref, vc_ref):
    def one(f_ref, pos_ref, w1_ref, w2_ref, o_ref):
        n = f_ref.shape[0] // CMP_STRIDE
        hw = N_GROUPS * CMP_HIDDEN
        first = jnp.zeros((n, hw), F32)
        second = jnp.zeros((n, hw), F32)
        for i in range(0, CMP_STRIDE, 2):
            tok = jnp.concatenate([f_ref[pl.ds(i + k, n, stride=CMP_STRIDE), :] for k in range(2)],
                                  axis=1)
            lanes = slice(i * KV_WIDTH, (i + 2) * KV_WIDTH)
            first = first + _dot((tok + pos_ref[0:1, lanes]).astype(BF16), w1_ref[lanes, 0:hw])
            second = second + _dot((tok + pos_ref[1:2, lanes]).astype(BF16), w1_ref[lanes, hw:2 * hw])
        hid = first + pltpu.roll(second, n - 1, axis=0)
        o_ref[0] = _dot(_silu(hid).astype(BF16), w2_ref[...])

    one(kf_ref, pk_ref, w1k_ref, w2k_ref, kc_ref)
    one(vf_ref, pv_ref, w1v_ref, w2v_ref, vc_ref)


def _compress(proj_kv, pk, pv, w1k, w1v, w2k, w2v, b, s):
    n = s // CMP_STRIDE
    const = lambda shape: pl.BlockSpec(shape, lambda i: (0,) * len(shape))
    kv_col = lambda c: pl.BlockSpec((s, KV_WIDTH), lambda i: (i, c))
    out = pl.BlockSpec((1, n, LANES), lambda i: (i, 0, 0))
    return pl.pallas_call(
        _compress_kernel,
        grid=(b,),
        in_specs=[kv_col(0), kv_col(1), const(pk.shape), const(pv.shape), const(w1k.shape),
                  const(w1v.shape), const(w2k.shape), const(w2v.shape)],
        out_specs=[out, out],
        out_shape=[jax.ShapeDtypeStruct((b, n, LANES), F32)] * 2,
        compiler_params=pltpu.CompilerParams(
            dimension_semantics=("parallel",), vmem_limit_bytes=VMEM_LIMIT),
        name="nsa_compress",
    )(proj_kv, proj_kv, pk, pv, w1k, w1v, w2k, w2v)


def _attn_kernel(q_ref, ks_ref, vs_ref, kw_ref, vw_ref, gt_ref, kc_ref, vc_ref, bc_ref, tw_ref,
                 ts_ref, ovt_ref, bx_ref, o_ref,
                 ksv, kwv, kcv, vst, vwt, vct, rhs, acc, obuf, s_a, s_b, s_w, *, n_top):
    qi = pl.program_id(1)
    n_kt = kwv.shape[1]
    n_sel = ovt_ref.shape[0]
    variants = 2 * N_GROUPS

    @pl.when(qi == 0)
    def _prepare_kv():
        def halves(k):
            lo = lax.broadcasted_iota(jnp.int32, k.shape, 1) < HEAD_DIM
            kr = pltpu.roll(k, HEAD_DIM, axis=1)
            z = jnp.zeros_like(k)
            c = lambda a: a.astype(BF16)
            return ((c(jnp.where(lo, k, z)), c(jnp.where(lo, z, kr))),
                    (c(jnp.where(lo, kr, z)), c(jnp.where(lo, z, k))))

        def extra_lanes(shape, v):
            pair = lax.broadcasted_iota(jnp.int32, shape, 0)
            r = lax.broadcasted_iota(jnp.int32, shape, 1)
            lane = lax.broadcasted_iota(jnp.int32, shape, 2)
            one = ((lane >= n_sel + 2 * v) & (lane < n_sel + 2 * v + 2)
                   | (lane == pair * (2 * Q_TILE // SEL_BLOCK) + r // SEL_BLOCK))
            return jnp.where(one, 1.0, 0.0).astype(BF16)

        pair_shape = (n_kt // 2, 2 * Q_TILE, LANES)
        tile_shape = (n_kt, Q_TILE, LANES)
        k_sel, k_win, k_cmp = halves(ks_ref[...]), halves(kw_ref[...]), halves(kc_ref[0])
        for g in range(N_GROUPS):
            for v in range(2):
                rows = slice(v * 2 * Q_TILE, (v + 1) * 2 * Q_TILE)
                ksv[g, :, rows, 0:LANES] = k_sel[g][v].reshape(pair_shape)
                ksv[g, :, rows, LANES:2 * LANES] = extra_lanes(pair_shape, v)
                kwv[g, :, v * Q_TILE:(v + 1) * Q_TILE, :] = k_win[g][v].reshape(tile_shape)
                kcv[g, v * LANES:(v + 1) * LANES, :] = k_cmp[g][v]
        vst[:, :, HEAD_DIM:, :] = jnp.ones((N_GROUPS, n_kt // 2, ONES_ROWS, 2 * Q_TILE), BF16)
        vwt[:, :, HEAD_DIM:, :] = jnp.ones((N_GROUPS, n_kt, ONES_ROWS, Q_TILE), BF16)
        for kt in range(n_kt):
            rows = slice(kt * Q_TILE, (kt + 1) * Q_TILE)
            half = slice((kt % 2) * Q_TILE, (kt % 2 + 1) * Q_TILE)
            vs_t = vs_ref[rows, :].T.astype(BF16)
            vw_t = vw_ref[rows, :].T.astype(BF16)
            for g in range(N_GROUPS):
                vst[g, kt // 2, 0:HEAD_DIM, half] = vs_t[g * HEAD_DIM:(g + 1) * HEAD_DIM]
                vwt[g, kt, 0:HEAD_DIM, :] = vw_t[g * HEAD_DIM:(g + 1) * HEAD_DIM]
        vct[...] = vc_ref[0].T.astype(BF16)

    qt = q_ref[...]
    gsig_t = _sigmoid(gt_ref[...]).T

    def col_max(s):
        return jnp.max(s.reshape(s.shape[0] // 8, 8, COLS), axis=0)

    def group_rows(g):
        return slice(g * HEAD_DIM, (g + 1) * HEAD_DIM)

    def online(gv, s, s_max, vt, m):
        m_new = jnp.maximum(m, jnp.max(s_max, axis=0, keepdims=True))
        e = jnp.exp2(s - m_new)
        acc[gv] = acc[gv] * jnp.exp2(m - m_new) + _dot(vt, e.astype(BF16))
        return m_new

    def fresh_state():
        acc[...] = jnp.zeros(acc.shape, F32)
        return tuple(jnp.full((1, COLS), NEG, F32) for _ in range(variants))

    def finish(branch):
        for g in range(N_GROUPS):
            halves = []
            for v in range(2):
                gv = g * 2 + v
                l = acc[gv, HEAD_DIM:HEAD_DIM + 1, :]
                halves.append(acc[gv, 0:HEAD_DIM, :] * (1.0 / jnp.maximum(l, 1e-30)))
            obuf[branch, g] = jnp.concatenate(halves, axis=0)

    groups = range(N_GROUPS)
    for g in groups:
        q4 = jnp.concatenate(
            [qt[:, (g * PAIRS_PER_GROUP + p) * LANES:(g * PAIRS_PER_GROUP + p + 1) * LANES]
             for p in range(PAIRS_PER_GROUP)], axis=0)
        rhs[g, :, 0:LANES] = q4

    bc_rows = pl.ds(pl.multiple_of(LANES - qi * (Q_TILE // CMP_STRIDE), SUBLANES), LANES)
    s_cmp = [_dot_nt(kcv[g], rhs[g, :, 0:LANES])
             + jnp.concatenate([bc_ref[g, 0, bc_rows, :], bc_ref[g, 1, bc_rows, :]], axis=0)
             for g in groups]
    psum_t = []
    for g in groups:
        tot = jnp.zeros((LANES, Q_TILE), F32)
        halves = []
        for v in range(2):
            s = s_cmp[g][v * LANES:(v + 1) * LANES]
            m = jnp.max(s, axis=0, keepdims=True)
            e = jnp.exp2(s - jnp.where(m > 0.5 * NEG, m, 0.0))
            p = e * (1.0 / jnp.maximum(jnp.sum(e, axis=0, keepdims=True), 1e-30))
            for pp in range(PAIRS_PER_GROUP):
                tot = tot + p[:, pp * Q_TILE:(pp + 1) * Q_TILE]
            halves.append(_dot(vct[group_rows(g), :], p.astype(BF16)))
        obuf[0, g] = jnp.concatenate(halves, axis=0)
        psum_t.append(tot)

    ovt = ovt_ref[...]
    imp_t = []
    for g in groups:
        p_hi = psum_t[g].astype(BF16)
        r1 = psum_t[g] - p_hi.astype(F32)
        p_mid = r1.astype(BF16)
        p_lo = (r1 - p_mid.astype(F32)).astype(BF16)
        imp_t.append(_dot(ovt, p_hi) + _dot(ovt, p_mid) + _dot(ovt, p_lo))

    j_idx = lax.broadcasted_iota(jnp.int32, (n_sel, Q_TILE), 0)
    r_idx = lax.broadcasted_iota(jnp.int32, (n_sel, Q_TILE), 1)
    blk_t = qi * (Q_TILE // SEL_BLOCK) + r_idx // SEL_BLOCK
    valid_blk = j_idx <= blk_t
    forced = (j_idx == 0) | (j_idx == blk_t) | (j_idx == blk_t - 1)
    prio = [jnp.where(valid_blk, jnp.where(forced, FORCE_SCORE, imp_t[g]), -FORCE_SCORE) for g in groups]
    rank = [jnp.zeros((n_sel, Q_TILE), F32) for _ in groups]
    for jj in range(n_sel):
        later = j_idx > jj
        for g in groups:
            row = prio[g][jj:jj + 1, :]
            beats = (row > prio[g]) | ((row == prio[g]) & later)
            rank[g] = rank[g] + jnp.where(beats, 1.0, 0.0)
    for g in groups:
        drop_t = jnp.where((rank[g] < n_top) & valid_blk, 0.0, NEG)
        drop_t = jnp.concatenate([drop_t, jnp.zeros((LANES - n_sel, Q_TILE), F32)], axis=0)
        drop = drop_t.T.astype(BF16)
        rhs[g, :, LANES:2 * LANES] = jnp.concatenate([drop] * PAIRS_PER_GROUP, axis=0) + bx_ref[g]

    pair_rows = 4 * Q_TILE

    def half_rows(v):
        return slice(v * 2 * Q_TILE, (v + 1) * 2 * Q_TILE)

    def max_rows(v):
        return slice(pair_rows + v * SUBLANES, pair_rows + (v + 1) * SUBLANES)

    def logits_group(buf, g, i, tables=None):
        s_all = _dot_nt(ksv[g, i], rhs[g])
        for v in range(2):
            s = s_all[half_rows(v)]
            if tables is not None:
                s = jnp.concatenate([s[h * Q_TILE:(h + 1) * Q_TILE] + ts_ref[g, v, tables[h]]
                                     for h in range(2)], axis=0)
            buf[g, half_rows(v), :] = s
            buf[g, max_rows(v), :] = col_max(s)

    def logits_into(buf, i):
        for g in range(N_GROUPS):
            logits_group(buf, g, i)

    def consume(buf, i, state):
        ms = list(state)
        for g in range(N_GROUPS):
            for v in range(2):
                gv = g * 2 + v
                ms[gv] = online(gv, buf[g, half_rows(v), :], buf[g, max_rows(v), :], vst[g, i], ms[gv])
        return tuple(ms)

    n_all = (qi + 2) // 2
    n_far = jnp.maximum(qi - 1, 0) // 2

    first = jnp.maximum(qi - (WIN_TILES - 1), 0)
    tiles = []
    for t in range(WIN_TILES):
        dd = qi - (first + t)
        table = jnp.where(dd >= 0, dd, WIN_TILES)
        tiles.append((jnp.minimum(first + t, n_kt - 1), table))
    m_win = []
    one_matmul = n_kt >= WIN_TILES
    for g in range(N_GROUPS):
        mx = [jnp.full((8, COLS), NEG, F32) for _ in range(2)]
        if one_matmul:
            k_rows = kwv[g, pl.ds(first, WIN_TILES)].reshape(WIN_TILES * 2 * Q_TILE, LANES)
            s_five = _dot_nt(k_rows, rhs[g, :, 0:LANES])
        for t, (kt, table) in enumerate(tiles):
            if one_matmul:
                s_all = s_five[t * 2 * Q_TILE:(t + 1) * 2 * Q_TILE]
            else:
                s_all = _dot_nt(kwv[g, kt], rhs[g, :, 0:LANES])
            for v in range(2):
                s = s_all[v * Q_TILE:(v + 1) * Q_TILE] + tw_ref[g, v, table]
                s_w[g, t, v * Q_TILE:(v + 1) * Q_TILE, :] = s
                mx[v] = jnp.maximum(mx[v], col_max(s))
        m_win += [jnp.max(mx[v], axis=0, keepdims=True) for v in range(2)]

    for g in range(N_GROUPS):
        logits_group(s_a, g, 0)
        halves = []
        for v in range(2):
            o = jnp.zeros((HEAD_DIM + ONES_ROWS, COLS), F32)
            for t, (kt, _) in enumerate(tiles):
                e = jnp.exp2(s_w[g, t, v * Q_TILE:(v + 1) * Q_TILE, :] - m_win[g * 2 + v])
                o = o + _dot(vwt[g, kt], e.astype(BF16))
            halves.append(o[0:HEAD_DIM] * (1.0 / jnp.maximum(o[HEAD_DIM:HEAD_DIM + 1], 1e-30)))
        obuf[2, g] = jnp.concatenate(halves, axis=0)

    def far_two(j, state):
        logits_into(s_b, 2 * j + 1)
        state = consume(s_a, 2 * j, state)
        logits_into(s_a, 2 * j + 2)
        return consume(s_b, 2 * j + 1, state)

    state = lax.fori_loop(0, n_far // 2, far_two, fresh_state())
    state = lax.fori_loop(0, n_far % 2, lambda _, st: consume(s_a, n_far - 1, st), state)

    def near_pair(i, state):
        tables = [jnp.clip(qi - (2 * i + h), 0, 2) for h in range(2)]
        for g in range(N_GROUPS):
            logits_group(s_b, g, i, tables)
        return consume(s_b, i, state)

    diag_alone = 1 - qi % 2
    state = lax.fori_loop(n_far, n_all - diag_alone, near_pair, state)

    def diagonal_tile(_, state):
        ms = list(state)
        i = n_all - 1
        for g in range(N_GROUPS):
            lhs = jnp.concatenate([ksv[g, i, v * 2 * Q_TILE:v * 2 * Q_TILE + Q_TILE, :] for v in range(2)],
                                  axis=0)
            s_all = _dot_nt(lhs, rhs[g])
            for v in range(2):
                s = s_all[v * Q_TILE:(v + 1) * Q_TILE] + ts_ref[g, v, 0]
                s_b[g, v * Q_TILE:(v + 1) * Q_TILE, :] = s
                s_b[g, max_rows(v), :] = col_max(s)
        for g in range(N_GROUPS):
            for v in range(2):
                gv = g * 2 + v
                ms[gv] = online(gv, s_b[g, v * Q_TILE:(v + 1) * Q_TILE, :], s_b[g, max_rows(v), :],
                                vst[g, i, :, 0:Q_TILE], ms[gv])
        return tuple(ms)

    lax.fori_loop(0, diag_alone, diagonal_tile, state)
    finish(1)

    for g in range(N_GROUPS):
        for p in range(PAIRS_PER_GROUP):
            halves = []
            for v in range(2):
                tot = jnp.zeros((HEAD_DIM, Q_TILE), F32)
                for br in range(3):
                    c = br * N_HEADS + g * HEADS_PER_GROUP + 2 * p + v
                    tot = tot + gsig_t[c:c + 1, :] * obuf[br, g, v * HEAD_DIM:(v + 1) * HEAD_DIM,
                                                          p * Q_TILE:(p + 1) * Q_TILE]
                halves.append(tot)
            col = (g * PAIRS_PER_GROUP + p) * LANES
            o_ref[:, col:col + LANES] = jnp.concatenate(halves, axis=0).T.astype(BF16)


def _attention(proj_main, proj_kv, kcmp, vcmp, bias_c, bias_w, bias_s, ovt, far_lanes, b, s, n_top):
    n_q = s // Q_TILE
    once = pl.Buffered(1)
    kv_col = lambda c: pl.BlockSpec((s, LANES), lambda bi, qi: (bi, c))
    cmp_spec = pl.BlockSpec((1, LANES, LANES), lambda bi, qi: (bi, 0, 0))
    return pl.pallas_call(
        functools.partial(_attn_kernel, n_top=n_top),
        grid=(b, n_q),
        in_specs=[
            pl.BlockSpec((Q_TILE, NSA_WIDTH), lambda bi, qi: (bi * n_q + qi, 0)),
            kv_col(2), kv_col(3), kv_col(4), kv_col(5),
            pl.BlockSpec((Q_TILE, LANES), lambda bi, qi: (bi * n_q + qi, 6)),
            cmp_spec, cmp_spec,
            pl.BlockSpec(bias_c.shape, lambda bi, qi: (0,) * 4, pipeline_mode=once),
            pl.BlockSpec(bias_w.shape, lambda bi, qi: (0,) * 5, pipeline_mode=once),
            pl.BlockSpec(bias_s.shape, lambda bi, qi: (0,) * 5, pipeline_mode=once),
            pl.BlockSpec(ovt.shape, lambda bi, qi: (0, 0), pipeline_mode=once),
            pl.BlockSpec(far_lanes.shape, lambda bi, qi: (0, 0, 0), pipeline_mode=once),
        ],
        out_specs=pl.BlockSpec((Q_TILE, NSA_WIDTH), lambda bi, qi: (bi * n_q + qi, 0)),
        out_shape=jax.ShapeDtypeStruct((b * s, NSA_WIDTH), BF16),
        scratch_shapes=[
            pltpu.VMEM((N_GROUPS, n_q // 2, 4 * Q_TILE, 2 * LANES), BF16),
            pltpu.VMEM((N_GROUPS, n_q, 2 * Q_TILE, LANES), BF16),
            pltpu.VMEM((N_GROUPS, 2 * LANES, LANES), BF16),
            pltpu.VMEM((N_GROUPS, n_q // 2, HEAD_DIM + ONES_ROWS, 2 * Q_TILE), BF16),
            pltpu.VMEM((N_GROUPS, n_q, HEAD_DIM + ONES_ROWS, Q_TILE), BF16),
            pltpu.VMEM((LANES, LANES), BF16),
            pltpu.VMEM((N_GROUPS, COLS, 2 * LANES), BF16),
            pltpu.VMEM((2 * N_GROUPS, HEAD_DIM + ONES_ROWS, COLS), F32),
            pltpu.VMEM((3, N_GROUPS, LANES, COLS), F32),
            pltpu.VMEM((N_GROUPS, 4 * Q_TILE + 2 * SUBLANES, COLS), F32),
            pltpu.VMEM((N_GROUPS, 4 * Q_TILE + 2 * SUBLANES, COLS), F32),
            pltpu.VMEM((N_GROUPS, WIN_TILES, 2 * Q_TILE, COLS), F32),
        ],
        compiler_params=pltpu.CompilerParams(
            dimension_semantics=("parallel", "arbitrary"), vmem_limit_bytes=VMEM_LIMIT),
        name="nsa_attention",
    )(proj_main, proj_kv, proj_kv, proj_kv, proj_kv, proj_kv, kcmp, vcmp, bias_c, bias_w, bias_s,
      ovt, far_lanes)


def _merge_kernel(zn_ref, a_ref, b_ref, zc_ref, gc_ref, gn_ref, ah_ref, bh_ref, on_ref, x_ref,
                  cw_ref, cb_ref, lg_ref, lb_ref, wcp_ref, wnp_ref, wo_ref, gf_ref, out_ref, uext,
                  conv, shifted):
    i = pl.program_id(1)
    ts = a_ref.shape[0]
    f = lambda r: r[...].astype(F32)

    n_cblk = D_MODEL // LANES
    u_halo = jnp.where(i > 0, f(ah_ref) * _sigmoid(f(bh_ref)), 0.0)
    u = f(a_ref) * _sigmoid(f(b_ref))
    for cblk in range(n_cblk):
        cols = slice(cblk * LANES, (cblk + 1) * LANES)
        uext[cblk, 0:CONV_HALO, :] = u_halo[:, cols]
        uext[cblk, CONV_HALO:CONV_HALO + ts, :] = u[:, cols]
        uext[cblk, CONV_HALO + ts:, :] = jnp.zeros((SUBLANES, LANES), F32)

    lead = CONV_HALO - (CONV_KERNEL - 1)
    chunk = Q_TILE

    def conv_block(cblk, carry):
        for shift in range(SUBLANES):
            shifted[shift] = uext[cblk, shift:shift + ts + CONV_HALO, :]
        w = cw_ref[cblk]
        for h in range(ts // chunk):
            c = jnp.broadcast_to(cb_ref[cblk], (chunk, LANES))
            for shift in range(SUBLANES):
                xs = shifted[shift, h * chunk:h * chunk + chunk + CONV_HALO, :]
                for j in range(CONV_KERNEL):
                    if (lead + j) % SUBLANES == shift:
                        base = lead + j - shift
                        c = c + w[j:j + 1, :] * xs[base:base + chunk]
            conv[cblk, h * chunk:(h + 1) * chunk, :] = c
        return carry

    lax.fori_loop(0, n_cblk, conv_block, 0)
    c = jnp.concatenate([conv[cblk] for cblk in range(n_cblk)], axis=1)

    mu = jnp.mean(c, axis=-1, keepdims=True)
    cc = c - mu
    var = jnp.mean(cc * cc, axis=-1, keepdims=True)
    y = (cc * lax.rsqrt(var + EPS)) * lg_ref[...] + lb_ref[...]
    conv_act = _silu(y) * _silu(f(zc_ref))
    y_conv = _dot(conv_act.astype(BF16), wcp_ref[...])

    nsa_act = f(on_ref) * _silu(f(zn_ref))
    y_nsa = _dot(nsa_act.astype(BF16), wnp_ref[...])

    merged = _sigmoid(f(gc_ref)) * y_conv + _sigmoid(f(gn_ref)) * y_nsa
    xo = x_ref[...] + _dot(merged.astype(BF16), wo_ref[...])
    ms = jnp.mean(xo * xo, axis=-1, keepdims=True)
    out_ref[...] = (xo * lax.rsqrt(ms + EPS)) * gf_ref[...]


def _merge(proj_main, o_nsa, x2, cw, cb, lg, lb, wcp, wnp, wo, gf, b, s, ts):
    n_t = s // ts
    n_cblk = D_MODEL // LANES
    halo_per_tile = ts // CONV_HALO
    cw = jnp.transpose(cw.reshape(cw.shape[0], n_cblk, LANES), (1, 0, 2))
    cb = cb.reshape(n_cblk, 1, LANES)
    col = lambda c: pl.BlockSpec((ts, COL_TILE), lambda bi, ti: (bi * n_t + ti, c))
    halo = lambda c: pl.BlockSpec(
        (CONV_HALO, COL_TILE),
        lambda bi, ti: (jnp.maximum((bi * n_t + ti) * halo_per_tile - 1, 0), c))
    const = lambda a: pl.BlockSpec(a.shape, lambda bi, ti: (0,) * a.ndim, pipeline_mode=pl.Buffered(1))
    rowblk = pl.BlockSpec((ts, D_MODEL), lambda bi, ti: (bi * n_t + ti, 0))
    return pl.pallas_call(
        _merge_kernel,
        grid=(b, n_t),
        in_specs=[col(2), col(3), col(4), col(5), col(6), col(7), halo(3), halo(4), rowblk, rowblk,
                  const(cw), const(cb), const(lg), const(lb), const(wcp), const(wnp), const(wo),
                  const(gf)],
        out_specs=rowblk,
        out_shape=jax.ShapeDtypeStruct((b * s, D_MODEL), F32),
        scratch_shapes=[pltpu.VMEM((n_cblk, ts + CONV_HALO + SUBLANES, LANES), F32),
                        pltpu.VMEM((n_cblk, ts, LANES), F32),
                        pltpu.VMEM((SUBLANES, ts + CONV_HALO, LANES), F32)],
        compiler_params=pltpu.CompilerParams(
            dimension_semantics=("parallel", "arbitrary"), vmem_limit_bytes=VMEM_LIMIT),
        name="conv_merge",
    )(proj_main, proj_main, proj_main, proj_main, proj_main, proj_main, proj_main, proj_main,
      o_nsa, x2, cw, cb, lg, lb, wcp, wnp, wo, gf)


def _t5_bucket_np(rel):
    rel = np.maximum(rel, 0)
    max_exact = REL_BUCKETS // 2
    relf = np.maximum(rel, 1).astype(np.float32)
    large = max_exact + (np.log(relf / np.float32(max_exact))
                         / np.float32(np.log(REL_MAX_DIST / max_exact))
                         * np.float32(REL_BUCKETS - max_exact)).astype(np.int32)
    large = np.minimum(large, REL_BUCKETS - 1)
    return np.where(rel < max_exact, rel, large)


def _pair_head_index():
    g = np.arange(N_GROUPS)[:, None, None]
    v = np.arange(2)[None, :, None]
    p = np.arange(PAIRS_PER_GROUP)[None, None, :]
    return g * HEADS_PER_GROUP + 2 * p + v


def _bias_lookup(rel_bias, rel):
    bucket = _t5_bucket_np(rel).reshape(-1)
    onehot = (jnp.arange(REL_BUCKETS)[:, None] == jnp.asarray(bucket)[None, :]).astype(F32)
    vals = jnp.dot(rel_bias.astype(F32).T * LOG2E, onehot, precision=lax.Precision.HIGHEST)
    vals = vals.reshape((N_HEADS,) + rel.shape)
    head = _pair_head_index()
    return jnp.stack([jnp.stack([jnp.concatenate([vals[h] for h in head[g, v]], axis=-1)
                                 for v in range(2)]) for g in range(N_GROUPS)])


def _bias_tables(rel_bias, s, n_sel):
    c = np.arange(LANES)[:, None]
    r = np.arange(Q_TILE)[None, :]
    far = rel_bias.astype(F32)[REL_BUCKETS - 1][_pair_head_index()] * LOG2E
    assert s // CMP_STRIDE <= LANES + Q_TILE // CMP_STRIDE
    rel_w = np.stack([dd * Q_TILE + r - c for dd in range(WIN_TILES)])
    assert (_t5_bucket_np(rel_w[2:]) == REL_BUCKETS - 1).all()
    rho = np.arange(2 * LANES)[:, None]
    rel_c = r - ((rho - LANES) * CMP_STRIDE + CMP_BLOCK - 1)
    vals = _bias_lookup(rel_bias, np.concatenate([rel_w, rel_c.reshape(2, LANES, Q_TILE)]))
    bias_w, bias_c = vals[:, :, :WIN_TILES], vals[:, :, WIN_TILES:].reshape(N_GROUPS, 2, 2 * LANES, COLS)
    cols = lambda a: np.tile(a, (1,) * (a.ndim - 1) + (PAIRS_PER_GROUP,))

    tw = jnp.where(cols((rel_w >= 0) & (rel_w < WINDOW)), bias_w, NEG)
    tw = jnp.concatenate([tw, jnp.full_like(tw[:, :, :1], NEG)], axis=2)
    far_cols = jnp.repeat(far, Q_TILE, axis=-1)[:, :, None, None, :]
    ts = jnp.where(cols(rel_w[:3] >= 0), bias_w[:, :, :3] - far_cols, NEG)
    bc = jnp.where(cols(rel_c >= 0), bias_c, NEG)
    hi = far.astype(BF16)
    lo = (far - hi.astype(F32)).astype(BF16)
    pieces = jnp.stack([hi[:, 0], lo[:, 0], hi[:, 1], lo[:, 1]], axis=-1)
    pieces = jnp.broadcast_to(pieces[:, :, None, :], (N_GROUPS, PAIRS_PER_GROUP, Q_TILE, 4))
    far_lanes = jnp.pad(pieces.reshape(N_GROUPS, COLS, 4),
                        ((0, 0), (0, 0), (n_sel, LANES - n_sel - 4)))
    return tw, ts, bc, far_lanes


def _overlap_t(s):
    n_cmp = (s - CMP_BLOCK) // CMP_STRIDE + 1
    cs = np.arange(LANES) * CMP_STRIDE
    ss = np.arange(s // SEL_BLOCK) * SEL_BLOCK
    ovt = ((cs[None, :] <= ss[:, None] + SEL_BLOCK - 1) & (cs[None, :] + CMP_BLOCK - 1 >= ss[:, None])
           & (np.arange(LANES)[None, :] < n_cmp))
    return jnp.asarray(ovt, BF16)


def _pad_w_in(w):
    used = NSA_WIDTH + 6 * KV_WIDTH + 3 * N_HEADS
    gap = (KV_COL_TILE + 1) * COL_TILE - used
    col = jnp.arange(w.shape[1])[None, :]
    w = jnp.where(col < NSA_WIDTH, w * (HEAD_DIM ** -0.5 * LOG2E), w)
    left = jnp.pad(w[:, :used], ((0, 0), (0, w.shape[1] - used + gap)))
    right = jnp.pad(w[:, used:], ((0, 0), (used + gap, 0)))
    return (left + right).astype(BF16)


def _compress_weights(pos, w1, w2):
    assert N_GROUPS == 2
    half = CMP_BLOCK // 2

    def per_group(w):
        zeros = [(0, 0)] * (w.ndim - 1)
        top = jnp.pad(w, zeros + [(0, w.shape[-1])])
        bottom = jnp.pad(w, zeros + [(w.shape[-1], 0)])
        return jnp.concatenate([top, bottom], axis=-2)

    w1g = per_group(w1.reshape(CMP_BLOCK, HEAD_DIM, CMP_HIDDEN))
    blk = lambda part: part.reshape(half * KV_WIDTH, N_GROUPS * CMP_HIDDEN)
    w1d = jnp.concatenate([blk(w1g[:half]), blk(w1g[half:])], axis=1).astype(BF16)
    w2d = per_group(w2).astype(BF16)
    tilepos = lambda part: jnp.broadcast_to(part[:, None, :], (half, N_GROUPS, HEAD_DIM)).reshape(1, -1)
    posd = jnp.concatenate([tilepos(pos[:half]), tilepos(pos[half:])], axis=0).astype(F32)
    return posd, w1d, w2d


def kernel(x, norm_in_g, w_in, pos_ck, w_ck1, w_ck2, pos_cv, w_cv1, w_cv2, rel_bias, conv_w, conv_b,
           conv_ln_g, conv_ln_b, w_conv_proj, w_nsa_proj, w_out, norm_f_g):
    b, s, d = x.shape
    n_sel = s // SEL_BLOCK
    assert d == D_MODEL and w_in.shape[0] == 1, "single-layer block with D_MODEL=1024"
    assert s % (2 * Q_TILE) == 0 and s // CMP_STRIDE <= LANES and s >= WINDOW
    assert n_sel + 4 <= LANES
    m = b * s
    x2 = x.reshape(m, d)
    row = lambda a: a.reshape(1, -1).astype(F32)

    proj_kv, proj_main = _input_projection(x2, row(norm_in_g[0]), _pad_w_in(w_in[0]), min(1024, m))

    chunks = s // CMP_STRIDE
    pk, w1k, w2k = _compress_weights(pos_ck[0], w_ck1[0], w_ck2[0])
    pv, w1v, w2v = _compress_weights(pos_cv[0], w_cv1[0], w_cv2[0])
    kcmp, vcmp = _compress(proj_kv, pk, pv, w1k, w1v, w2k, w2v, b, s)
    if chunks < LANES:
        padrows = ((0, 0), (0, LANES - chunks), (0, 0))
        kcmp, vcmp = jnp.pad(kcmp, padrows), jnp.pad(vcmp, padrows)

    bias_w, bias_s, bias_c, far_lanes = _bias_tables(rel_bias, s, n_sel)
    o_nsa = _attention(proj_main, proj_kv, kcmp, vcmp, bias_c, bias_w, bias_s, _overlap_t(s),
                       far_lanes, b, s, min(N_SELECT, n_sel))

    cw = jnp.pad(conv_w[0].astype(F32), ((0, CONV_HALO - CONV_KERNEL), (0, 0)))
    out = _merge(proj_main, o_nsa, x2, cw, row(conv_b[0]), row(conv_ln_g[0]), row(conv_ln_b[0]),
                 w_conv_proj[0].astype(BF16), w_nsa_proj[0].astype(BF16), w_out[0].astype(BF16),
                 row(norm_f_g), b, s, MERGE_ROWS)
    return out.reshape(b, s, d)
```

```python
import functools
import math

import numpy as np
import jax
import jax.numpy as jnp
from jax import lax
from jax.experimental import pallas as pl
from jax.experimental.pallas import tpu as pltpu

F32 = jnp.float32
BF16 = jnp.bfloat16

D_MODEL = 1024
N_HEADS = 16
N_GROUPS = 2
HEADS_PER_GROUP = N_HEADS // N_GROUPS
PAIRS_PER_GROUP = HEADS_PER_GROUP // 2
HEAD_DIM = 64
NSA_WIDTH = N_HEADS * HEAD_DIM
KV_WIDTH = N_GROUPS * HEAD_DIM
CMP_BLOCK = 32
CMP_STRIDE = 16
CMP_HIDDEN = 256
SEL_BLOCK = 64
N_SELECT = 8
WINDOW = 512
Q_TILE = 128
CONV_KERNEL = 31
CONV_HALO = 32
REL_BUCKETS = 32
REL_MAX_DIST = 128
EPS = 1e-6
NEG = -1e30
FORCE_SCORE = 1e6
LOG2E = math.log2(math.e)
LANES = 128
SUBLANES = 8
ONES_ROWS = 16
COLS = PAIRS_PER_GROUP * Q_TILE
COL_TILE = 1024
N_COL_TILES = 8
KV_COL_TILE = 1
MERGE_ROWS = 512
WIN_TILES = WINDOW // Q_TILE + 1
VMEM_LIMIT = 56 * 1024 * 1024


def _dot(a, b):
    return jnp.dot(a, b, preferred_element_type=F32)


def _dot_nt(a, b):
    return lax.dot_general(a, b, (((1,), (1,)), ((), ())), preferred_element_type=F32)


def _sigmoid(x):
    return 0.5 * jnp.tanh(0.5 * x) + 0.5


def _silu(x):
    h = 0.5 * x
    return h * jnp.tanh(h) + h


def _proj_kernel(x_ref, g_ref, w_ref, kv_ref, main_ref, h_ref):
    j = pl.program_id(1)

    @pl.when(j == 0)
    def _():
        x = x_ref[...]
        ms = jnp.mean(x * x, axis=-1, keepdims=True)
        h_ref[...] = ((x * lax.rsqrt(ms + EPS)) * g_ref[...]).astype(BF16)

    acc = _dot(h_ref[...], w_ref[...])
    main_ref[...] = acc.astype(BF16)

    @pl.when(j == KV_COL_TILE // 2)
    def _():
        kv_ref[...] = acc[:, (KV_COL_TILE % 2) * COL_TILE:(KV_COL_TILE % 2 + 1) * COL_TILE]


def _input_projection(x2, g, w_perm, tm):
    m = x2.shape[0]
    return pl.pallas_call(
        _proj_kernel,
        grid=(m // tm, N_COL_TILES // 2),
        in_specs=[
            pl.BlockSpec((tm, D_MODEL), lambda i, j: (i, 0)),
            pl.BlockSpec((1, D_MODEL), lambda i, j: (0, 0)),
            pl.BlockSpec((D_MODEL, 2 * COL_TILE), lambda i, j: (0, j)),
        ],
        out_specs=[
            pl.BlockSpec((tm, COL_TILE), lambda i, j: (i, 0)),
            pl.BlockSpec((tm, 2 * COL_TILE), lambda i, j: (i, j)),
        ],
        out_shape=[
            jax.ShapeDtypeStruct((m, COL_TILE), F32),
            jax.ShapeDtypeStruct((m, N_COL_TILES * COL_TILE), BF16),
        ],
        scratch_shapes=[pltpu.VMEM((tm, D_MODEL), BF16)],
        compiler_params=pltpu.CompilerParams(
            dimension_semantics=("parallel", "arbitrary"), vmem_limit_bytes=VMEM_LIMIT),
        name="input_projection",
    )(x2, g, w_perm)


def _compress_kernel(kf_ref, vf_ref, pk_ref, pv_ref, w1k_ref, w1v_ref, w2k_ref, w2v_ref,
                     kc_ref, vc_ref):
    def one(f_ref, pos_ref, w1_ref, w2_ref, o_ref):
        n = f_ref.shape[0] // CMP_STRIDE
        hw = N_GROUPS * CMP_HIDDEN
        first = jnp.zeros((n, hw), F32)
        second = jnp.zeros((n, hw), F32)
        for i in range(0, CMP_STRIDE, 2):
            tok = jnp.concatenate([f_ref[pl.ds(i + k, n, stride=CMP_STRIDE), :] for k in range(2)],
                                  axis=1)
            lanes = slice(i * KV_WIDTH, (i + 2) * KV_WIDTH)
            first = first + _dot((tok + pos_ref[0:1, lanes]).astype(BF16), w1_ref[lanes, 0:hw])
            second = second + _dot((tok + pos_ref[1:2, lanes]).astype(BF16), w1_ref[lanes, hw:2 * hw])
        hid = first + pltpu.roll(second, n - 1, axis=0)
        o_ref[0] = _dot(_silu(hid).astype(BF16), w2_ref[...])

    one(kf_ref, pk_ref, w1k_ref, w2k_ref, kc_ref)
    one(vf_ref, pv_ref, w1v_ref, w2v_ref, vc_ref)


def _compress(proj_kv, pk, pv, w1k, w1v, w2k, w2v, b, s):
    n = s // CMP_STRIDE
    const = lambda shape: pl.BlockSpec(shape, lambda i: (0,) * len(shape))
    kv_col = lambda c: pl.BlockSpec((s, KV_WIDTH), lambda i: (i, c))
    out = pl.BlockSpec((1, n, LANES), lambda i: (i, 0, 0))
    return pl.pallas_call(
        _compress_kernel,
        grid=(b,),
        in_specs=[kv_col(0), kv_col(1), const(pk.shape), const(pv.shape), const(w1k.shape),
                  const(w1v.shape), const(w2k.shape), const(w2v.shape)],
        out_specs=[out, out],
        out_shape=[jax.ShapeDtypeStruct((b, n, LANES), F32)] * 2,
        compiler_params=pltpu.CompilerParams(
            dimension_semantics=("parallel",), vmem_limit_bytes=VMEM_LIMIT),
        name="nsa_compress",
    )(proj_kv, proj_kv, pk, pv, w1k, w1v, w2k, w2v)


def _attn_kernel(q_ref, ks_ref, vs_ref, kw_ref, vw_ref, gt_ref, kc_ref, vc_ref, bc_ref, tw_ref,
                 ts_ref, ovt_ref, bx_ref, o_ref,
                 ksv, kwv, kcv, vst, vwt, vct, rhs, acc, obuf, s_a, s_b, s_w, *, n_top):
    qi = pl.program_id(1)
    n_kt = kwv.shape[1]
    n_sel = ovt_ref.shape[0]
    variants = 2 * N_GROUPS

    @pl.when(qi == 0)
    def _prepare_kv():
        def halves(k):
            lo = lax.broadcasted_iota(jnp.int32, k.shape, 1) < HEAD_DIM
            kr = pltpu.roll(k, HEAD_DIM, axis=1)
            z = jnp.zeros_like(k)
            c = lambda a: a.astype(BF16)
            return ((c(jnp.where(lo, k, z)), c(jnp.where(lo, z, kr))),
                    (c(jnp.where(lo, kr, z)), c(jnp.where(lo, z, k))))

        def extra_lanes(shape, v):
            pair = lax.broadcasted_iota(jnp.int32, shape, 0)
            r = lax.broadcasted_iota(jnp.int32, shape, 1)
            lane = lax.broadcasted_iota(jnp.int32, shape, 2)
            one = ((lane >= n_sel + 2 * v) & (lane < n_sel + 2 * v + 2)
                   | (lane == pair * (2 * Q_TILE // SEL_BLOCK) + r // SEL_BLOCK))
            return jnp.where(one, 1.0, 0.0).astype(BF16)

        pair_shape = (n_kt // 2, 2 * Q_TILE, LANES)
        tile_shape = (n_kt, Q_TILE, LANES)
        k_sel, k_win, k_cmp = halves(ks_ref[...]), halves(kw_ref[...]), halves(kc_ref[0])
        for g in range(N_GROUPS):
            for v in range(2):
                rows = slice(v * 2 * Q_TILE, (v + 1) * 2 * Q_TILE)
                ksv[g, :, rows, 0:LANES] = k_sel[g][v].reshape(pair_shape)
                ksv[g, :, rows, LANES:2 * LANES] = extra_lanes(pair_shape, v)
                kwv[g, :, v * Q_TILE:(v + 1) * Q_TILE, :] = k_win[g][v].reshape(tile_shape)
                kcv[g, v * LANES:(v + 1) * LANES, :] = k_cmp[g][v]
        vst[:, :, HEAD_DIM:, :] = jnp.ones((N_GROUPS, n_kt // 2, ONES_ROWS, 2 * Q_TILE), BF16)
        vwt[:, :, HEAD_DIM:, :] = jnp.ones((N_GROUPS, n_kt, ONES_ROWS, Q_TILE), BF16)
        for kt in range(n_kt):
            rows = slice(kt * Q_TILE, (kt + 1) * Q_TILE)
            half = slice((kt % 2) * Q_TILE, (kt % 2 + 1) * Q_TILE)
            vs_t = vs_ref[rows, :].T.astype(BF16)
            vw_t = vw_ref[rows, :].T.astype(BF16)
            for g in range(N_GROUPS):
                vst[g, kt // 2, 0:HEAD_DIM, half] = vs_t[g * HEAD_DIM:(g + 1) * HEAD_DIM]
                vwt[g, kt, 0:HEAD_DIM, :] = vw_t[g * HEAD_DIM:(g + 1) * HEAD_DIM]
        vct[...] = vc_ref[0].T.astype(BF16)

    qt = q_ref[...]
    gsig_t = _sigmoid(gt_ref[...]).T

    def col_max(s):
        return jnp.max(s.reshape(s.shape[0] // 8, 8, COLS), axis=0)

    def group_rows(g):
        return slice(g * HEAD_DIM, (g + 1) * HEAD_DIM)

    def online(gv, s, s_max, vt, m):
        m_new = jnp.maximum(m, jnp.max(s_max, axis=0, keepdims=True))
        e = jnp.exp2(s - m_new)
        acc[gv] = acc[gv] * jnp.exp2(m - m_new) + _dot(vt, e.astype(BF16))
        return m_new

    def fresh_state():
        acc[...] = jnp.zeros(acc.shape, F32)
        return tuple(jnp.full((1, COLS), NEG, F32) for _ in range(variants))

    def finish(branch):
        for g in range(N_GROUPS):
            halves = []
            for v in range(2):
                gv = g * 2 + v
                l = acc[gv, HEAD_DIM:HEAD_DIM + 1, :]
                halves.append(acc[gv, 0:HEAD_DIM, :] * (1.0 / jnp.maximum(l, 1e-30)))
            obuf[branch, g] = jnp.concatenate(halves, axis=0)

    groups = range(N_GROUPS)
    for g in groups:
        q4 = jnp.concatenate(
            [qt[:, (g * PAIRS_PER_GROUP + p) * LANES:(g * PAIRS_PER_GROUP + p + 1) * LANES]
             for p in range(PAIRS_PER_GROUP)], axis=0)
        rhs[g, :, 0:LANES] = q4

    bc_rows = pl.ds(pl.multiple_of(LANES - qi * (Q_TILE // CMP_STRIDE), SUBLANES), LANES)
    s_cmp = [_dot_nt(kcv[g], rhs[g, :, 0:LANES])
             + jnp.concatenate([bc_ref[g, 0, bc_rows, :], bc_ref[g, 1, bc_rows, :]], axis=0)
             for g in groups]
    psum_t = []
    for g in groups:
        tot = jnp.zeros((LANES, Q_TILE), F32)
        halves = []
        for v in range(2):
            s = s_cmp[g][v * LANES:(v + 1) * LANES]
            m = jnp.max(s, axis=0, keepdims=True)
            e = jnp.exp2(s - jnp.where(m > 0.5 * NEG, m, 0.0))
            p = e * (1.0 / jnp.maximum(jnp.sum(e, axis=0, keepdims=True), 1e-30))
            for pp in range(PAIRS_PER_GROUP):
                tot = tot + p[:, pp * Q_TILE:(pp + 1) * Q_TILE]
            halves.append(_dot(vct[group_rows(g), :], p.astype(BF16)))
        obuf[0, g] = jnp.concatenate(halves, axis=0)
        psum_t.append(tot)

    ovt = ovt_ref[...]
    imp_t = []
    for g in groups:
        p_hi = psum_t[g].astype(BF16)
        r1 = psum_t[g] - p_hi.astype(F32)
        p_mid = r1.astype(BF16)
        p_lo = (r1 - p_mid.astype(F32)).astype(BF16)
        imp_t.append(_dot(ovt, p_hi) + _dot(ovt, p_mid) + _dot(ovt, p_lo))

    j_idx = lax.broadcasted_iota(jnp.int32, (n_sel, Q_TILE), 0)
    r_idx = lax.broadcasted_iota(jnp.int32, (n_sel, Q_TILE), 1)
    blk_t = qi * (Q_TILE // SEL_BLOCK) + r_idx // SEL_BLOCK
    valid_blk = j_idx <= blk_t
    forced = (j_idx == 0) | (j_idx == blk_t) | (j_idx == blk_t - 1)
    prio = [jnp.where(valid_blk, jnp.where(forced, FORCE_SCORE, imp_t[g]), -FORCE_SCORE) for g in groups]
    rank = [jnp.zeros((n_sel, Q_TILE), F32) for _ in groups]
    for jj in range(n_sel):
        later = j_idx > jj
        for g in groups:
            row = prio[g][jj:jj + 1, :]
            beats = (row > prio[g]) | ((row == prio[g]) & later)
            rank[g] = rank[g] + jnp.where(beats, 1.0, 0.0)
    for g in groups:
        drop_t = jnp.where((rank[g] < n_top) & valid_blk, 0.0, NEG)
        drop_t = jnp.concatenate([drop_t, jnp.zeros((LANES - n_sel, Q_TILE), F32)], axis=0)
        drop = drop_t.T.astype(BF16)
        rhs[g, :, LANES:2 * LANES] = jnp.concatenate([drop] * PAIRS_PER_GROUP, axis=0) + bx_ref[g]

    pair_rows = 4 * Q_TILE

    def half_rows(v):
        return slice(v * 2 * Q_TILE, (v + 1) * 2 * Q_TILE)

    def max_rows(v):
        return slice(pair_rows + v * SUBLANES, pair_rows + (v + 1) * SUBLANES)

    def logits_group(buf, g, i, tables=None):
        s_all = _dot_nt(ksv[g, i], rhs[g])
        for v in range(2):
            s = s_all[half_rows(v)]
            if tables is not None:
                s = jnp.concatenate([s[h * Q_TILE:(h + 1) * Q_TILE] + ts_ref[g, v, tables[h]]
                                     for h in range(2)], axis=0)
            buf[g, half_rows(v), :] = s
            buf[g, max_rows(v), :] = col_max(s)

    def logits_into(buf, i):
        for g in range(N_GROUPS):
            logits_group(buf, g, i)

    def consume(buf, i, state):
        ms = list(state)
        for g in range(N_GROUPS):
            for v in range(2):
                gv = g * 2 + v
                ms[gv] = online(gv, buf[g, half_rows(v), :], buf[g, max_rows(v), :], vst[g, i], ms[gv])
        return tuple(ms)

    n_all = (qi + 2) // 2
    n_far = jnp.maximum(qi - 1, 0) // 2

    first = jnp.maximum(qi - (WIN_TILES - 1), 0)
    tiles = []
    for t in range(WIN_TILES):
        dd = qi - (first + t)
        table = jnp.where(dd >= 0, dd, WIN_TILES)
        tiles.append((jnp.minimum(first + t, n_kt - 1), table))
    m_win = []
    one_matmul = n_kt >= WIN_TILES
    for g in range(N_GROUPS):
        mx = [jnp.full((8, COLS), NEG, F32) for _ in range(2)]
        if one_matmul:
            k_rows = kwv[g, pl.ds(first, WIN_TILES)].reshape(WIN_TILES * 2 * Q_TILE, LANES)
            s_five = _dot_nt(k_rows, rhs[g, :, 0:LANES])
        for t, (kt, table) in enumerate(tiles):
            if one_matmul:
                s_all = s_five[t * 2 * Q_TILE:(t + 1) * 2 * Q_TILE]
            else:
                s_all = _dot_nt(kwv[g, kt], rhs[g, :, 0:LANES])
            for v in range(2):
                s = s_all[v * Q_TILE:(v + 1) * Q_TILE] + tw_ref[g, v, table]
                s_w[g, t, v * Q_TILE:(v + 1) * Q_TILE, :] = s
                mx[v] = jnp.maximum(mx[v], col_max(s))
        m_win += [jnp.max(mx[v], axis=0, keepdims=True) for v in range(2)]

    for g in range(N_GROUPS):
        logits_group(s_a, g, 0)
        halves = []
        for v in range(2):
            o = jnp.zeros((HEAD_DIM + ONES_ROWS, COLS), F32)
            for t, (kt, _) in enumerate(tiles):
                e = jnp.exp2(s_w[g, t, v * Q_TILE:(v + 1) * Q_TILE, :] - m_win[g * 2 + v])
                o = o + _dot(vwt[g, kt], e.astype(BF16))
            halves.append(o[0:HEAD_DIM] * (1.0 / jnp.maximum(o[HEAD_DIM:HEAD_DIM + 1], 1e-30)))
        obuf[2, g] = jnp.concatenate(halves, axis=0)

    def far_two(j, state):
        logits_into(s_b, 2 * j + 1)
        state = consume(s_a, 2 * j, state)
        logits_into(s_a, 2 * j + 2)
        return consume(s_b, 2 * j + 1, state)

    state = lax.fori_loop(0, n_far // 2, far_two, fresh_state())
    state = lax.fori_loop(0, n_far % 2, lambda _, st: consume(s_a, n_far - 1, st), state)

    def near_pair(i, state):
        tables = [jnp.clip(qi - (2 * i + h), 0, 2) for h in range(2)]
        for g in range(N_GROUPS):
            logits_group(s_b, g, i, tables)
        return consume(s_b, i, state)

    diag_alone = 1 - qi % 2
    state = lax.fori_loop(n_far, n_all - diag_alone, near_pair, state)

    def diagonal_tile(_, state):
        ms = list(state)
        i = n_all - 1
        for g in range(N_GROUPS):
            lhs = jnp.concatenate([ksv[g, i, v * 2 * Q_TILE:v * 2 * Q_TILE + Q_TILE, :] for v in range(2)],
                                  axis=0)
            s_all = _dot_nt(lhs, rhs[g])
            for v in range(2):
                s = s_all[v * Q_TILE:(v + 1) * Q_TILE] + ts_ref[g, v, 0]
                s_b[g, v * Q_TILE:(v + 1) * Q_TILE, :] = s
                s_b[g, max_rows(v), :] = col_max(s)
        for g in range(N_GROUPS):
            for v in range(2):
                gv = g * 2 + v
                ms[gv] = online(gv, s_b[g, v * Q_TILE:(v + 1) * Q_TILE, :], s_b[g, max_rows(v), :],
                                vst[g, i, :, 0:Q_TILE], ms[gv])
        return tuple(ms)

    lax.fori_loop(0, diag_alone, diagonal_tile, state)
    finish(1)

    for g in range(N_GROUPS):
        for p in range(PAIRS_PER_GROUP):
            halves = []
            for v in range(2):
                tot = jnp.zeros((HEAD_DIM, Q_TILE), F32)
                for br in range(3):
                    c = br * N_HEADS + g * HEADS_PER_GROUP + 2 * p + v
                    tot = tot + gsig_t[c:c + 1, :] * obuf[br, g, v * HEAD_DIM:(v + 1) * HEAD_DIM,
                                                          p * Q_TILE:(p + 1) * Q_TILE]
                halves.append(tot)
            col = (g * PAIRS_PER_GROUP + p) * LANES
            o_ref[:, col:col + LANES] = jnp.concatenate(halves, axis=0).T.astype(BF16)


def _attention(proj_main, proj_kv, kcmp, vcmp, bias_c, bias_w, bias_s, ovt, far_lanes, b, s, n_top):
    n_q = s // Q_TILE
    once = pl.Buffered(1)
    kv_col = lambda c: pl.BlockSpec((s, LANES), lambda bi, qi: (bi, c))
    cmp_spec = pl.BlockSpec((1, LANES, LANES), lambda bi, qi: (bi, 0, 0))
    return pl.pallas_call(
        functools.partial(_attn_kernel, n_top=n_top),
        grid=(b, n_q),
        in_specs=[
            pl.BlockSpec((Q_TILE, NSA_WIDTH), lambda bi, qi: (bi * n_q + qi, 0)),
            kv_col(2), kv_col(3), kv_col(4), kv_col(5),
            pl.BlockSpec((Q_TILE, LANES), lambda bi, qi: (bi * n_q + qi, 6)),
            cmp_spec, cmp_spec,
            pl.BlockSpec(bias_c.shape, lambda bi, qi: (0,) * 4, pipeline_mode=once),
            pl.BlockSpec(bias_w.shape, lambda bi, qi: (0,) * 5, pipeline_mode=once),
            pl.BlockSpec(bias_s.shape, lambda bi, qi: (0,) * 5, pipeline_mode=once),
            pl.BlockSpec(ovt.shape, lambda bi, qi: (0, 0), pipeline_mode=once),
            pl.BlockSpec(far_lanes.shape, lambda bi, qi: (0, 0, 0), pipeline_mode=once),
        ],
        out_specs=pl.BlockSpec((Q_TILE, NSA_WIDTH), lambda bi, qi: (bi * n_q + qi, 0)),
        out_shape=jax.ShapeDtypeStruct((b * s, NSA_WIDTH), BF16),
        scratch_shapes=[
            pltpu.VMEM((N_GROUPS, n_q // 2, 4 * Q_TILE, 2 * LANES), BF16),
            pltpu.VMEM((N_GROUPS, n_q, 2 * Q_TILE, LANES), BF16),
            pltpu.VMEM((N_GROUPS, 2 * LANES, LANES), BF16),
            pltpu.VMEM((N_GROUPS, n_q // 2, HEAD_DIM + ONES_ROWS, 2 * Q_TILE), BF16),
            pltpu.VMEM((N_GROUPS, n_q, HEAD_DIM + ONES_ROWS, Q_TILE), BF16),
            pltpu.VMEM((LANES, LANES), BF16),
            pltpu.VMEM((N_GROUPS, COLS, 2 * LANES), BF16),
            pltpu.VMEM((2 * N_GROUPS, HEAD_DIM + ONES_ROWS, COLS), F32),
            pltpu.VMEM((3, N_GROUPS, LANES, COLS), F32),
            pltpu.VMEM((N_GROUPS, 4 * Q_TILE + 2 * SUBLANES, COLS), F32),
            pltpu.VMEM((N_GROUPS, 4 * Q_TILE + 2 * SUBLANES, COLS), F32),
            pltpu.VMEM((N_GROUPS, WIN_TILES, 2 * Q_TILE, COLS), F32),
        ],
        compiler_params=pltpu.CompilerParams(
            dimension_semantics=("parallel", "arbitrary"), vmem_limit_bytes=VMEM_LIMIT),
        name="nsa_attention",
    )(proj_main, proj_kv, proj_kv, proj_kv, proj_kv, proj_kv, kcmp, vcmp, bias_c, bias_w, bias_s,
      ovt, far_lanes)


def _merge_kernel(zn_ref, a_ref, b_ref, zc_ref, gc_ref, gn_ref, ah_ref, bh_ref, on_ref, x_ref,
                  cw_ref, cb_ref, lg_ref, lb_ref, wcp_ref, wnp_ref, wo_ref, gf_ref, out_ref, uext,
                  conv, shifted):
    i = pl.program_id(1)
    ts = a_ref.shape[0]
    f = lambda r: r[...].astype(F32)

    n_cblk = D_MODEL // LANES
    u_halo = jnp.where(i > 0, f(ah_ref) * _sigmoid(f(bh_ref)), 0.0)
    u = f(a_ref) * _sigmoid(f(b_ref))
    for cblk in range(n_cblk):
        cols = slice(cblk * LANES, (cblk + 1) * LANES)
        uext[cblk, 0:CONV_HALO, :] = u_halo[:, cols]
        uext[cblk, CONV_HALO:CONV_HALO + ts, :] = u[:, cols]
        uext[cblk, CONV_HALO + ts:, :] = jnp.zeros((SUBLANES, LANES), F32)

    lead = CONV_HALO - (CONV_KERNEL - 1)
    chunk = Q_TILE

    def conv_block(cblk, carry):
        for shift in range(SUBLANES):
            shifted[shift] = uext[cblk, shift:shift + ts + CONV_HALO, :]
        w = cw_ref[cblk]
        for h in range(ts // chunk):
            c = jnp.broadcast_to(cb_ref[cblk], (chunk, LANES))
            for shift in range(SUBLANES):
                xs = shifted[shift, h * chunk:h * chunk + chunk + CONV_HALO, :]
                for j in range(CONV_KERNEL):
                    if (lead + j) % SUBLANES == shift:
                        base = lead + j - shift
                        c = c + w[j:j + 1, :] * xs[base:base + chunk]
            conv[cblk, h * chunk:(h + 1) * chunk, :] = c
        return carry

    lax.fori_loop(0, n_cblk, conv_block, 0)
    c = jnp.concatenate([conv[cblk] for cblk in range(n_cblk)], axis=1)

    mu = jnp.mean(c, axis=-1, keepdims=True)
    cc = c - mu
    var = jnp.mean(cc * cc, axis=-1, keepdims=True)
    y = (cc * lax.rsqrt(var + EPS)) * lg_ref[...] + lb_ref[...]
    conv_act = _silu(y) * _silu(f(zc_ref))
    y_conv = _dot(conv_act.astype(BF16), wcp_ref[...])

    nsa_act = f(on_ref) * _silu(f(zn_ref))
    y_nsa = _dot(nsa_act.astype(BF16), wnp_ref[...])

    merged = _sigmoid(f(gc_ref)) * y_conv + _sigmoid(f(gn_ref)) * y_nsa
    xo = x_ref[...] + _dot(merged.astype(BF16), wo_ref[...])
    ms = jnp.mean(xo * xo, axis=-1, keepdims=True)
    out_ref[...] = (xo * lax.rsqrt(ms + EPS)) * gf_ref[...]


def _merge(proj_main, o_nsa, x2, cw, cb, lg, lb, wcp, wnp, wo, gf, b, s, ts):
    n_t = s // ts
    n_cblk = D_MODEL // LANES
    halo_per_tile = ts // CONV_HALO
    cw = jnp.transpose(cw.reshape(cw.shape[0], n_cblk, LANES), (1, 0, 2))
    cb = cb.reshape(n_cblk, 1, LANES)
    col = lambda c: pl.BlockSpec((ts, COL_TILE), lambda bi, ti: (bi * n_t + ti, c))
    halo = lambda c: pl.BlockSpec(
        (CONV_HALO, COL_TILE),
        lambda bi, ti: (jnp.maximum((bi * n_t + ti) * halo_per_tile - 1, 0), c))
    const = lambda a: pl.BlockSpec(a.shape, lambda bi, ti: (0,) * a.ndim, pipeline_mode=pl.Buffered(1))
    rowblk = pl.BlockSpec((ts, D_MODEL), lambda bi, ti: (bi * n_t + ti, 0))
    return pl.pallas_call(
        _merge_kernel,
        grid=(b, n_t),
        in_specs=[col(2), col(3), col(4), col(5), col(6), col(7), halo(3), halo(4), rowblk, rowblk,
                  const(cw), const(cb), const(lg), const(lb), const(wcp), const(wnp), const(wo),
                  const(gf)],
        out_specs=rowblk,
        out_shape=jax.ShapeDtypeStruct((b * s, D_MODEL), F32),
        scratch_shapes=[pltpu.VMEM((n_cblk, ts + CONV_HALO + SUBLANES, LANES), F32),
                        pltpu.VMEM((n_cblk, ts, LANES), F32),
                        pltpu.VMEM((SUBLANES, ts + CONV_HALO, LANES), F32)],
        compiler_params=pltpu.CompilerParams(
            dimension_semantics=("parallel", "arbitrary"), vmem_limit_bytes=VMEM_LIMIT),
        name="conv_merge",
    )(proj_main, proj_main, proj_main, proj_main, proj_main, proj_main, proj_main, proj_main,
      o_nsa, x2, cw, cb, lg, lb, wcp, wnp, wo, gf)


def _t5_bucket_np(rel):
    rel = np.maximum(rel, 0)
    max_exact = REL_BUCKETS // 2
    relf = np.maximum(rel, 1).astype(np.float32)
    large = max_exact + (np.log(relf / np.float32(max_exact))
                         / np.float32(np.log(REL_MAX_DIST / max_exact))
                         * np.float32(REL_BUCKETS - max_exact)).astype(np.int32)
    large = np.minimum(large, REL_BUCKETS - 1)
    return np.where(rel < max_exact, rel, large)


def _pair_head_index():
    g = np.arange(N_GROUPS)[:, None, None]
    v = np.arange(2)[None, :, None]
    p = np.arange(PAIRS_PER_GROUP)[None, None, :]
    return g * HEADS_PER_GROUP + 2 * p + v


def _bias_lookup(rel_bias, rel):
    bucket = _t5_bucket_np(rel).reshape(-1)
    onehot = (jnp.arange(REL_BUCKETS)[:, None] == jnp.asarray(bucket)[None, :]).astype(F32)
    vals = jnp.dot(rel_bias.astype(F32).T * LOG2E, onehot, precision=lax.Precision.HIGHEST)
    vals = vals.reshape((N_HEADS,) + rel.shape)
    head = _pair_head_index()
    return jnp.stack([jnp.stack([jnp.concatenate([vals[h] for h in head[g, v]], axis=-1)
                                 for v in range(2)]) for g in range(N_GROUPS)])


def _bias_tables(rel_bias, s, n_sel):
    c = np.arange(LANES)[:, None]
    r = np.arange(Q_TILE)[None, :]
    far = rel_bias.astype(F32)[REL_BUCKETS - 1][_pair_head_index()] * LOG2E
    assert s // CMP_STRIDE <= LANES + Q_TILE // CMP_STRIDE
    rel_w = np.stack([dd * Q_TILE + r - c for dd in range(WIN_TILES)])
    assert (_t5_bucket_np(rel_w[2:]) == REL_BUCKETS - 1).all()
    rho = np.arange(2 * LANES)[:, None]
    rel_c = r - ((rho - LANES) * CMP_STRIDE + CMP_BLOCK - 1)
    vals = _bias_lookup(rel_bias, np.concatenate([rel_w, rel_c.reshape(2, LANES, Q_TILE)]))
    bias_w, bias_c = vals[:, :, :WIN_TILES], vals[:, :, WIN_TILES:].reshape(N_GROUPS, 2, 2 * LANES, COLS)
    cols = lambda a: np.tile(a, (1,) * (a.ndim - 1) + (PAIRS_PER_GROUP,))

    tw = jnp.where(cols((rel_w >= 0) & (rel_w < WINDOW)), bias_w, NEG)
    tw = jnp.concatenate([tw, jnp.full_like(tw[:, :, :1], NEG)], axis=2)
    far_cols = jnp.repeat(far, Q_TILE, axis=-1)[:, :, None, None, :]
    ts = jnp.where(cols(rel_w[:3] >= 0), bias_w[:, :, :3] - far_cols, NEG)
    bc = jnp.where(cols(rel_c >= 0), bias_c, NEG)
    hi = far.astype(BF16)
    lo = (far - hi.astype(F32)).astype(BF16)
    pieces = jnp.stack([hi[:, 0], lo[:, 0], hi[:, 1], lo[:, 1]], axis=-1)
    pieces = jnp.broadcast_to(pieces[:, :, None, :], (N_GROUPS, PAIRS_PER_GROUP, Q_TILE, 4))
    far_lanes = jnp.pad(pieces.reshape(N_GROUPS, COLS, 4),
                        ((0, 0), (0, 0), (n_sel, LANES - n_sel - 4)))
    return tw, ts, bc, far_lanes


def _overlap_t(s):
    n_cmp = (s - CMP_BLOCK) // CMP_STRIDE + 1
    cs = np.arange(LANES) * CMP_STRIDE
    ss = np.arange(s // SEL_BLOCK) * SEL_BLOCK
    ovt = ((cs[None, :] <= ss[:, None] + SEL_BLOCK - 1) & (cs[None, :] + CMP_BLOCK - 1 >= ss[:, None])
           & (np.arange(LANES)[None, :] < n_cmp))
    return jnp.asarray(ovt, BF16)


def _pad_w_in(w):
    used = NSA_WIDTH + 6 * KV_WIDTH + 3 * N_HEADS
    gap = (KV_COL_TILE + 1) * COL_TILE - used
    col = jnp.arange(w.shape[1])[None, :]
    w = jnp.where(col < NSA_WIDTH, w * (HEAD_DIM ** -0.5 * LOG2E), w)
    left = jnp.pad(w[:, :used], ((0, 0), (0, w.shape[1] - used + gap)))
    right = jnp.pad(w[:, used:], ((0, 0), (used + gap, 0)))
    return (left + right).astype(BF16)


def _compress_weights(pos, w1, w2):
    assert N_GROUPS == 2
    half = CMP_BLOCK // 2

    def per_group(w):
        zeros = [(0, 0)] * (w.ndim - 1)
        top = jnp.pad(w, zeros + [(0, w.shape[-1])])
        bottom = jnp.pad(w, zeros + [(w.shape[-1], 0)])
        return jnp.concatenate([top, bottom], axis=-2)

    w1g = per_group(w1.reshape(CMP_BLOCK, HEAD_DIM, CMP_HIDDEN))
    blk = lambda part: part.reshape(half * KV_WIDTH, N_GROUPS * CMP_HIDDEN)
    w1d = jnp.concatenate([blk(w1g[:half]), blk(w1g[half:])], axis=1).astype(BF16)
    w2d = per_group(w2).astype(BF16)
    tilepos = lambda part: jnp.broadcast_to(part[:, None, :], (half, N_GROUPS, HEAD_DIM)).reshape(1, -1)
    posd = jnp.concatenate([tilepos(pos[:half]), tilepos(pos[half:])], axis=0).astype(F32)
    return posd, w1d, w2d


def kernel(x, norm_in_g, w_in, pos_ck, w_ck1, w_ck2, pos_cv, w_cv1, w_cv2, rel_bias, conv_w, conv_b,
           conv_ln_g, conv_ln_b, w_conv_proj, w_nsa_proj, w_out, norm_f_g):
    b, s, d = x.shape
    n_sel = s // SEL_BLOCK
    assert d == D_MODEL and w_in.shape[0] == 1, "single-layer block with D_MODEL=1024"
    assert s % (2 * Q_TILE) == 0 and s // CMP_STRIDE <= LANES and s >= WINDOW
    assert n_sel + 4 <= LANES
    m = b * s
    x2 = x.reshape(m, d)
    row = lambda a: a.reshape(1, -1).astype(F32)

    proj_kv, proj_main = _input_projection(x2, row(norm_in_g[0]), _pad_w_in(w_in[0]), min(1024, m))

    chunks = s // CMP_STRIDE
    pk, w1k, w2k = _compress_weights(pos_ck[0], w_ck1[0], w_ck2[0])
    pv, w1v, w2v = _compress_weights(pos_cv[0], w_cv1[0], w_cv2[0])
    kcmp, vcmp = _compress(proj_kv, pk, pv, w1k, w1v, w2k, w2v, b, s)
    if chunks < LANES:
        padrows = ((0, 0), (0, LANES - chunks), (0, 0))
        kcmp, vcmp = jnp.pad(kcmp, padrows), jnp.pad(vcmp, padrows)

    bias_w, bias_s, bias_c, far_lanes = _bias_tables(rel_bias, s, n_sel)
    o_nsa = _attention(proj_main, proj_kv, kcmp, vcmp, bias_c, bias_w, bias_s, _overlap_t(s),
                       far_lanes, b, s, min(N_SELECT, n_sel))

    cw = jnp.pad(conv_w[0].astype(F32), ((0, CONV_HALO - CONV_KERNEL), (0, 0)))
    out = _merge(proj_main, o_nsa, x2, cw, row(conv_b[0]), row(conv_ln_g[0]), row(conv_ln_b[0]),
                 w_conv_proj[0].astype(BF16), w_nsa_proj[0].astype(BF16), w_out[0].astype(BF16),
                 row(norm_f_g), b, s, MERGE_ROWS)
    return out.reshape(b, s, d)
```

```python
import functools
import math

import numpy as np
import jax
import jax.numpy as jnp
from jax import lax
from jax.experimental import pallas as pl
from jax.experimental.pallas import tpu as pltpu

F32 = jnp.float32
BF16 = jnp.bfloat16

D_MODEL = 1024
N_HEADS = 16
N_GROUPS = 2
HEADS_PER_GROUP = N_HEADS // N_GROUPS
PAIRS_PER_GROUP = HEADS_PER_GROUP // 2
HEAD_DIM = 64
NSA_WIDTH = N_HEADS * HEAD_DIM
KV_WIDTH = N_GROUPS * HEAD_DIM
CMP_BLOCK = 32
CMP_STRIDE = 16
CMP_HIDDEN = 256
SEL_BLOCK = 64
N_SELECT = 8
WINDOW = 512
Q_TILE = 128
CONV_KERNEL = 31
CONV_HALO = 32
REL_BUCKETS = 32
REL_MAX_DIST = 128
EPS = 1e-6
NEG = -1e30
FORCE_SCORE = 1e6
LOG2E = math.log2(math.e)
LANES = 128
SUBLANES = 8
ONES_ROWS = 16
COLS = PAIRS_PER_GROUP * Q_TILE
COL_TILE = 1024
N_COL_TILES = 8
KV_COL_TILE = 1
MERGE_ROWS = 512
TILES_PER_STEP = 2
WIN_TILES = WINDOW // Q_TILE + 1
VMEM_LIMIT = 56 * 1024 * 1024


def _dot(a, b):
    return jnp.dot(a, b, preferred_element_type=F32)


def _dot_nt(a, b):
    return lax.dot_general(a, b, (((1,), (1,)), ((), ())), preferred_element_type=F32)


def _sigmoid(x):
    return 0.5 * jnp.tanh(0.5 * x) + 0.5


def _silu(x):
    h = 0.5 * x
    return h * jnp.tanh(h) + h


def _proj_kernel(x_ref, g_ref, w_ref, kv_ref, main_ref, h_ref):
    j = pl.program_id(1)

    @pl.when(j == 0)
    def _():
        x = x_ref[...]
        ms = jnp.mean(x * x, axis=-1, keepdims=True)
        h_ref[...] = ((x * lax.rsqrt(ms + EPS)) * g_ref[...]).astype(BF16)

    acc = _dot(h_ref[...], w_ref[...])
    main_ref[...] = acc.astype(BF16)

    @pl.when(j == KV_COL_TILE // 2)
    def _():
        kv_ref[...] = acc[:, (KV_COL_TILE % 2) * COL_TILE:(KV_COL_TILE % 2 + 1) * COL_TILE]


def _input_projection(x2, g, w_perm, tm):
    m = x2.shape[0]
    return pl.pallas_call(
        _proj_kernel,
        grid=(m // tm, N_COL_TILES // 2),
        in_specs=[
            pl.BlockSpec((tm, D_MODEL), lambda i, j: (i, 0)),
            pl.BlockSpec((1, D_MODEL), lambda i, j: (0, 0)),
            pl.BlockSpec((D_MODEL, 2 * COL_TILE), lambda i, j: (0, j)),
        ],
        out_specs=[
            pl.BlockSpec((tm, COL_TILE), lambda i, j: (i, 0)),
            pl.BlockSpec((tm, 2 * COL_TILE), lambda i, j: (i, j)),
        ],
        out_shape=[
            jax.ShapeDtypeStruct((m, COL_TILE), F32),
            jax.ShapeDtypeStruct((m, N_COL_TILES * COL_TILE), BF16),
        ],
        scratch_shapes=[pltpu.VMEM((tm, D_MODEL), BF16)],
        compiler_params=pltpu.CompilerParams(
            dimension_semantics=("parallel", "arbitrary"), vmem_limit_bytes=VMEM_LIMIT),
        name="input_projection",
    )(x2, g, w_perm)


def _compress_kernel(kf_ref, vf_ref, pk_ref, pv_ref, w1k_ref, w1v_ref, w2k_ref, w2v_ref,
                     kc_ref, vc_ref):
    def one(f_ref, pos_ref, w1_ref, w2_ref, o_ref):
        n = f_ref.shape[0] // CMP_STRIDE
        hw = N_GROUPS * CMP_HIDDEN
        first = jnp.zeros((n, hw), F32)
        second = jnp.zeros((n, hw), F32)
        for i in range(0, CMP_STRIDE, 2):
            tok = jnp.concatenate([f_ref[pl.ds(i + k, n, stride=CMP_STRIDE), :] for k in range(2)],
                                  axis=1)
            lanes = slice(i * KV_WIDTH, (i + 2) * KV_WIDTH)
            first = first + _dot((tok + pos_ref[0:1, lanes]).astype(BF16), w1_ref[lanes, 0:hw])
            second = second + _dot((tok + pos_ref[1:2, lanes]).astype(BF16), w1_ref[lanes, hw:2 * hw])
        hid = first + pltpu.roll(second, n - 1, axis=0)
        o_ref[0] = _dot(_silu(hid).astype(BF16), w2_ref[...])

    one(kf_ref, pk_ref, w1k_ref, w2k_ref, kc_ref)
    one(vf_ref, pv_ref, w1v_ref, w2v_ref, vc_ref)


def _compress(proj_kv, pk, pv, w1k, w1v, w2k, w2v, b, s):
    n = s // CMP_STRIDE
    const = lambda shape: pl.BlockSpec(shape, lambda i: (0,) * len(shape))
    kv_col = lambda c: pl.BlockSpec((s, KV_WIDTH), lambda i: (i, c))
    out = pl.BlockSpec((1, n, LANES), lambda i: (i, 0, 0))
    return pl.pallas_call(
        _compress_kernel,
        grid=(b,),
        in_specs=[kv_col(0), kv_col(1), const(pk.shape), const(pv.shape), const(w1k.shape),
                  const(w1v.shape), const(w2k.shape), const(w2v.shape)],
        out_specs=[out, out],
        out_shape=[jax.ShapeDtypeStruct((b, n, LANES), F32)] * 2,
        compiler_params=pltpu.CompilerParams(
            dimension_semantics=("parallel",), vmem_limit_bytes=VMEM_LIMIT),
        name="nsa_compress",
    )(proj_kv, proj_kv, pk, pv, w1k, w1v, w2k, w2v)


def _attn_kernel(q_ref, ks_ref, vs_ref, kw_ref, vw_ref, gt_ref, kc_ref, vc_ref, bc_ref, tw_ref,
                 ts_ref, ovt_ref, bx_ref, o_ref,
                 ksv, kwv, kcv, vst, vwt, vct, rhs, acc, obuf, s_a, s_b, s_w, *, n_top):
    step = pl.program_id(1)
    n_kt = kwv.shape[1]
    n_sel = ovt_ref.shape[0]

    @pl.when(step == 0)
    def _prepare_kv():
        def halves(k):
            lo = lax.broadcasted_iota(jnp.int32, k.shape, 1) < HEAD_DIM
            kr = pltpu.roll(k, HEAD_DIM, axis=1)
            z = jnp.zeros_like(k)
            c = lambda a: a.astype(BF16)
            return ((c(jnp.where(lo, k, z)), c(jnp.where(lo, z, kr))),
                    (c(jnp.where(lo, kr, z)), c(jnp.where(lo, z, k))))

        def extra_lanes(shape, v):
            pair = lax.broadcasted_iota(jnp.int32, shape, 0)
            r = lax.broadcasted_iota(jnp.int32, shape, 1)
            lane = lax.broadcasted_iota(jnp.int32, shape, 2)
            one = ((lane >= n_sel + 2 * v) & (lane < n_sel + 2 * v + 2)
                   | (lane == pair * (2 * Q_TILE // SEL_BLOCK) + r // SEL_BLOCK))
            return jnp.where(one, 1.0, 0.0).astype(BF16)

        pair_shape = (n_kt // 2, 2 * Q_TILE, LANES)
        tile_shape = (n_kt, Q_TILE, LANES)
        k_sel, k_win, k_cmp = halves(ks_ref[...]), halves(kw_ref[...]), halves(kc_ref[0])
        for g in range(N_GROUPS):
            for v in range(2):
                rows = slice(v * 2 * Q_TILE, (v + 1) * 2 * Q_TILE)
                ksv[g, :, rows, 0:LANES] = k_sel[g][v].reshape(pair_shape)
                ksv[g, :, rows, LANES:2 * LANES] = extra_lanes(pair_shape, v)
                kwv[g, :, v * Q_TILE:(v + 1) * Q_TILE, :] = k_win[g][v].reshape(tile_shape)
                kcv[g, v * LANES:(v + 1) * LANES, :] = k_cmp[g][v]
        vst[:, :, HEAD_DIM:, :] = jnp.ones((N_GROUPS, n_kt // 2, ONES_ROWS, 2 * Q_TILE), BF16)
        vwt[:, :, HEAD_DIM:, :] = jnp.ones((N_GROUPS, n_kt, ONES_ROWS, Q_TILE), BF16)
        for kt in range(n_kt):
            rows = slice(kt * Q_TILE, (kt + 1) * Q_TILE)
            half = slice((kt % 2) * Q_TILE, (kt % 2 + 1) * Q_TILE)
            vs_t = vs_ref[rows, :].T.astype(BF16)
            vw_t = vw_ref[rows, :].T.astype(BF16)
            for g in range(N_GROUPS):
                vst[g, kt // 2, 0:HEAD_DIM, half] = vs_t[g * HEAD_DIM:(g + 1) * HEAD_DIM]
                vwt[g, kt, 0:HEAD_DIM, :] = vw_t[g * HEAD_DIM:(g + 1) * HEAD_DIM]
        vct[...] = vc_ref[0].T.astype(BF16)

    for sub in range(TILES_PER_STEP):
        _attn_tile(step * TILES_PER_STEP + sub, slice(sub * Q_TILE, (sub + 1) * Q_TILE),
                   q_ref, gt_ref, bc_ref, tw_ref, ts_ref, ovt_ref, bx_ref, o_ref,
                   ksv, kwv, kcv, vst, vwt, vct, rhs, acc, obuf, s_a, s_b, s_w, n_top)


def _attn_tile(qi, rows, q_ref, gt_ref, bc_ref, tw_ref, ts_ref, ovt_ref, bx_ref, o_ref,
               ksv, kwv, kcv, vst, vwt, vct, rhs, acc, obuf, s_a, s_b, s_w, n_top):
    n_kt = kwv.shape[1]
    n_sel = ovt_ref.shape[0]
    variants = 2 * N_GROUPS
    qt = q_ref[rows, :]
    gsig_t = _sigmoid(gt_ref[rows, :]).T

    def col_max(s):
        return jnp.max(s.reshape(s.shape[0] // 8, 8, COLS), axis=0)

    def group_rows(g):
        return slice(g * HEAD_DIM, (g + 1) * HEAD_DIM)

    def online(gv, s, s_max, vt, m):
        m_new = jnp.maximum(m, jnp.max(s_max, axis=0, keepdims=True))
        e = jnp.exp2(s - m_new)
        acc[gv] = acc[gv] * jnp.exp2(m - m_new) + _dot(vt, e.astype(BF16))
        return m_new

    def fresh_state():
        acc[...] = jnp.zeros(acc.shape, F32)
        return tuple(jnp.full((1, COLS), NEG, F32) for _ in range(variants))

    def finish(branch):
        for g in range(N_GROUPS):
            halves = []
            for v in range(2):
                gv = g * 2 + v
                l = acc[gv, HEAD_DIM:HEAD_DIM + 1, :]
                halves.append(acc[gv, 0:HEAD_DIM, :] * (1.0 / jnp.maximum(l, 1e-30)))
            obuf[branch, g] = jnp.concatenate(halves, axis=0)

    groups = range(N_GROUPS)
    for g in groups:
        q4 = jnp.concatenate(
            [qt[:, (g * PAIRS_PER_GROUP + p) * LANES:(g * PAIRS_PER_GROUP + p + 1) * LANES]
             for p in range(PAIRS_PER_GROUP)], axis=0)
        rhs[g, :, 0:LANES] = q4

    bc_rows = pl.ds(pl.multiple_of(LANES - qi * (Q_TILE // CMP_STRIDE), SUBLANES), LANES)
    s_cmp = [_dot_nt(kcv[g], rhs[g, :, 0:LANES])
             + jnp.concatenate([bc_ref[g, 0, bc_rows, :], bc_ref[g, 1, bc_rows, :]], axis=0)
             for g in groups]
    psum_t = []
    for g in groups:
        tot = jnp.zeros((LANES, Q_TILE), F32)
        halves = []
        for v in range(2):
            s = s_cmp[g][v * LANES:(v + 1) * LANES]
            m = jnp.max(s, axis=0, keepdims=True)
            e = jnp.exp2(s - jnp.where(m > 0.5 * NEG, m, 0.0))
            p = e * (1.0 / jnp.maximum(jnp.sum(e, axis=0, keepdims=True), 1e-30))
            for pp in range(PAIRS_PER_GROUP):
                tot = tot + p[:, pp * Q_TILE:(pp + 1) * Q_TILE]
            halves.append(_dot(vct[group_rows(g), :], p.astype(BF16)))
        obuf[0, g] = jnp.concatenate(halves, axis=0)
        psum_t.append(tot)

    ovt = ovt_ref[...]
    imp_t = []
    for g in groups:
        p_hi = psum_t[g].astype(BF16)
        r1 = psum_t[g] - p_hi.astype(F32)
        p_mid = r1.astype(BF16)
        p_lo = (r1 - p_mid.astype(F32)).astype(BF16)
        imp_t.append(_dot(ovt, p_hi) + _dot(ovt, p_mid) + _dot(ovt, p_lo))

    j_idx = lax.broadcasted_iota(jnp.int32, (n_sel, Q_TILE), 0)
    r_idx = lax.broadcasted_iota(jnp.int32, (n_sel, Q_TILE), 1)
    blk_t = qi * (Q_TILE // SEL_BLOCK) + r_idx // SEL_BLOCK
    valid_blk = j_idx <= blk_t
    forced = (j_idx == 0) | (j_idx == blk_t) | (j_idx == blk_t - 1)
    prio = [jnp.where(valid_blk, jnp.where(forced, FORCE_SCORE, imp_t[g]), -FORCE_SCORE) for g in groups]
    rank = [jnp.zeros((n_sel, Q_TILE), F32) for _ in groups]
    for jj in range(n_sel):
        later = j_idx > jj
        for g in groups:
            row = prio[g][jj:jj + 1, :]
            beats = (row > prio[g]) | ((row == prio[g]) & later)
            rank[g] = rank[g] + jnp.where(beats, 1.0, 0.0)
    for g in groups:
        drop_t = jnp.where((rank[g] < n_top) & valid_blk, 0.0, NEG)
        drop_t = jnp.concatenate([drop_t, jnp.zeros((LANES - n_sel, Q_TILE), F32)], axis=0)
        drop = drop_t.T.astype(BF16)
        rhs[g, :, LANES:2 * LANES] = jnp.concatenate([drop] * PAIRS_PER_GROUP, axis=0) + bx_ref[g]

    pair_rows = 4 * Q_TILE

    def half_rows(v):
        return slice(v * 2 * Q_TILE, (v + 1) * 2 * Q_TILE)

    def max_rows(v):
        return slice(pair_rows + v * SUBLANES, pair_rows + (v + 1) * SUBLANES)

    def logits_group(buf, g, i, tables=None):
        s_all = _dot_nt(ksv[g, i], rhs[g])
        for v in range(2):
            s = s_all[half_rows(v)]
            if tables is not None:
                s = jnp.concatenate([s[h * Q_TILE:(h + 1) * Q_TILE] + ts_ref[g, v, tables[h]]
                                     for h in range(2)], axis=0)
            buf[g, half_rows(v), :] = s
            buf[g, max_rows(v), :] = col_max(s)

    def logits_into(buf, i):
        for g in range(N_GROUPS):
            logits_group(buf, g, i)

    def consume(buf, i, state):
        ms = list(state)
        for g in range(N_GROUPS):
            for v in range(2):
                gv = g * 2 + v
                ms[gv] = online(gv, buf[g, half_rows(v), :], buf[g, max_rows(v), :], vst[g, i], ms[gv])
        return tuple(ms)

    n_all = (qi + 2) // 2
    n_far = jnp.maximum(qi - 1, 0) // 2

    first = jnp.maximum(qi - (WIN_TILES - 1), 0)
    tiles = []
    for t in range(WIN_TILES):
        dd = qi - (first + t)
        table = jnp.where(dd >= 0, dd, WIN_TILES)
        tiles.append((jnp.minimum(first + t, n_kt - 1), table))
    m_win = []
    one_matmul = n_kt >= WIN_TILES
    for g in range(N_GROUPS):
        mx = [jnp.full((8, COLS), NEG, F32) for _ in range(2)]
        if one_matmul:
            k_rows = kwv[g, pl.ds(first, WIN_TILES)].reshape(WIN_TILES * 2 * Q_TILE, LANES)
            s_five = _dot_nt(k_rows, rhs[g, :, 0:LANES])
        for t, (kt, table) in enumerate(tiles):
            if one_matmul:
                s_all = s_five[t * 2 * Q_TILE:(t + 1) * 2 * Q_TILE]
            else:
                s_all = _dot_nt(kwv[g, kt], rhs[g, :, 0:LANES])
            for v in range(2):
                s = s_all[v * Q_TILE:(v + 1) * Q_TILE] + tw_ref[g, v, table]
                s_w[g, t, v * Q_TILE:(v + 1) * Q_TILE, :] = s
                mx[v] = jnp.maximum(mx[v], col_max(s))
        m_win += [jnp.max(mx[v], axis=0, keepdims=True) for v in range(2)]

    for g in range(N_GROUPS):
        logits_group(s_a, g, 0)
        halves = []
        for v in range(2):
            o = jnp.zeros((HEAD_DIM + ONES_ROWS, COLS), F32)
            for t, (kt, _) in enumerate(tiles):
                e = jnp.exp2(s_w[g, t, v * Q_TILE:(v + 1) * Q_TILE, :] - m_win[g * 2 + v])
                o = o + _dot(vwt[g, kt], e.astype(BF16))
            halves.append(o[0:HEAD_DIM] * (1.0 / jnp.maximum(o[HEAD_DIM:HEAD_DIM + 1], 1e-30)))
        obuf[2, g] = jnp.concatenate(halves, axis=0)

    def far_two(j, state):
        logits_into(s_b, 2 * j + 1)
        state = consume(s_a, 2 * j, state)
        logits_into(s_a, 2 * j + 2)
        return consume(s_b, 2 * j + 1, state)

    state = lax.fori_loop(0, n_far // 2, far_two, fresh_state())
    state = lax.fori_loop(0, n_far % 2, lambda _, st: consume(s_a, n_far - 1, st), state)

    def near_pair(i, state):
        tables = [jnp.clip(qi - (2 * i + h), 0, 2) for h in range(2)]
        for g in range(N_GROUPS):
            logits_group(s_b, g, i, tables)
        return consume(s_b, i, state)

    diag_alone = 1 - qi % 2
    state = lax.fori_loop(n_far, n_all - diag_alone, near_pair, state)

    def diagonal_tile(_, state):
        ms = list(state)
        i = n_all - 1
        for g in range(N_GROUPS):
            lhs = jnp.concatenate([ksv[g, i, v * 2 * Q_TILE:v * 2 * Q_TILE + Q_TILE, :] for v in range(2)],
                                  axis=0)
            s_all = _dot_nt(lhs, rhs[g])
            for v in range(2):
                s = s_all[v * Q_TILE:(v + 1) * Q_TILE] + ts_ref[g, v, 0]
                s_b[g, v * Q_TILE:(v + 1) * Q_TILE, :] = s
                s_b[g, max_rows(v), :] = col_max(s)
        for g in range(N_GROUPS):
            for v in range(2):
                gv = g * 2 + v
                ms[gv] = online(gv, s_b[g, v * Q_TILE:(v + 1) * Q_TILE, :], s_b[g, max_rows(v), :],
                                vst[g, i, :, 0:Q_TILE], ms[gv])
        return tuple(ms)

    lax.fori_loop(0, diag_alone, diagonal_tile, state)
    finish(1)

    for g in range(N_GROUPS):
        for p in range(PAIRS_PER_GROUP):
            halves = []
            for v in range(2):
                tot = jnp.zeros((HEAD_DIM, Q_TILE), F32)
                for br in range(3):
                    c = br * N_HEADS + g * HEADS_PER_GROUP + 2 * p + v
                    tot = tot + gsig_t[c:c + 1, :] * obuf[br, g, v * HEAD_DIM:(v + 1) * HEAD_DIM,
                                                          p * Q_TILE:(p + 1) * Q_TILE]
                halves.append(tot)
            col = (g * PAIRS_PER_GROUP + p) * LANES
            o_ref[rows, col:col + LANES] = jnp.concatenate(halves, axis=0).T.astype(BF16)


def _attention(proj_main, proj_kv, kcmp, vcmp, bias_c, bias_w, bias_s, ovt, far_lanes, b, s, n_top):
    n_q = s // Q_TILE
    n_steps = n_q // TILES_PER_STEP
    step_rows = TILES_PER_STEP * Q_TILE
    once = pl.Buffered(1)
    kv_col = lambda c: pl.BlockSpec((s, LANES), lambda bi, qi: (bi, c))
    cmp_spec = pl.BlockSpec((1, LANES, LANES), lambda bi, qi: (bi, 0, 0))
    return pl.pallas_call(
        functools.partial(_attn_kernel, n_top=n_top),
        grid=(b, n_steps),
        in_specs=[
            pl.BlockSpec((step_rows, NSA_WIDTH), lambda bi, qi: (bi * n_steps + qi, 0)),
            kv_col(2), kv_col(3), kv_col(4), kv_col(5),
            pl.BlockSpec((step_rows, LANES), lambda bi, qi: (bi * n_steps + qi, 6)),
            cmp_spec, cmp_spec,
            pl.BlockSpec(bias_c.shape, lambda bi, qi: (0,) * 4, pipeline_mode=once),
            pl.BlockSpec(bias_w.shape, lambda bi, qi: (0,) * 5, pipeline_mode=once),
            pl.BlockSpec(bias_s.shape, lambda bi, qi: (0,) * 5, pipeline_mode=once),
            pl.BlockSpec(ovt.shape, lambda bi, qi: (0, 0), pipeline_mode=once),
            pl.BlockSpec(far_lanes.shape, lambda bi, qi: (0, 0, 0), pipeline_mode=once),
        ],
        out_specs=pl.BlockSpec((step_rows, NSA_WIDTH), lambda bi, qi: (bi * n_steps + qi, 0)),
        out_shape=jax.ShapeDtypeStruct((b * s, NSA_WIDTH), BF16),
        scratch_shapes=[
            pltpu.VMEM((N_GROUPS, n_q // 2, 4 * Q_TILE, 2 * LANES), BF16),
            pltpu.VMEM((N_GROUPS, n_q, 2 * Q_TILE, LANES), BF16),
            pltpu.VMEM((N_GROUPS, 2 * LANES, LANES), BF16),
            pltpu.VMEM((N_GROUPS, n_q // 2, HEAD_DIM + ONES_ROWS, 2 * Q_TILE), BF16),
            pltpu.VMEM((N_GROUPS, n_q, HEAD_DIM + ONES_ROWS, Q_TILE), BF16),
            pltpu.VMEM((LANES, LANES), BF16),
            pltpu.VMEM((N_GROUPS, COLS, 2 * LANES), BF16),
            pltpu.VMEM((2 * N_GROUPS, HEAD_DIM + ONES_ROWS, COLS), F32),
            pltpu.VMEM((3, N_GROUPS, LANES, COLS), F32),
            pltpu.VMEM((N_GROUPS, 4 * Q_TILE + 2 * SUBLANES, COLS), F32),
            pltpu.VMEM((N_GROUPS, 4 * Q_TILE + 2 * SUBLANES, COLS), F32),
            pltpu.VMEM((N_GROUPS, WIN_TILES, 2 * Q_TILE, COLS), F32),
        ],
        compiler_params=pltpu.CompilerParams(
            dimension_semantics=("parallel", "arbitrary"), vmem_limit_bytes=VMEM_LIMIT),
        name="nsa_attention",
    )(proj_main, proj_kv, proj_kv, proj_kv, proj_kv, proj_kv, kcmp, vcmp, bias_c, bias_w, bias_s,
      ovt, far_lanes)


def _merge_kernel(zn_ref, a_ref, b_ref, zc_ref, gc_ref, gn_ref, ah_ref, bh_ref, on_ref, x_ref,
                  cw_ref, cb_ref, lg_ref, lb_ref, wcp_ref, wnp_ref, wo_ref, gf_ref, out_ref, uext,
                  conv, shifted):
    i = pl.program_id(1)
    ts = a_ref.shape[0]
    f = lambda r: r[...].astype(F32)

    n_cblk = D_MODEL // LANES
    u_halo = jnp.where(i > 0, f(ah_ref) * _sigmoid(f(bh_ref)), 0.0)
    u = f(a_ref) * _sigmoid(f(b_ref))
    for cblk in range(n_cblk):
        cols = slice(cblk * LANES, (cblk + 1) * LANES)
        uext[cblk, 0:CONV_HALO, :] = u_halo[:, cols]
        uext[cblk, CONV_HALO:CONV_HALO + ts, :] = u[:, cols]
        uext[cblk, CONV_HALO + ts:, :] = jnp.zeros((SUBLANES, LANES), F32)

    lead = CONV_HALO - (CONV_KERNEL - 1)
    chunk = Q_TILE

    def conv_block(cblk, carry):
        for shift in range(SUBLANES):
            shifted[shift] = uext[cblk, shift:shift + ts + CONV_HALO, :]
        w = cw_ref[cblk]
        for h in range(ts // chunk):
            c = jnp.broadcast_to(cb_ref[cblk], (chunk, LANES))
            for shift in range(SUBLANES):
                xs = shifted[shift, h * chunk:h * chunk + chunk + CONV_HALO, :]
                for j in range(CONV_KERNEL):
                    if (lead + j) % SUBLANES == shift:
                        base = lead + j - shift
                        c = c + w[j:j + 1, :] * xs[base:base + chunk]
            conv[cblk, h * chunk:(h + 1) * chunk, :] = c
        return carry

    lax.fori_loop(0, n_cblk, conv_block, 0)
    c = jnp.concatenate([conv[cblk] for cblk in range(n_cblk)], axis=1)

    mu = jnp.mean(c, axis=-1, keepdims=True)
    cc = c - mu
    var = jnp.mean(cc * cc, axis=-1, keepdims=True)
    y = (cc * lax.rsqrt(var + EPS)) * lg_ref[...] + lb_ref[...]
    conv_act = _silu(y) * _silu(f(zc_ref))
    y_conv = _dot(conv_act.astype(BF16), wcp_ref[...])

    nsa_act = f(on_ref) * _silu(f(zn_ref))
    y_nsa = _dot(nsa_act.astype(BF16), wnp_ref[...])

    merged = _sigmoid(f(gc_ref)) * y_conv + _sigmoid(f(gn_ref)) * y_nsa
    xo = x_ref[...] + _dot(merged.astype(BF16), wo_ref[...])
    ms = jnp.mean(xo * xo, axis=-1, keepdims=True)
    out_ref[...] = (xo * lax.rsqrt(ms + EPS)) * gf_ref[...]


def _merge(proj_main, o_nsa, x2, cw, cb, lg, lb, wcp, wnp, wo, gf, b, s, ts):
    n_t = s // ts
    n_cblk = D_MODEL // LANES
    halo_per_tile = ts // CONV_HALO
    cw = jnp.transpose(cw.reshape(cw.shape[0], n_cblk, LANES), (1, 0, 2))
    cb = cb.reshape(n_cblk, 1, LANES)
    col = lambda c: pl.BlockSpec((ts, COL_TILE), lambda bi, ti: (bi * n_t + ti, c))
    halo = lambda c: pl.BlockSpec(
        (CONV_HALO, COL_TILE),
        lambda bi, ti: (jnp.maximum((bi * n_t + ti) * halo_per_tile - 1, 0), c))
    const = lambda a: pl.BlockSpec(a.shape, lambda bi, ti: (0,) * a.ndim, pipeline_mode=pl.Buffered(1))
    rowblk = pl.BlockSpec((ts, D_MODEL), lambda bi, ti: (bi * n_t + ti, 0))
    return pl.pallas_call(
        _merge_kernel,
        grid=(b, n_t),
        in_specs=[col(2), col(3), col(4), col(5), col(6), col(7), halo(3), halo(4), rowblk, rowblk,
                  const(cw), const(cb), const(lg), const(lb), const(wcp), const(wnp), const(wo),
                  const(gf)],
        out_specs=rowblk,
        out_shape=jax.ShapeDtypeStruct((b * s, D_MODEL), F32),
        scratch_shapes=[pltpu.VMEM((n_cblk, ts + CONV_HALO + SUBLANES, LANES), F32),
                        pltpu.VMEM((n_cblk, ts, LANES), F32),
                        pltpu.VMEM((SUBLANES, ts + CONV_HALO, LANES), F32)],
        compiler_params=pltpu.CompilerParams(
            dimension_semantics=("parallel", "arbitrary"), vmem_limit_bytes=VMEM_LIMIT),
        name="conv_merge",
    )(proj_main, proj_main, proj_main, proj_main, proj_main, proj_main, proj_main, proj_main,
      o_nsa, x2, cw, cb, lg, lb, wcp, wnp, wo, gf)


def _t5_bucket_np(rel):
    rel = np.maximum(rel, 0)
    max_exact = REL_BUCKETS // 2
    relf = np.maximum(rel, 1).astype(np.float32)
    large = max_exact + (np.log(relf / np.float32(max_exact))
                         / np.float32(np.log(REL_MAX_DIST / max_exact))
                         * np.float32(REL_BUCKETS - max_exact)).astype(np.int32)
    large = np.minimum(large, REL_BUCKETS - 1)
    return np.where(rel < max_exact, rel, large)


def _pair_head_index():
    g = np.arange(N_GROUPS)[:, None, None]
    v = np.arange(2)[None, :, None]
    p = np.arange(PAIRS_PER_GROUP)[None, None, :]
    return g * HEADS_PER_GROUP + 2 * p + v


def _bias_lookup(rel_bias, rel):
    bucket = _t5_bucket_np(rel).reshape(-1)
    onehot = (jnp.arange(REL_BUCKETS)[:, None] == jnp.asarray(bucket)[None, :]).astype(F32)
    vals = jnp.dot(rel_bias.astype(F32).T * LOG2E, onehot, precision=lax.Precision.HIGHEST)
    vals = vals.reshape((N_HEADS,) + rel.shape)
    head = _pair_head_index()
    return jnp.stack([jnp.stack([jnp.concatenate([vals[h] for h in head[g, v]], axis=-1)
                                 for v in range(2)]) for g in range(N_GROUPS)])


def _bias_tables(rel_bias, s, n_sel):
    c = np.arange(LANES)[:, None]
    r = np.arange(Q_TILE)[None, :]
    far = rel_bias.astype(F32)[REL_BUCKETS - 1][_pair_head_index()] * LOG2E
    assert s // CMP_STRIDE <= LANES + Q_TILE // CMP_STRIDE
    rel_w = np.stack([dd * Q_TILE + r - c for dd in range(WIN_TILES)])
    assert (_t5_bucket_np(rel_w[2:]) == REL_BUCKETS - 1).all()
    rho = np.arange(2 * LANES)[:, None]
    rel_c = r - ((rho - LANES) * CMP_STRIDE + CMP_BLOCK - 1)
    vals = _bias_lookup(rel_bias, np.concatenate([rel_w, rel_c.reshape(2, LANES, Q_TILE)]))
    bias_w, bias_c = vals[:, :, :WIN_TILES], vals[:, :, WIN_TILES:].reshape(N_GROUPS, 2, 2 * LANES, COLS)
    cols = lambda a: np.tile(a, (1,) * (a.ndim - 1) + (PAIRS_PER_GROUP,))

    tw = jnp.where(cols((rel_w >= 0) & (rel_w < WINDOW)), bias_w, NEG)
    tw = jnp.concatenate([tw, jnp.full_like(tw[:, :, :1], NEG)], axis=2)
    far_cols = jnp.repeat(far, Q_TILE, axis=-1)[:, :, None, None, :]
    ts = jnp.where(cols(rel_w[:3] >= 0), bias_w[:, :, :3] - far_cols, NEG)
    bc = jnp.where(cols(rel_c >= 0), bias_c, NEG)
    hi = far.astype(BF16)
    lo = (far - hi.astype(F32)).astype(BF16)
    pieces = jnp.stack([hi[:, 0], lo[:, 0], hi[:, 1], lo[:, 1]], axis=-1)
    pieces = jnp.broadcast_to(pieces[:, :, None, :], (N_GROUPS, PAIRS_PER_GROUP, Q_TILE, 4))
    far_lanes = jnp.pad(pieces.reshape(N_GROUPS, COLS, 4),
                        ((0, 0), (0, 0), (n_sel, LANES - n_sel - 4)))
    return tw, ts, bc, far_lanes


def _overlap_t(s):
    n_cmp = (s - CMP_BLOCK) // CMP_STRIDE + 1
    cs = np.arange(LANES) * CMP_STRIDE
    ss = np.arange(s // SEL_BLOCK) * SEL_BLOCK
    ovt = ((cs[None, :] <= ss[:, None] + SEL_BLOCK - 1) & (cs[None, :] + CMP_BLOCK - 1 >= ss[:, None])
           & (np.arange(LANES)[None, :] < n_cmp))
    return jnp.asarray(ovt, BF16)


def _pad_w_in(w):
    used = NSA_WIDTH + 6 * KV_WIDTH + 3 * N_HEADS
    gap = (KV_COL_TILE + 1) * COL_TILE - used
    col = jnp.arange(w.shape[1])[None, :]
    w = jnp.where(col < NSA_WIDTH, w * (HEAD_DIM ** -0.5 * LOG2E), w)
    left = jnp.pad(w[:, :used], ((0, 0), (0, w.shape[1] - used + gap)))
    right = jnp.pad(w[:, used:], ((0, 0), (used + gap, 0)))
    return (left + right).astype(BF16)


def _compress_weights(pos, w1, w2):
    assert N_GROUPS == 2
    half = CMP_BLOCK // 2

    def per_group(w):
        zeros = [(0, 0)] * (w.ndim - 1)
        top = jnp.pad(w, zeros + [(0, w.shape[-1])])
        bottom = jnp.pad(w, zeros + [(w.shape[-1], 0)])
        return jnp.concatenate([top, bottom], axis=-2)

    w1g = per_group(w1.reshape(CMP_BLOCK, HEAD_DIM, CMP_HIDDEN))
    blk = lambda part: part.reshape(half * KV_WIDTH, N_GROUPS * CMP_HIDDEN)
    w1d = jnp.concatenate([blk(w1g[:half]), blk(w1g[half:])], axis=1).astype(BF16)
    w2d = per_group(w2).astype(BF16)
    tilepos = lambda part: jnp.broadcast_to(part[:, None, :], (half, N_GROUPS, HEAD_DIM)).reshape(1, -1)
    posd = jnp.concatenate([tilepos(pos[:half]), tilepos(pos[half:])], axis=0).astype(F32)
    return posd, w1d, w2d


def kernel(x, norm_in_g, w_in, pos_ck, w_ck1, w_ck2, pos_cv, w_cv1, w_cv2, rel_bias, conv_w, conv_b,
           conv_ln_g, conv_ln_b, w_conv_proj, w_nsa_proj, w_out, norm_f_g):
    b, s, d = x.shape
    n_sel = s // SEL_BLOCK
    assert d == D_MODEL and w_in.shape[0] == 1, "single-layer block with D_MODEL=1024"
    assert s % (2 * Q_TILE) == 0 and s // CMP_STRIDE <= LANES and s >= WINDOW
    assert n_sel + 4 <= LANES
    m = b * s
    x2 = x.reshape(m, d)
    row = lambda a: a.reshape(1, -1).astype(F32)

    proj_kv, proj_main = _input_projection(x2, row(norm_in_g[0]), _pad_w_in(w_in[0]), min(1024, m))

    chunks = s // CMP_STRIDE
    pk, w1k, w2k = _compress_weights(pos_ck[0], w_ck1[0], w_ck2[0])
    pv, w1v, w2v = _compress_weights(pos_cv[0], w_cv1[0], w_cv2[0])
    kcmp, vcmp = _compress(proj_kv, pk, pv, w1k, w1v, w2k, w2v, b, s)
    if chunks < LANES:
        padrows = ((0, 0), (0, LANES - chunks), (0, 0))
        kcmp, vcmp = jnp.pad(kcmp, padrows), jnp.pad(vcmp, padrows)

    bias_w, bias_s, bias_c, far_lanes = _bias_tables(rel_bias, s, n_sel)
    o_nsa = _attention(proj_main, proj_kv, kcmp, vcmp, bias_c, bias_w, bias_s, _overlap_t(s),
                       far_lanes, b, s, min(N_SELECT, n_sel))

    cw = jnp.pad(conv_w[0].astype(F32), ((0, CONV_HALO - CONV_KERNEL), (0, 0)))
    out = _merge(proj_main, o_nsa, x2, cw, row(conv_b[0]), row(conv_ln_g[0]), row(conv_ln_b[0]),
                 w_conv_proj[0].astype(BF16), w_nsa_proj[0].astype(BF16), w_out[0].astype(BF16),
                 row(norm_f_g), b, s, MERGE_ROWS)
    return out.reshape(b, s, d)
```

```python
import functools
import math

import numpy as np
import jax
import jax.numpy as jnp
from jax import lax
from jax.experimental import pallas as pl
from jax.experimental.pallas import tpu as pltpu

F32 = jnp.float32
BF16 = jnp.bfloat16

D_MODEL = 1024
N_HEADS = 16
N_GROUPS = 2
HEADS_PER_GROUP = N_HEADS // N_GROUPS
PAIRS_PER_GROUP = HEADS_PER_GROUP // 2
HEAD_DIM = 64
NSA_WIDTH = N_HEADS * HEAD_DIM
KV_WIDTH = N_GROUPS * HEAD_DIM
CMP_BLOCK = 32
CMP_STRIDE = 16
CMP_HIDDEN = 256
SEL_BLOCK = 64
N_SELECT = 8
WINDOW = 512
Q_TILE = 128
CONV_KERNEL = 31
CONV_HALO = 32
REL_BUCKETS = 32
REL_MAX_DIST = 128
EPS = 1e-6
NEG = -1e30
FORCE_SCORE = 1e6
LOG2E = math.log2(math.e)
LANES = 128
SUBLANES = 8
ONES_ROWS = 16
COLS = PAIRS_PER_GROUP * Q_TILE
COL_TILE = 1024
N_COL_TILES = 8
KV_COL_TILE = 1
MERGE_ROWS = 512
TILES_PER_STEP = 4
WIN_TILES = WINDOW // Q_TILE + 1
VMEM_LIMIT = 56 * 1024 * 1024


def _dot(a, b):
    return jnp.dot(a, b, preferred_element_type=F32)


def _dot_nt(a, b):
    return lax.dot_general(a, b, (((1,), (1,)), ((), ())), preferred_element_type=F32)


def _sigmoid(x):
    return 0.5 * jnp.tanh(0.5 * x) + 0.5


def _silu(x):
    h = 0.5 * x
    return h * jnp.tanh(h) + h


def _proj_kernel(x_ref, g_ref, w_ref, kv_ref, main_ref, h_ref):
    j = pl.program_id(1)

    @pl.when(j == 0)
    def _():
        x = x_ref[...]
        ms = jnp.mean(x * x, axis=-1, keepdims=True)
        h_ref[...] = ((x * lax.rsqrt(ms + EPS)) * g_ref[...]).astype(BF16)

    acc = _dot(h_ref[...], w_ref[...])
    main_ref[...] = acc.astype(BF16)

    @pl.when(j == KV_COL_TILE // 2)
    def _():
        kv_ref[...] = acc[:, (KV_COL_TILE % 2) * COL_TILE:(KV_COL_TILE % 2 + 1) * COL_TILE]


def _input_projection(x2, g, w_perm, tm):
    m = x2.shape[0]
    return pl.pallas_call(
        _proj_kernel,
        grid=(m // tm, N_COL_TILES // 2),
        in_specs=[
            pl.BlockSpec((tm, D_MODEL), lambda i, j: (i, 0)),
            pl.BlockSpec((1, D_MODEL), lambda i, j: (0, 0)),
            pl.BlockSpec((D_MODEL, 2 * COL_TILE), lambda i, j: (0, j)),
        ],
        out_specs=[
            pl.BlockSpec((tm, COL_TILE), lambda i, j: (i, 0)),
            pl.BlockSpec((tm, 2 * COL_TILE), lambda i, j: (i, j)),
        ],
        out_shape=[
            jax.ShapeDtypeStruct((m, COL_TILE), F32),
            jax.ShapeDtypeStruct((m, N_COL_TILES * COL_TILE), BF16),
        ],
        scratch_shapes=[pltpu.VMEM((tm, D_MODEL), BF16)],
        compiler_params=pltpu.CompilerParams(
            dimension_semantics=("parallel", "arbitrary"), vmem_limit_bytes=VMEM_LIMIT),
        name="input_projection",
    )(x2, g, w_perm)


def _compress_kernel(kf_ref, vf_ref, pk_ref, pv_ref, w1k_ref, w1v_ref, w2k_ref, w2v_ref,
                     kc_ref, vc_ref):
    def one(f_ref, pos_ref, w1_ref, w2_ref, o_ref):
        n = f_ref.shape[0] // CMP_STRIDE
        hw = N_GROUPS * CMP_HIDDEN
        first = jnp.zeros((n, hw), F32)
        second = jnp.zeros((n, hw), F32)
        for i in range(0, CMP_STRIDE, 2):
            tok = jnp.concatenate([f_ref[pl.ds(i + k, n, stride=CMP_STRIDE), :] for k in range(2)],
                                  axis=1)
            lanes = slice(i * KV_WIDTH, (i + 2) * KV_WIDTH)
            first = first + _dot((tok + pos_ref[0:1, lanes]).astype(BF16), w1_ref[lanes, 0:hw])
            second = second + _dot((tok + pos_ref[1:2, lanes]).astype(BF16), w1_ref[lanes, hw:2 * hw])
        hid = first + pltpu.roll(second, n - 1, axis=0)
        o_ref[0] = _dot(_silu(hid).astype(BF16), w2_ref[...])

    one(kf_ref, pk_ref, w1k_ref, w2k_ref, kc_ref)
    one(vf_ref, pv_ref, w1v_ref, w2v_ref, vc_ref)


def _compress(proj_kv, pk, pv, w1k, w1v, w2k, w2v, b, s):
    n = s // CMP_STRIDE
    const = lambda shape: pl.BlockSpec(shape, lambda i: (0,) * len(shape))
    kv_col = lambda c: pl.BlockSpec((s, KV_WIDTH), lambda i: (i, c))
    out = pl.BlockSpec((1, n, LANES), lambda i: (i, 0, 0))
    return pl.pallas_call(
        _compress_kernel,
        grid=(b,),
        in_specs=[kv_col(0), kv_col(1), const(pk.shape), const(pv.shape), const(w1k.shape),
                  const(w1v.shape), const(w2k.shape), const(w2v.shape)],
        out_specs=[out, out],
        out_shape=[jax.ShapeDtypeStruct((b, n, LANES), F32)] * 2,
        compiler_params=pltpu.CompilerParams(
            dimension_semantics=("parallel",), vmem_limit_bytes=VMEM_LIMIT),
        name="nsa_compress",
    )(proj_kv, proj_kv, pk, pv, w1k, w1v, w2k, w2v)


def _attn_kernel(q_ref, ks_ref, vs_ref, kw_ref, vw_ref, gt_ref, kc_ref, vc_ref, bc_ref, tw_ref,
                 ts_ref, ovt_ref, bx_ref, o_ref,
                 ksv, kwv, kcv, vst, vwt, vct, rhs, acc, obuf, s_a, s_b, s_w, *, n_top):
    step = pl.program_id(1)
    n_kt = kwv.shape[1]
    n_sel = ovt_ref.shape[0]

    @pl.when(step == 0)
    def _prepare_kv():
        def halves(k):
            lo = lax.broadcasted_iota(jnp.int32, k.shape, 1) < HEAD_DIM
            kr = pltpu.roll(k, HEAD_DIM, axis=1)
            z = jnp.zeros_like(k)
            c = lambda a: a.astype(BF16)
            return ((c(jnp.where(lo, k, z)), c(jnp.where(lo, z, kr))),
                    (c(jnp.where(lo, kr, z)), c(jnp.where(lo, z, k))))

        def extra_lanes(shape, v):
            pair = lax.broadcasted_iota(jnp.int32, shape, 0)
            r = lax.broadcasted_iota(jnp.int32, shape, 1)
            lane = lax.broadcasted_iota(jnp.int32, shape, 2)
            one = ((lane >= n_sel + 2 * v) & (lane < n_sel + 2 * v + 2)
                   | (lane == pair * (2 * Q_TILE // SEL_BLOCK) + r // SEL_BLOCK))
            return jnp.where(one, 1.0, 0.0).astype(BF16)

        pair_shape = (n_kt // 2, 2 * Q_TILE, LANES)
        tile_shape = (n_kt, Q_TILE, LANES)
        k_sel, k_win, k_cmp = halves(ks_ref[...]), halves(kw_ref[...]), halves(kc_ref[0])
        for g in range(N_GROUPS):
            for v in range(2):
                rows = slice(v * 2 * Q_TILE, (v + 1) * 2 * Q_TILE)
                ksv[g, :, rows, 0:LANES] = k_sel[g][v].reshape(pair_shape)
                ksv[g, :, rows, LANES:2 * LANES] = extra_lanes(pair_shape, v)
                kwv[g, :, v * Q_TILE:(v + 1) * Q_TILE, :] = k_win[g][v].reshape(tile_shape)
                kcv[g, v * LANES:(v + 1) * LANES, :] = k_cmp[g][v]
        vst[:, :, HEAD_DIM:, :] = jnp.ones((N_GROUPS, n_kt // 2, ONES_ROWS, 2 * Q_TILE), BF16)
        vwt[:, :, HEAD_DIM:, :] = jnp.ones((N_GROUPS, n_kt, ONES_ROWS, Q_TILE), BF16)
        for kt in range(n_kt):
            rows = slice(kt * Q_TILE, (kt + 1) * Q_TILE)
            half = slice((kt % 2) * Q_TILE, (kt % 2 + 1) * Q_TILE)
            vs_t = vs_ref[rows, :].T.astype(BF16)
            vw_t = vw_ref[rows, :].T.astype(BF16)
            for g in range(N_GROUPS):
                vst[g, kt // 2, 0:HEAD_DIM, half] = vs_t[g * HEAD_DIM:(g + 1) * HEAD_DIM]
                vwt[g, kt, 0:HEAD_DIM, :] = vw_t[g * HEAD_DIM:(g + 1) * HEAD_DIM]
        vct[...] = vc_ref[0].T.astype(BF16)

    for sub in range(TILES_PER_STEP):
        _attn_tile(step * TILES_PER_STEP + sub, slice(sub * Q_TILE, (sub + 1) * Q_TILE),
                   q_ref, gt_ref, bc_ref, tw_ref, ts_ref, ovt_ref, bx_ref, o_ref,
                   ksv, kwv, kcv, vst, vwt, vct, rhs, acc, obuf, s_a, s_b, s_w, n_top)


def _attn_tile(qi, rows, q_ref, gt_ref, bc_ref, tw_ref, ts_ref, ovt_ref, bx_ref, o_ref,
               ksv, kwv, kcv, vst, vwt, vct, rhs, acc, obuf, s_a, s_b, s_w, n_top):
    n_kt = kwv.shape[1]
    n_sel = ovt_ref.shape[0]
    variants = 2 * N_GROUPS
    qt = q_ref[rows, :]
    gsig_t = _sigmoid(gt_ref[rows, :]).T

    def col_max(s):
        return jnp.max(s.reshape(s.shape[0] // 8, 8, COLS), axis=0)

    def group_rows(g):
        return slice(g * HEAD_DIM, (g + 1) * HEAD_DIM)

    def online(gv, s, s_max, vt, m):
        m_new = jnp.maximum(m, jnp.max(s_max, axis=0, keepdims=True))
        e = jnp.exp2(s - m_new)
        acc[gv] = acc[gv] * jnp.exp2(m - m_new) + _dot(vt, e.astype(BF16))
        return m_new

    def fresh_state():
        acc[...] = jnp.zeros(acc.shape, F32)
        return tuple(jnp.full((1, COLS), NEG, F32) for _ in range(variants))

    def finish(branch):
        for g in range(N_GROUPS):
            halves = []
            for v in range(2):
                gv = g * 2 + v
                l = acc[gv, HEAD_DIM:HEAD_DIM + 1, :]
                halves.append(acc[gv, 0:HEAD_DIM, :] * (1.0 / jnp.maximum(l, 1e-30)))
            obuf[branch, g] = jnp.concatenate(halves, axis=0)

    groups = range(N_GROUPS)
    for g in groups:
        q4 = jnp.concatenate(
            [qt[:, (g * PAIRS_PER_GROUP + p) * LANES:(g * PAIRS_PER_GROUP + p + 1) * LANES]
             for p in range(PAIRS_PER_GROUP)], axis=0)
        rhs[g, :, 0:LANES] = q4

    bc_rows = pl.ds(pl.multiple_of(LANES - qi * (Q_TILE // CMP_STRIDE), SUBLANES), LANES)
    s_cmp = [_dot_nt(kcv[g], rhs[g, :, 0:LANES])
             + jnp.concatenate([bc_ref[g, 0, bc_rows, :], bc_ref[g, 1, bc_rows, :]], axis=0)
             for g in groups]
    psum_t = []
    for g in groups:
        tot = jnp.zeros((LANES, Q_TILE), F32)
        halves = []
        for v in range(2):
            s = s_cmp[g][v * LANES:(v + 1) * LANES]
            m = jnp.max(s, axis=0, keepdims=True)
            e = jnp.exp2(s - jnp.where(m > 0.5 * NEG, m, 0.0))
            p = e * (1.0 / jnp.maximum(jnp.sum(e, axis=0, keepdims=True), 1e-30))
            for pp in range(PAIRS_PER_GROUP):
                tot = tot + p[:, pp * Q_TILE:(pp + 1) * Q_TILE]
            halves.append(_dot(vct[group_rows(g), :], p.astype(BF16)))
        obuf[0, g] = jnp.concatenate(halves, axis=0)
        psum_t.append(tot)

    ovt = ovt_ref[...]
    imp_t = []
    for g in groups:
        p_hi = psum_t[g].astype(BF16)
        r1 = psum_t[g] - p_hi.astype(F32)
        p_mid = r1.astype(BF16)
        p_lo = (r1 - p_mid.astype(F32)).astype(BF16)
        imp_t.append(_dot(ovt, p_hi) + _dot(ovt, p_mid) + _dot(ovt, p_lo))

    j_idx = lax.broadcasted_iota(jnp.int32, (n_sel, Q_TILE), 0)
    r_idx = lax.broadcasted_iota(jnp.int32, (n_sel, Q_TILE), 1)
    blk_t = qi * (Q_TILE // SEL_BLOCK) + r_idx // SEL_BLOCK
    valid_blk = j_idx <= blk_t
    forced = (j_idx == 0) | (j_idx == blk_t) | (j_idx == blk_t - 1)
    prio = [jnp.where(valid_blk, jnp.where(forced, FORCE_SCORE, imp_t[g]), -FORCE_SCORE) for g in groups]
    rank = [jnp.zeros((n_sel, Q_TILE), F32) for _ in groups]
    for jj in range(n_sel):
        later = j_idx > jj
        for g in groups:
            row = prio[g][jj:jj + 1, :]
            beats = (row > prio[g]) | ((row == prio[g]) & later)
            rank[g] = rank[g] + jnp.where(beats, 1.0, 0.0)
    for g in groups:
        drop_t = jnp.where((rank[g] < n_top) & valid_blk, 0.0, NEG)
        drop_t = jnp.concatenate([drop_t, jnp.zeros((LANES - n_sel, Q_TILE), F32)], axis=0)
        drop = drop_t.T.astype(BF16)
        rhs[g, :, LANES:2 * LANES] = jnp.concatenate([drop] * PAIRS_PER_GROUP, axis=0) + bx_ref[g]

    pair_rows = 4 * Q_TILE

    def half_rows(v):
        return slice(v * 2 * Q_TILE, (v + 1) * 2 * Q_TILE)

    def max_rows(v):
        return slice(pair_rows + v * SUBLANES, pair_rows + (v + 1) * SUBLANES)

    def logits_group(buf, g, i, tables=None):
        s_all = _dot_nt(ksv[g, i], rhs[g])
        for v in range(2):
            s = s_all[half_rows(v)]
            if tables is not None:
                s = jnp.concatenate([s[h * Q_TILE:(h + 1) * Q_TILE] + ts_ref[g, v, tables[h]]
                                     for h in range(2)], axis=0)
            buf[g, half_rows(v), :] = s
            buf[g, max_rows(v), :] = col_max(s)

    def logits_into(buf, i):
        for g in range(N_GROUPS):
            logits_group(buf, g, i)

    def consume(buf, i, state):
        ms = list(state)
        for g in range(N_GROUPS):
            for v in range(2):
                gv = g * 2 + v
                ms[gv] = online(gv, buf[g, half_rows(v), :], buf[g, max_rows(v), :], vst[g, i], ms[gv])
        return tuple(ms)

    n_all = (qi + 2) // 2
    n_far = jnp.maximum(qi - 1, 0) // 2

    first = jnp.maximum(qi - (WIN_TILES - 1), 0)
    tiles = []
    for t in range(WIN_TILES):
        dd = qi - (first + t)
        table = jnp.where(dd >= 0, dd, WIN_TILES)
        tiles.append((jnp.minimum(first + t, n_kt - 1), table))
    m_win = []
    one_matmul = n_kt >= WIN_TILES
    for g in range(N_GROUPS):
        mx = [jnp.full((8, COLS), NEG, F32) for _ in range(2)]
        if one_matmul:
            k_rows = kwv[g, pl.ds(first, WIN_TILES)].reshape(WIN_TILES * 2 * Q_TILE, LANES)
            s_five = _dot_nt(k_rows, rhs[g, :, 0:LANES])
        for t, (kt, table) in enumerate(tiles):
            if one_matmul:
                s_all = s_five[t * 2 * Q_TILE:(t + 1) * 2 * Q_TILE]
            else:
                s_all = _dot_nt(kwv[g, kt], rhs[g, :, 0:LANES])
            for v in range(2):
                s = s_all[v * Q_TILE:(v + 1) * Q_TILE] + tw_ref[g, v, table]
                s_w[g, t, v * Q_TILE:(v + 1) * Q_TILE, :] = s
                mx[v] = jnp.maximum(mx[v], col_max(s))
        m_win += [jnp.max(mx[v], axis=0, keepdims=True) for v in range(2)]

    for g in range(N_GROUPS):
        logits_group(s_a, g, 0)
        halves = []
        for v in range(2):
            o = jnp.zeros((HEAD_DIM + ONES_ROWS, COLS), F32)
            for t, (kt, _) in enumerate(tiles):
                e = jnp.exp2(s_w[g, t, v * Q_TILE:(v + 1) * Q_TILE, :] - m_win[g * 2 + v])
                o = o + _dot(vwt[g, kt], e.astype(BF16))
            halves.append(o[0:HEAD_DIM] * (1.0 / jnp.maximum(o[HEAD_DIM:HEAD_DIM + 1], 1e-30)))
        obuf[2, g] = jnp.concatenate(halves, axis=0)

    def far_two(j, state):
        logits_into(s_b, 2 * j + 1)
        state = consume(s_a, 2 * j, state)
        logits_into(s_a, 2 * j + 2)
        return consume(s_b, 2 * j + 1, state)

    state = lax.fori_loop(0, n_far // 2, far_two, fresh_state())
    state = lax.fori_loop(0, n_far % 2, lambda _, st: consume(s_a, n_far - 1, st), state)

    def near_pair(i, state):
        tables = [jnp.clip(qi - (2 * i + h), 0, 2) for h in range(2)]
        for g in range(N_GROUPS):
            logits_group(s_b, g, i, tables)
        return consume(s_b, i, state)

    diag_alone = 1 - qi % 2
    state = lax.fori_loop(n_far, n_all - diag_alone, near_pair, state)

    def diagonal_tile(_, state):
        ms = list(state)
        i = n_all - 1
        for g in range(N_GROUPS):
            lhs = jnp.concatenate([ksv[g, i, v * 2 * Q_TILE:v * 2 * Q_TILE + Q_TILE, :] for v in range(2)],
                                  axis=0)
            s_all = _dot_nt(lhs, rhs[g])
            for v in range(2):
                s = s_all[v * Q_TILE:(v + 1) * Q_TILE] + ts_ref[g, v, 0]
                s_b[g, v * Q_TILE:(v + 1) * Q_TILE, :] = s
                s_b[g, max_rows(v), :] = col_max(s)
        for g in range(N_GROUPS):
            for v in range(2):
                gv = g * 2 + v
                ms[gv] = online(gv, s_b[g, v * Q_TILE:(v + 1) * Q_TILE, :], s_b[g, max_rows(v), :],
                                vst[g, i, :, 0:Q_TILE], ms[gv])
        return tuple(ms)

    lax.fori_loop(0, diag_alone, diagonal_tile, state)
    finish(1)

    for g in range(N_GROUPS):
        for p in range(PAIRS_PER_GROUP):
            halves = []
            for v in range(2):
                tot = jnp.zeros((HEAD_DIM, Q_TILE), F32)
                for br in range(3):
                    c = br * N_HEADS + g * HEADS_PER_GROUP + 2 * p + v
                    tot = tot + gsig_t[c:c + 1, :] * obuf[br, g, v * HEAD_DIM:(v + 1) * HEAD_DIM,
                                                          p * Q_TILE:(p + 1) * Q_TILE]
                halves.append(tot)
            col = (g * PAIRS_PER_GROUP + p) * LANES
            o_ref[rows, col:col + LANES] = jnp.concatenate(halves, axis=0).T.astype(BF16)


def _attention(proj_main, proj_kv, kcmp, vcmp, bias_c, bias_w, bias_s, ovt, far_lanes, b, s, n_top):
    n_q = s // Q_TILE
    n_steps = n_q // TILES_PER_STEP
    step_rows = TILES_PER_STEP * Q_TILE
    once = pl.Buffered(1)
    kv_col = lambda c: pl.BlockSpec((s, LANES), lambda bi, qi: (bi, c))
    cmp_spec = pl.BlockSpec((1, LANES, LANES), lambda bi, qi: (bi, 0, 0))
    return pl.pallas_call(
        functools.partial(_attn_kernel, n_top=n_top),
        grid=(b, n_steps),
        in_specs=[
            pl.BlockSpec((step_rows, NSA_WIDTH), lambda bi, qi: (bi * n_steps + qi, 0)),
            kv_col(2), kv_col(3), kv_col(4), kv_col(5),
            pl.BlockSpec((step_rows, LANES), lambda bi, qi: (bi * n_steps + qi, 6)),
            cmp_spec, cmp_spec,
            pl.BlockSpec(bias_c.shape, lambda bi, qi: (0,) * 4, pipeline_mode=once),
            pl.BlockSpec(bias_w.shape, lambda bi, qi: (0,) * 5, pipeline_mode=once),
            pl.BlockSpec(bias_s.shape, lambda bi, qi: (0,) * 5, pipeline_mode=once),
            pl.BlockSpec(ovt.shape, lambda bi, qi: (0, 0), pipeline_mode=once),
            pl.BlockSpec(far_lanes.shape, lambda bi, qi: (0, 0, 0), pipeline_mode=once),
        ],
        out_specs=pl.BlockSpec((step_rows, NSA_WIDTH), lambda bi, qi: (bi * n_steps + qi, 0)),
        out_shape=jax.ShapeDtypeStruct((b * s, NSA_WIDTH), BF16),
        scratch_shapes=[
            pltpu.VMEM((N_GROUPS, n_q // 2, 4 * Q_TILE, 2 * LANES), BF16),
            pltpu.VMEM((N_GROUPS, n_q, 2 * Q_TILE, LANES), BF16),
            pltpu.VMEM((N_GROUPS, 2 * LANES, LANES), BF16),
            pltpu.VMEM((N_GROUPS, n_q // 2, HEAD_DIM + ONES_ROWS, 2 * Q_TILE), BF16),
            pltpu.VMEM((N_GROUPS, n_q, HEAD_DIM + ONES_ROWS, Q_TILE), BF16),
            pltpu.VMEM((LANES, LANES), BF16),
            pltpu.VMEM((N_GROUPS, COLS, 2 * LANES), BF16),
            pltpu.VMEM((2 * N_GROUPS, HEAD_DIM + ONES_ROWS, COLS), F32),
            pltpu.VMEM((3, N_GROUPS, LANES, COLS), F32),
            pltpu.VMEM((N_GROUPS, 4 * Q_TILE + 2 * SUBLANES, COLS), F32),
            pltpu.VMEM((N_GROUPS, 4 * Q_TILE + 2 * SUBLANES, COLS), F32),
            pltpu.VMEM((N_GROUPS, WIN_TILES, 2 * Q_TILE, COLS), F32),
        ],
        compiler_params=pltpu.CompilerParams(
            dimension_semantics=("parallel", "arbitrary"), vmem_limit_bytes=VMEM_LIMIT),
        name="nsa_attention",
    )(proj_main, proj_kv, proj_kv, proj_kv, proj_kv, proj_kv, kcmp, vcmp, bias_c, bias_w, bias_s,
      ovt, far_lanes)


def _merge_kernel(zn_ref, a_ref, b_ref, zc_ref, gc_ref, gn_ref, ah_ref, bh_ref, on_ref, x_ref,
                  cw_ref, cb_ref, lg_ref, lb_ref, wcp_ref, wnp_ref, wo_ref, gf_ref, out_ref, uext,
                  conv, shifted):
    i = pl.program_id(1)
    ts = a_ref.shape[0]
    f = lambda r: r[...].astype(F32)

    n_cblk = D_MODEL // LANES
    u_halo = jnp.where(i > 0, f(ah_ref) * _sigmoid(f(bh_ref)), 0.0)
    u = f(a_ref) * _sigmoid(f(b_ref))
    for cblk in range(n_cblk):
        cols = slice(cblk * LANES, (cblk + 1) * LANES)
        uext[cblk, 0:CONV_HALO, :] = u_halo[:, cols]
        uext[cblk, CONV_HALO:CONV_HALO + ts, :] = u[:, cols]
        uext[cblk, CONV_HALO + ts:, :] = jnp.zeros((SUBLANES, LANES), F32)

    lead = CONV_HALO - (CONV_KERNEL - 1)
    chunk = Q_TILE

    def conv_block(cblk, carry):
        for shift in range(SUBLANES):
            shifted[shift] = uext[cblk, shift:shift + ts + CONV_HALO, :]
        w = cw_ref[cblk]
        for h in range(ts // chunk):
            c = jnp.broadcast_to(cb_ref[cblk], (chunk, LANES))
            for shift in range(SUBLANES):
                xs = shifted[shift, h * chunk:h * chunk + chunk + CONV_HALO, :]
                for j in range(CONV_KERNEL):
                    if (lead + j) % SUBLANES == shift:
                        base = lead + j - shift
                        c = c + w[j:j + 1, :] * xs[base:base + chunk]
            conv[cblk, h * chunk:(h + 1) * chunk, :] = c
        return carry

    lax.fori_loop(0, n_cblk, conv_block, 0)
    c = jnp.concatenate([conv[cblk] for cblk in range(n_cblk)], axis=1)

    mu = jnp.mean(c, axis=-1, keepdims=True)
    cc = c - mu
    var = jnp.mean(cc * cc, axis=-1, keepdims=True)
    y = (cc * lax.rsqrt(var + EPS)) * lg_ref[...] + lb_ref[...]
    conv_act = _silu(y) * _silu(f(zc_ref))
    y_conv = _dot(conv_act.astype(BF16), wcp_ref[...])

    nsa_act = f(on_ref) * _silu(f(zn_ref))
    y_nsa = _dot(nsa_act.astype(BF16), wnp_ref[...])

    merged = _sigmoid(f(gc_ref)) * y_conv + _sigmoid(f(gn_ref)) * y_nsa
    xo = x_ref[...] + _dot(merged.astype(BF16), wo_ref[...])
    ms = jnp.mean(xo * xo, axis=-1, keepdims=True)
    out_ref[...] = (xo * lax.rsqrt(ms + EPS)) * gf_ref[...]


def _merge(proj_main, o_nsa, x2, cw, cb, lg, lb, wcp, wnp, wo, gf, b, s, ts):
    n_t = s // ts
    n_cblk = D_MODEL // LANES
    halo_per_tile = ts // CONV_HALO
    cw = jnp.transpose(cw.reshape(cw.shape[0], n_cblk, LANES), (1, 0, 2))
    cb = cb.reshape(n_cblk, 1, LANES)
    col = lambda c: pl.BlockSpec((ts, COL_TILE), lambda bi, ti: (bi * n_t + ti, c))
    halo = lambda c: pl.BlockSpec(
        (CONV_HALO, COL_TILE),
        lambda bi, ti: (jnp.maximum((bi * n_t + ti) * halo_per_tile - 1, 0), c))
    const = lambda a: pl.BlockSpec(a.shape, lambda bi, ti: (0,) * a.ndim, pipeline_mode=pl.Buffered(1))
    rowblk = pl.BlockSpec((ts, D_MODEL), lambda bi, ti: (bi * n_t + ti, 0))
    return pl.pallas_call(
        _merge_kernel,
        grid=(b, n_t),
        in_specs=[col(2), col(3), col(4), col(5), col(6), col(7), halo(3), halo(4), rowblk, rowblk,
                  const(cw), const(cb), const(lg), const(lb), const(wcp), const(wnp), const(wo),
                  const(gf)],
        out_specs=rowblk,
        out_shape=jax.ShapeDtypeStruct((b * s, D_MODEL), F32),
        scratch_shapes=[pltpu.VMEM((n_cblk, ts + CONV_HALO + SUBLANES, LANES), F32),
                        pltpu.VMEM((n_cblk, ts, LANES), F32),
                        pltpu.VMEM((SUBLANES, ts + CONV_HALO, LANES), F32)],
        compiler_params=pltpu.CompilerParams(
            dimension_semantics=("parallel", "arbitrary"), vmem_limit_bytes=VMEM_LIMIT),
        name="conv_merge",
    )(proj_main, proj_main, proj_main, proj_main, proj_main, proj_main, proj_main, proj_main,
      o_nsa, x2, cw, cb, lg, lb, wcp, wnp, wo, gf)


def _t5_bucket_np(rel):
    rel = np.maximum(rel, 0)
    max_exact = REL_BUCKETS // 2
    relf = np.maximum(rel, 1).astype(np.float32)
    large = max_exact + (np.log(relf / np.float32(max_exact))
                         / np.float32(np.log(REL_MAX_DIST / max_exact))
                         * np.float32(REL_BUCKETS - max_exact)).astype(np.int32)
    large = np.minimum(large, REL_BUCKETS - 1)
    return np.where(rel < max_exact, rel, large)


def _pair_head_index():
    g = np.arange(N_GROUPS)[:, None, None]
    v = np.arange(2)[None, :, None]
    p = np.arange(PAIRS_PER_GROUP)[None, None, :]
    return g * HEADS_PER_GROUP + 2 * p + v


def _bias_lookup(rel_bias, rel):
    bucket = _t5_bucket_np(rel).reshape(-1)
    onehot = (jnp.arange(REL_BUCKETS)[:, None] == jnp.asarray(bucket)[None, :]).astype(F32)
    vals = jnp.dot(rel_bias.astype(F32).T * LOG2E, onehot, precision=lax.Precision.HIGHEST)
    vals = vals.reshape((N_HEADS,) + rel.shape)
    head = _pair_head_index()
    return jnp.stack([jnp.stack([jnp.concatenate([vals[h] for h in head[g, v]], axis=-1)
                                 for v in range(2)]) for g in range(N_GROUPS)])


def _bias_tables(rel_bias, s, n_sel):
    c = np.arange(LANES)[:, None]
    r = np.arange(Q_TILE)[None, :]
    far = rel_bias.astype(F32)[REL_BUCKETS - 1][_pair_head_index()] * LOG2E
    assert s // CMP_STRIDE <= LANES + Q_TILE // CMP_STRIDE
    rel_w = np.stack([dd * Q_TILE + r - c for dd in range(WIN_TILES)])
    assert (_t5_bucket_np(rel_w[2:]) == REL_BUCKETS - 1).all()
    rho = np.arange(2 * LANES)[:, None]
    rel_c = r - ((rho - LANES) * CMP_STRIDE + CMP_BLOCK - 1)
    vals = _bias_lookup(rel_bias, np.concatenate([rel_w, rel_c.reshape(2, LANES, Q_TILE)]))
    bias_w, bias_c = vals[:, :, :WIN_TILES], vals[:, :, WIN_TILES:].reshape(N_GROUPS, 2, 2 * LANES, COLS)
    cols = lambda a: np.tile(a, (1,) * (a.ndim - 1) + (PAIRS_PER_GROUP,))

    tw = jnp.where(cols((rel_w >= 0) & (rel_w < WINDOW)), bias_w, NEG)
    tw = jnp.concatenate([tw, jnp.full_like(tw[:, :, :1], NEG)], axis=2)
    far_cols = jnp.repeat(far, Q_TILE, axis=-1)[:, :, None, None, :]
    ts = jnp.where(cols(rel_w[:3] >= 0), bias_w[:, :, :3] - far_cols, NEG)
    bc = jnp.where(cols(rel_c >= 0), bias_c, NEG)
    hi = far.astype(BF16)
    lo = (far - hi.astype(F32)).astype(BF16)
    pieces = jnp.stack([hi[:, 0], lo[:, 0], hi[:, 1], lo[:, 1]], axis=-1)
    pieces = jnp.broadcast_to(pieces[:, :, None, :], (N_GROUPS, PAIRS_PER_GROUP, Q_TILE, 4))
    far_lanes = jnp.pad(pieces.reshape(N_GROUPS, COLS, 4),
                        ((0, 0), (0, 0), (n_sel, LANES - n_sel - 4)))
    return tw, ts, bc, far_lanes


def _overlap_t(s):
    n_cmp = (s - CMP_BLOCK) // CMP_STRIDE + 1
    cs = np.arange(LANES) * CMP_STRIDE
    ss = np.arange(s // SEL_BLOCK) * SEL_BLOCK
    ovt = ((cs[None, :] <= ss[:, None] + SEL_BLOCK - 1) & (cs[None, :] + CMP_BLOCK - 1 >= ss[:, None])
           & (np.arange(LANES)[None, :] < n_cmp))
    return jnp.asarray(ovt, BF16)


def _pad_w_in(w):
    used = NSA_WIDTH + 6 * KV_WIDTH + 3 * N_HEADS
    gap = (KV_COL_TILE + 1) * COL_TILE - used
    col = jnp.arange(w.shape[1])[None, :]
    w = jnp.where(col < NSA_WIDTH, w * (HEAD_DIM ** -0.5 * LOG2E), w)
    left = jnp.pad(w[:, :used], ((0, 0), (0, w.shape[1] - used + gap)))
    right = jnp.pad(w[:, used:], ((0, 0), (used + gap, 0)))
    return (left + right).astype(BF16)


def _compress_weights(pos, w1, w2):
    assert N_GROUPS == 2
    half = CMP_BLOCK // 2

    def per_group(w):
        zeros = [(0, 0)] * (w.ndim - 1)
        top = jnp.pad(w, zeros + [(0, w.shape[-1])])
        bottom = jnp.pad(w, zeros + [(w.shape[-1], 0)])
        return jnp.concatenate([top, bottom], axis=-2)

    w1g = per_group(w1.reshape(CMP_BLOCK, HEAD_DIM, CMP_HIDDEN))
    blk = lambda part: part.reshape(half * KV_WIDTH, N_GROUPS * CMP_HIDDEN)
    w1d = jnp.concatenate([blk(w1g[:half]), blk(w1g[half:])], axis=1).astype(BF16)
    w2d = per_group(w2).astype(BF16)
    tilepos = lambda part: jnp.broadcast_to(part[:, None, :], (half, N_GROUPS, HEAD_DIM)).reshape(1, -1)
    posd = jnp.concatenate([tilepos(pos[:half]), tilepos(pos[half:])], axis=0).astype(F32)
    return posd, w1d, w2d


def kernel(x, norm_in_g, w_in, pos_ck, w_ck1, w_ck2, pos_cv, w_cv1, w_cv2, rel_bias, conv_w, conv_b,
           conv_ln_g, conv_ln_b, w_conv_proj, w_nsa_proj, w_out, norm_f_g):
    b, s, d = x.shape
    n_sel = s // SEL_BLOCK
    assert d == D_MODEL and w_in.shape[0] == 1, "single-layer block with D_MODEL=1024"
    assert s % (2 * Q_TILE) == 0 and s // CMP_STRIDE <= LANES and s >= WINDOW
    assert n_sel + 4 <= LANES
    m = b * s
    x2 = x.reshape(m, d)
    row = lambda a: a.reshape(1, -1).astype(F32)

    proj_kv, proj_main = _input_projection(x2, row(norm_in_g[0]), _pad_w_in(w_in[0]), min(1024, m))

    chunks = s // CMP_STRIDE
    pk, w1k, w2k = _compress_weights(pos_ck[0], w_ck1[0], w_ck2[0])
    pv, w1v, w2v = _compress_weights(pos_cv[0], w_cv1[0], w_cv2[0])
    kcmp, vcmp = _compress(proj_kv, pk, pv, w1k, w1v, w2k, w2v, b, s)
    if chunks < LANES:
        padrows = ((0, 0), (0, LANES - chunks), (0, 0))
        kcmp, vcmp = jnp.pad(kcmp, padrows), jnp.pad(vcmp, padrows)

    bias_w, bias_s, bias_c, far_lanes = _bias_tables(rel_bias, s, n_sel)
    o_nsa = _attention(proj_main, proj_kv, kcmp, vcmp, bias_c, bias_w, bias_s, _overlap_t(s),
                       far_lanes, b, s, min(N_SELECT, n_sel))

    cw = jnp.pad(conv_w[0].astype(F32), ((0, CONV_HALO - CONV_KERNEL), (0, 0)))
    out = _merge(proj_main, o_nsa, x2, cw, row(conv_b[0]), row(conv_ln_g[0]), row(conv_ln_b[0]),
                 w_conv_proj[0].astype(BF16), w_nsa_proj[0].astype(BF16), w_out[0].astype(BF16),
                 row(norm_f_g), b, s, MERGE_ROWS)
    return out.reshape(b, s, d)
```

```python
import functools
import math

import numpy as np
import jax
import jax.numpy as jnp
from jax import lax
from jax.experimental import pallas as pl
from jax.experimental.pallas import tpu as pltpu

F32 = jnp.float32
BF16 = jnp.bfloat16

D_MODEL = 1024
N_HEADS = 16
N_GROUPS = 2
HEADS_PER_GROUP = N_HEADS // N_GROUPS
PAIRS_PER_GROUP = HEADS_PER_GROUP // 2
HEAD_DIM = 64
NSA_WIDTH = N_HEADS * HEAD_DIM
KV_WIDTH = N_GROUPS * HEAD_DIM
CMP_BLOCK = 32
CMP_STRIDE = 16
CMP_HIDDEN = 256
SEL_BLOCK = 64
N_SELECT = 8
WINDOW = 512
Q_TILE = 128
CONV_KERNEL = 31
CONV_HALO = 32
REL_BUCKETS = 32
REL_MAX_DIST = 128
EPS = 1e-6
NEG = -1e30
FORCE_SCORE = 1e6
LOG2E = math.log2(math.e)
LANES = 128
SUBLANES = 8
ONES_ROWS = 16
COLS = PAIRS_PER_GROUP * Q_TILE
COL_TILE = 1024
N_COL_TILES = 8
KV_COL_TILE = 1
_W_USED = NSA_WIDTH + 6 * KV_WIDTH + 3 * N_HEADS
MERGE_ROWS = 512
TILES_PER_STEP = 4
WIN_TILES = WINDOW // Q_TILE + 1
VMEM_LIMIT = 56 * 1024 * 1024


def _dot(a, b):
    return jnp.dot(a, b, preferred_element_type=F32)


def _dot_nt(a, b):
    return lax.dot_general(a, b, (((1,), (1,)), ((), ())), preferred_element_type=F32)


def _sigmoid(x):
    return 0.5 * jnp.tanh(0.5 * x) + 0.5


def _silu(x):
    h = 0.5 * x
    return h * jnp.tanh(h) + h


def _proj_kernel(x_ref, g_ref, wt_ref, kv_ref, main_ref, h_ref):
    j = pl.program_id(1)

    @pl.when(j == 0)
    def _():
        x = x_ref[...]
        ms = jnp.mean(x * x, axis=-1, keepdims=True)
        h = ((x * lax.rsqrt(ms + EPS)) * g_ref[...]).astype(BF16)
        h_ref[...] = h
        acc = _dot_nt(h, wt_ref[...])
        main_ref[...] = acc.astype(BF16)
        kv_ref[...] = acc[:, KV_COL_TILE * COL_TILE:(KV_COL_TILE + 1) * COL_TILE]

    @pl.when(j > 0)
    def _():
        main_ref[...] = _dot_nt(h_ref[...], wt_ref[...]).astype(BF16)


def _input_projection(x2, g, w_t, tm):
    m = x2.shape[0]
    return pl.pallas_call(
        _proj_kernel,
        grid=(m // tm, N_COL_TILES // 2),
        in_specs=[
            pl.BlockSpec((tm, D_MODEL), lambda i, j: (i, 0)),
            pl.BlockSpec((1, D_MODEL), lambda i, j: (0, 0)),
            pl.BlockSpec((pl.Element(2 * COL_TILE), pl.Element(D_MODEL)),
                         lambda i, j: (16 * jnp.where(j == 0, 0, (_W_USED + (j - 1) * 2 * COL_TILE) // 16),
                                       0)),
        ],
        out_specs=[
            pl.BlockSpec((tm, COL_TILE), lambda i, j: (i, 0)),
            pl.BlockSpec((tm, 2 * COL_TILE), lambda i, j: (i, j)),
        ],
        out_shape=[
            jax.ShapeDtypeStruct((m, COL_TILE), F32),
            jax.ShapeDtypeStruct((m, N_COL_TILES * COL_TILE), BF16),
        ],
        scratch_shapes=[pltpu.VMEM((tm, D_MODEL), BF16)],
        compiler_params=pltpu.CompilerParams(
            dimension_semantics=("parallel", "arbitrary"), vmem_limit_bytes=VMEM_LIMIT),
        name="input_projection",
    )(x2, g, w_t)


def _compress_kernel(kf_ref, vf_ref, pk_ref, pv_ref, w1k_ref, w1v_ref, w2k_ref, w2v_ref,
                     kc_ref, vc_ref):
    def one(f_ref, pos_ref, w1_ref, w2_ref, o_ref):
        n = f_ref.shape[0] // CMP_STRIDE
        hw = N_GROUPS * CMP_HIDDEN
        first = jnp.zeros((n, hw), F32)
        second = jnp.zeros((n, hw), F32)
        for i in range(0, CMP_STRIDE, 2):
            tok = jnp.concatenate([f_ref[pl.ds(i + k, n, stride=CMP_STRIDE), :] for k in range(2)],
                                  axis=1)
            lanes = slice(i * KV_WIDTH, (i + 2) * KV_WIDTH)
            first = first + _dot((tok + pos_ref[0:1, lanes]).astype(BF16), w1_ref[lanes, 0:hw])
            second = second + _dot((tok + pos_ref[1:2, lanes]).astype(BF16), w1_ref[lanes, hw:2 * hw])
        hid = first + pltpu.roll(second, n - 1, axis=0)
        o_ref[0] = _dot(_silu(hid).astype(BF16), w2_ref[...])

    one(kf_ref, pk_ref, w1k_ref, w2k_ref, kc_ref)
    one(vf_ref, pv_ref, w1v_ref, w2v_ref, vc_ref)


def _compress(proj_kv, pk, pv, w1k, w1v, w2k, w2v, b, s):
    n = s // CMP_STRIDE
    const = lambda shape: pl.BlockSpec(shape, lambda i: (0,) * len(shape))
    kv_col = lambda c: pl.BlockSpec((s, KV_WIDTH), lambda i: (i, c))
    out = pl.BlockSpec((1, n, LANES), lambda i: (i, 0, 0))
    return pl.pallas_call(
        _compress_kernel,
        grid=(b,),
        in_specs=[kv_col(0), kv_col(1), const(pk.shape), const(pv.shape), const(w1k.shape),
                  const(w1v.shape), const(w2k.shape), const(w2v.shape)],
        out_specs=[out, out],
        out_shape=[jax.ShapeDtypeStruct((b, n, LANES), F32)] * 2,
        compiler_params=pltpu.CompilerParams(
            dimension_semantics=("parallel",), vmem_limit_bytes=VMEM_LIMIT),
        name="nsa_compress",
    )(proj_kv, proj_kv, pk, pv, w1k, w1v, w2k, w2v)


def _attn_kernel(q_ref, ks_ref, vs_ref, kw_ref, vw_ref, gt_ref, kc_ref, vc_ref, bc_ref, tw_ref,
                 ts_ref, ovt_ref, bx_ref, o_ref,
                 ksv, kwv, kcv, vst, vwt, vct, rhs, acc, obuf, s_a, s_b, s_w, *, n_top):
    step = pl.program_id(1)
    n_kt = kwv.shape[1]
    n_sel = ovt_ref.shape[0]

    @pl.when(step == 0)
    def _prepare_kv():
        def halves(k):
            lo = lax.broadcasted_iota(jnp.int32, k.shape, 1) < HEAD_DIM
            kr = pltpu.roll(k, HEAD_DIM, axis=1)
            z = jnp.zeros_like(k)
            c = lambda a: a.astype(BF16)
            return ((c(jnp.where(lo, k, z)), c(jnp.where(lo, z, kr))),
                    (c(jnp.where(lo, kr, z)), c(jnp.where(lo, z, k))))

        def extra_lanes(shape, v):
            pair = lax.broadcasted_iota(jnp.int32, shape, 0)
            r = lax.broadcasted_iota(jnp.int32, shape, 1)
            lane = lax.broadcasted_iota(jnp.int32, shape, 2)
            one = ((lane >= n_sel + 2 * v) & (lane < n_sel + 2 * v + 2)
                   | (lane == pair * (2 * Q_TILE // SEL_BLOCK) + r // SEL_BLOCK))
            return jnp.where(one, 1.0, 0.0).astype(BF16)

        pair_shape = (n_kt // 2, 2 * Q_TILE, LANES)
        tile_shape = (n_kt, Q_TILE, LANES)
        k_sel, k_win, k_cmp = halves(ks_ref[...]), halves(kw_ref[...]), halves(kc_ref[0])
        for g in range(N_GROUPS):
            for v in range(2):
                rows = slice(v * 2 * Q_TILE, (v + 1) * 2 * Q_TILE)
                ksv[g, :, rows, 0:LANES] = k_sel[g][v].reshape(pair_shape)
                ksv[g, :, rows, LANES:2 * LANES] = extra_lanes(pair_shape, v)
                kwv[g, :, v * Q_TILE:(v + 1) * Q_TILE, :] = k_win[g][v].reshape(tile_shape)
                kcv[g, v * LANES:(v + 1) * LANES, :] = k_cmp[g][v]
        vst[:, :, HEAD_DIM:, :] = jnp.ones((N_GROUPS, n_kt // 2, ONES_ROWS, 2 * Q_TILE), BF16)
        vwt[:, :, HEAD_DIM:, :] = jnp.ones((N_GROUPS, n_kt, ONES_ROWS, Q_TILE), BF16)
        for kt in range(n_kt):
            rows = slice(kt * Q_TILE, (kt + 1) * Q_TILE)
            half = slice((kt % 2) * Q_TILE, (kt % 2 + 1) * Q_TILE)
            vs_t = vs_ref[rows, :].T.astype(BF16)
            vw_t = vw_ref[rows, :].T.astype(BF16)
            for g in range(N_GROUPS):
                vst[g, kt // 2, 0:HEAD_DIM, half] = vs_t[g * HEAD_DIM:(g + 1) * HEAD_DIM]
                vwt[g, kt, 0:HEAD_DIM, :] = vw_t[g * HEAD_DIM:(g + 1) * HEAD_DIM]
        vct[...] = vc_ref[0].T.astype(BF16)

    for sub in range(TILES_PER_STEP):
        _attn_tile(step * TILES_PER_STEP + sub, slice(sub * Q_TILE, (sub + 1) * Q_TILE),
                   q_ref, gt_ref, bc_ref, tw_ref, ts_ref, ovt_ref, bx_ref, o_ref,
                   ksv, kwv, kcv, vst, vwt, vct, rhs, acc, obuf, s_a, s_b, s_w, n_top)


def _attn_tile(qi, rows, q_ref, gt_ref, bc_ref, tw_ref, ts_ref, ovt_ref, bx_ref, o_ref,
               ksv, kwv, kcv, vst, vwt, vct, rhs, acc, obuf, s_a, s_b, s_w, n_top):
    n_kt = kwv.shape[1]
    n_sel = ovt_ref.shape[0]
    variants = 2 * N_GROUPS
    qt = q_ref[rows, :]
    gsig_t = _sigmoid(gt_ref[rows, :]).T

    def col_max(s):
        return jnp.max(s.reshape(s.shape[0] // 8, 8, COLS), axis=0)

    def group_rows(g):
        return slice(g * HEAD_DIM, (g + 1) * HEAD_DIM)

    def online(gv, s, s_max, vt, m):
        m_new = jnp.maximum(m, jnp.max(s_max, axis=0, keepdims=True))
        e = jnp.exp2(s - m_new)
        acc[gv] = acc[gv] * jnp.exp2(m - m_new) + _dot(vt, e.astype(BF16))
        return m_new

    def fresh_state():
        acc[...] = jnp.zeros(acc.shape, F32)
        return tuple(jnp.full((1, COLS), NEG, F32) for _ in range(variants))

    def finish(branch):
        for g in range(N_GROUPS):
            halves = []
            for v in range(2):
                gv = g * 2 + v
                l = acc[gv, HEAD_DIM:HEAD_DIM + 1, :]
                halves.append(acc[gv, 0:HEAD_DIM, :] * (1.0 / jnp.maximum(l, 1e-30)))
            obuf[branch, g] = jnp.concatenate(halves, axis=0)

    groups = range(N_GROUPS)
    for g in groups:
        q4 = jnp.concatenate(
            [qt[:, (g * PAIRS_PER_GROUP + p) * LANES:(g * PAIRS_PER_GROUP + p + 1) * LANES]
             for p in range(PAIRS_PER_GROUP)], axis=0)
        rhs[g, :, 0:LANES] = q4

    bc_rows = pl.ds(pl.multiple_of(LANES - qi * (Q_TILE // CMP_STRIDE), SUBLANES), LANES)
    s_cmp = [_dot_nt(kcv[g], rhs[g, :, 0:LANES])
             + jnp.concatenate([bc_ref[g, 0, bc_rows, :], bc_ref[g, 1, bc_rows, :]], axis=0)
             for g in groups]
    psum_t = []
    for g in groups:
        tot = jnp.zeros((LANES, Q_TILE), F32)
        halves = []
        for v in range(2):
            s = s_cmp[g][v * LANES:(v + 1) * LANES]
            m = jnp.max(s, axis=0, keepdims=True)
            e = jnp.exp2(s - jnp.where(m > 0.5 * NEG, m, 0.0))
            p = e * (1.0 / jnp.maximum(jnp.sum(e, axis=0, keepdims=True), 1e-30))
            for pp in range(PAIRS_PER_GROUP):
                tot = tot + p[:, pp * Q_TILE:(pp + 1) * Q_TILE]
            halves.append(_dot(vct[group_rows(g), :], p.astype(BF16)))
        obuf[0, g] = jnp.concatenate(halves, axis=0)
        psum_t.append(tot)

    ovt = ovt_ref[...]
    imp_t = []
    for g in groups:
        p_hi = psum_t[g].astype(BF16)
        r1 = psum_t[g] - p_hi.astype(F32)
        p_mid = r1.astype(BF16)
        p_lo = (r1 - p_mid.astype(F32)).astype(BF16)
        imp_t.append(_dot(ovt, p_hi) + _dot(ovt, p_mid) + _dot(ovt, p_lo))

    j_idx = lax.broadcasted_iota(jnp.int32, (n_sel, Q_TILE), 0)
    r_idx = lax.broadcasted_iota(jnp.int32, (n_sel, Q_TILE), 1)
    blk_t = qi * (Q_TILE // SEL_BLOCK) + r_idx // SEL_BLOCK
    valid_blk = j_idx <= blk_t
    forced = (j_idx == 0) | (j_idx == blk_t) | (j_idx == blk_t - 1)
    prio = [jnp.where(valid_blk, jnp.where(forced, FORCE_SCORE, imp_t[g]), -FORCE_SCORE) for g in groups]
    rank = [jnp.zeros((n_sel, Q_TILE), F32) for _ in groups]
    for jj in range(n_sel):
        later = j_idx > jj
        for g in groups:
            row = prio[g][jj:jj + 1, :]
            beats = (row > prio[g]) | ((row == prio[g]) & later)
            rank[g] = rank[g] + jnp.where(beats, 1.0, 0.0)
    for g in groups:
        drop_t = jnp.where((rank[g] < n_top) & valid_blk, 0.0, NEG)
        drop_t = jnp.concatenate([drop_t, jnp.zeros((LANES - n_sel, Q_TILE), F32)], axis=0)
        drop = drop_t.T.astype(BF16)
        rhs[g, :, LANES:2 * LANES] = jnp.concatenate([drop] * PAIRS_PER_GROUP, axis=0) + bx_ref[g]

    pair_rows = 4 * Q_TILE

    def half_rows(v):
        return slice(v * 2 * Q_TILE, (v + 1) * 2 * Q_TILE)

    def max_rows(v):
        return slice(pair_rows + v * SUBLANES, pair_rows + (v + 1) * SUBLANES)

    def logits_group(buf, g, i, tables=None):
        s_all = _dot_nt(ksv[g, i], rhs[g])
        for v in range(2):
            s = s_all[half_rows(v)]
            if tables is not None:
                s = jnp.concatenate([s[h * Q_TILE:(h + 1) * Q_TILE] + ts_ref[g, v, tables[h]]
                                     for h in range(2)], axis=0)
            buf[g, half_rows(v), :] = s
            buf[g, max_rows(v), :] = col_max(s)

    def logits_into(buf, i):
        for g in range(N_GROUPS):
            logits_group(buf, g, i)

    def consume(buf, i, state):
        ms = list(state)
        for g in range(N_GROUPS):
            for v in range(2):
                gv = g * 2 + v
                ms[gv] = online(gv, buf[g, half_rows(v), :], buf[g, max_rows(v), :], vst[g, i], ms[gv])
        return tuple(ms)

    n_all = (qi + 2) // 2
    n_far = jnp.maximum(qi - 1, 0) // 2

    first = jnp.maximum(qi - (WIN_TILES - 1), 0)
    tiles = []
    for t in range(WIN_TILES):
        dd = qi - (first + t)
        table = jnp.where(dd >= 0, dd, WIN_TILES)
        tiles.append((jnp.minimum(first + t, n_kt - 1), table))
    m_win = []
    one_matmul = n_kt >= WIN_TILES
    for g in range(N_GROUPS):
        mx = [jnp.full((8, COLS), NEG, F32) for _ in range(2)]
        if one_matmul:
            k_rows = kwv[g, pl.ds(first, WIN_TILES)].reshape(WIN_TILES * 2 * Q_TILE, LANES)
            s_five = _dot_nt(k_rows, rhs[g, :, 0:LANES])
        for t, (kt, table) in enumerate(tiles):
            if one_matmul:
                s_all = s_five[t * 2 * Q_TILE:(t + 1) * 2 * Q_TILE]
            else:
                s_all = _dot_nt(kwv[g, kt], rhs[g, :, 0:LANES])
            for v in range(2):
                s = s_all[v * Q_TILE:(v + 1) * Q_TILE] + tw_ref[g, v, table]
                s_w[g, t, v * Q_TILE:(v + 1) * Q_TILE, :] = s
                mx[v] = jnp.maximum(mx[v], col_max(s))
        m_win += [jnp.max(mx[v], axis=0, keepdims=True) for v in range(2)]

    for g in range(N_GROUPS):
        logits_group(s_a, g, 0)
        halves = []
        for v in range(2):
            o = jnp.zeros((HEAD_DIM + ONES_ROWS, COLS), F32)
            for t, (kt, _) in enumerate(tiles):
                e = jnp.exp2(s_w[g, t, v * Q_TILE:(v + 1) * Q_TILE, :] - m_win[g * 2 + v])
                o = o + _dot(vwt[g, kt], e.astype(BF16))
            halves.append(o[0:HEAD_DIM] * (1.0 / jnp.maximum(o[HEAD_DIM:HEAD_DIM + 1], 1e-30)))
        obuf[2, g] = jnp.concatenate(halves, axis=0)

    def far_two(j, state):
        logits_into(s_b, 2 * j + 1)
        state = consume(s_a, 2 * j, state)
        logits_into(s_a, 2 * j + 2)
        return consume(s_b, 2 * j + 1, state)

    state = lax.fori_loop(0, n_far // 2, far_two, fresh_state())
    state = lax.fori_loop(0, n_far % 2, lambda _, st: consume(s_a, n_far - 1, st), state)

    def near_pair(i, state):
        tables = [jnp.clip(qi - (2 * i + h), 0, 2) for h in range(2)]
        for g in range(N_GROUPS):
            logits_group(s_b, g, i, tables)
        return consume(s_b, i, state)

    diag_alone = 1 - qi % 2
    state = lax.fori_loop(n_far, n_all - diag_alone, near_pair, state)

    def diagonal_tile(_, state):
        ms = list(state)
        i = n_all - 1
        for g in range(N_GROUPS):
            lhs = jnp.concatenate([ksv[g, i, v * 2 * Q_TILE:v * 2 * Q_TILE + Q_TILE, :] for v in range(2)],
                                  axis=0)
            s_all = _dot_nt(lhs, rhs[g])
            for v in range(2):
                s = s_all[v * Q_TILE:(v + 1) * Q_TILE] + ts_ref[g, v, 0]
                s_b[g, v * Q_TILE:(v + 1) * Q_TILE, :] = s
                s_b[g, max_rows(v), :] = col_max(s)
        for g in range(N_GROUPS):
            for v in range(2):
                gv = g * 2 + v
                ms[gv] = online(gv, s_b[g, v * Q_TILE:(v + 1) * Q_TILE, :], s_b[g, max_rows(v), :],
                                vst[g, i, :, 0:Q_TILE], ms[gv])
        return tuple(ms)

    lax.fori_loop(0, diag_alone, diagonal_tile, state)
    finish(1)

    for g in range(N_GROUPS):
        for p in range(PAIRS_PER_GROUP):
            halves = []
            for v in range(2):
                tot = jnp.zeros((HEAD_DIM, Q_TILE), F32)
                for br in range(3):
                    c = br * N_HEADS + g * HEADS_PER_GROUP + 2 * p + v
                    tot = tot + gsig_t[c:c + 1, :] * obuf[br, g, v * HEAD_DIM:(v + 1) * HEAD_DIM,
                                                          p * Q_TILE:(p + 1) * Q_TILE]
                halves.append(tot)
            col = (g * PAIRS_PER_GROUP + p) * LANES
            o_ref[rows, col:col + LANES] = jnp.concatenate(halves, axis=0).T.astype(BF16)


def _attention(proj_main, proj_kv, kcmp, vcmp, bias_c, bias_w, bias_s, ovt, far_lanes, b, s, n_top):
    n_q = s // Q_TILE
    n_steps = n_q // TILES_PER_STEP
    step_rows = TILES_PER_STEP * Q_TILE
    once = pl.Buffered(1)
    kv_col = lambda c: pl.BlockSpec((s, LANES), lambda bi, qi: (bi, c))
    cmp_spec = pl.BlockSpec((1, LANES, LANES), lambda bi, qi: (bi, 0, 0))
    return pl.pallas_call(
        functools.partial(_attn_kernel, n_top=n_top),
        grid=(b, n_steps),
        in_specs=[
            pl.BlockSpec((step_rows, NSA_WIDTH), lambda bi, qi: (bi * n_steps + qi, 0)),
            kv_col(2), kv_col(3), kv_col(4), kv_col(5),
            pl.BlockSpec((step_rows, LANES), lambda bi, qi: (bi * n_steps + qi, 6)),
            cmp_spec, cmp_spec,
            pl.BlockSpec(bias_c.shape, lambda bi, qi: (0,) * 4, pipeline_mode=once),
            pl.BlockSpec(bias_w.shape, lambda bi, qi: (0,) * 5, pipeline_mode=once),
            pl.BlockSpec(bias_s.shape, lambda bi, qi: (0,) * 5, pipeline_mode=once),
            pl.BlockSpec(ovt.shape, lambda bi, qi: (0, 0), pipeline_mode=once),
            pl.BlockSpec(far_lanes.shape, lambda bi, qi: (0, 0, 0), pipeline_mode=once),
        ],
        out_specs=pl.BlockSpec((step_rows, NSA_WIDTH), lambda bi, qi: (bi * n_steps + qi, 0)),
        out_shape=jax.ShapeDtypeStruct((b * s, NSA_WIDTH), BF16),
        scratch_shapes=[
            pltpu.VMEM((N_GROUPS, n_q // 2, 4 * Q_TILE, 2 * LANES), BF16),
            pltpu.VMEM((N_GROUPS, n_q, 2 * Q_TILE, LANES), BF16),
            pltpu.VMEM((N_GROUPS, 2 * LANES, LANES), BF16),
            pltpu.VMEM((N_GROUPS, n_q // 2, HEAD_DIM + ONES_ROWS, 2 * Q_TILE), BF16),
            pltpu.VMEM((N_GROUPS, n_q, HEAD_DIM + ONES_ROWS, Q_TILE), BF16),
            pltpu.VMEM((LANES, LANES), BF16),
            pltpu.VMEM((N_GROUPS, COLS, 2 * LANES), BF16),
            pltpu.VMEM((2 * N_GROUPS, HEAD_DIM + ONES_ROWS, COLS), F32),
            pltpu.VMEM((3, N_GROUPS, LANES, COLS), F32),
            pltpu.VMEM((N_GROUPS, 4 * Q_TILE + 2 * SUBLANES, COLS), F32),
            pltpu.VMEM((N_GROUPS, 4 * Q_TILE + 2 * SUBLANES, COLS), F32),
            pltpu.VMEM((N_GROUPS, WIN_TILES, 2 * Q_TILE, COLS), F32),
        ],
        compiler_params=pltpu.CompilerParams(
            dimension_semantics=("parallel", "arbitrary"), vmem_limit_bytes=VMEM_LIMIT),
        name="nsa_attention",
    )(proj_main, proj_kv, proj_kv, proj_kv, proj_kv, proj_kv, kcmp, vcmp, bias_c, bias_w, bias_s,
      ovt, far_lanes)


def _merge_kernel(zn_ref, a_ref, b_ref, zc_ref, gc_ref, gn_ref, ah_ref, bh_ref, on_ref, x_ref,
                  cw_ref, cb_ref, lg_ref, lb_ref, wcp_ref, wnp_ref, wo_ref, gf_ref, out_ref, uext,
                  conv, shifted):
    i = pl.program_id(1)
    ts = a_ref.shape[0]
    f = lambda r: r[...].astype(F32)

    n_cblk = D_MODEL // LANES
    u_halo = jnp.where(i > 0, f(ah_ref) * _sigmoid(f(bh_ref)), 0.0)
    u = f(a_ref) * _sigmoid(f(b_ref))
    for cblk in range(n_cblk):
        cols = slice(cblk * LANES, (cblk + 1) * LANES)
        uext[cblk, 0:CONV_HALO, :] = u_halo[:, cols]
        uext[cblk, CONV_HALO:CONV_HALO + ts, :] = u[:, cols]
        uext[cblk, CONV_HALO + ts:, :] = jnp.zeros((SUBLANES, LANES), F32)

    lead = CONV_HALO - (CONV_KERNEL - 1)
    chunk = Q_TILE

    def conv_block(cblk, carry):
        for shift in range(SUBLANES):
            shifted[shift] = uext[cblk, shift:shift + ts + CONV_HALO, :]
        w = cw_ref[cblk]
        for h in range(ts // chunk):
            c = jnp.broadcast_to(cb_ref[cblk], (chunk, LANES))
            for shift in range(SUBLANES):
                xs = shifted[shift, h * chunk:h * chunk + chunk + CONV_HALO, :]
                for j in range(CONV_KERNEL):
                    if (lead + j) % SUBLANES == shift:
                        base = lead + j - shift
                        c = c + w[j:j + 1, :] * xs[base:base + chunk]
            conv[cblk, h * chunk:(h + 1) * chunk, :] = c
        return carry

    lax.fori_loop(0, n_cblk, conv_block, 0)
    c = jnp.concatenate([conv[cblk] for cblk in range(n_cblk)], axis=1)

    mu = jnp.mean(c, axis=-1, keepdims=True)
    cc = c - mu
    var = jnp.mean(cc * cc, axis=-1, keepdims=True)
    y = (cc * lax.rsqrt(var + EPS)) * lg_ref[...] + lb_ref[...]
    conv_act = _silu(y) * _silu(f(zc_ref))
    y_conv = _dot(conv_act.astype(BF16), wcp_ref[...])

    nsa_act = f(on_ref) * _silu(f(zn_ref))
    y_nsa = _dot(nsa_act.astype(BF16), wnp_ref[...])

    merged = _sigmoid(f(gc_ref)) * y_conv + _sigmoid(f(gn_ref)) * y_nsa
    xo = x_ref[...] + _dot(merged.astype(BF16), wo_ref[...])
    ms = jnp.mean(xo * xo, axis=-1, keepdims=True)
    out_ref[...] = (xo * lax.rsqrt(ms + EPS)) * gf_ref[...]


def _merge(proj_main, o_nsa, x2, cw, cb, lg, lb, wcp, wnp, wo, gf, b, s, ts):
    n_t = s // ts
    n_cblk = D_MODEL // LANES
    halo_per_tile = ts // CONV_HALO
    cw = jnp.transpose(cw.reshape(cw.shape[0], n_cblk, LANES), (1, 0, 2))
    cb = cb.reshape(n_cblk, 1, LANES)
    col = lambda c: pl.BlockSpec((ts, COL_TILE), lambda bi, ti: (bi * n_t + ti, c))
    halo = lambda c: pl.BlockSpec(
        (CONV_HALO, COL_TILE),
        lambda bi, ti: (jnp.maximum((bi * n_t + ti) * halo_per_tile - 1, 0), c))
    const = lambda a: pl.BlockSpec(a.shape, lambda bi, ti: (0,) * a.ndim, pipeline_mode=pl.Buffered(1))
    rowblk = pl.BlockSpec((ts, D_MODEL), lambda bi, ti: (bi * n_t + ti, 0))
    return pl.pallas_call(
        _merge_kernel,
        grid=(b, n_t),
        in_specs=[col(2), col(3), col(4), col(5), col(6), col(7), halo(3), halo(4), rowblk, rowblk,
                  const(cw), const(cb), const(lg), const(lb), const(wcp), const(wnp), const(wo),
                  const(gf)],
        out_specs=rowblk,
        out_shape=jax.ShapeDtypeStruct((b * s, D_MODEL), F32),
        scratch_shapes=[pltpu.VMEM((n_cblk, ts + CONV_HALO + SUBLANES, LANES), F32),
                        pltpu.VMEM((n_cblk, ts, LANES), F32),
                        pltpu.VMEM((SUBLANES, ts + CONV_HALO, LANES), F32)],
        compiler_params=pltpu.CompilerParams(
            dimension_semantics=("parallel", "arbitrary"), vmem_limit_bytes=VMEM_LIMIT),
        name="conv_merge",
    )(proj_main, proj_main, proj_main, proj_main, proj_main, proj_main, proj_main, proj_main,
      o_nsa, x2, cw, cb, lg, lb, wcp, wnp, wo, gf)


def _t5_bucket_np(rel):
    rel = np.maximum(rel, 0)
    max_exact = REL_BUCKETS // 2
    relf = np.maximum(rel, 1).astype(np.float32)
    large = max_exact + (np.log(relf / np.float32(max_exact))
                         / np.float32(np.log(REL_MAX_DIST / max_exact))
                         * np.float32(REL_BUCKETS - max_exact)).astype(np.int32)
    large = np.minimum(large, REL_BUCKETS - 1)
    return np.where(rel < max_exact, rel, large)


def _pair_head_index():
    g = np.arange(N_GROUPS)[:, None, None]
    v = np.arange(2)[None, :, None]
    p = np.arange(PAIRS_PER_GROUP)[None, None, :]
    return g * HEADS_PER_GROUP + 2 * p + v


def _bias_lookup(rel_bias, rel):
    bucket = _t5_bucket_np(rel).reshape(-1)
    onehot = (jnp.arange(REL_BUCKETS)[:, None] == jnp.asarray(bucket)[None, :]).astype(F32)
    vals = jnp.dot(rel_bias.astype(F32).T * LOG2E, onehot, precision=lax.Precision.HIGHEST)
    vals = vals.reshape((N_HEADS,) + rel.shape)
    head = _pair_head_index()
    return jnp.stack([jnp.stack([jnp.concatenate([vals[h] for h in head[g, v]], axis=-1)
                                 for v in range(2)]) for g in range(N_GROUPS)])


def _bias_tables(rel_bias, s, n_sel):
    c = np.arange(LANES)[:, None]
    r = np.arange(Q_TILE)[None, :]
    far = rel_bias.astype(F32)[REL_BUCKETS - 1][_pair_head_index()] * LOG2E
    assert s // CMP_STRIDE <= LANES + Q_TILE // CMP_STRIDE
    rel_w = np.stack([dd * Q_TILE + r - c for dd in range(WIN_TILES)])
    assert (_t5_bucket_np(rel_w[2:]) == REL_BUCKETS - 1).all()
    rho = np.arange(2 * LANES)[:, None]
    rel_c = r - ((rho - LANES) * CMP_STRIDE + CMP_BLOCK - 1)
    vals = _bias_lookup(rel_bias, np.concatenate([rel_w, rel_c.reshape(2, LANES, Q_TILE)]))
    bias_w, bias_c = vals[:, :, :WIN_TILES], vals[:, :, WIN_TILES:].reshape(N_GROUPS, 2, 2 * LANES, COLS)
    cols = lambda a: np.tile(a, (1,) * (a.ndim - 1) + (PAIRS_PER_GROUP,))

    tw = jnp.where(cols((rel_w >= 0) & (rel_w < WINDOW)), bias_w, NEG)
    tw = jnp.concatenate([tw, jnp.full_like(tw[:, :, :1], NEG)], axis=2)
    far_cols = jnp.repeat(far, Q_TILE, axis=-1)[:, :, None, None, :]
    ts = jnp.where(cols(rel_w[:3] >= 0), bias_w[:, :, :3] - far_cols, NEG)
    bc = jnp.where(cols(rel_c >= 0), bias_c, NEG)
    hi = far.astype(BF16)
    lo = (far - hi.astype(F32)).astype(BF16)
    pieces = jnp.stack([hi[:, 0], lo[:, 0], hi[:, 1], lo[:, 1]], axis=-1)
    pieces = jnp.broadcast_to(pieces[:, :, None, :], (N_GROUPS, PAIRS_PER_GROUP, Q_TILE, 4))
    far_lanes = jnp.pad(pieces.reshape(N_GROUPS, COLS, 4),
                        ((0, 0), (0, 0), (n_sel, LANES - n_sel - 4)))
    return tw, ts, bc, far_lanes


def _overlap_t(s):
    n_cmp = (s - CMP_BLOCK) // CMP_STRIDE + 1
    cs = np.arange(LANES) * CMP_STRIDE
    ss = np.arange(s // SEL_BLOCK) * SEL_BLOCK
    ovt = ((cs[None, :] <= ss[:, None] + SEL_BLOCK - 1) & (cs[None, :] + CMP_BLOCK - 1 >= ss[:, None])
           & (np.arange(LANES)[None, :] < n_cmp))
    return jnp.asarray(ovt, BF16)


def _transposed_w_in(w):
    assert w.shape[1] == _W_USED + (N_COL_TILES - 2) * COL_TILE and _W_USED % 16 == 0
    col = jnp.arange(w.shape[1])[:, None]
    w_t = jnp.swapaxes(w, 0, 1)
    return jnp.where(col < NSA_WIDTH, w_t * (HEAD_DIM ** -0.5 * LOG2E), w_t).astype(BF16)


def _compress_weights(pos, w1, w2):
    assert N_GROUPS == 2
    half = CMP_BLOCK // 2

    def per_group(w):
        zeros = [(0, 0)] * (w.ndim - 1)
        top = jnp.pad(w, zeros + [(0, w.shape[-1])])
        bottom = jnp.pad(w, zeros + [(w.shape[-1], 0)])
        return jnp.concatenate([top, bottom], axis=-2)

    w1g = per_group(w1.reshape(CMP_BLOCK, HEAD_DIM, CMP_HIDDEN))
    blk = lambda part: part.reshape(half * KV_WIDTH, N_GROUPS * CMP_HIDDEN)
    w1d = jnp.concatenate([blk(w1g[:half]), blk(w1g[half:])], axis=1).astype(BF16)
    w2d = per_group(w2).astype(BF16)
    tilepos = lambda part: jnp.broadcast_to(part[:, None, :], (half, N_GROUPS, HEAD_DIM)).reshape(1, -1)
    posd = jnp.concatenate([tilepos(pos[:half]), tilepos(pos[half:])], axis=0).astype(F32)
    return posd, w1d, w2d


def kernel(x, norm_in_g, w_in, pos_ck, w_ck1, w_ck2, pos_cv, w_cv1, w_cv2, rel_bias, conv_w, conv_b,
           conv_ln_g, conv_ln_b, w_conv_proj, w_nsa_proj, w_out, norm_f_g):
    b, s, d = x.shape
    n_sel = s // SEL_BLOCK
    assert d == D_MODEL and w_in.shape[0] == 1, "single-layer block with D_MODEL=1024"
    assert s % (2 * Q_TILE) == 0 and s // CMP_STRIDE <= LANES and s >= WINDOW
    assert n_sel + 4 <= LANES
    m = b * s
    x2 = x.reshape(m, d)
    row = lambda a: a.reshape(1, -1).astype(F32)

    proj_kv, proj_main = _input_projection(x2, row(norm_in_g[0]), _transposed_w_in(w_in[0]), min(1024, m))

    chunks = s // CMP_STRIDE
    pk, w1k, w2k = _compress_weights(pos_ck[0], w_ck1[0], w_ck2[0])
    pv, w1v, w2v = _compress_weights(pos_cv[0], w_cv1[0], w_cv2[0])
    kcmp, vcmp = _compress(proj_kv, pk, pv, w1k, w1v, w2k, w2v, b, s)
    if chunks < LANES:
        padrows = ((0, 0), (0, LANES - chunks), (0, 0))
        kcmp, vcmp = jnp.pad(kcmp, padrows), jnp.pad(vcmp, padrows)

    bias_w, bias_s, bias_c, far_lanes = _bias_tables(rel_bias, s, n_sel)
    o_nsa = _attention(proj_main, proj_kv, kcmp, vcmp, bias_c, bias_w, bias_s, _overlap_t(s),
                       far_lanes, b, s, min(N_SELECT, n_sel))

    cw = jnp.pad(conv_w[0].astype(F32), ((0, CONV_HALO - CONV_KERNEL), (0, 0)))
    out = _merge(proj_main, o_nsa, x2, cw, row(conv_b[0]), row(conv_ln_g[0]), row(conv_ln_b[0]),
                 w_conv_proj[0].astype(BF16), w_nsa_proj[0].astype(BF16), w_out[0].astype(BF16),
                 row(norm_f_g), b, s, MERGE_ROWS)
    return out.reshape(b, s, d)
```

```python
import functools
import math

import numpy as np
import jax
import jax.numpy as jnp
from jax import lax
from jax.experimental import pallas as pl
from jax.experimental.pallas import tpu as pltpu

F32 = jnp.float32
BF16 = jnp.bfloat16

D_MODEL = 1024
N_HEADS = 16
N_GROUPS = 2
HEADS_PER_GROUP = N_HEADS // N_GROUPS
PAIRS_PER_GROUP = HEADS_PER_GROUP // 2
HEAD_DIM = 64
NSA_WIDTH = N_HEADS * HEAD_DIM
KV_WIDTH = N_GROUPS * HEAD_DIM
CMP_BLOCK = 32
CMP_STRIDE = 16
CMP_HIDDEN = 256
SEL_BLOCK = 64
N_SELECT = 8
WINDOW = 512
Q_TILE = 128
CONV_KERNEL = 31
CONV_HALO = 32
REL_BUCKETS = 32
REL_MAX_DIST = 128
EPS = 1e-6
NEG = -1e30
FORCE_SCORE = 1e6
LOG2E = math.log2(math.e)
LANES = 128
SUBLANES = 8
ONES_ROWS = 16
COLS = PAIRS_PER_GROUP * Q_TILE
COL_TILE = 1024
N_COL_TILES = 8
KV_COL_TILE = 1
_W_USED = NSA_WIDTH + 6 * KV_WIDTH + 3 * N_HEADS
MERGE_ROWS = 512
TILES_PER_STEP = 4
WIN_TILES = WINDOW // Q_TILE + 1
VMEM_LIMIT = 56 * 1024 * 1024


def _dot(a, b):
    return jnp.dot(a, b, preferred_element_type=F32)


def _dot_nt(a, b):
    return lax.dot_general(a, b, (((1,), (1,)), ((), ())), preferred_element_type=F32)


def _sigmoid(x):
    return 0.5 * jnp.tanh(0.5 * x) + 0.5


def _silu(x):
    h = 0.5 * x
    return h * jnp.tanh(h) + h


def _proj_kernel(x_ref, g_ref, wt_ref, kv_ref, main_ref, h_ref):
    j = pl.program_id(1)

    @pl.when(j == 0)
    def _():
        x = x_ref[...]
        ms = jnp.mean(x * x, axis=-1, keepdims=True)
        h = ((x * lax.rsqrt(ms + EPS)) * g_ref[...]).astype(BF16)
        h_ref[...] = h
        acc = _dot_nt(h, wt_ref[...])
        main_ref[...] = acc.astype(BF16)
        kv_ref[...] = acc[:, KV_COL_TILE * COL_TILE:(KV_COL_TILE + 1) * COL_TILE]

    @pl.when(j > 0)
    def _():
        main_ref[...] = _dot_nt(h_ref[...], wt_ref[...]).astype(BF16)


def _input_projection(x2, g, w_t, tm):
    m = x2.shape[0]
    return pl.pallas_call(
        _proj_kernel,
        grid=(m // tm, N_COL_TILES // 2),
        in_specs=[
            pl.BlockSpec((tm, D_MODEL), lambda i, j: (i, 0)),
            pl.BlockSpec((1, D_MODEL), lambda i, j: (0, 0)),
            pl.BlockSpec((pl.Element(2 * COL_TILE), pl.Element(D_MODEL)),
                         lambda i, j: (16 * jnp.where(j == 0, 0, (_W_USED + (j - 1) * 2 * COL_TILE) // 16),
                                       0)),
        ],
        out_specs=[
            pl.BlockSpec((tm, COL_TILE), lambda i, j: (i, 0)),
            pl.BlockSpec((tm, 2 * COL_TILE), lambda i, j: (i, j)),
        ],
        out_shape=[
            jax.ShapeDtypeStruct((m, COL_TILE), F32),
            jax.ShapeDtypeStruct((m, N_COL_TILES * COL_TILE), BF16),
        ],
        scratch_shapes=[pltpu.VMEM((tm, D_MODEL), BF16)],
        compiler_params=pltpu.CompilerParams(
            dimension_semantics=("parallel", "arbitrary"), vmem_limit_bytes=VMEM_LIMIT),
        name="input_projection",
    )(x2, g, w_t)


def _compress_kernel(kf_ref, vf_ref, pk_ref, pv_ref, w1k_ref, w1v_ref, w2k_ref, w2v_ref,
                     kc_ref, vc_ref):
    rows_per_step = kc_ref.shape[0]

    def one(f_ref, pos_ref, w1_ref, w2_ref, o_ref):
        for r in range(rows_per_step):
            one_row(f_ref, pos_ref, w1_ref, w2_ref, o_ref, r)

    def one_row(f_ref, pos_ref, w1_ref, w2_ref, o_ref, r):
        n = f_ref.shape[0] // (CMP_STRIDE * rows_per_step)
        base = r * n * CMP_STRIDE
        hw = N_GROUPS * CMP_HIDDEN
        first = jnp.zeros((n, hw), F32)
        second = jnp.zeros((n, hw), F32)
        for i in range(0, CMP_STRIDE, 2):
            tok = jnp.concatenate([f_ref[pl.ds(base + i + k, n, stride=CMP_STRIDE), :] for k in range(2)],
                                  axis=1)
            lanes = slice(i * KV_WIDTH, (i + 2) * KV_WIDTH)
            first = first + _dot((tok + pos_ref[0:1, lanes]).astype(BF16), w1_ref[lanes, 0:hw])
            second = second + _dot((tok + pos_ref[1:2, lanes]).astype(BF16), w1_ref[lanes, hw:2 * hw])
        hid = first + pltpu.roll(second, n - 1, axis=0)
        o_ref[r] = _dot(_silu(hid).astype(BF16), w2_ref[...])

    one(kf_ref, pk_ref, w1k_ref, w2k_ref, kc_ref)
    one(vf_ref, pv_ref, w1v_ref, w2v_ref, vc_ref)


def _compress(proj_kv, pk, pv, w1k, w1v, w2k, w2v, b, s):
    n = s // CMP_STRIDE
    rows = 2 if b % 2 == 0 else 1
    const = lambda shape: pl.BlockSpec(shape, lambda i: (0,) * len(shape))
    kv_col = lambda c: pl.BlockSpec((rows * s, KV_WIDTH), lambda i: (i, c))
    out = pl.BlockSpec((rows, n, LANES), lambda i: (i, 0, 0))
    return pl.pallas_call(
        _compress_kernel,
        grid=(b // rows,),
        in_specs=[kv_col(0), kv_col(1), const(pk.shape), const(pv.shape), const(w1k.shape),
                  const(w1v.shape), const(w2k.shape), const(w2v.shape)],
        out_specs=[out, out],
        out_shape=[jax.ShapeDtypeStruct((b, n, LANES), F32)] * 2,
        compiler_params=pltpu.CompilerParams(
            dimension_semantics=("parallel",), vmem_limit_bytes=VMEM_LIMIT),
        name="nsa_compress",
    )(proj_kv, proj_kv, pk, pv, w1k, w1v, w2k, w2v)


def _attn_kernel(q_ref, ks_ref, vs_ref, kw_ref, vw_ref, gt_ref, kc_ref, vc_ref, bc_ref, tw_ref,
                 ts_ref, ovt_ref, bx_ref, o_ref,
                 ksv, kwv, kcv, vst, vwt, vct, rhs, acc, obuf, s_a, s_b, s_w, *, n_top):
    step = pl.program_id(1)
    n_kt = kwv.shape[1]
    n_sel = ovt_ref.shape[0]

    @pl.when(step == 0)
    def _prepare_kv():
        def halves(k):
            lo = lax.broadcasted_iota(jnp.int32, k.shape, 1) < HEAD_DIM
            kr = pltpu.roll(k, HEAD_DIM, axis=1)
            z = jnp.zeros_like(k)
            c = lambda a: a.astype(BF16)
            return ((c(jnp.where(lo, k, z)), c(jnp.where(lo, z, kr))),
                    (c(jnp.where(lo, kr, z)), c(jnp.where(lo, z, k))))

        def extra_lanes(shape, v):
            pair = lax.broadcasted_iota(jnp.int32, shape, 0)
            r = lax.broadcasted_iota(jnp.int32, shape, 1)
            lane = lax.broadcasted_iota(jnp.int32, shape, 2)
            one = ((lane >= n_sel + 2 * v) & (lane < n_sel + 2 * v + 2)
                   | (lane == pair * (2 * Q_TILE // SEL_BLOCK) + r // SEL_BLOCK))
            return jnp.where(one, 1.0, 0.0).astype(BF16)

        pair_shape = (n_kt // 2, 2 * Q_TILE, LANES)
        tile_shape = (n_kt, Q_TILE, LANES)
        k_sel, k_win, k_cmp = halves(ks_ref[...]), halves(kw_ref[...]), halves(kc_ref[0])
        for g in range(N_GROUPS):
            for v in range(2):
                rows = slice(v * 2 * Q_TILE, (v + 1) * 2 * Q_TILE)
                ksv[g, :, rows, 0:LANES] = k_sel[g][v].reshape(pair_shape)
                ksv[g, :, rows, LANES:2 * LANES] = extra_lanes(pair_shape, v)
                kwv[g, :, v * Q_TILE:(v + 1) * Q_TILE, :] = k_win[g][v].reshape(tile_shape)
                kcv[g, v * LANES:(v + 1) * LANES, :] = k_cmp[g][v]
        vst[:, :, HEAD_DIM:, :] = jnp.ones((N_GROUPS, n_kt // 2, ONES_ROWS, 2 * Q_TILE), BF16)
        vwt[:, :, HEAD_DIM:, :] = jnp.ones((N_GROUPS, n_kt, ONES_ROWS, Q_TILE), BF16)
        for kt in range(n_kt):
            rows = slice(kt * Q_TILE, (kt + 1) * Q_TILE)
            half = slice((kt % 2) * Q_TILE, (kt % 2 + 1) * Q_TILE)
            vs_t = vs_ref[rows, :].T.astype(BF16)
            vw_t = vw_ref[rows, :].T.astype(BF16)
            for g in range(N_GROUPS):
                vst[g, kt // 2, 0:HEAD_DIM, half] = vs_t[g * HEAD_DIM:(g + 1) * HEAD_DIM]
                vwt[g, kt, 0:HEAD_DIM, :] = vw_t[g * HEAD_DIM:(g + 1) * HEAD_DIM]
        vct[...] = vc_ref[0].T.astype(BF16)

    for sub in range(TILES_PER_STEP):
        _attn_tile(step * TILES_PER_STEP + sub, slice(sub * Q_TILE, (sub + 1) * Q_TILE),
                   q_ref, gt_ref, bc_ref, tw_ref, ts_ref, ovt_ref, bx_ref, o_ref,
                   ksv, kwv, kcv, vst, vwt, vct, rhs, acc, obuf, s_a, s_b, s_w, n_top)


def _attn_tile(qi, rows, q_ref, gt_ref, bc_ref, tw_ref, ts_ref, ovt_ref, bx_ref, o_ref,
               ksv, kwv, kcv, vst, vwt, vct, rhs, acc, obuf, s_a, s_b, s_w, n_top):
    n_kt = kwv.shape[1]
    n_sel = ovt_ref.shape[0]
    variants = 2 * N_GROUPS
    qt = q_ref[rows, :]
    gsig_t = _sigmoid(gt_ref[rows, :]).T

    def col_max(s):
        return jnp.max(s.reshape(s.shape[0] // 8, 8, COLS), axis=0)

    def group_rows(g):
        return slice(g * HEAD_DIM, (g + 1) * HEAD_DIM)

    def online(gv, s, s_max, vt, m):
        m_new = jnp.maximum(m, jnp.max(s_max, axis=0, keepdims=True))
        e = jnp.exp2(s - m_new)
        acc[gv] = acc[gv] * jnp.exp2(m - m_new) + _dot(vt, e.astype(BF16))
        return m_new

    def fresh_state():
        acc[...] = jnp.zeros(acc.shape, F32)
        return tuple(jnp.full((1, COLS), NEG, F32) for _ in range(variants))

    def finish(branch):
        for g in range(N_GROUPS):
            halves = []
            for v in range(2):
                gv = g * 2 + v
                l = acc[gv, HEAD_DIM:HEAD_DIM + 1, :]
                halves.append(acc[gv, 0:HEAD_DIM, :] * (1.0 / jnp.maximum(l, 1e-30)))
            obuf[branch, g] = jnp.concatenate(halves, axis=0)

    groups = range(N_GROUPS)
    for g in groups:
        q4 = jnp.concatenate(
            [qt[:, (g * PAIRS_PER_GROUP + p) * LANES:(g * PAIRS_PER_GROUP + p + 1) * LANES]
             for p in range(PAIRS_PER_GROUP)], axis=0)
        rhs[g, :, 0:LANES] = q4

    bc_rows = pl.ds(pl.multiple_of(LANES - qi * (Q_TILE // CMP_STRIDE), SUBLANES), LANES)
    s_cmp = [_dot_nt(kcv[g], rhs[g, :, 0:LANES])
             + jnp.concatenate([bc_ref[g, 0, bc_rows, :], bc_ref[g, 1, bc_rows, :]], axis=0)
             for g in groups]
    psum_t = []
    for g in groups:
        tot = jnp.zeros((LANES, Q_TILE), F32)
        halves = []
        for v in range(2):
            s = s_cmp[g][v * LANES:(v + 1) * LANES]
            m = jnp.max(s, axis=0, keepdims=True)
            e = jnp.exp2(s - jnp.where(m > 0.5 * NEG, m, 0.0))
            p = e * (1.0 / jnp.maximum(jnp.sum(e, axis=0, keepdims=True), 1e-30))
            for pp in range(PAIRS_PER_GROUP):
                tot = tot + p[:, pp * Q_TILE:(pp + 1) * Q_TILE]
            halves.append(_dot(vct[group_rows(g), :], p.astype(BF16)))
        obuf[0, g] = jnp.concatenate(halves, axis=0)
        psum_t.append(tot)

    ovt = ovt_ref[...]
    imp_t = []
    for g in groups:
        p_hi = psum_t[g].astype(BF16)
        r1 = psum_t[g] - p_hi.astype(F32)
        p_mid = r1.astype(BF16)
        p_lo = (r1 - p_mid.astype(F32)).astype(BF16)
        imp_t.append(_dot(ovt, p_hi) + _dot(ovt, p_mid) + _dot(ovt, p_lo))

    j_idx = lax.broadcasted_iota(jnp.int32, (n_sel, Q_TILE), 0)
    r_idx = lax.broadcasted_iota(jnp.int32, (n_sel, Q_TILE), 1)
    blk_t = qi * (Q_TILE // SEL_BLOCK) + r_idx // SEL_BLOCK
    valid_blk = j_idx <= blk_t
    forced = (j_idx == 0) | (j_idx == blk_t) | (j_idx == blk_t - 1)
    prio = [jnp.where(valid_blk, jnp.where(forced, FORCE_SCORE, imp_t[g]), -FORCE_SCORE) for g in groups]
    rank = [jnp.zeros((n_sel, Q_TILE), F32) for _ in groups]
    for jj in range(n_sel):
        later = j_idx > jj
        for g in groups:
            row = prio[g][jj:jj + 1, :]
            beats = (row > prio[g]) | ((row == prio[g]) & later)
            rank[g] = rank[g] + jnp.where(beats, 1.0, 0.0)
    for g in groups:
        drop_t = jnp.where((rank[g] < n_top) & valid_blk, 0.0, NEG)
        drop_t = jnp.concatenate([drop_t, jnp.zeros((LANES - n_sel, Q_TILE), F32)], axis=0)
        drop = drop_t.T.astype(BF16)
        rhs[g, :, LANES:2 * LANES] = jnp.concatenate([drop] * PAIRS_PER_GROUP, axis=0) + bx_ref[g]

    pair_rows = 4 * Q_TILE

    def half_rows(v):
        return slice(v * 2 * Q_TILE, (v + 1) * 2 * Q_TILE)

    def max_rows(v):
        return slice(pair_rows + v * SUBLANES, pair_rows + (v + 1) * SUBLANES)

    def logits_group(buf, g, i, tables=None):
        s_all = _dot_nt(ksv[g, i], rhs[g])
        for v in range(2):
            s = s_all[half_rows(v)]
            if tables is not None:
                s = jnp.concatenate([s[h * Q_TILE:(h + 1) * Q_TILE] + ts_ref[g, v, tables[h]]
                                     for h in range(2)], axis=0)
            buf[g, half_rows(v), :] = s
            buf[g, max_rows(v), :] = col_max(s)

    def logits_into(buf, i):
        for g in range(N_GROUPS):
            logits_group(buf, g, i)

    def consume(buf, i, state):
        ms = list(state)
        for g in range(N_GROUPS):
            for v in range(2):
                gv = g * 2 + v
                ms[gv] = online(gv, buf[g, half_rows(v), :], buf[g, max_rows(v), :], vst[g, i], ms[gv])
        return tuple(ms)

    n_all = (qi + 2) // 2
    n_far = jnp.maximum(qi - 1, 0) // 2

    first = jnp.maximum(qi - (WIN_TILES - 1), 0)
    tiles = []
    for t in range(WIN_TILES):
        dd = qi - (first + t)
        table = jnp.where(dd >= 0, dd, WIN_TILES)
        tiles.append((jnp.minimum(first + t, n_kt - 1), table))
    m_win = []
    one_matmul = n_kt >= WIN_TILES
    for g in range(N_GROUPS):
        mx = [jnp.full((8, COLS), NEG, F32) for _ in range(2)]
        if one_matmul:
            k_rows = kwv[g, pl.ds(first, WIN_TILES)].reshape(WIN_TILES * 2 * Q_TILE, LANES)
            s_five = _dot_nt(k_rows, rhs[g, :, 0:LANES])
        for t, (kt, table) in enumerate(tiles):
            if one_matmul:
                s_all = s_five[t * 2 * Q_TILE:(t + 1) * 2 * Q_TILE]
            else:
                s_all = _dot_nt(kwv[g, kt], rhs[g, :, 0:LANES])
            for v in range(2):
                s = s_all[v * Q_TILE:(v + 1) * Q_TILE] + tw_ref[g, v, table]
                s_w[g, t, v * Q_TILE:(v + 1) * Q_TILE, :] = s
                mx[v] = jnp.maximum(mx[v], col_max(s))
        m_win += [jnp.max(mx[v], axis=0, keepdims=True) for v in range(2)]

    for g in range(N_GROUPS):
        logits_group(s_a, g, 0)
        halves = []
        for v in range(2):
            o = jnp.zeros((HEAD_DIM + ONES_ROWS, COLS), F32)
            for t, (kt, _) in enumerate(tiles):
                e = jnp.exp2(s_w[g, t, v * Q_TILE:(v + 1) * Q_TILE, :] - m_win[g * 2 + v])
                o = o + _dot(vwt[g, kt], e.astype(BF16))
            halves.append(o[0:HEAD_DIM] * (1.0 / jnp.maximum(o[HEAD_DIM:HEAD_DIM + 1], 1e-30)))
        obuf[2, g] = jnp.concatenate(halves, axis=0)

    def far_two(j, state):
        logits_into(s_b, 2 * j + 1)
        state = consume(s_a, 2 * j, state)
        logits_into(s_a, 2 * j + 2)
        return consume(s_b, 2 * j + 1, state)

    state = lax.fori_loop(0, n_far // 2, far_two, fresh_state())
    state = lax.fori_loop(0, n_far % 2, lambda _, st: consume(s_a, n_far - 1, st), state)

    def near_pair(i, state):
        tables = [jnp.clip(qi - (2 * i + h), 0, 2) for h in range(2)]
        for g in range(N_GROUPS):
            logits_group(s_b, g, i, tables)
        return consume(s_b, i, state)

    diag_alone = 1 - qi % 2
    state = lax.fori_loop(n_far, n_all - diag_alone, near_pair, state)

    def diagonal_tile(_, state):
        ms = list(state)
        i = n_all - 1
        for g in range(N_GROUPS):
            lhs = jnp.concatenate([ksv[g, i, v * 2 * Q_TILE:v * 2 * Q_TILE + Q_TILE, :] for v in range(2)],
                                  axis=0)
            s_all = _dot_nt(lhs, rhs[g])
            for v in range(2):
                s = s_all[v * Q_TILE:(v + 1) * Q_TILE] + ts_ref[g, v, 0]
                s_b[g, v * Q_TILE:(v + 1) * Q_TILE, :] = s
                s_b[g, max_rows(v), :] = col_max(s)
        for g in range(N_GROUPS):
            for v in range(2):
                gv = g * 2 + v
                ms[gv] = online(gv, s_b[g, v * Q_TILE:(v + 1) * Q_TILE, :], s_b[g, max_rows(v), :],
                                vst[g, i, :, 0:Q_TILE], ms[gv])
        return tuple(ms)

    lax.fori_loop(0, diag_alone, diagonal_tile, state)
    finish(1)

    for g in range(N_GROUPS):
        for p in range(PAIRS_PER_GROUP):
            halves = []
            for v in range(2):
                tot = jnp.zeros((HEAD_DIM, Q_TILE), F32)
                for br in range(3):
                    c = br * N_HEADS + g * HEADS_PER_GROUP + 2 * p + v
                    tot = tot + gsig_t[c:c + 1, :] * obuf[br, g, v * HEAD_DIM:(v + 1) * HEAD_DIM,
                                                          p * Q_TILE:(p + 1) * Q_TILE]
                halves.append(tot)
            col = (g * PAIRS_PER_GROUP + p) * LANES
            o_ref[rows, col:col + LANES] = jnp.concatenate(halves, axis=0).T.astype(BF16)


def _attention(proj_main, proj_kv, kcmp, vcmp, bias_c, bias_w, bias_s, ovt, far_lanes, b, s, n_top):
    n_q = s // Q_TILE
    n_steps = n_q // TILES_PER_STEP
    step_rows = TILES_PER_STEP * Q_TILE
    once = pl.Buffered(1)
    kv_col = lambda c: pl.BlockSpec((s, LANES), lambda bi, qi: (bi, c))
    cmp_spec = pl.BlockSpec((1, LANES, LANES), lambda bi, qi: (bi, 0, 0))
    return pl.pallas_call(
        functools.partial(_attn_kernel, n_top=n_top),
        grid=(b, n_steps),
        in_specs=[
            pl.BlockSpec((step_rows, NSA_WIDTH), lambda bi, qi: (bi * n_steps + qi, 0)),
            kv_col(2), kv_col(3), kv_col(4), kv_col(5),
            pl.BlockSpec((step_rows, LANES), lambda bi, qi: (bi * n_steps + qi, 6)),
            cmp_spec, cmp_spec,
            pl.BlockSpec(bias_c.shape, lambda bi, qi: (0,) * 4, pipeline_mode=once),
            pl.BlockSpec(bias_w.shape, lambda bi, qi: (0,) * 5, pipeline_mode=once),
            pl.BlockSpec(bias_s.shape, lambda bi, qi: (0,) * 5, pipeline_mode=once),
            pl.BlockSpec(ovt.shape, lambda bi, qi: (0, 0), pipeline_mode=once),
            pl.BlockSpec(far_lanes.shape, lambda bi, qi: (0, 0, 0), pipeline_mode=once),
        ],
        out_specs=pl.BlockSpec((step_rows, NSA_WIDTH), lambda bi, qi: (bi * n_steps + qi, 0)),
        out_shape=jax.ShapeDtypeStruct((b * s, NSA_WIDTH), BF16),
        scratch_shapes=[
            pltpu.VMEM((N_GROUPS, n_q // 2, 4 * Q_TILE, 2 * LANES), BF16),
            pltpu.VMEM((N_GROUPS, n_q, 2 * Q_TILE, LANES), BF16),
            pltpu.VMEM((N_GROUPS, 2 * LANES, LANES), BF16),
            pltpu.VMEM((N_GROUPS, n_q // 2, HEAD_DIM + ONES_ROWS, 2 * Q_TILE), BF16),
            pltpu.VMEM((N_GROUPS, n_q, HEAD_DIM + ONES_ROWS, Q_TILE), BF16),
            pltpu.VMEM((LANES, LANES), BF16),
            pltpu.VMEM((N_GROUPS, COLS, 2 * LANES), BF16),
            pltpu.VMEM((2 * N_GROUPS, HEAD_DIM + ONES_ROWS, COLS), F32),
            pltpu.VMEM((3, N_GROUPS, LANES, COLS), F32),
            pltpu.VMEM((N_GROUPS, 4 * Q_TILE + 2 * SUBLANES, COLS), F32),
            pltpu.VMEM((N_GROUPS, 4 * Q_TILE + 2 * SUBLANES, COLS), F32),
            pltpu.VMEM((N_GROUPS, WIN_TILES, 2 * Q_TILE, COLS), F32),
        ],
        compiler_params=pltpu.CompilerParams(
            dimension_semantics=("parallel", "arbitrary"), vmem_limit_bytes=VMEM_LIMIT),
        name="nsa_attention",
    )(proj_main, proj_kv, proj_kv, proj_kv, proj_kv, proj_kv, kcmp, vcmp, bias_c, bias_w, bias_s,
      ovt, far_lanes)


def _merge_kernel(zn_ref, a_ref, b_ref, zc_ref, gc_ref, gn_ref, ah_ref, bh_ref, on_ref, x_ref,
                  cw_ref, cb_ref, lg_ref, lb_ref, wcp_ref, wnp_ref, wo_ref, gf_ref, out_ref, uext,
                  conv, shifted):
    i = pl.program_id(1)
    ts = a_ref.shape[0]
    f = lambda r: r[...].astype(F32)

    n_cblk = D_MODEL // LANES
    u_halo = jnp.where(i > 0, f(ah_ref) * _sigmoid(f(bh_ref)), 0.0)
    u = f(a_ref) * _sigmoid(f(b_ref))
    for cblk in range(n_cblk):
        cols = slice(cblk * LANES, (cblk + 1) * LANES)
        uext[cblk, 0:CONV_HALO, :] = u_halo[:, cols]
        uext[cblk, CONV_HALO:CONV_HALO + ts, :] = u[:, cols]
        uext[cblk, CONV_HALO + ts:, :] = jnp.zeros((SUBLANES, LANES), F32)

    lead = CONV_HALO - (CONV_KERNEL - 1)
    chunk = Q_TILE

    def conv_block(cblk, carry):
        for shift in range(SUBLANES):
            shifted[shift] = uext[cblk, shift:shift + ts + CONV_HALO, :]
        w = cw_ref[cblk]
        for h in range(ts // chunk):
            c = jnp.broadcast_to(cb_ref[cblk], (chunk, LANES))
            for shift in range(SUBLANES):
                xs = shifted[shift, h * chunk:h * chunk + chunk + CONV_HALO, :]
                for j in range(CONV_KERNEL):
                    if (lead + j) % SUBLANES == shift:
                        base = lead + j - shift
                        c = c + w[j:j + 1, :] * xs[base:base + chunk]
            conv[cblk, h * chunk:(h + 1) * chunk, :] = c
        return carry

    lax.fori_loop(0, n_cblk, conv_block, 0)
    c = jnp.concatenate([conv[cblk] for cblk in range(n_cblk)], axis=1)

    mu = jnp.mean(c, axis=-1, keepdims=True)
    cc = c - mu
    var = jnp.mean(cc * cc, axis=-1, keepdims=True)
    y = (cc * lax.rsqrt(var + EPS)) * lg_ref[...] + lb_ref[...]
    conv_act = _silu(y) * _silu(f(zc_ref))
    y_conv = _dot(conv_act.astype(BF16), wcp_ref[...])

    nsa_act = f(on_ref) * _silu(f(zn_ref))
    y_nsa = _dot(nsa_act.astype(BF16), wnp_ref[...])

    merged = _sigmoid(f(gc_ref)) * y_conv + _sigmoid(f(gn_ref)) * y_nsa
    xo = x_ref[...] + _dot(merged.astype(BF16), wo_ref[...])
    ms = jnp.mean(xo * xo, axis=-1, keepdims=True)
    out_ref[...] = (xo * lax.rsqrt(ms + EPS)) * gf_ref[...]


def _merge(proj_main, o_nsa, x2, cw, cb, lg, lb, wcp, wnp, wo, gf, b, s, ts):
    n_t = s // ts
    n_cblk = D_MODEL // LANES
    halo_per_tile = ts // CONV_HALO
    cw = jnp.transpose(cw.reshape(cw.shape[0], n_cblk, LANES), (1, 0, 2))
    cb = cb.reshape(n_cblk, 1, LANES)
    col = lambda c: pl.BlockSpec((ts, COL_TILE), lambda bi, ti: (bi * n_t + ti, c))
    halo = lambda c: pl.BlockSpec(
        (CONV_HALO, COL_TILE),
        lambda bi, ti: (jnp.maximum((bi * n_t + ti) * halo_per_tile - 1, 0), c))
    const = lambda a: pl.BlockSpec(a.shape, lambda bi, ti: (0,) * a.ndim, pipeline_mode=pl.Buffered(1))
    rowblk = pl.BlockSpec((ts, D_MODEL), lambda bi, ti: (bi * n_t + ti, 0))
    return pl.pallas_call(
        _merge_kernel,
        grid=(b, n_t),
        in_specs=[col(2), col(3), col(4), col(5), col(6), col(7), halo(3), halo(4), rowblk, rowblk,
                  const(cw), const(cb), const(lg), const(lb), const(wcp), const(wnp), const(wo),
                  const(gf)],
        out_specs=rowblk,
        out_shape=jax.ShapeDtypeStruct((b * s, D_MODEL), F32),
        scratch_shapes=[pltpu.VMEM((n_cblk, ts + CONV_HALO + SUBLANES, LANES), F32),
                        pltpu.VMEM((n_cblk, ts, LANES), F32),
                        pltpu.VMEM((SUBLANES, ts + CONV_HALO, LANES), F32)],
        compiler_params=pltpu.CompilerParams(
            dimension_semantics=("parallel", "arbitrary"), vmem_limit_bytes=VMEM_LIMIT),
        name="conv_merge",
    )(proj_main, proj_main, proj_main, proj_main, proj_main, proj_main, proj_main, proj_main,
      o_nsa, x2, cw, cb, lg, lb, wcp, wnp, wo, gf)


def _t5_bucket_np(rel):
    rel = np.maximum(rel, 0)
    max_exact = REL_BUCKETS // 2
    relf = np.maximum(rel, 1).astype(np.float32)
    large = max_exact + (np.log(relf / np.float32(max_exact))
                         / np.float32(np.log(REL_MAX_DIST / max_exact))
                         * np.float32(REL_BUCKETS - max_exact)).astype(np.int32)
    large = np.minimum(large, REL_BUCKETS - 1)
    return np.where(rel < max_exact, rel, large)


def _pair_head_index():
    g = np.arange(N_GROUPS)[:, None, None]
    v = np.arange(2)[None, :, None]
    p = np.arange(PAIRS_PER_GROUP)[None, None, :]
    return g * HEADS_PER_GROUP + 2 * p + v


def _bias_lookup(rel_bias, rel):
    bucket = _t5_bucket_np(rel).reshape(-1)
    onehot = (jnp.arange(REL_BUCKETS)[:, None] == jnp.asarray(bucket)[None, :]).astype(F32)
    vals = jnp.dot(rel_bias.astype(F32).T * LOG2E, onehot, precision=lax.Precision.HIGHEST)
    vals = vals.reshape((N_HEADS,) + rel.shape)
    head = _pair_head_index()
    return jnp.stack([jnp.stack([jnp.concatenate([vals[h] for h in head[g, v]], axis=-1)
                                 for v in range(2)]) for g in range(N_GROUPS)])


def _bias_tables(rel_bias, s, n_sel):
    c = np.arange(LANES)[:, None]
    r = np.arange(Q_TILE)[None, :]
    far = rel_bias.astype(F32)[REL_BUCKETS - 1][_pair_head_index()] * LOG2E
    assert s // CMP_STRIDE <= LANES + Q_TILE // CMP_STRIDE
    rel_w = np.stack([dd * Q_TILE + r - c for dd in range(WIN_TILES)])
    assert (_t5_bucket_np(rel_w[2:]) == REL_BUCKETS - 1).all()
    rho = np.arange(2 * LANES)[:, None]
    rel_c = r - ((rho - LANES) * CMP_STRIDE + CMP_BLOCK - 1)
    vals = _bias_lookup(rel_bias, np.concatenate([rel_w, rel_c.reshape(2, LANES, Q_TILE)]))
    bias_w, bias_c = vals[:, :, :WIN_TILES], vals[:, :, WIN_TILES:].reshape(N_GROUPS, 2, 2 * LANES, COLS)
    cols = lambda a: np.tile(a, (1,) * (a.ndim - 1) + (PAIRS_PER_GROUP,))

    tw = jnp.where(cols((rel_w >= 0) & (rel_w < WINDOW)), bias_w, NEG)
    tw = jnp.concatenate([tw, jnp.full_like(tw[:, :, :1], NEG)], axis=2)
    far_cols = jnp.repeat(far, Q_TILE, axis=-1)[:, :, None, None, :]
    ts = jnp.where(cols(rel_w[:3] >= 0), bias_w[:, :, :3] - far_cols, NEG)
    bc = jnp.where(cols(rel_c >= 0), bias_c, NEG)
    hi = far.astype(BF16)
    lo = (far - hi.astype(F32)).astype(BF16)
    pieces = jnp.stack([hi[:, 0], lo[:, 0], hi[:, 1], lo[:, 1]], axis=-1)
    pieces = jnp.broadcast_to(pieces[:, :, None, :], (N_GROUPS, PAIRS_PER_GROUP, Q_TILE, 4))
    far_lanes = jnp.pad(pieces.reshape(N_GROUPS, COLS, 4),
                        ((0, 0), (0, 0), (n_sel, LANES - n_sel - 4)))
    return tw, ts, bc, far_lanes


def _overlap_t(s):
    n_cmp = (s - CMP_BLOCK) // CMP_STRIDE + 1
    cs = np.arange(LANES) * CMP_STRIDE
    ss = np.arange(s // SEL_BLOCK) * SEL_BLOCK
    ovt = ((cs[None, :] <= ss[:, None] + SEL_BLOCK - 1) & (cs[None, :] + CMP_BLOCK - 1 >= ss[:, None])
           & (np.arange(LANES)[None, :] < n_cmp))
    return jnp.asarray(ovt, BF16)


def _transposed_w_in(w):
    assert w.shape[1] == _W_USED + (N_COL_TILES - 2) * COL_TILE and _W_USED % 16 == 0
    col = jnp.arange(w.shape[1])[:, None]
    w_t = jnp.swapaxes(w, 0, 1)
    return jnp.where(col < NSA_WIDTH, w_t * (HEAD_DIM ** -0.5 * LOG2E), w_t).astype(BF16)


def _compress_weights(pos, w1, w2):
    assert N_GROUPS == 2
    half = CMP_BLOCK // 2

    def per_group(w):
        zeros = [(0, 0)] * (w.ndim - 1)
        top = jnp.pad(w, zeros + [(0, w.shape[-1])])
        bottom = jnp.pad(w, zeros + [(w.shape[-1], 0)])
        return jnp.concatenate([top, bottom], axis=-2)

    w1g = per_group(w1.reshape(CMP_BLOCK, HEAD_DIM, CMP_HIDDEN))
    blk = lambda part: part.reshape(half * KV_WIDTH, N_GROUPS * CMP_HIDDEN)
    w1d = jnp.concatenate([blk(w1g[:half]), blk(w1g[half:])], axis=1).astype(BF16)
    w2d = per_group(w2).astype(BF16)
    tilepos = lambda part: jnp.broadcast_to(part[:, None, :], (half, N_GROUPS, HEAD_DIM)).reshape(1, -1)
    posd = jnp.concatenate([tilepos(pos[:half]), tilepos(pos[half:])], axis=0).astype(F32)
    return posd, w1d, w2d


def kernel(x, norm_in_g, w_in, pos_ck, w_ck1, w_ck2, pos_cv, w_cv1, w_cv2, rel_bias, conv_w, conv_b,
           conv_ln_g, conv_ln_b, w_conv_proj, w_nsa_proj, w_out, norm_f_g):
    b, s, d = x.shape
    n_sel = s // SEL_BLOCK
    assert d == D_MODEL and w_in.shape[0] == 1, "single-layer block with D_MODEL=1024"
    assert s % (2 * Q_TILE) == 0 and s // CMP_STRIDE <= LANES and s >= WINDOW
    assert n_sel + 4 <= LANES
    m = b * s
    x2 = x.reshape(m, d)
    row = lambda a: a.reshape(1, -1).astype(F32)

    proj_kv, proj_main = _input_projection(x2, row(norm_in_g[0]), _transposed_w_in(w_in[0]), min(1024, m))

    chunks = s // CMP_STRIDE
    pk, w1k, w2k = _compress_weights(pos_ck[0], w_ck1[0], w_ck2[0])
    pv, w1v, w2v = _compress_weights(pos_cv[0], w_cv1[0], w_cv2[0])
    kcmp, vcmp = _compress(proj_kv, pk, pv, w1k, w1v, w2k, w2v, b, s)
    if chunks < LANES:
        padrows = ((0, 0), (0, LANES - chunks), (0, 0))
        kcmp, vcmp = jnp.pad(kcmp, padrows), jnp.pad(vcmp, padrows)

    bias_w, bias_s, bias_c, far_lanes = _bias_tables(rel_bias, s, n_sel)
    o_nsa = _attention(proj_main, proj_kv, kcmp, vcmp, bias_c, bias_w, bias_s, _overlap_t(s),
                       far_lanes, b, s, min(N_SELECT, n_sel))

    cw = jnp.pad(conv_w[0].astype(F32), ((0, CONV_HALO - CONV_KERNEL), (0, 0)))
    out = _merge(proj_main, o_nsa, x2, cw, row(conv_b[0]), row(conv_ln_g[0]), row(conv_ln_b[0]),
                 w_conv_proj[0].astype(BF16), w_nsa_proj[0].astype(BF16), w_out[0].astype(BF16),
                 row(norm_f_g), b, s, MERGE_ROWS)
    return out.reshape(b, s, d)
```
